```python
import jax, jax.numpy as jnp
from jax import lax
import numpy as np

D_MODEL = 1024
BATCH = 8
SEQ = 8192
DEPTH = 4

D_MIX = D_MODEL
D_CONF = D_MIX // 4
D_ATT = D_MIX // 4
D_SC = D_MIX // 4
D_POOL = D_MIX - D_CONF - D_ATT - D_SC
HEAD_DIM = 64
N_ATT_HEADS = D_ATT // HEAD_DIM
CONF_KERNEL = 31
SC_KERNEL = 3
POOL_WINDOWS = (2, 4, 8, 16)
N_POOL_GROUPS = len(POOL_WINDOWS)
POOL_GROUP_DIM = D_POOL // N_POOL_GROUPS
D_FF = 4 * D_MODEL
D_PLE = 256
Q_BLOCK = 128
EPS = 1e-6
SPLITS = (2 * D_CONF, D_ATT, D_ATT, D_ATT, N_ATT_HEADS, D_SC, D_SC, D_SC, D_POOL)
D_IN = sum(SPLITS)
SPLIT_IDX = tuple(int(s) for s in np.cumsum(SPLITS)[:-1])

kernel_name = "hybrid_parallel_groups_fox_conv_pool"


def rms_norm(x, g):
    x32 = x.astype(jnp.float32)
    y = x32 * lax.rsqrt(jnp.mean(x32 * x32, axis=-1, keepdims=True) + EPS)
    return (y * g.astype(jnp.float32)).astype(x.dtype)


def layer_norm(x, g, b):
    x32 = x.astype(jnp.float32)
    mu = jnp.mean(x32, axis=-1, keepdims=True)
    xc = x32 - mu
    y = xc * lax.rsqrt(jnp.mean(xc * xc, axis=-1, keepdims=True) + EPS)
    return (y * g.astype(jnp.float32) + b.astype(jnp.float32)).astype(x.dtype)


def causal_depthwise_conv(u, w):
    k = w.shape[0]
    return lax.conv_general_dilated(
        u, w[:, None, :].astype(u.dtype), window_strides=(1,), padding=((k - 1, 0),),
        dimension_numbers=("NWC", "WIO", "NWC"), feature_group_count=u.shape[-1])


def conformer_conv(ab, w_dw, ln_g, ln_b, w_pw):
    a, b = jnp.split(ab, 2, axis=-1)
    u = a * jax.nn.sigmoid(b)
    u = causal_depthwise_conv(u, w_dw)
    u = jax.nn.silu(layer_norm(u, ln_g, ln_b))
    return u @ w_pw


def forgetting_attention(q, k, v, f_logit):
    b, s, _ = q.shape
    q = q.reshape(b, s, N_ATT_HEADS, HEAD_DIM).transpose(0, 2, 1, 3)
    k = k.reshape(b, s, N_ATT_HEADS, HEAD_DIM).transpose(0, 2, 1, 3)
    v = v.reshape(b, s, N_ATT_HEADS, HEAD_DIM).transpose(0, 2, 1, 3)
    log_f = jax.nn.log_sigmoid(f_logit.astype(jnp.float32))
    c = jnp.cumsum(log_f, axis=1).transpose(0, 2, 1)
    nb = s // Q_BLOCK
    qb = q.reshape(b, N_ATT_HEADS, nb, Q_BLOCK, HEAD_DIM).transpose(2, 0, 1, 3, 4)
    cb = c.reshape(b, N_ATT_HEADS, nb, Q_BLOCK).transpose(2, 0, 1, 3)
    pos = jnp.arange(s, dtype=jnp.int32)
    posb = pos.reshape(nb, Q_BLOCK)
    k32 = k.astype(jnp.float32)
    scale = HEAD_DIM ** -0.5

    def block(args):
        qi, ci, pi = args
        logits = jnp.einsum("bhqd,bhkd->bhqk", qi.astype(jnp.float32), k32) * scale
        logits = logits + ci[..., None] - c[:, :, None, :]
        mask = pi[:, None] >= pos[None, :]
        logits = jnp.where(mask, logits, -jnp.inf)
        probs = jax.nn.softmax(logits, axis=-1)
        return jnp.einsum("bhqk,bhkd->bhqd", probs.astype(v.dtype), v)

    o = lax.map(block, (qb, cb, posb))
    return o.transpose(1, 0, 3, 2, 4).reshape(b, s, D_ATT)


def short_conv_mixer(h, bg, cg, w_sc):
    return bg * causal_depthwise_conv(cg * h, w_sc)


def multiscale_pool(v, w_pool, scale):
    b, s, _ = v.shape
    v32 = v.astype(jnp.float32)
    count = jnp.arange(1, s + 1, dtype=jnp.float32)[None, :, None]
    groups = jnp.split(v32, N_POOL_GROUPS, axis=-1)
    outs = []
    for g, w in zip(groups, POOL_WINDOWS):
        csum = jnp.cumsum(g, axis=1)
        lag = jnp.pad(csum, ((0, 0), (w, 0), (0, 0)))[:, :s]
        mean = (csum - lag) / jnp.minimum(count, w)
        outs.append(mean - g)
    d = jnp.stack(outs, axis=2).astype(v.dtype)
    d = jnp.einsum("bsgc,gcd->bsgd", d, w_pool).reshape(b, s, D_POOL)
    return d * scale


def _fwd_setup_inputs(seed: int = 0) -> dict:
    key = jax.random.key(seed)
    ks = jax.random.split(key, 24)
    f32 = jnp.float32
    L = DEPTH

    def nrm(k, shape, fan_in):
        return jax.random.normal(k, shape, f32) * (fan_in ** -0.5)

    def gain(k, shape):
        return 1.0 + 0.05 * jax.random.normal(k, shape, f32)

    return {
        "x": jax.random.normal(ks[0], (BATCH, SEQ, D_MODEL), f32),
        "p": jax.random.normal(ks[1], (DEPTH, BATCH, SEQ, D_PLE), f32),
        "g_mix_pre": gain(ks[2], (L, D_MODEL)),
        "w_in": nrm(ks[3], (L, D_MODEL, D_IN), D_MODEL),
        "b_forget": 0.1 * jax.random.normal(ks[4], (L, N_ATT_HEADS), f32),
        "w_conf_dw": nrm(ks[5], (L, CONF_KERNEL, D_CONF), CONF_KERNEL),
        "conf_ln_g": gain(ks[6], (L, D_CONF)),
        "conf_ln_b": 0.02 * jax.random.normal(ks[7], (L, D_CONF), f32),
        "w_conf_pw": nrm(ks[8], (L, D_CONF, D_CONF), D_CONF),
        "w_sc": nrm(ks[9], (L, SC_KERNEL, D_SC), SC_KERNEL),
        "w_pool": nrm(ks[10], (L, N_POOL_GROUPS, POOL_GROUP_DIM, POOL_GROUP_DIM), POOL_GROUP_DIM),
        "pool_scale": gain(ks[11], (L, D_POOL)),
        "w_out": nrm(ks[12], (L, D_MIX, D_MODEL), D_MIX),
        "g_mix_post": gain(ks[13], (L, D_MODEL)),
        "g_mlp_pre": gain(ks[14], (L, D_MODEL)),
        "w_up": nrm(ks[15], (L, D_MODEL, D_FF), D_MODEL),
        "w_down": nrm(ks[16], (L, D_FF, D_MODEL), D_FF),
        "g_mlp_post": gain(ks[17], (L, D_MODEL)),
        "g_ple_pre": gain(ks[18], (L, D_MODEL)),
        "w_ple_gate": nrm(ks[19], (L, D_MODEL, D_MODEL), D_MODEL),
        "w_ple_proj": nrm(ks[20], (L, D_PLE, D_MODEL), D_PLE),
        "g_ple_post": gain(ks[21], (L, D_MODEL)),
    }


def _fwd_reference(x, p, g_mix_pre, w_in, b_forget, w_conf_dw, conf_ln_g, conf_ln_b, w_conf_pw,
              w_sc, w_pool, pool_scale, w_out, g_mix_post, g_mlp_pre, w_up, w_down,
              g_mlp_post, g_ple_pre, w_ple_gate, w_ple_proj, g_ple_post):
    h = x
    for i in range(DEPTH):
        xn = rms_norm(h, g_mix_pre[i])
        z = xn @ w_in[i]
        conf_ab, q, k, v, f_logit, sc_h, sc_b, sc_c, pool_v = jnp.split(z, SPLIT_IDX, axis=-1)
        y_conf = conformer_conv(conf_ab, w_conf_dw[i], conf_ln_g[i], conf_ln_b[i], w_conf_pw[i])
        y_att = forgetting_attention(q, k, v, f_logit + b_forget[i])
        y_sc = short_conv_mixer(sc_h, sc_b, sc_c, w_sc[i])
        y_pool = multiscale_pool(pool_v, w_pool[i], pool_scale[i])
        mix = jnp.concatenate([y_conf, y_att, y_sc, y_pool], axis=-1) @ w_out[i]
        h = h + rms_norm(mix, g_mix_post[i])
        hn = rms_norm(h, g_mlp_pre[i])
        ff = jnp.square(jax.nn.relu(hn @ w_up[i])) @ w_down[i]
        h = h + rms_norm(ff, g_mlp_post[i])
        gate = jax.nn.sigmoid(rms_norm(h, g_ple_pre[i]) @ w_ple_gate[i])
        e = (p[i] @ w_ple_proj[i]) * gate
        h = h + rms_norm(e, g_ple_post[i])
    return h


import jax as _jax
import jax.numpy as _jnp

TWIN_FORMAT = 'train_step'
FWD_PARAMS = ['x', 'p', 'g_mix_pre', 'w_in', 'b_forget', 'w_conf_dw', 'conf_ln_g', 'conf_ln_b', 'w_conf_pw', 'w_sc', 'w_pool', 'pool_scale', 'w_out', 'g_mix_post', 'g_mlp_pre', 'w_up', 'w_down', 'g_mlp_post', 'g_ple_pre', 'w_ple_gate', 'w_ple_proj', 'g_ple_post']
TWIN_WEIGHTS = ['g_mix_pre', 'w_in', 'b_forget', 'w_conf_dw', 'conf_ln_g', 'conf_ln_b', 'w_conf_pw', 'w_sc', 'w_pool', 'pool_scale', 'w_out', 'g_mix_post', 'g_mlp_pre', 'w_up', 'w_down', 'g_mlp_post', 'g_ple_pre', 'w_ple_gate', 'w_ple_proj', 'g_ple_post']
TWIN_DIFF_INPUT = 'x'
TWIN_INPUTS = ['x', 'p', 'g_mix_pre', 'w_in', 'b_forget', 'w_conf_dw', 'conf_ln_g', 'conf_ln_b', 'w_conf_pw', 'w_sc', 'w_pool', 'pool_scale', 'w_out', 'g_mix_post', 'g_mlp_pre', 'w_up', 'w_down', 'g_mlp_post', 'g_ple_pre', 'w_ple_gate', 'w_ple_proj', 'g_ple_post', 'loss_target', 'm_g_mix_pre', 'm_w_in', 'm_b_forget', 'm_w_conf_dw', 'm_conf_ln_g', 'm_conf_ln_b', 'm_w_conf_pw', 'm_w_sc', 'm_w_pool', 'm_pool_scale', 'm_w_out', 'm_g_mix_post', 'm_g_mlp_pre', 'm_w_up', 'm_w_down', 'm_g_mlp_post', 'm_g_ple_pre', 'm_w_ple_gate', 'm_w_ple_proj', 'm_g_ple_post', 'v_g_mix_pre', 'v_w_in', 'v_b_forget', 'v_w_conf_dw', 'v_conf_ln_g', 'v_conf_ln_b', 'v_w_conf_pw', 'v_w_sc', 'v_w_pool', 'v_pool_scale', 'v_w_out', 'v_g_mix_post', 'v_g_mlp_pre', 'v_w_up', 'v_w_down', 'v_g_mlp_post', 'v_g_ple_pre', 'v_w_ple_gate', 'v_w_ple_proj', 'v_g_ple_post']
TWIN_OUTPUTS = ['loss', 'grad_x', 'grad_g_mix_pre', 'grad_w_in', 'grad_b_forget', 'grad_w_conf_dw', 'grad_conf_ln_g', 'grad_conf_ln_b', 'grad_w_conf_pw', 'grad_w_sc', 'grad_w_pool', 'grad_pool_scale', 'grad_w_out', 'grad_g_mix_post', 'grad_g_mlp_pre', 'grad_w_up', 'grad_w_down', 'grad_g_mlp_post', 'grad_g_ple_pre', 'grad_w_ple_gate', 'grad_w_ple_proj', 'grad_g_ple_post', 'delta_g_mix_pre', 'delta_w_in', 'delta_b_forget', 'delta_w_conf_dw', 'delta_conf_ln_g', 'delta_conf_ln_b', 'delta_w_conf_pw', 'delta_w_sc', 'delta_w_pool', 'delta_pool_scale', 'delta_w_out', 'delta_g_mix_post', 'delta_g_mlp_pre', 'delta_w_up', 'delta_w_down', 'delta_g_mlp_post', 'delta_g_ple_pre', 'delta_w_ple_gate', 'delta_w_ple_proj', 'delta_g_ple_post', 'new_m_g_mix_pre', 'new_m_w_in', 'new_m_b_forget', 'new_m_w_conf_dw', 'new_m_conf_ln_g', 'new_m_conf_ln_b', 'new_m_w_conf_pw', 'new_m_w_sc', 'new_m_w_pool', 'new_m_pool_scale', 'new_m_w_out', 'new_m_g_mix_post', 'new_m_g_mlp_pre', 'new_m_w_up', 'new_m_w_down', 'new_m_g_mlp_post', 'new_m_g_ple_pre', 'new_m_w_ple_gate', 'new_m_w_ple_proj', 'new_m_g_ple_post', 'new_v_g_mix_pre', 'new_v_w_in', 'new_v_b_forget', 'new_v_w_conf_dw', 'new_v_conf_ln_g', 'new_v_conf_ln_b', 'new_v_w_conf_pw', 'new_v_w_sc', 'new_v_w_pool', 'new_v_pool_scale', 'new_v_w_out', 'new_v_g_mix_post', 'new_v_g_mlp_pre', 'new_v_w_up', 'new_v_w_down', 'new_v_g_mlp_post', 'new_v_g_ple_pre', 'new_v_w_ple_gate', 'new_v_w_ple_proj', 'new_v_g_ple_post']
TWIN_LEAF_KINDS = {'loss': 'loss', 'grad_x': 'grad_x', 'grad_g_mix_pre': 'grad_w', 'grad_w_in': 'grad_w', 'grad_b_forget': 'grad_w', 'grad_w_conf_dw': 'grad_w', 'grad_conf_ln_g': 'grad_w', 'grad_conf_ln_b': 'grad_w', 'grad_w_conf_pw': 'grad_w', 'grad_w_sc': 'grad_w', 'grad_w_pool': 'grad_w', 'grad_pool_scale': 'grad_w', 'grad_w_out': 'grad_w', 'grad_g_mix_post': 'grad_w', 'grad_g_mlp_pre': 'grad_w', 'grad_w_up': 'grad_w', 'grad_w_down': 'grad_w', 'grad_g_mlp_post': 'grad_w', 'grad_g_ple_pre': 'grad_w', 'grad_w_ple_gate': 'grad_w', 'grad_w_ple_proj': 'grad_w', 'grad_g_ple_post': 'grad_w', 'delta_g_mix_pre': 'delta_w', 'delta_w_in': 'delta_w', 'delta_b_forget': 'delta_w', 'delta_w_conf_dw': 'delta_w', 'delta_conf_ln_g': 'delta_w', 'delta_conf_ln_b': 'delta_w', 'delta_w_conf_pw': 'delta_w', 'delta_w_sc': 'delta_w', 'delta_w_pool': 'delta_w', 'delta_pool_scale': 'delta_w', 'delta_w_out': 'delta_w', 'delta_g_mix_post': 'delta_w', 'delta_g_mlp_pre': 'delta_w', 'delta_w_up': 'delta_w', 'delta_w_down': 'delta_w', 'delta_g_mlp_post': 'delta_w', 'delta_g_ple_pre': 'delta_w', 'delta_w_ple_gate': 'delta_w', 'delta_w_ple_proj': 'delta_w', 'delta_g_ple_post': 'delta_w', 'new_m_g_mix_pre': 'new_m', 'new_m_w_in': 'new_m', 'new_m_b_forget': 'new_m', 'new_m_w_conf_dw': 'new_m', 'new_m_conf_ln_g': 'new_m', 'new_m_conf_ln_b': 'new_m', 'new_m_w_conf_pw': 'new_m', 'new_m_w_sc': 'new_m', 'new_m_w_pool': 'new_m', 'new_m_pool_scale': 'new_m', 'new_m_w_out': 'new_m', 'new_m_g_mix_post': 'new_m', 'new_m_g_mlp_pre': 'new_m', 'new_m_w_up': 'new_m', 'new_m_w_down': 'new_m', 'new_m_g_mlp_post': 'new_m', 'new_m_g_ple_pre': 'new_m', 'new_m_w_ple_gate': 'new_m', 'new_m_w_ple_proj': 'new_m', 'new_m_g_ple_post': 'new_m', 'new_v_g_mix_pre': 'new_v', 'new_v_w_in': 'new_v', 'new_v_b_forget': 'new_v', 'new_v_w_conf_dw': 'new_v', 'new_v_conf_ln_g': 'new_v', 'new_v_conf_ln_b': 'new_v', 'new_v_w_conf_pw': 'new_v', 'new_v_w_sc': 'new_v', 'new_v_w_pool': 'new_v', 'new_v_pool_scale': 'new_v', 'new_v_w_out': 'new_v', 'new_v_g_mix_post': 'new_v', 'new_v_g_mlp_pre': 'new_v', 'new_v_w_up': 'new_v', 'new_v_w_down': 'new_v', 'new_v_g_mlp_post': 'new_v', 'new_v_g_ple_pre': 'new_v', 'new_v_w_ple_gate': 'new_v', 'new_v_w_ple_proj': 'new_v', 'new_v_g_ple_post': 'new_v'}


def _forward(args):
    return _fwd_reference(*[args[k] for k in FWD_PARAMS])


def _output_shape():
    def fwd():
        inp = _fwd_setup_inputs(0)
        return _fwd_reference(*[inp[k] for k in FWD_PARAMS])
    out = _jax.eval_shape(fwd)
    return out.shape, out.dtype

N_MICROBATCH = 1
ADAM_LR = 0.001
ADAM_B1 = 0.9
ADAM_B2 = 0.999
ADAM_EPS = 1e-08
ADAM_WD = 0.01
ADAM_STEP = 10
PER_EXAMPLE_BATCH_AXIS = {'x': 0, 'p': 1, 'loss_target': 0}
SHARED_INPUTS = []
_WEIGHT_DTYPES = {'g_mix_pre': _jnp.float32, 'w_in': _jnp.float32, 'b_forget': _jnp.float32, 'w_conf_dw': _jnp.float32, 'conf_ln_g': _jnp.float32, 'conf_ln_b': _jnp.float32, 'w_conf_pw': _jnp.float32, 'w_sc': _jnp.float32, 'w_pool': _jnp.float32, 'pool_scale': _jnp.float32, 'w_out': _jnp.float32, 'g_mix_post': _jnp.float32, 'g_mlp_pre': _jnp.float32, 'w_up': _jnp.float32, 'w_down': _jnp.float32, 'g_mlp_post': _jnp.float32, 'g_ple_pre': _jnp.float32, 'w_ple_gate': _jnp.float32, 'w_ple_proj': _jnp.float32, 'g_ple_post': _jnp.float32}
MOMENT_SCALE = {'g_mix_pre': 7.826954e+00, 'w_in': 5.275574e+00, 'b_forget': 5.148309e+00, 'w_conf_dw': 8.513590e+00, 'conf_ln_g': 3.121298e+01, 'conf_ln_b': 4.466708e+01, 'w_conf_pw': 2.054254e+01, 'w_sc': 2.538778e+00, 'w_pool': 5.032833e+00, 'pool_scale': 5.495951e+00, 'w_out': 1.241838e+01, 'g_mix_post': 6.532620e+01, 'g_mlp_pre': 8.644203e+00, 'w_up': 4.224815e+00, 'w_down': 2.629633e+01, 'g_mlp_post': 7.247647e+01, 'g_ple_pre': 6.537849e-01, 'w_ple_gate': 6.383647e-01, 'w_ple_proj': 1.826034e+00, 'g_ple_post': 6.507336e+01}


def _to_microbatches(a, axis):
    t = _jnp.moveaxis(a, axis, 0)
    t = t.reshape((N_MICROBATCH, t.shape[0] // N_MICROBATCH) + t.shape[1:])
    return _jnp.moveaxis(t, 1, axis + 1)


def setup_inputs(seed: int = 0) -> dict:
    inp = _fwd_setup_inputs(seed)
    key = _jax.random.fold_in(_jax.random.key(seed), 7919)
    shape, _ = _output_shape()
    out = dict(inp)
    out["loss_target"] = _jax.random.normal(_jax.random.fold_in(key, 0), shape, _jnp.float32)
    for i, name in enumerate(TWIN_WEIGHTS):
        w = inp[name].astype(_jnp.float32)
        if MOMENT_SCALE is None:
            s = _jnp.sqrt(_jnp.mean(_jnp.square(w)) + 1e-30)
        else:
            s = MOMENT_SCALE[name]
        km, kv = _jax.random.split(_jax.random.fold_in(key, i + 1))
        out[name] = w
        out["m_" + name] = s * _jax.random.normal(km, w.shape, _jnp.float32)
        out["v_" + name] = (s * s) * _jax.random.uniform(kv, w.shape, _jnp.float32, 0.5, 1.5)
    if N_MICROBATCH > 1:
        for name, axis in PER_EXAMPLE_BATCH_AXIS.items():
            out[name] = _to_microbatches(out[name], axis)
    return {'x': out['x'], 'p': out['p'], 'g_mix_pre': out['g_mix_pre'], 'w_in': out['w_in'], 'b_forget': out['b_forget'], 'w_conf_dw': out['w_conf_dw'], 'conf_ln_g': out['conf_ln_g'], 'conf_ln_b': out['conf_ln_b'], 'w_conf_pw': out['w_conf_pw'], 'w_sc': out['w_sc'], 'w_pool': out['w_pool'], 'pool_scale': out['pool_scale'], 'w_out': out['w_out'], 'g_mix_post': out['g_mix_post'], 'g_mlp_pre': out['g_mlp_pre'], 'w_up': out['w_up'], 'w_down': out['w_down'], 'g_mlp_post': out['g_mlp_post'], 'g_ple_pre': out['g_ple_pre'], 'w_ple_gate': out['w_ple_gate'], 'w_ple_proj': out['w_ple_proj'], 'g_ple_post': out['g_ple_post'], 'loss_target': out['loss_target'], 'm_g_mix_pre': out['m_g_mix_pre'], 'm_w_in': out['m_w_in'], 'm_b_forget': out['m_b_forget'], 'm_w_conf_dw': out['m_w_conf_dw'], 'm_conf_ln_g': out['m_conf_ln_g'], 'm_conf_ln_b': out['m_conf_ln_b'], 'm_w_conf_pw': out['m_w_conf_pw'], 'm_w_sc': out['m_w_sc'], 'm_w_pool': out['m_w_pool'], 'm_pool_scale': out['m_pool_scale'], 'm_w_out': out['m_w_out'], 'm_g_mix_post': out['m_g_mix_post'], 'm_g_mlp_pre': out['m_g_mlp_pre'], 'm_w_up': out['m_w_up'], 'm_w_down': out['m_w_down'], 'm_g_mlp_post': out['m_g_mlp_post'], 'm_g_ple_pre': out['m_g_ple_pre'], 'm_w_ple_gate': out['m_w_ple_gate'], 'm_w_ple_proj': out['m_w_ple_proj'], 'm_g_ple_post': out['m_g_ple_post'], 'v_g_mix_pre': out['v_g_mix_pre'], 'v_w_in': out['v_w_in'], 'v_b_forget': out['v_b_forget'], 'v_w_conf_dw': out['v_w_conf_dw'], 'v_conf_ln_g': out['v_conf_ln_g'], 'v_conf_ln_b': out['v_conf_ln_b'], 'v_w_conf_pw': out['v_w_conf_pw'], 'v_w_sc': out['v_w_sc'], 'v_w_pool': out['v_w_pool'], 'v_pool_scale': out['v_pool_scale'], 'v_w_out': out['v_w_out'], 'v_g_mix_post': out['v_g_mix_post'], 'v_g_mlp_pre': out['v_g_mlp_pre'], 'v_w_up': out['v_w_up'], 'v_w_down': out['v_w_down'], 'v_g_mlp_post': out['v_g_mlp_post'], 'v_g_ple_pre': out['v_g_ple_pre'], 'v_w_ple_gate': out['v_w_ple_gate'], 'v_w_ple_proj': out['v_w_ple_proj'], 'v_g_ple_post': out['v_g_ple_post']}


def _loss(weights, diff, rest, loss_target):
    with _jax.named_scope("forward"):
        args = {**rest, TWIN_DIFF_INPUT: diff, **{k: w.astype(_WEIGHT_DTYPES[k]) for k, w in weights.items()}}
        y = _forward(args)
    with _jax.named_scope("loss_head"):
        err = _jnp.square(y.astype(_jnp.float32) - loss_target)
        return 0.5 * _jnp.sum(_jnp.mean(err, axis=-1)) if err.ndim else 0.5 * err


def _adamw(w, g, m, v):
    m = ADAM_B1 * m + (1.0 - ADAM_B1) * g
    v = ADAM_B2 * v + (1.0 - ADAM_B2) * _jnp.square(g)
    m_hat = m / (1.0 - ADAM_B1 ** ADAM_STEP)
    v_hat = v / (1.0 - ADAM_B2 ** ADAM_STEP)
    delta = -ADAM_LR * (m_hat / (_jnp.sqrt(v_hat) + ADAM_EPS) + ADAM_WD * w)
    return delta, m, v


def reference(x, p, g_mix_pre, w_in, b_forget, w_conf_dw, conf_ln_g, conf_ln_b, w_conf_pw, w_sc, w_pool, pool_scale, w_out, g_mix_post, g_mlp_pre, w_up, w_down, g_mlp_post, g_ple_pre, w_ple_gate, w_ple_proj, g_ple_post, loss_target, m_g_mix_pre, m_w_in, m_b_forget, m_w_conf_dw, m_conf_ln_g, m_conf_ln_b, m_w_conf_pw, m_w_sc, m_w_pool, m_pool_scale, m_w_out, m_g_mix_post, m_g_mlp_pre, m_w_up, m_w_down, m_g_mlp_post, m_g_ple_pre, m_w_ple_gate, m_w_ple_proj, m_g_ple_post, v_g_mix_pre, v_w_in, v_b_forget, v_w_conf_dw, v_conf_ln_g, v_conf_ln_b, v_w_conf_pw, v_w_sc, v_w_pool, v_pool_scale, v_w_out, v_g_mix_post, v_g_mlp_pre, v_w_up, v_w_down, v_g_mlp_post, v_g_ple_pre, v_w_ple_gate, v_w_ple_proj, v_g_ple_post):
    given = dict(x=x, p=p, g_mix_pre=g_mix_pre, w_in=w_in, b_forget=b_forget, w_conf_dw=w_conf_dw, conf_ln_g=conf_ln_g, conf_ln_b=conf_ln_b, w_conf_pw=w_conf_pw, w_sc=w_sc, w_pool=w_pool, pool_scale=pool_scale, w_out=w_out, g_mix_post=g_mix_post, g_mlp_pre=g_mlp_pre, w_up=w_up, w_down=w_down, g_mlp_post=g_mlp_post, g_ple_pre=g_ple_pre, w_ple_gate=w_ple_gate, w_ple_proj=w_ple_proj, g_ple_post=g_ple_post, loss_target=loss_target, m_g_mix_pre=m_g_mix_pre, m_w_in=m_w_in, m_b_forget=m_b_forget, m_w_conf_dw=m_w_conf_dw, m_conf_ln_g=m_conf_ln_g, m_conf_ln_b=m_conf_ln_b, m_w_conf_pw=m_w_conf_pw, m_w_sc=m_w_sc, m_w_pool=m_w_pool, m_pool_scale=m_pool_scale, m_w_out=m_w_out, m_g_mix_post=m_g_mix_post, m_g_mlp_pre=m_g_mlp_pre, m_w_up=m_w_up, m_w_down=m_w_down, m_g_mlp_post=m_g_mlp_post, m_g_ple_pre=m_g_ple_pre, m_w_ple_gate=m_w_ple_gate, m_w_ple_proj=m_w_ple_proj, m_g_ple_post=m_g_ple_post, v_g_mix_pre=v_g_mix_pre, v_w_in=v_w_in, v_b_forget=v_b_forget, v_w_conf_dw=v_w_conf_dw, v_conf_ln_g=v_conf_ln_g, v_conf_ln_b=v_conf_ln_b, v_w_conf_pw=v_w_conf_pw, v_w_sc=v_w_sc, v_w_pool=v_w_pool, v_pool_scale=v_pool_scale, v_w_out=v_w_out, v_g_mix_post=v_g_mix_post, v_g_mlp_pre=v_g_mlp_pre, v_w_up=v_w_up, v_w_down=v_w_down, v_g_mlp_post=v_g_mlp_post, v_g_ple_pre=v_g_ple_pre, v_w_ple_gate=v_w_ple_gate, v_w_ple_proj=v_w_ple_proj, v_g_ple_post=v_g_ple_post)
    weights = {n: given[n] for n in TWIN_WEIGHTS}
    shared = {n: given[n] for n in SHARED_INPUTS}
    per_example = {n: given[n] for n in ['x', 'p']}
    grad_fn = _jax.value_and_grad(_loss, argnums=(0, 1))

    def one_microbatch(ex, loss_target):
        ex = dict(ex)
        diff = ex.pop(TWIN_DIFF_INPUT)
        return grad_fn(weights, diff, {**shared, **ex}, loss_target)

    if N_MICROBATCH == 1:
        loss, (grad_w, grad_x) = one_microbatch(per_example, given["loss_target"])
    else:
        def body(carry, xs):
            loss_sum, grad_sum = carry
            l_k, (gw_k, gx_k) = one_microbatch(xs[0], xs[1])
            with _jax.named_scope("update"):
                return (loss_sum + l_k, _jax.tree.map(_jnp.add, grad_sum, gw_k)), gx_k

        init = (_jnp.zeros((), _jnp.float32), _jax.tree.map(_jnp.zeros_like, weights))
        (loss, grad_w), grad_x = _jax.lax.scan(body, init, (per_example, given["loss_target"]))
    with _jax.named_scope("update"):
        delta_w, new_m, new_v = {}, {}, {}
        for n in TWIN_WEIGHTS:
            delta_w[n], new_m[n], new_v[n] = _adamw(weights[n], grad_w[n], given["m_" + n], given["v_" + n])
    return (loss, grad_x, *[grad_w[n] for n in TWIN_WEIGHTS], *[delta_w[n] for n in TWIN_WEIGHTS],
            *[new_m[n] for n in TWIN_WEIGHTS], *[new_v[n] for n in TWIN_WEIGHTS])
```

```python
import functools

import jax
import jax.numpy as jnp
from jax import lax
from jax.experimental import pallas as pl
from jax.experimental.pallas import tpu as pltpu

F32 = jnp.float32
BF = jnp.bfloat16

DEPTH = 4
D_MODEL = 1024
D_GRP = 256
HEAD_DIM = 64
N_HEADS = 4
CONF_K = 31
SC_K = 3
D_FF = 4096
D_PLE = 256
EPS = 1e-6
N_CHIPS = 4
Z_CONF, Z_QKV, Z_SC, Z_POOL, Z_F = 0, 512, 1280, 2048, 2304
Z_W = 2432
F_PAD = 128
D_IN = 2308
F_OFF = 1280

ADAM_LR, ADAM_B1, ADAM_B2, ADAM_EPS, ADAM_WD, ADAM_STEP = 0.001, 0.9, 0.999, 1e-08, 0.01, 10

VMEM_LIMIT_BYTES = 56 * 1024 * 1024
HALO = 32
NEG = -1e30
MESH_T = pl.DeviceIdType.MESH


def _cparams(sem=None):
    return pltpu.CompilerParams(dimension_semantics=sem, vmem_limit_bytes=VMEM_LIMIT_BYTES)


def _dot(a, b):
    return jnp.dot(a, b, preferred_element_type=F32)


def _dot_nt(a, b):
    return lax.dot_general(a, b, (((1,), (1,)), ((), ())), preferred_element_type=F32)


def _dot_tn(a, b):
    return lax.dot_general(a, b, (((0,), (0,)), ((), ())), preferred_element_type=F32)


def _sig(x):
    return jax.nn.sigmoid(x)


def _rms_fwd(x, g):
    r = lax.rsqrt(jnp.mean(x * x, axis=-1, keepdims=True) + EPS)
    return x * r * g


def _rms_bwd(x, g, dy):
    r = lax.rsqrt(jnp.mean(x * x, axis=-1, keepdims=True) + EPS)
    xh = x * r
    dg = jnp.sum(dy * xh, axis=0, keepdims=True)
    dxh = dy * g
    dx = r * (dxh - xh * jnp.mean(dxh * xh, axis=-1, keepdims=True))
    return dx, dg


def _back(ext, d):
    return ext if d == 0 else pltpu.roll(ext, d, 0)


def _ahead(ext, d):
    return ext if d == 0 else pltpu.roll(ext, ext.shape[0] - d, 0)


def _rows_call(name, fn, t_len, tm, rows, fulls, out_rows, out_accs=(), prevs=(), nexts=()):
    n = t_len // tm
    hb = tm // HALO
    nhb = t_len // HALO
    n_rows, n_prev, n_next, n_full = len(rows), len(prevs), len(nexts), len(fulls)
    in_specs = [pl.BlockSpec((tm, a.shape[1]), lambda i: (i, 0)) for a in rows]
    in_specs += [pl.BlockSpec((HALO, a.shape[1]), lambda i: (jnp.maximum(i * hb - 1, 0), 0)) for a in prevs]
    in_specs += [pl.BlockSpec((HALO, a.shape[1]), lambda i: (jnp.minimum((i + 1) * hb, nhb - 1), 0)) for a in nexts]
    in_specs += [pl.BlockSpec(a.shape, lambda i, nd=a.ndim: (0,) * nd) for a in fulls]
    out_shape = [jax.ShapeDtypeStruct((t_len, c), dt) for c, dt in out_rows]
    out_shape += [jax.ShapeDtypeStruct(s, F32) for s in out_accs]
    out_specs = [pl.BlockSpec((tm, c), lambda i: (i, 0)) for c, _ in out_rows]
    out_specs += [pl.BlockSpec(s, lambda i, nd=len(s): (0,) * nd) for s in out_accs]
    n_in = n_rows + n_prev + n_next + n_full
    n_ro = len(out_rows)

    def body(*refs):
        i = pl.program_id(0)
        ins, outs = refs[:n_in], refs[n_in:]
        rv = [r[...] for r in ins[:n_rows]]
        pv = [r[...] for r in ins[n_rows:n_rows + n_prev]]
        nv = [r[...] for r in ins[n_rows + n_prev:n_rows + n_prev + n_next]]
        fv = list(ins[n_rows + n_prev + n_next:])
        ro, ao = fn(i, n, rv, pv, nv, fv)
        for r, v in zip(outs[:n_ro], ro):
            r[...] = v.astype(r.dtype)
        if out_accs:
            acc = outs[n_ro:]

            @pl.when(i == 0)
            def _():
                for r in acc:
                    r[...] = jnp.zeros(r.shape, r.dtype)

            for r, v in zip(acc, ao):
                r[...] += v

    res = pl.pallas_call(
        body, name=name, grid=(n,), in_specs=in_specs, out_specs=out_specs, out_shape=out_shape,
        compiler_params=_cparams(("arbitrary",)),
    )(*rows, *prevs, *nexts, *fulls)
    return res


def _mm_tn(name, x, y, tk, tn, tt, pro=None):
    t_len, k_dim = x.shape
    n_dim = y.shape[1]

    def body(x_ref, y_ref, o_ref):
        @pl.when(pl.program_id(2) == 0)
        def _():
            o_ref[...] = jnp.zeros(o_ref.shape, o_ref.dtype)

        xv = x_ref[...]
        if pro is not None:
            xv = pro(xv)
        o_ref[...] += _dot_tn(xv.astype(BF), y_ref[...].astype(BF))

    return pl.pallas_call(
        body, name=name, grid=(k_dim // tk, n_dim // tn, t_len // tt),
        in_specs=[pl.BlockSpec((tt, tk), lambda a, b, t: (t, a)), pl.BlockSpec((tt, tn), lambda a, b, t: (t, b))],
        out_specs=pl.BlockSpec((tk, tn), lambda a, b, t: (a, b)),
        out_shape=jax.ShapeDtypeStruct((k_dim, n_dim), F32),
        compiler_params=_cparams(("parallel", "parallel", "arbitrary")),
    )(x, y)


def _mix_in_fwd(t_len, tm, h, g_pre, w_a):
    def fn(i, n, rows, prevs, nexts, fulls):
        (hv,), (g, w) = rows, fulls
        z = _dot(_rms_fwd(hv, g[...]).astype(BF), w[...])
        return [z[:, Z_CONF:Z_QKV], z[:, Z_QKV:Z_SC], z[:, Z_SC:Z_POOL], z[:, Z_POOL:Z_F], z[:, Z_F:Z_W]], []

    return _rows_call("mix_in_fwd", fn, t_len, tm, [h], [g_pre, w_a],
                      [(512, F32), (768, BF), (768, F32), (256, F32), (F_PAD, F32)])


def _mix_in_bwd(t_len, tm, dh, h, dzc, dqkv, dzs, dzp, dzf, g_pre, w_a):
    def fn(i, n, rows, prevs, nexts, fulls):
        dhv, hv, a, b, c, d, e = rows
        g, w = fulls
        dz = jnp.concatenate([a, b, c, d, e], axis=1)
        dxn = _dot_nt(dz, w[...])
        dx, dg = _rms_bwd(hv, g[...], dxn)
        xn = _rms_fwd(hv, g[...])
        return [dhv + dx, xn, dz], [dg]

    return _rows_call("mix_in_bwd", fn, t_len, tm, [dh, h, dzc, dqkv, dzs, dzp, dzf], [g_pre, w_a],
                      [(D_MODEL, F32), (D_MODEL, BF), (Z_W, BF)], [(1, D_MODEL)])


def _glu_ext(i, zc, zc_prev):
    ext = jnp.concatenate([zc_prev, zc], axis=0)
    u = ext[:, :D_GRP] * _sig(ext[:, D_GRP:])
    row = lax.broadcasted_iota(jnp.int32, u.shape, 0)
    return jnp.where((row >= HALO) | (i > 0), u, 0.0)


def _conf_fwd(t_len, tm, zc, w_dw, ln_g, ln_b, w_pw):
    def fn(i, n, rows, prevs, nexts, fulls):
        (zv,), (zp,) = rows, prevs
        wdw, lg, lb, wpw = fulls
        u = _glu_ext(i, zv, zp)
        cv = jnp.zeros((tm, D_GRP), F32)
        for k in range(CONF_K):
            cv = cv + wdw[k:k + 1, :] * _back(u, CONF_K - 1 - k)[HALO:, :]
        mu = jnp.mean(cv, axis=-1, keepdims=True)
        xc = cv - mu
        ln = xc * lax.rsqrt(jnp.mean(xc * xc, axis=-1, keepdims=True) + EPS) * lg[...] + lb[...]
        s = ln * _sig(ln)
        return [cv, _dot(s.astype(BF), wpw[...])], []

    return _rows_call("conf_fwd", fn, t_len, tm, [zc], [w_dw, ln_g, ln_b, w_pw], [(D_GRP, F32), (D_GRP, BF)], prevs=[zc])


def _conf_bwd(t_len, tm, zc, cv, dy, w_dw, ln_g, ln_b, w_pw):
    def fn(i, n, rows, prevs, nexts, fulls):
        zv, cvv, dyv = rows
        (zp,) = prevs
        cvn, dyn = nexts
        wdw, lg, lb, wpw = fulls
        cve = jnp.concatenate([cvv, cvn], axis=0)
        dye = jnp.concatenate([dyv, dyn], axis=0)
        mu = jnp.mean(cve, axis=-1, keepdims=True)
        xc = cve - mu
        rs = lax.rsqrt(jnp.mean(xc * xc, axis=-1, keepdims=True) + EPS)
        xh = xc * rs
        ln = xh * lg[...] + lb[...]
        sg = _sig(ln)
        s = ln * sg
        ds = _dot_nt(dye.astype(BF), wpw[...])
        dln = ds * (sg * (1.0 + ln * (1.0 - sg)))
        dxh = dln * lg[...]
        dcv = rs * (dxh - jnp.mean(dxh, axis=-1, keepdims=True) - xh * jnp.mean(dxh * xh, axis=-1, keepdims=True))
        row = lax.broadcasted_iota(jnp.int32, dcv.shape, 0)
        dcv = jnp.where((row < tm) | (i < n - 1), dcv, 0.0)
        d_lg = jnp.sum((dln * xh)[:tm], axis=0, keepdims=True)
        d_lb = jnp.sum(dln[:tm], axis=0, keepdims=True)
        d_wpw = _dot_tn(s[:tm].astype(BF), dyv.astype(BF))
        u = _glu_ext(i, zv, zp)
        dcv_cur = dcv[:tm]
        du = jnp.zeros((tm, D_GRP), F32)
        d_wdw = jnp.zeros((32, D_GRP), F32)
        krow = lax.broadcasted_iota(jnp.int32, (32, D_GRP), 0)
        for k in range(CONF_K):
            d = CONF_K - 1 - k
            du = du + wdw[k:k + 1, :] * _ahead(dcv, d)[:tm, :]
            tap = _back(u, d)[HALO:, :]
            d_wdw = d_wdw + jnp.where(krow == k, jnp.sum(dcv_cur * tap, axis=0, keepdims=True), 0.0)
        a, b = zv[:, :D_GRP], zv[:, D_GRP:]
        sb = _sig(b)
        dz = jnp.concatenate([du * sb, du * a * sb * (1.0 - sb)], axis=1)
        return [dz], [d_wdw, d_lg, d_lb, d_wpw]

    return _rows_call("conf_bwd", fn, t_len, tm, [zc, cv, dy], [w_dw, ln_g, ln_b, w_pw], [(512, BF)],
                      [(32, D_GRP), (1, D_GRP), (1, D_GRP), (D_GRP, D_GRP)], prevs=[zc], nexts=[cv, dy])


def _sc_ext(i, zs, zs_prev):
    ext = jnp.concatenate([zs_prev, zs], axis=0)
    e = ext[:, 2 * D_GRP:] * ext[:, :D_GRP]
    row = lax.broadcasted_iota(jnp.int32, e.shape, 0)
    return jnp.where((row >= HALO) | (i > 0), e, 0.0)


def _sconv_fwd(t_len, tm, zs, w_sc):
    def fn(i, n, rows, prevs, nexts, fulls):
        (zv,), (zp,), (w,) = rows, prevs, fulls
        e = _sc_ext(i, zv, zp)
        cv = jnp.zeros((tm, D_GRP), F32)
        for k in range(SC_K):
            cv = cv + w[k:k + 1, :] * _back(e, SC_K - 1 - k)[HALO:, :]
        return [zv[:, D_GRP:2 * D_GRP] * cv], []

    return _rows_call("sconv_fwd", fn, t_len, tm, [zs], [w_sc], [(D_GRP, BF)], prevs=[zs])


def _sconv_bwd(t_len, tm, zs, dy, w_sc):
    def fn(i, n, rows, prevs, nexts, fulls):
        zv, dyv = rows
        (zp,) = prevs
        zn, dyn = nexts
        (w,) = fulls
        e = _sc_ext(i, zv, zp)
        taps = [_back(e, SC_K - 1 - k)[HALO:, :] for k in range(SC_K)]
        cv = w[0:1, :] * taps[0] + w[1:2, :] * taps[1] + w[2:3, :] * taps[2]
        bg = zv[:, D_GRP:2 * D_GRP]
        dcv = jnp.concatenate([dyv * bg, dyn * zn[:, D_GRP:2 * D_GRP]], axis=0)
        row = lax.broadcasted_iota(jnp.int32, dcv.shape, 0)
        dcv = jnp.where((row < tm) | (i < n - 1), dcv, 0.0)
        de = jnp.zeros((tm, D_GRP), F32)
        d_w = jnp.zeros((8, D_GRP), F32)
        krow = lax.broadcasted_iota(jnp.int32, (8, D_GRP), 0)
        for k in range(SC_K):
            de = de + w[k:k + 1, :] * _ahead(dcv, SC_K - 1 - k)[:tm, :]
            d_w = d_w + jnp.where(krow == k, jnp.sum(dcv[:tm] * taps[k], axis=0, keepdims=True), 0.0)
        dz = jnp.concatenate([de * zv[:, 2 * D_GRP:], dyv * cv, de * zv[:, :D_GRP]], axis=1)
        return [dz], [d_w]

    return _rows_call("sconv_bwd", fn, t_len, tm, [zs, dy], [w_sc], [(768, BF)], [(8, D_GRP)], prevs=[zs], nexts=[zs, dy])


def _pool_window(shape):
    grp = lax.broadcasted_iota(jnp.int32, shape, 1) // 64
    return grp, jnp.where(grp == 0, 2.0, jnp.where(grp == 1, 4.0, jnp.where(grp == 2, 8.0, 16.0)))


def _pool_d(i, tm, zv, zp):
    ext = jnp.concatenate([zp, zv], axis=0)
    row = lax.broadcasted_iota(jnp.int32, ext.shape, 0)
    ext = jnp.where((row >= HALO) | (i > 0), ext, 0.0)
    s2 = ext + _back(ext, 1)
    s4 = s2 + _back(s2, 2)
    s8 = s4 + _back(s4, 4)
    s16 = s8 + _back(s8, 8)
    grp, win = _pool_window((tm, D_GRP))
    sel = jnp.where(grp == 0, s2[HALO:], jnp.where(grp == 1, s4[HALO:], jnp.where(grp == 2, s8[HALO:], s16[HALO:])))
    pos = (i * tm + lax.broadcasted_iota(jnp.int32, (tm, D_GRP), 0) + 1).astype(F32)
    return sel / jnp.minimum(pos, win) - zv


def _pool_fwd(t_len, tm, zpool, w_bd, scale):
    def fn(i, n, rows, prevs, nexts, fulls):
        (zv,), (zp,) = rows, prevs
        w, sc = fulls
        d = _pool_d(i, tm, zv, zp)
        return [_dot(d.astype(BF), w[...]) * sc[...]], []

    return _rows_call("pool_fwd", fn, t_len, tm, [zpool], [w_bd, scale], [(D_GRP, BF)], prevs=[zpool])


def _pool_bwd(t_len, tm, zpool, dy, w_bd, scale):
    def fn(i, n, rows, prevs, nexts, fulls):
        zv, dyv = rows
        (zp,) = prevs
        (dyn,) = nexts
        w, sc = fulls
        d = _pool_d(i, tm, zv, zp)
        lin = _dot(d.astype(BF), w[...])
        d_sc = jnp.sum(dyv * lin, axis=0, keepdims=True)
        dye = jnp.concatenate([dyv, dyn], axis=0) * sc[...]
        d_w = _dot_tn(d.astype(BF), dye[:tm].astype(BF))
        dd = _dot_nt(dye.astype(BF), w[...])
        row = lax.broadcasted_iota(jnp.int32, dd.shape, 0)
        dd = jnp.where((row < tm) | (i < n - 1), dd, 0.0)
        grp, win = _pool_window(dd.shape)
        pos = (i * tm + row + 1).astype(F32)
        ddc = dd / jnp.minimum(pos, win)
        f2 = ddc + _ahead(ddc, 1)
        f4 = f2 + _ahead(f2, 2)
        f8 = f4 + _ahead(f4, 4)
        f16 = f8 + _ahead(f8, 8)
        sel = jnp.where(grp == 0, f2, jnp.where(grp == 1, f4, jnp.where(grp == 2, f8, f16)))
        return [(sel - dd)[:tm]], [d_w, d_sc]

    return _rows_call("pool_bwd", fn, t_len, tm, [zpool, dy], [w_bd, scale], [(D_GRP, BF)],
                      [(D_GRP, D_GRP), (1, D_GRP)], prevs=[zpool], nexts=[dy])


def _mix_out_fwd(t_len, tm, ys, h, w_out, g_post):
    def fn(i, n, rows, prevs, nexts, fulls):
        y0, y1, y2, y3, hv = rows
        w, g = fulls
        mix = _dot(jnp.concatenate([y0, y1, y2, y3], axis=1), w[...])
        return [mix, hv + _rms_fwd(mix, g[...])], []

    return _rows_call("mix_out_fwd", fn, t_len, tm, [*ys, h], [w_out, g_post], [(D_MODEL, F32), (D_MODEL, F32)])


def _mix_out_bwd(t_len, tm, dh, mix, ys, w_out, g_post):
    def fn(i, n, rows, prevs, nexts, fulls):
        dhv, mv, y0, y1, y2, y3 = rows
        w, g = fulls
        dmix, dg = _rms_bwd(mv, g[...], dhv)
        dmb = dmix.astype(BF)
        dcat = _dot_nt(dmb, w[...])
        d_w = _dot_tn(jnp.concatenate([y0, y1, y2, y3], axis=1), dmb)
        return [dcat[:, :256], dcat[:, 256:512], dcat[:, 512:768], dcat[:, 768:]], [d_w, dg]

    return _rows_call("mix_out_bwd", fn, t_len, tm, [dh, mix, *ys], [w_out, g_post],
                      [(D_GRP, F32)] * 4, [(D_MODEL, D_MODEL), (1, D_MODEL)])


def _log_sigmoid(x):
    return jnp.minimum(x, 0.0) - jnp.log(1.0 + jnp.exp(-jnp.abs(x)))


def _fox_prep(t_len, zf, b_f):
    blk = min(256, t_len)

    def body(zf_ref, b_ref, c_ref):
        tri = (lax.broadcasted_iota(jnp.int32, (blk, blk), 0) >= lax.broadcasted_iota(jnp.int32, (blk, blk), 1)).astype(F32)

        def step(bi, carry):
            rows = pl.ds(pl.multiple_of(bi * blk, blk), blk)
            lf = _log_sigmoid(zf_ref[rows, :] + b_ref[...])
            cs = jnp.dot(tri, lf, precision=lax.Precision.HIGHEST, preferred_element_type=F32) + carry
            c_ref[rows, :] = cs
            return cs[blk - 1:blk, :]

        lax.fori_loop(0, t_len // blk, step, jnp.zeros((1, F_PAD), F32))

    return pl.pallas_call(body, name="fox_prep", out_shape=jax.ShapeDtypeStruct((t_len, F_PAD), F32),
                          compiler_params=_cparams())(zf, b_f)


def _fox_post(t_len, dc, zf, b_f):
    blk = min(256, t_len)
    nb = t_len // blk

    def body(dc_ref, zf_ref, b_ref, dz_ref, db_ref):
        tri = (lax.broadcasted_iota(jnp.int32, (blk, blk), 0) <= lax.broadcasted_iota(jnp.int32, (blk, blk), 1)).astype(F32)

        def step(s, carry):
            car, db = carry
            rows = pl.ds(pl.multiple_of((nb - 1 - s) * blk, blk), blk)
            dlf = jnp.dot(tri, dc_ref[rows, :], precision=lax.Precision.HIGHEST, preferred_element_type=F32) + car
            dz = dlf * _sig(-(zf_ref[rows, :] + b_ref[...]))
            dz_ref[rows, :] = dz.astype(dz_ref.dtype)
            return dlf[0:1, :], db + jnp.sum(dz, axis=0, keepdims=True)

        _, db = lax.fori_loop(0, nb, step, (jnp.zeros((1, F_PAD), F32), jnp.zeros((1, F_PAD), F32)))
        db_ref[...] = db

    return pl.pallas_call(body, name="fox_post",
                          out_shape=(jax.ShapeDtypeStruct((t_len, F_PAD), BF), jax.ShapeDtypeStruct((1, F_PAD), F32)),
                          compiler_params=_cparams())(dc, zf, b_f)


def _causal(tq, tk, i, j, transposed=False):
    if transposed:
        kpos = j * tk + lax.broadcasted_iota(jnp.int32, (tk, tq), 0)
        qpos = i * tq + lax.broadcasted_iota(jnp.int32, (tk, tq), 1)
    else:
        qpos = i * tq + lax.broadcasted_iota(jnp.int32, (tq, tk), 0)
        kpos = j * tk + lax.broadcasted_iota(jnp.int32, (tq, tk), 1)
    return qpos >= kpos


def _fox_fwd(t_len, tq, q, k, v, ccol, crow):
    nq = t_len // tq
    scale = HEAD_DIM ** -0.5

    def body(q_ref, k_ref, v_ref, cc_ref, cr_ref, o_ref, lse_ref, m_s, l_s, acc_s):
        i, j = pl.program_id(1), pl.program_id(2)

        @pl.when(j == 0)
        def _():
            m_s[...] = jnp.full(m_s.shape, NEG, F32)
            l_s[...] = jnp.zeros(l_s.shape, F32)
            acc_s[...] = jnp.zeros(acc_s.shape, F32)

        @pl.when(j <= i)
        def _():
            s = _dot_nt(q_ref[0], k_ref[0]) * scale + cc_ref[0] - cr_ref[0]
            s = jnp.where(_causal(tq, tq, i, j), s, NEG)
            m_new = jnp.maximum(m_s[...], jnp.max(s, axis=-1, keepdims=True))
            alpha = jnp.exp(m_s[...] - m_new)
            p = jnp.exp(s - m_new)
            l_s[...] = alpha * l_s[...] + jnp.sum(p, axis=-1, keepdims=True)
            acc_s[...] = alpha * acc_s[...] + _dot(p.astype(BF), v_ref[0])
            m_s[...] = m_new

        @pl.when(j == i)
        def _():
            o_ref[0] = acc_s[...] / l_s[...]
            lse_ref[0] = m_s[...] + jnp.log(l_s[...])

    qspec = pl.BlockSpec((1, tq, HEAD_DIM), lambda h, i, j: (h, i, 0))
    kspec = pl.BlockSpec((1, tq, HEAD_DIM), lambda h, i, j: (h, jnp.minimum(j, i), 0))
    return pl.pallas_call(
        body, name="fox_fwd", grid=(N_HEADS, nq, nq),
        in_specs=[qspec, kspec, kspec, pl.BlockSpec((1, tq, 1), lambda h, i, j: (h, i, 0)),
                  pl.BlockSpec((1, 1, tq), lambda h, i, j: (h, 0, jnp.minimum(j, i)))],
        out_specs=[qspec, pl.BlockSpec((1, tq, 1), lambda h, i, j: (h, i, 0))],
        out_shape=[jax.ShapeDtypeStruct((N_HEADS, t_len, HEAD_DIM), F32), jax.ShapeDtypeStruct((N_HEADS, t_len, 1), F32)],
        scratch_shapes=[pltpu.VMEM((tq, 1), F32), pltpu.VMEM((tq, 1), F32), pltpu.VMEM((tq, HEAD_DIM), F32)],
        compiler_params=_cparams(("parallel", "parallel", "arbitrary")),
    )(q, k, v, ccol, crow)


def _fox_dq(t_len, tq, q, k, v, do, o, ccol, crow, lse):
    nq = t_len // tq
    scale = HEAD_DIM ** -0.5

    def body(q_ref, k_ref, v_ref, do_ref, o_ref, cc_ref, cr_ref, lse_ref, dq_ref, dl_ref, dc_ref, acc_s, dc_s):
        i, j = pl.program_id(1), pl.program_id(2)

        @pl.when(j == 0)
        def _():
            acc_s[...] = jnp.zeros(acc_s.shape, F32)
            dc_s[...] = jnp.zeros(dc_s.shape, F32)
            dl_ref[0] = jnp.sum(do_ref[0] * o_ref[0], axis=-1, keepdims=True)

        @pl.when(j <= i)
        def _():
            s = _dot_nt(q_ref[0], k_ref[0]) * scale + cc_ref[0] - cr_ref[0]
            p = jnp.where(_causal(tq, tq, i, j), jnp.exp(s - lse_ref[0]), 0.0)
            dp = _dot_nt(do_ref[0].astype(BF), v_ref[0])
            ds = p * (dp - dl_ref[0])
            acc_s[...] += _dot(ds.astype(BF), k_ref[0])
            dc_s[...] += jnp.sum(ds, axis=-1, keepdims=True)

        @pl.when(j == i)
        def _():
            dq_ref[0] = acc_s[...] * scale
            dc_ref[0] = dc_s[...]

    qspec = pl.BlockSpec((1, tq, HEAD_DIM), lambda h, i, j: (h, i, 0))
    kspec = pl.BlockSpec((1, tq, HEAD_DIM), lambda h, i, j: (h, jnp.minimum(j, i), 0))
    cspec = pl.BlockSpec((1, tq, 1), lambda h, i, j: (h, i, 0))
    return pl.pallas_call(
        body, name="fox_dq", grid=(N_HEADS, nq, nq),
        in_specs=[qspec, kspec, kspec, qspec, qspec, cspec,
                  pl.BlockSpec((1, 1, tq), lambda h, i, j: (h, 0, jnp.minimum(j, i))), cspec],
        out_specs=[qspec, cspec, cspec],
        out_shape=[jax.ShapeDtypeStruct((N_HEADS, t_len, HEAD_DIM), F32), jax.ShapeDtypeStruct((N_HEADS, t_len, 1), F32),
                   jax.ShapeDtypeStruct((N_HEADS, t_len, 1), F32)],
        scratch_shapes=[pltpu.VMEM((tq, HEAD_DIM), F32), pltpu.VMEM((tq, 1), F32)],
        compiler_params=_cparams(("parallel", "parallel", "arbitrary")),
    )(q, k, v, do, o, ccol, crow, lse)


def _fox_dkv(t_len, tq, q, k, v, do, ccol, crow, lse_row, dl_row):
    nq = t_len // tq
    scale = HEAD_DIM ** -0.5

    def body(q_ref, k_ref, v_ref, do_ref, cc_ref, cr_ref, lse_ref, dl_ref, dk_ref, dv_ref, dc_ref):
        j, i = pl.program_id(1), pl.program_id(2)

        @pl.when(i == 0)
        def _():
            dk_ref[...] = jnp.zeros(dk_ref.shape, F32)
            dv_ref[...] = jnp.zeros(dv_ref.shape, F32)
            dc_ref[...] = jnp.zeros(dc_ref.shape, F32)

        @pl.when(i >= j)
        def _():
            st = _dot_nt(k_ref[0], q_ref[0]) * scale + cr_ref[0] - cc_ref[0]
            pt = jnp.where(_causal(tq, tq, i, j, transposed=True), jnp.exp(st - lse_ref[0]), 0.0)
            dob = do_ref[0].astype(BF)
            dv_ref[0] += _dot(pt.astype(BF), dob)
            dpt = _dot_nt(v_ref[0], dob)
            dst = pt * (dpt - dl_ref[0])
            dk_ref[0] += _dot(dst.astype(BF), q_ref[0]) * scale
            dc_ref[0] -= jnp.sum(dst, axis=-1, keepdims=True)

    kspec = pl.BlockSpec((1, tq, HEAD_DIM), lambda h, j, i: (h, j, 0))
    qspec = pl.BlockSpec((1, tq, HEAD_DIM), lambda h, j, i: (h, jnp.maximum(i, j), 0))
    rspec = pl.BlockSpec((1, 1, tq), lambda h, j, i: (h, 0, jnp.maximum(i, j)))
    cspec = pl.BlockSpec((1, tq, 1), lambda h, j, i: (h, j, 0))
    return pl.pallas_call(
        body, name="fox_dkv", grid=(N_HEADS, nq, nq),
        in_specs=[qspec, kspec, kspec, qspec, cspec, rspec, rspec, rspec],
        out_specs=[kspec, kspec, cspec],
        out_shape=[jax.ShapeDtypeStruct((N_HEADS, t_len, HEAD_DIM), F32), jax.ShapeDtypeStruct((N_HEADS, t_len, HEAD_DIM), F32),
                   jax.ShapeDtypeStruct((N_HEADS, t_len, 1), F32)],
        compiler_params=_cparams(("parallel", "parallel", "arbitrary")),
    )(q, k, v, do, ccol, crow, lse_row, dl_row)


def _to_heads(a):
    t_len = a.shape[0]
    return a.reshape(t_len, N_HEADS, HEAD_DIM).transpose(1, 0, 2)


def _from_heads(a):
    t_len = a.shape[1]
    return a.transpose(1, 0, 2).reshape(t_len, N_HEADS * HEAD_DIM)


def _mlp_up_fwd(t_len, tm, h, g_pre, w_up):
    def fn(i, n, rows, prevs, nexts, fulls):
        (hv,), (g, w) = rows, fulls
        return [_dot(_rms_fwd(hv, g[...]).astype(BF), w[...])], []

    return _rows_call("mlp_up_fwd", fn, t_len, tm, [h], [g_pre, w_up], [(D_FF, F32)])[0]


def _mlp_down_fwd(t_len, tm, up, h, w_down, g_post):
    def fn(i, n, rows, prevs, nexts, fulls):
        (uv, hv), (w, g) = rows, fulls
        a = jnp.square(jnp.maximum(uv, 0.0))
        ff = _dot(a.astype(BF), w[...])
        return [ff, hv + _rms_fwd(ff, g[...])], []

    return _rows_call("mlp_down_fwd", fn, t_len, tm, [up, h], [w_down, g_post], [(D_MODEL, F32), (D_MODEL, F32)])


def _mlp_bwd_a(t_len, tm, dh, ff, up, w_down, g_post):
    def fn(i, n, rows, prevs, nexts, fulls):
        (dhv, fv, uv), (w, g) = rows, fulls
        dff, dg = _rms_bwd(fv, g[...], dhv)
        dfb = dff.astype(BF)
        dup = _dot_nt(dfb, w[...]) * (2.0 * jnp.maximum(uv, 0.0))
        return [dfb, dup], [dg]

    return _rows_call("mlp_bwd_a", fn, t_len, tm, [dh, ff, up], [w_down, g_post], [(D_MODEL, BF), (D_FF, BF)], [(1, D_MODEL)])


def _mlp_bwd_b(t_len, tm, dh, h, dup, w_up, g_pre):
    def fn(i, n, rows, prevs, nexts, fulls):
        (dhv, hv, duv), (w, g) = rows, fulls
        dhn = _dot_nt(duv, w[...])
        dx, dg = _rms_bwd(hv, g[...], dhn)
        return [dhv + dx, _rms_fwd(hv, g[...])], [dg]

    return _rows_call("mlp_bwd_b", fn, t_len, tm, [dh, h, dup], [w_up, g_pre], [(D_MODEL, F32), (D_MODEL, BF)], [(1, D_MODEL)])


def _ple_fwd(t_len, tm, h, p_i, g_pre, w_gate, w_proj, g_post):
    def fn(i, n, rows, prevs, nexts, fulls):
        (hv, pv), (g, wg, wp, gp) = rows, fulls
        gpre = _dot(_rms_fwd(hv, g[...]).astype(BF), wg[...])
        pe = _dot(pv.astype(BF), wp[...])
        return [gpre, pe, hv + _rms_fwd(pe * _sig(gpre), gp[...])], []

    return _rows_call("ple_fwd", fn, t_len, tm, [h, p_i], [g_pre, w_gate, w_proj, g_post], [(D_MODEL, F32)] * 3)


def _ple_bwd(t_len, tm, dh, h, gpre, pe, p_i, g_pre, w_gate, g_post):
    def fn(i, n, rows, prevs, nexts, fulls):
        (dhv, hv, gv, pev, pv), (g, wg, gp) = rows, fulls
        sg = _sig(gv)
        de, d_gp = _rms_bwd(pev * sg, gp[...], dhv)
        dpe = (de * sg).astype(BF)
        dgate = (de * pev * sg * (1.0 - sg)).astype(BF)
        d_wp = _dot_tn(pv.astype(BF), dpe)
        hn = _rms_fwd(hv, g[...])
        d_wg = _dot_tn(hn.astype(BF), dgate)
        dx, d_g = _rms_bwd(hv, g[...], _dot_nt(dgate, wg[...]))
        return [dhv + dx], [d_wg, d_wp, d_g, d_gp]

    return _rows_call("ple_bwd", fn, t_len, tm, [dh, h, gpre, pe, p_i], [g_pre, w_gate, g_post], [(D_MODEL, F32)],
                      [(D_MODEL, D_MODEL), (D_PLE, D_MODEL), (1, D_MODEL), (1, D_MODEL)])


def _loss_call(t_len, tm, h, target):
    def fn(i, n, rows, prevs, nexts, fulls):
        hv, tv = rows
        err = hv - tv
        part = 0.5 * jnp.sum(jnp.mean(err * err, axis=-1, keepdims=True), axis=0, keepdims=True)
        return [err * (1.0 / D_MODEL)], [jnp.broadcast_to(part, (8, 128))]

    return _rows_call("loss", fn, t_len, tm, [h, target], [], [(D_MODEL, F32)], [(8, 128)])


def _adamw_call(name, w, g, m, v):
    n_l, n_r, n_c = w.shape
    tr = 256 if n_r % 256 == 0 else n_r

    def body(w_ref, g_ref, m_ref, v_ref, d_ref, nm_ref, nv_ref):
        gv = g_ref[...]
        nm = ADAM_B1 * m_ref[...] + (1.0 - ADAM_B1) * gv
        nv = ADAM_B2 * v_ref[...] + (1.0 - ADAM_B2) * jnp.square(gv)
        m_hat = nm / (1.0 - ADAM_B1 ** ADAM_STEP)
        v_hat = nv / (1.0 - ADAM_B2 ** ADAM_STEP)
        d_ref[...] = -ADAM_LR * (m_hat / (jnp.sqrt(v_hat) + ADAM_EPS) + ADAM_WD * w_ref[...])
        nm_ref[...] = nm
        nv_ref[...] = nv

    spec = pl.BlockSpec((1, tr, n_c), lambda l, r: (l, r, 0))
    return pl.pallas_call(
        body, name=name, grid=(n_l, n_r // tr), in_specs=[spec] * 4, out_specs=[spec] * 3,
        out_shape=[jax.ShapeDtypeStruct(w.shape, F32)] * 3,
        compiler_params=_cparams(("parallel", "parallel")),
    )(w, g, m, v)


ANY = pl.BlockSpec(memory_space=pl.ANY)


def _place():
    x, y, c = lax.axis_index("x"), lax.axis_index("y"), lax.axis_index("c")
    chips = [(1 - x, y), (x, 1 - y), (1 - x, 1 - y)]
    return x, y, c, 2 * x + y, chips


def _remote(src, dst, send_sem, recv_sem, dev):
    return pltpu.make_async_remote_copy(src_ref=src, dst_ref=dst, send_sem=send_sem, recv_sem=recv_sem,
                                        device_id=dev, device_id_type=MESH_T)


def _allgather_weights(shards):
    n = len(shards)

    def body(*refs):
        ins, outs = refs[:n], refs[n:2 * n]
        send_sems, recv_sems, loc_sems = refs[2 * n:]
        x, y, c, q, chips = _place()
        me, sib = (x, y, c), (x, y, 1 - c)
        half, ohalf = pl.ds(2 * c, 2), pl.ds(2 * (1 - c), 2)
        locs = [pltpu.make_async_copy(ins[a], outs[a].at[q], loc_sems.at[a]) for a in range(n)]
        for cp in locs:
            cp.start()
        sends = []
        for a in range(n):
            for j, chip in enumerate(chips):
                sends.append(_remote(ins[a].at[half], outs[a].at[q, half], send_sems.at[6 * a + j], recv_sems.at[6 * a + j], (*chip, c)))
                sends[-1].start()
        for a in range(n):
            for j, (cx, cy) in enumerate(chips):
                land = outs[a].at[2 * cx + cy, half]
                _remote(land, land, send_sems.at[6 * a + j], recv_sems.at[6 * a + j], me).wait_recv()
                sends.append(_remote(land, land, send_sems.at[6 * a + 3 + j], recv_sems.at[6 * a + 3 + j], sib))
                sends[-1].start()
        for a in range(n):
            for j, (cx, cy) in enumerate(chips):
                land = outs[a].at[2 * cx + cy, ohalf]
                _remote(land, land, send_sems.at[6 * a + 3 + j], recv_sems.at[6 * a + 3 + j], me).wait_recv()
        for cp in sends:
            cp.wait_send()
        for cp in locs:
            cp.wait()

    return pl.pallas_call(
        body, name="allgather_weights", in_specs=[ANY] * n, out_specs=[ANY] * n,
        out_shape=[jax.ShapeDtypeStruct((N_CHIPS, *s.shape), s.dtype) for s in shards],
        scratch_shapes=[pltpu.SemaphoreType.DMA((6 * n,)), pltpu.SemaphoreType.DMA((6 * n,)), pltpu.SemaphoreType.DMA((n,))],
    )(*shards)


def _pair_exchange(grads):
    n = len(grads)

    def body(*refs):
        ins, outs = refs[:n], refs[n:2 * n]
        send_sems, recv_sems = refs[2 * n:]
        x, y, c, q, chips = _place()
        cps = [_remote(ins[a].at[pl.ds(0, N_CHIPS), pl.ds(2 * (1 - c), 2)], outs[a], send_sems.at[a], recv_sems.at[a], (x, y, 1 - c))
               for a in range(n)]
        for cp in cps:
            cp.start()
        for cp in cps:
            cp.wait()

    return pl.pallas_call(
        body, name="grad_pair_exchange", in_specs=[ANY] * n, out_specs=[ANY] * n,
        out_shape=[jax.ShapeDtypeStruct((N_CHIPS, 2, *g.shape[2:]), g.dtype) for g in grads],
        scratch_shapes=[pltpu.SemaphoreType.DMA((n,)), pltpu.SemaphoreType.DMA((n,))],
    )(*grads)


def _pair_sum(name, g, peer, c_arr):
    _, _, n_r, n_c = g.shape
    tr = 256 if n_r % 256 == 0 else n_r

    def body(c_ref, g_ref, p_ref, o_ref):
        o_ref[...] = (g_ref[...] + p_ref[...]).astype(o_ref.dtype)

    blk = (1, 1, tr, n_c)
    return pl.pallas_call(
        body, name=name,
        grid_spec=pltpu.PrefetchScalarGridSpec(
            num_scalar_prefetch=1, grid=(N_CHIPS, 2, n_r // tr),
            in_specs=[pl.BlockSpec(blk, lambda qi, li, ri, c_ref: (qi, 2 * c_ref[0] + li, ri, 0)),
                      pl.BlockSpec(blk, lambda qi, li, ri, c_ref: (qi, li, ri, 0))],
            out_specs=pl.BlockSpec(blk, lambda qi, li, ri, c_ref: (qi, li, ri, 0))),
        out_shape=jax.ShapeDtypeStruct(peer.shape, BF),
        compiler_params=_cparams(("parallel", "parallel", "parallel")),
    )(c_arr, g, peer)


def _chip_exchange(parts):
    n = len(parts)

    def body(*refs):
        ins, outs = refs[:n], refs[n:2 * n]
        send_sems, recv_sems, loc_sems = refs[2 * n:]
        x, y, c, q, chips = _place()
        locs = [pltpu.make_async_copy(ins[a].at[q], outs[a].at[q], loc_sems.at[a]) for a in range(n)]
        for cp in locs:
            cp.start()
        cps = []
        for a in range(n):
            for j, (cx, cy) in enumerate(chips):
                cps.append(_remote(ins[a].at[2 * cx + cy], outs[a].at[q], send_sems.at[3 * a + j], recv_sems.at[3 * a + j], (cx, cy, c)))
                cps[-1].start()
        for cp in cps:
            cp.wait()
        for cp in locs:
            cp.wait()

    return pl.pallas_call(
        body, name="grad_chip_exchange", in_specs=[ANY] * n, out_specs=[ANY] * n,
        out_shape=[jax.ShapeDtypeStruct(s.shape, s.dtype) for s in parts],
        scratch_shapes=[pltpu.SemaphoreType.DMA((3 * n,)), pltpu.SemaphoreType.DMA((3 * n,)), pltpu.SemaphoreType.DMA((n,))],
    )(*parts)


def _chip_sum(name, r):
    _, _, n_r, n_c = r.shape
    tr = 256 if n_r % 256 == 0 else n_r

    def body(r0, r1, r2, r3, o_ref):
        o_ref[...] = ((r0[0].astype(F32) + r1[0].astype(F32)) + r2[0].astype(F32)) + r3[0].astype(F32)

    return pl.pallas_call(
        body, name=name, grid=(2, n_r // tr),
        in_specs=[pl.BlockSpec((1, 1, tr, n_c), lambda li, ri, s=s: (s, li, ri, 0)) for s in range(N_CHIPS)],
        out_specs=pl.BlockSpec((1, tr, n_c), lambda li, ri: (li, ri, 0)),
        out_shape=jax.ShapeDtypeStruct((2, n_r, n_c), F32),
        compiler_params=_cparams(("parallel", "parallel")),
    )(r, r, r, r)


def _pair_share(halves):
    n = len(halves)

    def body(*refs):
        ins, outs = refs[:n], refs[n:2 * n]
        send_sems, recv_sems, loc_sems = refs[2 * n:]
        x, y, c, q, chips = _place()
        half = pl.ds(2 * c, 2)
        locs = [pltpu.make_async_copy(ins[a], outs[a].at[half], loc_sems.at[a]) for a in range(n)]
        for cp in locs:
            cp.start()
        cps = [_remote(ins[a], outs[a].at[half], send_sems.at[a], recv_sems.at[a], (x, y, 1 - c)) for a in range(n)]
        for cp in cps:
            cp.start()
        for cp in cps:
            cp.wait()
        for cp in locs:
            cp.wait()

    return pl.pallas_call(
        body, name="grad_pair_share", in_specs=[ANY] * n, out_specs=[ANY] * n,
        out_shape=[jax.ShapeDtypeStruct((DEPTH, *h.shape[1:]), h.dtype) for h in halves],
        scratch_shapes=[pltpu.SemaphoreType.DMA((n,)), pltpu.SemaphoreType.DMA((n,)), pltpu.SemaphoreType.DMA((n,))],
    )(*halves)


def _allreduce_small(v):
    n_r = v.shape[0]

    def body(v_ref, o_ref, slots, send_sems, recv_sems):
        x, y, c = lax.axis_index("x"), lax.axis_index("y"), lax.axis_index("c")
        me = 4 * x + 2 * y + c
        slots[me] = v_ref[...]
        cps = []
        for r in range(1, 8):
            px = 1 - x if r & 4 else x
            py = 1 - y if r & 2 else y
            pc = 1 - c if r & 1 else c
            cps.append(_remote(v_ref, slots.at[me], send_sems.at[r - 1], recv_sems.at[r - 1], (px, py, pc)))
            cps[-1].start()
        for cp in cps:
            cp.wait()
        tot = slots[0]
        for d in range(1, 8):
            tot = tot + slots[d]
        o_ref[...] = tot

    return pl.pallas_call(
        body, name="allreduce_small",
        in_specs=[pl.BlockSpec(memory_space=pltpu.VMEM)], out_specs=pl.BlockSpec(memory_space=pltpu.VMEM),
        out_shape=jax.ShapeDtypeStruct(v.shape, F32),
        scratch_shapes=[pltpu.VMEM((8, n_r, 128), F32), pltpu.SemaphoreType.DMA((7,)), pltpu.SemaphoreType.DMA((7,))],
        compiler_params=pltpu.CompilerParams(vmem_limit_bytes=VMEM_LIMIT_BYTES),
    )(v)


def _cols_full(g, l):
    s = g[:, l]
    return s.transpose(1, 0, 2).reshape(s.shape[1], -1)


def _rows_full(g, l):
    s = g[:, l]
    return s.reshape(-1, s.shape[-1])


def _cols_split(full):
    r = full.shape[0]
    return full.reshape(r, N_CHIPS, -1).transpose(1, 0, 2)


def _rows_split(full):
    return full.reshape(N_CHIPS, -1, full.shape[-1])


def _pack(parts):
    flat = []
    for a in parts:
        f = a.reshape(-1).astype(F32)
        flat.append(jnp.pad(f, (0, (-f.shape[0]) % 1024)))
    return jnp.concatenate(flat).reshape(-1, 128)


def _unpack(buf, shapes):
    flat = buf.reshape(-1)
    out, off = [], 0
    for s in shapes:
        size = 1
        for d in s:
            size *= d
        out.append(flat[off:off + size].reshape(s))
        off += size + (-size) % 1024
    return out


def kernel(x, p, g_mix_pre, w_in, b_forget, w_conf_dw, conf_ln_g, conf_ln_b, w_conf_pw, w_sc, w_pool, pool_scale, w_out, g_mix_post, g_mlp_pre, w_up, w_down, g_mlp_post, g_ple_pre, w_ple_gate, w_ple_proj, g_ple_post, loss_target, m_g_mix_pre, m_w_in, m_b_forget, m_w_conf_dw, m_conf_ln_g, m_conf_ln_b, m_w_conf_pw, m_w_sc, m_w_pool, m_pool_scale, m_w_out, m_g_mix_post, m_g_mlp_pre, m_w_up, m_w_down, m_g_mlp_post, m_g_ple_pre, m_w_ple_gate, m_w_ple_proj, m_g_ple_post, v_g_mix_pre, v_w_in, v_b_forget, v_w_conf_dw, v_conf_ln_g, v_conf_ln_b, v_w_conf_pw, v_w_sc, v_w_pool, v_pool_scale, v_w_out, v_g_mix_post, v_g_mlp_pre, v_w_up, v_w_down, v_g_mlp_post, v_g_ple_pre, v_w_ple_gate, v_w_ple_proj, v_g_ple_post):
    names = ['g_mix_pre', 'w_in', 'b_forget', 'w_conf_dw', 'conf_ln_g', 'conf_ln_b', 'w_conf_pw', 'w_sc', 'w_pool', 'pool_scale',
             'w_out', 'g_mix_post', 'g_mlp_pre', 'w_up', 'w_down', 'g_mlp_post', 'g_ple_pre', 'w_ple_gate', 'w_ple_proj', 'g_ple_post']
    env = locals()
    wts = {k: env[k] for k in names}
    mom = {k: env["m_" + k] for k in names}
    var = {k: env["v_" + k] for k in names}

    t_len = x.shape[1]
    tm = min(256, t_len)
    tq = min(512, max(t_len // 2, 128))
    chip = 2 * lax.axis_index("x") + lax.axis_index("y")
    core = lax.axis_index("c")

    big = ['w_in', 'w_conf_pw', 'w_out', 'w_up', 'w_down', 'w_ple_gate', 'w_ple_proj']
    tiny = ['w_conf_dw', 'w_sc']
    gathered = _allgather_weights([wts[k].astype(BF) for k in big] + [wts[k] for k in tiny])
    gat = dict(zip(big + tiny, gathered))

    def layer_weights(l):
        w_full = _cols_full(gat['w_in'], l)
        w_a = jnp.concatenate([w_full[:, :F_OFF], w_full[:, F_OFF + N_HEADS:], w_full[:, F_OFF:F_OFF + N_HEADS],
                               jnp.zeros((D_MODEL, Z_W - D_IN), BF)], axis=1)
        w_bd = jnp.zeros((D_GRP, D_GRP), F32)
        for g in range(4):
            w_bd = lax.dynamic_update_slice(w_bd, wts['w_pool'][l, g], (64 * g, 64 * g))
        row = lambda a: a[l][None, :]
        return dict(
            w_a=w_a, w_dw=jnp.pad(_cols_full(gat['w_conf_dw'], l), ((0, 1), (0, 0))), w_pw=_rows_full(gat['w_conf_pw'], l),
            w_sc=jnp.pad(_cols_full(gat['w_sc'], l), ((0, 5), (0, 0))), w_bd=w_bd.astype(BF),
            w_out=_rows_full(gat['w_out'], l), w_up=_cols_full(gat['w_up'], l), w_down=_rows_full(gat['w_down'], l),
            w_gate=_rows_full(gat['w_ple_gate'], l), w_proj=_cols_full(gat['w_ple_proj'], l),
            b_f=jnp.pad(wts['b_forget'][l], (0, F_PAD - N_HEADS))[None, :],
            ln_g=row(wts['conf_ln_g']), ln_b=row(wts['conf_ln_b']), pool_scale=row(wts['pool_scale']),
            g_mix_pre=row(wts['g_mix_pre']), g_mix_post=row(wts['g_mix_post']), g_mlp_pre=row(wts['g_mlp_pre']),
            g_mlp_post=row(wts['g_mlp_post']), g_ple_pre=row(wts['g_ple_pre']), g_ple_post=row(wts['g_ple_post']))

    lw = [layer_weights(l) for l in range(DEPTH)]

    h = x[0]
    saved = []
    for l in range(DEPTH):
        w = lw[l]
        s = dict(h0=h)
        s['zc'], s['qkv'], s['zs'], s['zp'], s['zf'] = _mix_in_fwd(t_len, tm, h, w['g_mix_pre'], w['w_a'])
        s['cv'], y_conf = _conf_fwd(t_len, tm, s['zc'], w['w_dw'], w['ln_g'], w['ln_b'], w['w_pw'])
        c_all = _fox_prep(t_len, s['zf'], w['b_f'])[:, :N_HEADS].T
        s['ccol'], s['crow'] = c_all.reshape(N_HEADS, t_len, 1), c_all.reshape(N_HEADS, 1, t_len)
        s['q'], s['k'], s['v'] = (_to_heads(s['qkv'][:, 256 * a:256 * (a + 1)]) for a in range(3))
        s['o'], s['lse'] = _fox_fwd(t_len, tq, s['q'], s['k'], s['v'], s['ccol'], s['crow'])
        y_att = _from_heads(s['o']).astype(BF)
        (y_sc,) = _sconv_fwd(t_len, tm, s['zs'], w['w_sc'])
        (y_pool,) = _pool_fwd(t_len, tm, s['zp'], w['w_bd'], w['pool_scale'])
        s['ys'] = [y_conf, y_att, y_sc, y_pool]
        s['mix'], h = _mix_out_fwd(t_len, tm, s['ys'], h, w['w_out'], w['g_mix_post'])
        s['h1'] = h
        s['up'] = _mlp_up_fwd(t_len, tm, h, w['g_mlp_pre'], w['w_up'])
        s['ff'], h = _mlp_down_fwd(t_len, tm, s['up'], h, w['w_down'], w['g_mlp_post'])
        s['h2'] = h
        s['gpre'], s['pe'], h = _ple_fwd(t_len, tm, h, p[l, 0], w['g_ple_pre'], w['w_gate'], w['w_proj'], w['g_ple_post'])
        saved.append(s)

    dh, loss_part = _loss_call(t_len, tm, h, loss_target[0])

    grads = [None] * DEPTH
    for l in reversed(range(DEPTH)):
        w, s, g = lw[l], saved[l], {}
        dh, g['w_ple_gate'], g['w_ple_proj'], g['g_ple_pre'], g['g_ple_post'] = _ple_bwd(
            t_len, tm, dh, s['h2'], s['gpre'], s['pe'], p[l, 0], w['g_ple_pre'], w['w_gate'], w['g_ple_post'])
        dff, dup, g['g_mlp_post'] = _mlp_bwd_a(t_len, tm, dh, s['ff'], s['up'], w['w_down'], w['g_mlp_post'])
        dh, hn, g['g_mlp_pre'] = _mlp_bwd_b(t_len, tm, dh, s['h1'], dup, w['w_up'], w['g_mlp_pre'])
        tt = min(512, t_len)
        g['w_up'] = _mm_tn("mlp_dw_up", hn, dup, D_MODEL, 1024, tt)
        g['w_down'] = _mm_tn("mlp_dw_down", s['up'], dff, 1024, D_MODEL, tt, pro=lambda u: jnp.square(jnp.maximum(u, 0.0)))
        dy_conf, dy_att, dy_sc, dy_pool, g['w_out'], g['g_mix_post'] = _mix_out_bwd(t_len, tm, dh, s['mix'], s['ys'], w['w_out'], w['g_mix_post'])
        dzc, g['w_conf_dw'], g['conf_ln_g'], g['conf_ln_b'], g['w_conf_pw'] = _conf_bwd(
            t_len, tm, s['zc'], s['cv'], dy_conf, w['w_dw'], w['ln_g'], w['ln_b'], w['w_pw'])
        dzs, g['w_sc'] = _sconv_bwd(t_len, tm, s['zs'], dy_sc, w['w_sc'])
        dzp, d_wbd, g['pool_scale'] = _pool_bwd(t_len, tm, s['zp'], dy_pool, w['w_bd'], w['pool_scale'])
        g['w_pool'] = jnp.stack([d_wbd[64 * a:64 * (a + 1), 64 * a:64 * (a + 1)] for a in range(4)])
        do = _to_heads(dy_att)
        dq, delta, dc_q = _fox_dq(t_len, tq, s['q'], s['k'], s['v'], do, s['o'], s['ccol'], s['crow'], s['lse'])
        dk, dv, dc = _fox_dkv(t_len, tq, s['q'], s['k'], s['v'], do, s['ccol'], s['crow'],
                              s['lse'].reshape(N_HEADS, 1, t_len), delta.reshape(N_HEADS, 1, t_len))
        dc_pad = jnp.pad((dc + dc_q).reshape(N_HEADS, t_len).T, ((0, 0), (0, F_PAD - N_HEADS)))
        dzf, d_bf = _fox_post(t_len, dc_pad, s['zf'], w['b_f'])
        g['b_forget'] = d_bf[0, :N_HEADS]
        dqkv = jnp.concatenate([_from_heads(dq), _from_heads(dk), _from_heads(dv)], axis=1).astype(BF)
        dh, xn, dz, g['g_mix_pre'] = _mix_in_bwd(t_len, tm, dh, s['h0'], dzc, dqkv, dzs, dzp, dzf, w['g_mix_pre'], w['w_a'])
        d_wa = _mm_tn("mix_dw_in", xn, dz, D_MODEL, Z_W, tt)
        g['w_in'] = jnp.concatenate([d_wa[:, :F_OFF], d_wa[:, Z_F:Z_F + N_HEADS], d_wa[:, F_OFF:Z_F]], axis=1)
        g['w_conf_dw'] = g['w_conf_dw'][:CONF_K]
        g['w_sc'] = g['w_sc'][:SC_K]
        grads[l] = g
    grad_x = dh[None]

    split = dict(w_in=_cols_split, w_conf_pw=_rows_split, w_out=_rows_split, w_up=_cols_split, w_down=_rows_split,
                 w_ple_gate=_rows_split, w_ple_proj=_cols_split)
    contrib = [jnp.stack([split[k](grads[l][k]) for l in range(DEPTH)], axis=1) for k in big]
    peer = _pair_exchange(contrib)
    c_arr = core.astype(jnp.int32).reshape(1)
    parts = [_pair_sum("grad_pair_sum_" + k, a, b, c_arr) for k, a, b in zip(big, contrib, peer)]
    landed = _chip_exchange(parts)
    halves = [_chip_sum("grad_chip_sum_" + k, r) for k, r in zip(big, landed)]
    reduced = dict(zip(big, _pair_share(halves)))

    small = [k for k in names if k not in big]
    small_shapes = [(DEPTH, *grads[0][k].shape) for k in small]
    packed = _pack([jnp.stack([grads[l][k] for l in range(DEPTH)]) for k in small] + [loss_part])
    summed = _allreduce_small(packed)
    small_sum = _unpack(summed, small_shapes + [(8, 128)])
    loss = small_sum[-1][0, 0]
    for k, a in zip(small, small_sum[:-1]):
        if k in tiny:
            a = lax.dynamic_slice_in_dim(a, chip * 64, 64, axis=2)
        reduced[k] = a.reshape(wts[k].shape)

    delta_w, new_m, new_v = {}, {}, {}
    for k in names:
        shp = wts[k].shape
        as3 = (lambda a: a.reshape(shp[0], -1, shp[-1])) if len(shp) > 2 else (lambda a: a.reshape(1, shp[0], shp[1]))
        d, nm, nv = _adamw_call("adamw_" + k, as3(wts[k]), as3(reduced[k]), as3(mom[k]), as3(var[k]))
        delta_w[k], new_m[k], new_v[k] = d.reshape(shp), nm.reshape(shp), nv.reshape(shp)

    return (loss, grad_x, *[reduced[k] for k in names], *[delta_w[k] for k in names],
            *[new_m[k] for k in names], *[new_v[k] for k in names])
```

```python
import functools

import jax
import jax.numpy as jnp
from jax import lax
from jax.experimental import pallas as pl
from jax.experimental.pallas import tpu as pltpu

F32 = jnp.float32
BF = jnp.bfloat16

DEPTH = 4
D_MODEL = 1024
D_GRP = 256
HEAD_DIM = 64
N_HEADS = 4
CONF_K = 31
SC_K = 3
D_FF = 4096
D_PLE = 256
EPS = 1e-6
N_CHIPS = 4
Z_CONF, Z_QKV, Z_SC, Z_POOL, Z_F = 0, 512, 1280, 2048, 2304
Z_W = 2432
F_PAD = 128
D_IN = 2308
F_OFF = 1280

ADAM_LR, ADAM_B1, ADAM_B2, ADAM_EPS, ADAM_WD, ADAM_STEP = 0.001, 0.9, 0.999, 1e-08, 0.01, 10

VMEM_LIMIT_BYTES = 56 * 1024 * 1024
HALO = 32
NEG = -1e30
MESH_T = pl.DeviceIdType.MESH


def _cparams(sem=None):
    return pltpu.CompilerParams(dimension_semantics=sem, vmem_limit_bytes=VMEM_LIMIT_BYTES)


def _dot(a, b):
    return jnp.dot(a, b, preferred_element_type=F32)


def _dot_nt(a, b):
    return lax.dot_general(a, b, (((1,), (1,)), ((), ())), preferred_element_type=F32)


def _dot_tn(a, b):
    return lax.dot_general(a, b, (((0,), (0,)), ((), ())), preferred_element_type=F32)


def _sig(x):
    return jax.nn.sigmoid(x)


def _rms_fwd(x, g):
    r = lax.rsqrt(jnp.mean(x * x, axis=-1, keepdims=True) + EPS)
    return x * r * g


def _rms_bwd(x, g, dy):
    r = lax.rsqrt(jnp.mean(x * x, axis=-1, keepdims=True) + EPS)
    xh = x * r
    dg = jnp.sum(dy * xh, axis=0, keepdims=True)
    dxh = dy * g
    dx = r * (dxh - xh * jnp.mean(dxh * xh, axis=-1, keepdims=True))
    return dx, dg


def _back(ext, d):
    return ext if d == 0 else pltpu.roll(ext, d, 0)


def _ahead(ext, d):
    return ext if d == 0 else pltpu.roll(ext, ext.shape[0] - d, 0)


def _rows_call(name, fn, t_len, tm, rows, fulls, out_rows, out_accs=(), prevs=(), nexts=()):
    n = t_len // tm
    hb = tm // HALO
    nhb = t_len // HALO
    n_rows, n_prev, n_next, n_full = len(rows), len(prevs), len(nexts), len(fulls)
    in_specs = [pl.BlockSpec((tm, a.shape[1]), lambda i: (i, 0)) for a in rows]
    in_specs += [pl.BlockSpec((HALO, a.shape[1]), lambda i: (jnp.maximum(i * hb - 1, 0), 0)) for a in prevs]
    in_specs += [pl.BlockSpec((HALO, a.shape[1]), lambda i: (jnp.minimum((i + 1) * hb, nhb - 1), 0)) for a in nexts]
    in_specs += [pl.BlockSpec(a.shape, lambda i, nd=a.ndim: (0,) * nd) for a in fulls]
    out_shape = [jax.ShapeDtypeStruct((t_len, c), dt) for c, dt in out_rows]
    out_shape += [jax.ShapeDtypeStruct(s, F32) for s in out_accs]
    out_specs = [pl.BlockSpec((tm, c), lambda i: (i, 0)) for c, _ in out_rows]
    out_specs += [pl.BlockSpec(s, lambda i, nd=len(s): (0,) * nd) for s in out_accs]
    n_in = n_rows + n_prev + n_next + n_full
    n_ro = len(out_rows)

    def body(*refs):
        i = pl.program_id(0)
        ins, outs = refs[:n_in], refs[n_in:]
        rv = [r[...] for r in ins[:n_rows]]
        pv = [r[...] for r in ins[n_rows:n_rows + n_prev]]
        nv = [r[...] for r in ins[n_rows + n_prev:n_rows + n_prev + n_next]]
        fv = list(ins[n_rows + n_prev + n_next:])
        ro, ao = fn(i, n, rv, pv, nv, fv)
        for r, v in zip(outs[:n_ro], ro):
            r[...] = v.astype(r.dtype)
        if out_accs:
            acc = outs[n_ro:]

            @pl.when(i == 0)
            def _():
                for r in acc:
                    r[...] = jnp.zeros(r.shape, r.dtype)

            for r, v in zip(acc, ao):
                r[...] += v

    res = pl.pallas_call(
        body, name=name, grid=(n,), in_specs=in_specs, out_specs=out_specs, out_shape=out_shape,
        compiler_params=_cparams(("arbitrary",)),
    )(*rows, *prevs, *nexts, *fulls)
    return res


def _mm_tn(name, x, y, tk, tn, tt, pro=None):
    t_len, k_dim = x.shape
    n_dim = y.shape[1]

    def body(x_ref, y_ref, o_ref):
        @pl.when(pl.program_id(2) == 0)
        def _():
            o_ref[...] = jnp.zeros(o_ref.shape, o_ref.dtype)

        xv = x_ref[...]
        if pro is not None:
            xv = pro(xv)
        o_ref[...] += _dot_tn(xv.astype(BF), y_ref[...].astype(BF))

    return pl.pallas_call(
        body, name=name, grid=(k_dim // tk, n_dim // tn, t_len // tt),
        in_specs=[pl.BlockSpec((tt, tk), lambda a, b, t: (t, a)), pl.BlockSpec((tt, tn), lambda a, b, t: (t, b))],
        out_specs=pl.BlockSpec((tk, tn), lambda a, b, t: (a, b)),
        out_shape=jax.ShapeDtypeStruct((k_dim, n_dim), F32),
        compiler_params=_cparams(("parallel", "parallel", "arbitrary")),
    )(x, y)


def _mix_in_fwd(t_len, tm, h, g_pre, w_a):
    def fn(i, n, rows, prevs, nexts, fulls):
        (hv,), (g, w) = rows, fulls
        z = _dot(_rms_fwd(hv, g[...]).astype(BF), w[...])
        return [z[:, Z_CONF:Z_QKV], z[:, Z_QKV:Z_SC], z[:, Z_SC:Z_POOL], z[:, Z_POOL:Z_F], z[:, Z_F:Z_W]], []

    return _rows_call("mix_in_fwd", fn, t_len, tm, [h], [g_pre, w_a],
                      [(512, F32), (768, BF), (768, F32), (256, F32), (F_PAD, F32)])


def _mix_in_bwd(t_len, tm, dh, h, dzc, dqkv, dzs, dzp, dzf, g_pre, w_a):
    def fn(i, n, rows, prevs, nexts, fulls):
        dhv, hv, a, b, c, d, e = rows
        g, w = fulls
        dz = jnp.concatenate([a, b, c, d, e], axis=1)
        dxn = _dot_nt(dz, w[...])
        dx, dg = _rms_bwd(hv, g[...], dxn)
        xn = _rms_fwd(hv, g[...])
        return [dhv + dx, xn, dz], [dg]

    return _rows_call("mix_in_bwd", fn, t_len, tm, [dh, h, dzc, dqkv, dzs, dzp, dzf], [g_pre, w_a],
                      [(D_MODEL, F32), (D_MODEL, BF), (Z_W, BF)], [(1, D_MODEL)])


def _glu_ext(i, zc, zc_prev):
    ext = jnp.concatenate([zc_prev, zc], axis=0)
    u = ext[:, :D_GRP] * _sig(ext[:, D_GRP:])
    row = lax.broadcasted_iota(jnp.int32, u.shape, 0)
    return jnp.where((row >= HALO) | (i > 0), u, 0.0)


def _conf_fwd(t_len, tm, zc, w_dw, ln_g, ln_b, w_pw):
    def fn(i, n, rows, prevs, nexts, fulls):
        (zv,), (zp,) = rows, prevs
        wdw, lg, lb, wpw = fulls
        u = _glu_ext(i, zv, zp)
        cv = jnp.zeros((tm, D_GRP), F32)
        for k in range(CONF_K):
            cv = cv + wdw[k:k + 1, :] * _back(u, CONF_K - 1 - k)[HALO:, :]
        mu = jnp.mean(cv, axis=-1, keepdims=True)
        xc = cv - mu
        ln = xc * lax.rsqrt(jnp.mean(xc * xc, axis=-1, keepdims=True) + EPS) * lg[...] + lb[...]
        s = ln * _sig(ln)
        return [cv, _dot(s.astype(BF), wpw[...])], []

    return _rows_call("conf_fwd", fn, t_len, tm, [zc], [w_dw, ln_g, ln_b, w_pw], [(D_GRP, F32), (D_GRP, BF)], prevs=[zc])


def _conf_bwd(t_len, tm, zc, cv, dy, w_dw, ln_g, ln_b, w_pw):
    def fn(i, n, rows, prevs, nexts, fulls):
        zv, cvv, dyv = rows
        (zp,) = prevs
        cvn, dyn = nexts
        wdw, lg, lb, wpw = fulls
        cve = jnp.concatenate([cvv, cvn], axis=0)
        dye = jnp.concatenate([dyv, dyn], axis=0)
        mu = jnp.mean(cve, axis=-1, keepdims=True)
        xc = cve - mu
        rs = lax.rsqrt(jnp.mean(xc * xc, axis=-1, keepdims=True) + EPS)
        xh = xc * rs
        ln = xh * lg[...] + lb[...]
        sg = _sig(ln)
        s = ln * sg
        ds = _dot_nt(dye.astype(BF), wpw[...])
        dln = ds * (sg * (1.0 + ln * (1.0 - sg)))
        dxh = dln * lg[...]
        dcv = rs * (dxh - jnp.mean(dxh, axis=-1, keepdims=True) - xh * jnp.mean(dxh * xh, axis=-1, keepdims=True))
        row = lax.broadcasted_iota(jnp.int32, dcv.shape, 0)
        dcv = jnp.where((row < tm) | (i < n - 1), dcv, 0.0)
        d_lg = jnp.sum((dln * xh)[:tm], axis=0, keepdims=True)
        d_lb = jnp.sum(dln[:tm], axis=0, keepdims=True)
        d_wpw = _dot_tn(s[:tm].astype(BF), dyv.astype(BF))
        u = _glu_ext(i, zv, zp)
        dcv_cur = dcv[:tm]
        du = jnp.zeros((tm, D_GRP), F32)
        d_wdw = jnp.zeros((32, D_GRP), F32)
        krow = lax.broadcasted_iota(jnp.int32, (32, D_GRP), 0)
        for k in range(CONF_K):
            d = CONF_K - 1 - k
            du = du + wdw[k:k + 1, :] * _ahead(dcv, d)[:tm, :]
            tap = _back(u, d)[HALO:, :]
            d_wdw = d_wdw + jnp.where(krow == k, jnp.sum(dcv_cur * tap, axis=0, keepdims=True), 0.0)
        a, b = zv[:, :D_GRP], zv[:, D_GRP:]
        sb = _sig(b)
        dz = jnp.concatenate([du * sb, du * a * sb * (1.0 - sb)], axis=1)
        return [dz], [d_wdw, d_lg, d_lb, d_wpw]

    return _rows_call("conf_bwd", fn, t_len, tm, [zc, cv, dy], [w_dw, ln_g, ln_b, w_pw], [(512, BF)],
                      [(32, D_GRP), (1, D_GRP), (1, D_GRP), (D_GRP, D_GRP)], prevs=[zc], nexts=[cv, dy])


def _sc_ext(i, zs, zs_prev):
    ext = jnp.concatenate([zs_prev, zs], axis=0)
    e = ext[:, 2 * D_GRP:] * ext[:, :D_GRP]
    row = lax.broadcasted_iota(jnp.int32, e.shape, 0)
    return jnp.where((row >= HALO) | (i > 0), e, 0.0)


def _sconv_fwd(t_len, tm, zs, w_sc):
    def fn(i, n, rows, prevs, nexts, fulls):
        (zv,), (zp,), (w,) = rows, prevs, fulls
        e = _sc_ext(i, zv, zp)
        cv = jnp.zeros((tm, D_GRP), F32)
        for k in range(SC_K):
            cv = cv + w[k:k + 1, :] * _back(e, SC_K - 1 - k)[HALO:, :]
        return [zv[:, D_GRP:2 * D_GRP] * cv], []

    return _rows_call("sconv_fwd", fn, t_len, tm, [zs], [w_sc], [(D_GRP, BF)], prevs=[zs])


def _sconv_bwd(t_len, tm, zs, dy, w_sc):
    def fn(i, n, rows, prevs, nexts, fulls):
        zv, dyv = rows
        (zp,) = prevs
        zn, dyn = nexts
        (w,) = fulls
        e = _sc_ext(i, zv, zp)
        taps = [_back(e, SC_K - 1 - k)[HALO:, :] for k in range(SC_K)]
        cv = w[0:1, :] * taps[0] + w[1:2, :] * taps[1] + w[2:3, :] * taps[2]
        bg = zv[:, D_GRP:2 * D_GRP]
        dcv = jnp.concatenate([dyv * bg, dyn * zn[:, D_GRP:2 * D_GRP]], axis=0)
        row = lax.broadcasted_iota(jnp.int32, dcv.shape, 0)
        dcv = jnp.where((row < tm) | (i < n - 1), dcv, 0.0)
        de = jnp.zeros((tm, D_GRP), F32)
        d_w = jnp.zeros((8, D_GRP), F32)
        krow = lax.broadcasted_iota(jnp.int32, (8, D_GRP), 0)
        for k in range(SC_K):
            de = de + w[k:k + 1, :] * _ahead(dcv, SC_K - 1 - k)[:tm, :]
            d_w = d_w + jnp.where(krow == k, jnp.sum(dcv[:tm] * taps[k], axis=0, keepdims=True), 0.0)
        dz = jnp.concatenate([de * zv[:, 2 * D_GRP:], dyv * cv, de * zv[:, :D_GRP]], axis=1)
        return [dz], [d_w]

    return _rows_call("sconv_bwd", fn, t_len, tm, [zs, dy], [w_sc], [(768, BF)], [(8, D_GRP)], prevs=[zs], nexts=[zs, dy])


def _pool_window(shape):
    grp = lax.broadcasted_iota(jnp.int32, shape, 1) // 64
    return grp, jnp.where(grp == 0, 2.0, jnp.where(grp == 1, 4.0, jnp.where(grp == 2, 8.0, 16.0)))


def _pool_d(i, tm, zv, zp):
    ext = jnp.concatenate([zp, zv], axis=0)
    row = lax.broadcasted_iota(jnp.int32, ext.shape, 0)
    ext = jnp.where((row >= HALO) | (i > 0), ext, 0.0)
    s2 = ext + _back(ext, 1)
    s4 = s2 + _back(s2, 2)
    s8 = s4 + _back(s4, 4)
    s16 = s8 + _back(s8, 8)
    grp, win = _pool_window((tm, D_GRP))
    sel = jnp.where(grp == 0, s2[HALO:], jnp.where(grp == 1, s4[HALO:], jnp.where(grp == 2, s8[HALO:], s16[HALO:])))
    pos = (i * tm + lax.broadcasted_iota(jnp.int32, (tm, D_GRP), 0) + 1).astype(F32)
    return sel / jnp.minimum(pos, win) - zv


def _pool_fwd(t_len, tm, zpool, w_bd, scale):
    def fn(i, n, rows, prevs, nexts, fulls):
        (zv,), (zp,) = rows, prevs
        w, sc = fulls
        d = _pool_d(i, tm, zv, zp)
        return [_dot(d.astype(BF), w[...]) * sc[...]], []

    return _rows_call("pool_fwd", fn, t_len, tm, [zpool], [w_bd, scale], [(D_GRP, BF)], prevs=[zpool])


def _pool_bwd(t_len, tm, zpool, dy, w_bd, scale):
    def fn(i, n, rows, prevs, nexts, fulls):
        zv, dyv = rows
        (zp,) = prevs
        (dyn,) = nexts
        w, sc = fulls
        d = _pool_d(i, tm, zv, zp)
        lin = _dot(d.astype(BF), w[...])
        d_sc = jnp.sum(dyv * lin, axis=0, keepdims=True)
        dye = jnp.concatenate([dyv, dyn], axis=0) * sc[...]
        d_w = _dot_tn(d.astype(BF), dye[:tm].astype(BF))
        dd = _dot_nt(dye.astype(BF), w[...])
        row = lax.broadcasted_iota(jnp.int32, dd.shape, 0)
        dd = jnp.where((row < tm) | (i < n - 1), dd, 0.0)
        grp, win = _pool_window(dd.shape)
        pos = (i * tm + row + 1).astype(F32)
        ddc = dd / jnp.minimum(pos, win)
        f2 = ddc + _ahead(ddc, 1)
        f4 = f2 + _ahead(f2, 2)
        f8 = f4 + _ahead(f4, 4)
        f16 = f8 + _ahead(f8, 8)
        sel = jnp.where(grp == 0, f2, jnp.where(grp == 1, f4, jnp.where(grp == 2, f8, f16)))
        return [(sel - dd)[:tm]], [d_w, d_sc]

    return _rows_call("pool_bwd", fn, t_len, tm, [zpool, dy], [w_bd, scale], [(D_GRP, BF)],
                      [(D_GRP, D_GRP), (1, D_GRP)], prevs=[zpool], nexts=[dy])


def _mix_out_fwd(t_len, tm, ys, h, w_out, g_post):
    def fn(i, n, rows, prevs, nexts, fulls):
        y0, y1, y2, y3, hv = rows
        w, g = fulls
        mix = _dot(jnp.concatenate([y0, y1, y2, y3], axis=1), w[...])
        return [mix, hv + _rms_fwd(mix, g[...])], []

    return _rows_call("mix_out_fwd", fn, t_len, tm, [*ys, h], [w_out, g_post], [(D_MODEL, F32), (D_MODEL, F32)])


def _mix_out_bwd(t_len, tm, dh, mix, ys, w_out, g_post):
    def fn(i, n, rows, prevs, nexts, fulls):
        dhv, mv, y0, y1, y2, y3 = rows
        w, g = fulls
        dmix, dg = _rms_bwd(mv, g[...], dhv)
        dmb = dmix.astype(BF)
        dcat = _dot_nt(dmb, w[...])
        d_w = _dot_tn(jnp.concatenate([y0, y1, y2, y3], axis=1), dmb)
        return [dcat[:, :256], dcat[:, 256:512], dcat[:, 512:768], dcat[:, 768:]], [d_w, dg]

    return _rows_call("mix_out_bwd", fn, t_len, tm, [dh, mix, *ys], [w_out, g_post],
                      [(D_GRP, F32)] * 4, [(D_MODEL, D_MODEL), (1, D_MODEL)])


def _log_sigmoid(x):
    return jnp.minimum(x, 0.0) - jnp.log(1.0 + jnp.exp(-jnp.abs(x)))


def _fox_prep(t_len, zf, b_f):
    blk = min(256, t_len)

    def body(zf_ref, b_ref, c_ref):
        tri = (lax.broadcasted_iota(jnp.int32, (blk, blk), 0) >= lax.broadcasted_iota(jnp.int32, (blk, blk), 1)).astype(F32)

        def step(bi, carry):
            rows = pl.ds(pl.multiple_of(bi * blk, blk), blk)
            lf = _log_sigmoid(zf_ref[rows, :] + b_ref[...])
            cs = jnp.dot(tri, lf, precision=lax.Precision.HIGHEST, preferred_element_type=F32) + carry
            neg = -cs
            hi = neg.astype(BF)
            rest = neg - hi.astype(F32)
            mid = rest.astype(BF)
            c_ref[rows, :] = jnp.concatenate([hi, mid, (rest - mid.astype(F32)).astype(BF)], axis=1)
            return cs[blk - 1:blk, :]

        lax.fori_loop(0, t_len // blk, step, jnp.zeros((1, F_PAD), F32))

    return pl.pallas_call(body, name="fox_prep", out_shape=jax.ShapeDtypeStruct((t_len, 3 * F_PAD), BF),
                          compiler_params=_cparams())(zf, b_f)


def _fox_post(t_len, dc, zf, b_f):
    blk = min(256, t_len)
    nb = t_len // blk

    def body(dc_ref, zf_ref, b_ref, dz_ref, db_ref):
        tri = (lax.broadcasted_iota(jnp.int32, (blk, blk), 0) <= lax.broadcasted_iota(jnp.int32, (blk, blk), 1)).astype(F32)

        def step(s, carry):
            car, db = carry
            rows = pl.ds(pl.multiple_of((nb - 1 - s) * blk, blk), blk)
            dlf = jnp.dot(tri, dc_ref[rows, :], precision=lax.Precision.HIGHEST, preferred_element_type=F32) + car
            dz = dlf * _sig(-(zf_ref[rows, :] + b_ref[...]))
            dz_ref[rows, :] = dz.astype(dz_ref.dtype)
            return dlf[0:1, :], db + jnp.sum(dz, axis=0, keepdims=True)

        _, db = lax.fori_loop(0, nb, step, (jnp.zeros((1, F_PAD), F32), jnp.zeros((1, F_PAD), F32)))
        db_ref[...] = db

    return pl.pallas_call(body, name="fox_post",
                          out_shape=(jax.ShapeDtypeStruct((t_len, F_PAD), BF), jax.ShapeDtypeStruct((1, F_PAD), F32)),
                          compiler_params=_cparams())(dc, zf, b_f)


QK_W = 128
C_COL = HEAD_DIM
ONE_COL = HEAD_DIM + 3


def _tri_mask(tq, key_rows):
    r = lax.broadcasted_iota(jnp.int32, (tq, tq), 0)
    c = lax.broadcasted_iota(jnp.int32, (tq, tq), 1)
    return (r <= c) if key_rows else (r >= c)


def _fox_fwd(t_len, tq, qa, ka, v):
    nq = t_len // tq

    def body(q_ref, k_ref, v_ref, o_ref, lse_ref):
        i = pl.program_id(1)
        q = q_ref[0]

        def tile(j, carry, diagonal):
            m, l, acc = carry
            rows = pl.ds(pl.multiple_of(j * tq, tq), tq)
            s = _dot_nt(q, k_ref[0, rows, :])
            if diagonal:
                s = jnp.where(_tri_mask(tq, False), s, NEG)
            m_new = jnp.maximum(m, jnp.max(s, axis=-1, keepdims=True))
            alpha = jnp.exp(m - m_new)
            p = jnp.exp(s - m_new)
            l = alpha * l + jnp.sum(p, axis=-1, keepdims=True)
            acc = alpha * acc + _dot(p.astype(BF), v_ref[0, rows, :])
            return m_new, l, acc

        init = (jnp.full((tq, 1), NEG, F32), jnp.zeros((tq, 1), F32), jnp.zeros((tq, HEAD_DIM), F32))
        carry = lax.fori_loop(0, i, lambda j, c: tile(j, c, False), init)
        m, l, acc = tile(i, carry, True)
        o_ref[0] = acc / l
        lse_ref[0] = m + jnp.log(l)

    return pl.pallas_call(
        body, name="fox_fwd", grid=(N_HEADS, nq),
        in_specs=[pl.BlockSpec((1, tq, QK_W), lambda h, i: (h, i, 0)),
                  pl.BlockSpec((1, t_len, QK_W), lambda h, i: (h, 0, 0)),
                  pl.BlockSpec((1, t_len, HEAD_DIM), lambda h, i: (h, 0, 0))],
        out_specs=[pl.BlockSpec((1, tq, HEAD_DIM), lambda h, i: (h, i, 0)), pl.BlockSpec((1, tq, 1), lambda h, i: (h, i, 0))],
        out_shape=[jax.ShapeDtypeStruct((N_HEADS, t_len, HEAD_DIM), F32), jax.ShapeDtypeStruct((N_HEADS, t_len, 1), F32)],
        compiler_params=_cparams(("parallel", "arbitrary")),
    )(qa, ka, v)


def _fox_dq(t_len, tq, qa, ka, v, do, o, lse):
    nq = t_len // tq

    def body(q_ref, k_ref, v_ref, do_ref, o_ref, lse_ref, dq_ref, dl_ref):
        i = pl.program_id(1)
        q = q_ref[0]
        dob = do_ref[0].astype(BF)
        lse = lse_ref[0]
        delta = jnp.sum(do_ref[0] * o_ref[0], axis=-1, keepdims=True)
        dl_ref[0] = delta

        def tile(j, acc, diagonal):
            rows = pl.ds(pl.multiple_of(j * tq, tq), tq)
            kt = k_ref[0, rows, :]
            p = jnp.exp(_dot_nt(q, kt) - lse)
            if diagonal:
                p = jnp.where(_tri_mask(tq, False), p, 0.0)
            ds = p * (_dot_nt(dob, v_ref[0, rows, :]) - delta)
            return acc + _dot(ds.astype(BF), kt)

        acc = lax.fori_loop(0, i, lambda j, a: tile(j, a, False), jnp.zeros((tq, QK_W), F32))
        dq_ref[0] = tile(i, acc, True) * (HEAD_DIM ** -0.5)

    qspec = pl.BlockSpec((1, tq, QK_W), lambda h, i: (h, i, 0))
    ospec = pl.BlockSpec((1, tq, HEAD_DIM), lambda h, i: (h, i, 0))
    cspec = pl.BlockSpec((1, tq, 1), lambda h, i: (h, i, 0))
    return pl.pallas_call(
        body, name="fox_dq", grid=(N_HEADS, nq),
        in_specs=[qspec, pl.BlockSpec((1, t_len, QK_W), lambda h, i: (h, 0, 0)),
                  pl.BlockSpec((1, t_len, HEAD_DIM), lambda h, i: (h, 0, 0)), ospec, ospec, cspec],
        out_specs=[qspec, cspec],
        out_shape=[jax.ShapeDtypeStruct((N_HEADS, t_len, QK_W), F32), jax.ShapeDtypeStruct((N_HEADS, t_len, 1), F32)],
        compiler_params=_cparams(("parallel", "arbitrary")),
    )(qa, ka, v, do, o, lse)


def _fox_dkv(t_len, tq, qa, ka, v, do, lse_row, dl_row):
    nq = t_len // tq

    def body(q_ref, k_ref, v_ref, do_ref, lse_ref, dl_ref, dk_ref, dv_ref):
        j = pl.program_id(1)
        kt, vt = k_ref[0], v_ref[0]

        def tile(i, carry, diagonal):
            dk, dv = carry
            start = pl.multiple_of(i * tq, tq)
            qt = q_ref[0, pl.ds(start, tq), :]
            dob = do_ref[0, pl.ds(start, tq), :].astype(BF)
            pt = jnp.exp(_dot_nt(kt, qt) - lse_ref[0, :, pl.ds(start, tq)])
            if diagonal:
                pt = jnp.where(_tri_mask(tq, True), pt, 0.0)
            dst = pt * (_dot_nt(vt, dob) - dl_ref[0, :, pl.ds(start, tq)])
            return dk + _dot(dst.astype(BF), qt), dv + _dot(pt.astype(BF), dob)

        carry = tile(j, (jnp.zeros((tq, QK_W), F32), jnp.zeros((tq, HEAD_DIM), F32)), True)
        dk, dv = lax.fori_loop(j + 1, nq, lambda i, c: tile(i, c, False), carry)
        dk_ref[0] = dk
        dv_ref[0] = dv

    kspec = pl.BlockSpec((1, tq, QK_W), lambda h, j: (h, j, 0))
    vspec = pl.BlockSpec((1, tq, HEAD_DIM), lambda h, j: (h, j, 0))
    rspec = pl.BlockSpec((1, 1, t_len), lambda h, j: (h, 0, 0))
    return pl.pallas_call(
        body, name="fox_dkv", grid=(N_HEADS, nq),
        in_specs=[pl.BlockSpec((1, t_len, QK_W), lambda h, j: (h, 0, 0)), kspec, vspec,
                  pl.BlockSpec((1, t_len, HEAD_DIM), lambda h, j: (h, 0, 0)), rspec, rspec],
        out_specs=[kspec, vspec],
        out_shape=[jax.ShapeDtypeStruct((N_HEADS, t_len, QK_W), F32), jax.ShapeDtypeStruct((N_HEADS, t_len, HEAD_DIM), F32)],
        compiler_params=_cparams(("parallel", "arbitrary")),
    )(qa, ka, v, do, lse_row, dl_row)


def _to_heads(a):
    t_len = a.shape[0]
    return a.reshape(t_len, N_HEADS, HEAD_DIM).transpose(1, 0, 2)


def _attn_operands(qkv, negc):
    t_len = qkv.shape[0]
    q, k, v = (_to_heads(qkv[:, D_GRP * a:D_GRP * (a + 1)]) for a in range(3))
    terms = jnp.stack([negc[:, F_PAD * a:F_PAD * a + N_HEADS] for a in range(3)], axis=-1).transpose(1, 0, 2)
    qa = jnp.concatenate([q * jnp.asarray(HEAD_DIM ** -0.5, BF), jnp.ones((N_HEADS, t_len, 3), BF),
                          jnp.zeros((N_HEADS, t_len, QK_W - ONE_COL), BF)], axis=-1)
    ka = jnp.concatenate([k, terms, jnp.full((N_HEADS, t_len, 1), HEAD_DIM ** 0.5, BF),
                          jnp.zeros((N_HEADS, t_len, QK_W - ONE_COL - 1), BF)], axis=-1)
    return qa, ka, v


def _from_heads(a):
    t_len = a.shape[1]
    return a.transpose(1, 0, 2).reshape(t_len, N_HEADS * HEAD_DIM)


def _mlp_up_fwd(t_len, tm, h, g_pre, w_up):
    def fn(i, n, rows, prevs, nexts, fulls):
        (hv,), (g, w) = rows, fulls
        return [_dot(_rms_fwd(hv, g[...]).astype(BF), w[...])], []

    return _rows_call("mlp_up_fwd", fn, t_len, tm, [h], [g_pre, w_up], [(D_FF, F32)])[0]


def _mlp_down_fwd(t_len, tm, up, h, w_down, g_post):
    def fn(i, n, rows, prevs, nexts, fulls):
        (uv, hv), (w, g) = rows, fulls
        a = jnp.square(jnp.maximum(uv, 0.0))
        ff = _dot(a.astype(BF), w[...])
        return [ff, hv + _rms_fwd(ff, g[...])], []

    return _rows_call("mlp_down_fwd", fn, t_len, tm, [up, h], [w_down, g_post], [(D_MODEL, F32), (D_MODEL, F32)])


def _mlp_bwd_a(t_len, tm, dh, ff, up, w_down, g_post):
    def fn(i, n, rows, prevs, nexts, fulls):
        (dhv, fv, uv), (w, g) = rows, fulls
        dff, dg = _rms_bwd(fv, g[...], dhv)
        dfb = dff.astype(BF)
        dup = _dot_nt(dfb, w[...]) * (2.0 * jnp.maximum(uv, 0.0))
        return [dfb, dup], [dg]

    return _rows_call("mlp_bwd_a", fn, t_len, tm, [dh, ff, up], [w_down, g_post], [(D_MODEL, BF), (D_FF, BF)], [(1, D_MODEL)])


def _mlp_bwd_b(t_len, tm, dh, h, dup, w_up, g_pre):
    def fn(i, n, rows, prevs, nexts, fulls):
        (dhv, hv, duv), (w, g) = rows, fulls
        dhn = _dot_nt(duv, w[...])
        dx, dg = _rms_bwd(hv, g[...], dhn)
        return [dhv + dx, _rms_fwd(hv, g[...])], [dg]

    return _rows_call("mlp_bwd_b", fn, t_len, tm, [dh, h, dup], [w_up, g_pre], [(D_MODEL, F32), (D_MODEL, BF)], [(1, D_MODEL)])


def _ple_fwd(t_len, tm, h, p_i, g_pre, w_gate, w_proj, g_post):
    def fn(i, n, rows, prevs, nexts, fulls):
        (hv, pv), (g, wg, wp, gp) = rows, fulls
        gpre = _dot(_rms_fwd(hv, g[...]).astype(BF), wg[...])
        pe = _dot(pv.astype(BF), wp[...])
        return [gpre, pe, hv + _rms_fwd(pe * _sig(gpre), gp[...])], []

    return _rows_call("ple_fwd", fn, t_len, tm, [h, p_i], [g_pre, w_gate, w_proj, g_post], [(D_MODEL, F32)] * 3)


def _ple_bwd(t_len, tm, dh, h, gpre, pe, p_i, g_pre, w_gate, g_post):
    def fn(i, n, rows, prevs, nexts, fulls):
        (dhv, hv, gv, pev, pv), (g, wg, gp) = rows, fulls
        sg = _sig(gv)
        de, d_gp = _rms_bwd(pev * sg, gp[...], dhv)
        dpe = (de * sg).astype(BF)
        dgate = (de * pev * sg * (1.0 - sg)).astype(BF)
        d_wp = _dot_tn(pv.astype(BF), dpe)
        hn = _rms_fwd(hv, g[...])
        d_wg = _dot_tn(hn.astype(BF), dgate)
        dx, d_g = _rms_bwd(hv, g[...], _dot_nt(dgate, wg[...]))
        return [dhv + dx], [d_wg, d_wp, d_g, d_gp]

    return _rows_call("ple_bwd", fn, t_len, tm, [dh, h, gpre, pe, p_i], [g_pre, w_gate, g_post], [(D_MODEL, F32)],
                      [(D_MODEL, D_MODEL), (D_PLE, D_MODEL), (1, D_MODEL), (1, D_MODEL)])


def _loss_call(t_len, tm, h, target):
    def fn(i, n, rows, prevs, nexts, fulls):
        hv, tv = rows
        err = hv - tv
        part = 0.5 * jnp.sum(jnp.mean(err * err, axis=-1, keepdims=True), axis=0, keepdims=True)
        return [err * (1.0 / D_MODEL)], [jnp.broadcast_to(part, (8, 128))]

    return _rows_call("loss", fn, t_len, tm, [h, target], [], [(D_MODEL, F32)], [(8, 128)])


def _adamw_call(name, w, g, m, v):
    n_l, n_r, n_c = w.shape
    tr = 256 if n_r % 256 == 0 else n_r

    def body(w_ref, g_ref, m_ref, v_ref, d_ref, nm_ref, nv_ref):
        gv = g_ref[...]
        nm = ADAM_B1 * m_ref[...] + (1.0 - ADAM_B1) * gv
        nv = ADAM_B2 * v_ref[...] + (1.0 - ADAM_B2) * jnp.square(gv)
        m_hat = nm / (1.0 - ADAM_B1 ** ADAM_STEP)
        v_hat = nv / (1.0 - ADAM_B2 ** ADAM_STEP)
        d_ref[...] = -ADAM_LR * (m_hat / (jnp.sqrt(v_hat) + ADAM_EPS) + ADAM_WD * w_ref[...])
        nm_ref[...] = nm
        nv_ref[...] = nv

    spec = pl.BlockSpec((1, tr, n_c), lambda l, r: (l, r, 0))
    return pl.pallas_call(
        body, name=name, grid=(n_l, n_r // tr), in_specs=[spec] * 4, out_specs=[spec] * 3,
        out_shape=[jax.ShapeDtypeStruct(w.shape, F32)] * 3,
        compiler_params=_cparams(("parallel", "parallel")),
    )(w, g, m, v)


ANY = pl.BlockSpec(memory_space=pl.ANY)


def _place():
    x, y, c = lax.axis_index("x"), lax.axis_index("y"), lax.axis_index("c")
    chips = [(1 - x, y), (x, 1 - y), (1 - x, 1 - y)]
    return x, y, c, 2 * x + y, chips


def _remote(src, dst, send_sem, recv_sem, dev):
    return pltpu.make_async_remote_copy(src_ref=src, dst_ref=dst, send_sem=send_sem, recv_sem=recv_sem,
                                        device_id=dev, device_id_type=MESH_T)


def _allgather_weights(shards):
    n = len(shards)

    def body(*refs):
        ins, outs = refs[:n], refs[n:2 * n]
        send_sems, recv_sems, loc_sems = refs[2 * n:]
        x, y, c, q, chips = _place()
        me, sib = (x, y, c), (x, y, 1 - c)
        half, ohalf = pl.ds(2 * c, 2), pl.ds(2 * (1 - c), 2)
        locs = [pltpu.make_async_copy(ins[a], outs[a].at[q], loc_sems.at[a]) for a in range(n)]
        for cp in locs:
            cp.start()
        sends = []
        for a in range(n):
            for j, chip in enumerate(chips):
                sends.append(_remote(ins[a].at[half], outs[a].at[q, half], send_sems.at[6 * a + j], recv_sems.at[6 * a + j], (*chip, c)))
                sends[-1].start()
        for a in range(n):
            for j, (cx, cy) in enumerate(chips):
                land = outs[a].at[2 * cx + cy, half]
                _remote(land, land, send_sems.at[6 * a + j], recv_sems.at[6 * a + j], me).wait_recv()
                sends.append(_remote(land, land, send_sems.at[6 * a + 3 + j], recv_sems.at[6 * a + 3 + j], sib))
                sends[-1].start()
        for a in range(n):
            for j, (cx, cy) in enumerate(chips):
                land = outs[a].at[2 * cx + cy, ohalf]
                _remote(land, land, send_sems.at[6 * a + 3 + j], recv_sems.at[6 * a + 3 + j], me).wait_recv()
        for cp in sends:
            cp.wait_send()
        for cp in locs:
            cp.wait()

    return pl.pallas_call(
        body, name="allgather_weights", in_specs=[ANY] * n, out_specs=[ANY] * n,
        out_shape=[jax.ShapeDtypeStruct((N_CHIPS, *s.shape), s.dtype) for s in shards],
        scratch_shapes=[pltpu.SemaphoreType.DMA((6 * n,)), pltpu.SemaphoreType.DMA((6 * n,)), pltpu.SemaphoreType.DMA((n,))],
    )(*shards)


def _pair_exchange(grads):
    n = len(grads)

    def body(*refs):
        ins, outs = refs[:n], refs[n:2 * n]
        send_sems, recv_sems = refs[2 * n:]
        x, y, c, q, chips = _place()
        cps = [_remote(ins[a].at[pl.ds(0, N_CHIPS), pl.ds(2 * (1 - c), 2)], outs[a], send_sems.at[a], recv_sems.at[a], (x, y, 1 - c))
               for a in range(n)]
        for cp in cps:
            cp.start()
        for cp in cps:
            cp.wait()

    return pl.pallas_call(
        body, name="grad_pair_exchange", in_specs=[ANY] * n, out_specs=[ANY] * n,
        out_shape=[jax.ShapeDtypeStruct((N_CHIPS, 2, *g.shape[2:]), g.dtype) for g in grads],
        scratch_shapes=[pltpu.SemaphoreType.DMA((n,)), pltpu.SemaphoreType.DMA((n,))],
    )(*grads)


def _pair_sum(name, g, peer, c_arr):
    _, _, n_r, n_c = g.shape
    tr = 256 if n_r % 256 == 0 else n_r

    def body(c_ref, g_ref, p_ref, o_ref):
        o_ref[...] = (g_ref[...] + p_ref[...]).astype(o_ref.dtype)

    blk = (1, 1, tr, n_c)
    return pl.pallas_call(
        body, name=name,
        grid_spec=pltpu.PrefetchScalarGridSpec(
            num_scalar_prefetch=1, grid=(N_CHIPS, 2, n_r // tr),
            in_specs=[pl.BlockSpec(blk, lambda qi, li, ri, c_ref: (qi, 2 * c_ref[0] + li, ri, 0)),
                      pl.BlockSpec(blk, lambda qi, li, ri, c_ref: (qi, li, ri, 0))],
            out_specs=pl.BlockSpec(blk, lambda qi, li, ri, c_ref: (qi, li, ri, 0))),
        out_shape=jax.ShapeDtypeStruct(peer.shape, BF),
        compiler_params=_cparams(("parallel", "parallel", "parallel")),
    )(c_arr, g, peer)


def _chip_exchange(parts):
    n = len(parts)

    def body(*refs):
        ins, outs = refs[:n], refs[n:2 * n]
        send_sems, recv_sems, loc_sems = refs[2 * n:]
        x, y, c, q, chips = _place()
        locs = [pltpu.make_async_copy(ins[a].at[q], outs[a].at[q], loc_sems.at[a]) for a in range(n)]
        for cp in locs:
            cp.start()
        cps = []
        for a in range(n):
            for j, (cx, cy) in enumerate(chips):
                cps.append(_remote(ins[a].at[2 * cx + cy], outs[a].at[q], send_sems.at[3 * a + j], recv_sems.at[3 * a + j], (cx, cy, c)))
                cps[-1].start()
        for cp in cps:
            cp.wait()
        for cp in locs:
            cp.wait()

    return pl.pallas_call(
        body, name="grad_chip_exchange", in_specs=[ANY] * n, out_specs=[ANY] * n,
        out_shape=[jax.ShapeDtypeStruct(s.shape, s.dtype) for s in parts],
        scratch_shapes=[pltpu.SemaphoreType.DMA((3 * n,)), pltpu.SemaphoreType.DMA((3 * n,)), pltpu.SemaphoreType.DMA((n,))],
    )(*parts)


def _chip_sum(name, r):
    _, _, n_r, n_c = r.shape
    tr = 256 if n_r % 256 == 0 else n_r

    def body(r0, r1, r2, r3, o_ref):
        o_ref[...] = ((r0[0].astype(F32) + r1[0].astype(F32)) + r2[0].astype(F32)) + r3[0].astype(F32)

    return pl.pallas_call(
        body, name=name, grid=(2, n_r // tr),
        in_specs=[pl.BlockSpec((1, 1, tr, n_c), lambda li, ri, s=s: (s, li, ri, 0)) for s in range(N_CHIPS)],
        out_specs=pl.BlockSpec((1, tr, n_c), lambda li, ri: (li, ri, 0)),
        out_shape=jax.ShapeDtypeStruct((2, n_r, n_c), F32),
        compiler_params=_cparams(("parallel", "parallel")),
    )(r, r, r, r)


def _pair_share(halves):
    n = len(halves)

    def body(*refs):
        ins, outs = refs[:n], refs[n:2 * n]
        send_sems, recv_sems, loc_sems = refs[2 * n:]
        x, y, c, q, chips = _place()
        half = pl.ds(2 * c, 2)
        locs = [pltpu.make_async_copy(ins[a], outs[a].at[half], loc_sems.at[a]) for a in range(n)]
        for cp in locs:
            cp.start()
        cps = [_remote(ins[a], outs[a].at[half], send_sems.at[a], recv_sems.at[a], (x, y, 1 - c)) for a in range(n)]
        for cp in cps:
            cp.start()
        for cp in cps:
            cp.wait()
        for cp in locs:
            cp.wait()

    return pl.pallas_call(
        body, name="grad_pair_share", in_specs=[ANY] * n, out_specs=[ANY] * n,
        out_shape=[jax.ShapeDtypeStruct((DEPTH, *h.shape[1:]), h.dtype) for h in halves],
        scratch_shapes=[pltpu.SemaphoreType.DMA((n,)), pltpu.SemaphoreType.DMA((n,)), pltpu.SemaphoreType.DMA((n,))],
    )(*halves)


def _allreduce_small(v):
    n_r = v.shape[0]

    def body(v_ref, o_ref, slots, send_sems, recv_sems):
        x, y, c = lax.axis_index("x"), lax.axis_index("y"), lax.axis_index("c")
        me = 4 * x + 2 * y + c
        slots[me] = v_ref[...]
        cps = []
        for r in range(1, 8):
            px = 1 - x if r & 4 else x
            py = 1 - y if r & 2 else y
            pc = 1 - c if r & 1 else c
            cps.append(_remote(v_ref, slots.at[me], send_sems.at[r - 1], recv_sems.at[r - 1], (px, py, pc)))
            cps[-1].start()
        for cp in cps:
            cp.wait()
        tot = slots[0]
        for d in range(1, 8):
            tot = tot + slots[d]
        o_ref[...] = tot

    return pl.pallas_call(
        body, name="allreduce_small",
        in_specs=[pl.BlockSpec(memory_space=pltpu.VMEM)], out_specs=pl.BlockSpec(memory_space=pltpu.VMEM),
        out_shape=jax.ShapeDtypeStruct(v.shape, F32),
        scratch_shapes=[pltpu.VMEM((8, n_r, 128), F32), pltpu.SemaphoreType.DMA((7,)), pltpu.SemaphoreType.DMA((7,))],
        compiler_params=pltpu.CompilerParams(vmem_limit_bytes=VMEM_LIMIT_BYTES),
    )(v)


def _cols_full(g, l):
    s = g[:, l]
    return s.transpose(1, 0, 2).reshape(s.shape[1], -1)


def _rows_full(g, l):
    s = g[:, l]
    return s.reshape(-1, s.shape[-1])


def _cols_split(full):
    r = full.shape[0]
    return full.reshape(r, N_CHIPS, -1).transpose(1, 0, 2)


def _rows_split(full):
    return full.reshape(N_CHIPS, -1, full.shape[-1])


def _pack(parts):
    flat = []
    for a in parts:
        f = a.reshape(-1).astype(F32)
        flat.append(jnp.pad(f, (0, (-f.shape[0]) % 1024)))
    return jnp.concatenate(flat).reshape(-1, 128)


def _unpack(buf, shapes):
    flat = buf.reshape(-1)
    out, off = [], 0
    for s in shapes:
        size = 1
        for d in s:
            size *= d
        out.append(flat[off:off + size].reshape(s))
        off += size + (-size) % 1024
    return out


def kernel(x, p, g_mix_pre, w_in, b_forget, w_conf_dw, conf_ln_g, conf_ln_b, w_conf_pw, w_sc, w_pool, pool_scale, w_out, g_mix_post, g_mlp_pre, w_up, w_down, g_mlp_post, g_ple_pre, w_ple_gate, w_ple_proj, g_ple_post, loss_target, m_g_mix_pre, m_w_in, m_b_forget, m_w_conf_dw, m_conf_ln_g, m_conf_ln_b, m_w_conf_pw, m_w_sc, m_w_pool, m_pool_scale, m_w_out, m_g_mix_post, m_g_mlp_pre, m_w_up, m_w_down, m_g_mlp_post, m_g_ple_pre, m_w_ple_gate, m_w_ple_proj, m_g_ple_post, v_g_mix_pre, v_w_in, v_b_forget, v_w_conf_dw, v_conf_ln_g, v_conf_ln_b, v_w_conf_pw, v_w_sc, v_w_pool, v_pool_scale, v_w_out, v_g_mix_post, v_g_mlp_pre, v_w_up, v_w_down, v_g_mlp_post, v_g_ple_pre, v_w_ple_gate, v_w_ple_proj, v_g_ple_post):
    names = ['g_mix_pre', 'w_in', 'b_forget', 'w_conf_dw', 'conf_ln_g', 'conf_ln_b', 'w_conf_pw', 'w_sc', 'w_pool', 'pool_scale',
             'w_out', 'g_mix_post', 'g_mlp_pre', 'w_up', 'w_down', 'g_mlp_post', 'g_ple_pre', 'w_ple_gate', 'w_ple_proj', 'g_ple_post']
    env = locals()
    wts = {k: env[k] for k in names}
    mom = {k: env["m_" + k] for k in names}
    var = {k: env["v_" + k] for k in names}

    t_len = x.shape[1]
    tm = min(256, t_len)
    tq = min(512, max(t_len // 2, 128))
    chip = 2 * lax.axis_index("x") + lax.axis_index("y")
    core = lax.axis_index("c")

    big = ['w_in', 'w_conf_pw', 'w_out', 'w_up', 'w_down', 'w_ple_gate', 'w_ple_proj']
    tiny = ['w_conf_dw', 'w_sc']
    gathered = _allgather_weights([wts[k].astype(BF) for k in big] + [wts[k] for k in tiny])
    gat = dict(zip(big + tiny, gathered))

    def layer_weights(l):
        w_full = _cols_full(gat['w_in'], l)
        w_a = jnp.concatenate([w_full[:, :F_OFF], w_full[:, F_OFF + N_HEADS:], w_full[:, F_OFF:F_OFF + N_HEADS],
                               jnp.zeros((D_MODEL, Z_W - D_IN), BF)], axis=1)
        w_bd = jnp.zeros((D_GRP, D_GRP), F32)
        for g in range(4):
            w_bd = lax.dynamic_update_slice(w_bd, wts['w_pool'][l, g], (64 * g, 64 * g))
        row = lambda a: a[l][None, :]
        return dict(
            w_a=w_a, w_dw=jnp.pad(_cols_full(gat['w_conf_dw'], l), ((0, 1), (0, 0))), w_pw=_rows_full(gat['w_conf_pw'], l),
            w_sc=jnp.pad(_cols_full(gat['w_sc'], l), ((0, 5), (0, 0))), w_bd=w_bd.astype(BF),
            w_out=_rows_full(gat['w_out'], l), w_up=_cols_full(gat['w_up'], l), w_down=_rows_full(gat['w_down'], l),
            w_gate=_rows_full(gat['w_ple_gate'], l), w_proj=_cols_full(gat['w_ple_proj'], l),
            b_f=jnp.pad(wts['b_forget'][l], (0, F_PAD - N_HEADS))[None, :],
            ln_g=row(wts['conf_ln_g']), ln_b=row(wts['conf_ln_b']), pool_scale=row(wts['pool_scale']),
            g_mix_pre=row(wts['g_mix_pre']), g_mix_post=row(wts['g_mix_post']), g_mlp_pre=row(wts['g_mlp_pre']),
            g_mlp_post=row(wts['g_mlp_post']), g_ple_pre=row(wts['g_ple_pre']), g_ple_post=row(wts['g_ple_post']))

    lw = [layer_weights(l) for l in range(DEPTH)]

    h = x[0]
    saved = []
    for l in range(DEPTH):
        w = lw[l]
        s = dict(h0=h)
        s['zc'], s['qkv'], s['zs'], s['zp'], s['zf'] = _mix_in_fwd(t_len, tm, h, w['g_mix_pre'], w['w_a'])
        s['cv'], y_conf = _conf_fwd(t_len, tm, s['zc'], w['w_dw'], w['ln_g'], w['ln_b'], w['w_pw'])
        s['qa'], s['ka'], s['v'] = _attn_operands(s['qkv'], _fox_prep(t_len, s['zf'], w['b_f']))
        s['o'], s['lse'] = _fox_fwd(t_len, tq, s['qa'], s['ka'], s['v'])
        y_att = _from_heads(s['o']).astype(BF)
        (y_sc,) = _sconv_fwd(t_len, tm, s['zs'], w['w_sc'])
        (y_pool,) = _pool_fwd(t_len, tm, s['zp'], w['w_bd'], w['pool_scale'])
        s['ys'] = [y_conf, y_att, y_sc, y_pool]
        s['mix'], h = _mix_out_fwd(t_len, tm, s['ys'], h, w['w_out'], w['g_mix_post'])
        s['h1'] = h
        s['up'] = _mlp_up_fwd(t_len, tm, h, w['g_mlp_pre'], w['w_up'])
        s['ff'], h = _mlp_down_fwd(t_len, tm, s['up'], h, w['w_down'], w['g_mlp_post'])
        s['h2'] = h
        s['gpre'], s['pe'], h = _ple_fwd(t_len, tm, h, p[l, 0], w['g_ple_pre'], w['w_gate'], w['w_proj'], w['g_ple_post'])
        saved.append(s)

    dh, loss_part = _loss_call(t_len, tm, h, loss_target[0])

    grads = [None] * DEPTH
    for l in reversed(range(DEPTH)):
        w, s, g = lw[l], saved[l], {}
        dh, g['w_ple_gate'], g['w_ple_proj'], g['g_ple_pre'], g['g_ple_post'] = _ple_bwd(
            t_len, tm, dh, s['h2'], s['gpre'], s['pe'], p[l, 0], w['g_ple_pre'], w['w_gate'], w['g_ple_post'])
        dff, dup, g['g_mlp_post'] = _mlp_bwd_a(t_len, tm, dh, s['ff'], s['up'], w['w_down'], w['g_mlp_post'])
        dh, hn, g['g_mlp_pre'] = _mlp_bwd_b(t_len, tm, dh, s['h1'], dup, w['w_up'], w['g_mlp_pre'])
        tt = min(512, t_len)
        g['w_up'] = _mm_tn("mlp_dw_up", hn, dup, D_MODEL, 1024, tt)
        g['w_down'] = _mm_tn("mlp_dw_down", s['up'], dff, 1024, D_MODEL, tt, pro=lambda u: jnp.square(jnp.maximum(u, 0.0)))
        dy_conf, dy_att, dy_sc, dy_pool, g['w_out'], g['g_mix_post'] = _mix_out_bwd(t_len, tm, dh, s['mix'], s['ys'], w['w_out'], w['g_mix_post'])
        dzc, g['w_conf_dw'], g['conf_ln_g'], g['conf_ln_b'], g['w_conf_pw'] = _conf_bwd(
            t_len, tm, s['zc'], s['cv'], dy_conf, w['w_dw'], w['ln_g'], w['ln_b'], w['w_pw'])
        dzs, g['w_sc'] = _sconv_bwd(t_len, tm, s['zs'], dy_sc, w['w_sc'])
        dzp, d_wbd, g['pool_scale'] = _pool_bwd(t_len, tm, s['zp'], dy_pool, w['w_bd'], w['pool_scale'])
        g['w_pool'] = jnp.stack([d_wbd[64 * a:64 * (a + 1), 64 * a:64 * (a + 1)] for a in range(4)])
        do = _to_heads(dy_att)
        dqa, delta = _fox_dq(t_len, tq, s['qa'], s['ka'], s['v'], do, s['o'], s['lse'])
        dka, dv = _fox_dkv(t_len, tq, s['qa'], s['ka'], s['v'], do,
                           s['lse'].reshape(N_HEADS, 1, t_len), delta.reshape(N_HEADS, 1, t_len))
        dq, dk = dqa[:, :, :HEAD_DIM], dka[:, :, :HEAD_DIM]
        dc_pad = jnp.pad((dqa[:, :, ONE_COL] - dka[:, :, C_COL]).T, ((0, 0), (0, F_PAD - N_HEADS)))
        dzf, d_bf = _fox_post(t_len, dc_pad, s['zf'], w['b_f'])
        g['b_forget'] = d_bf[0, :N_HEADS]
        dqkv = jnp.concatenate([_from_heads(dq), _from_heads(dk), _from_heads(dv)], axis=1).astype(BF)
        dh, xn, dz, g['g_mix_pre'] = _mix_in_bwd(t_len, tm, dh, s['h0'], dzc, dqkv, dzs, dzp, dzf, w['g_mix_pre'], w['w_a'])
        d_wa = _mm_tn("mix_dw_in", xn, dz, D_MODEL, Z_W, tt)
        g['w_in'] = jnp.concatenate([d_wa[:, :F_OFF], d_wa[:, Z_F:Z_F + N_HEADS], d_wa[:, F_OFF:Z_F]], axis=1)
        g['w_conf_dw'] = g['w_conf_dw'][:CONF_K]
        g['w_sc'] = g['w_sc'][:SC_K]
        grads[l] = g
    grad_x = dh[None]

    split = dict(w_in=_cols_split, w_conf_pw=_rows_split, w_out=_rows_split, w_up=_cols_split, w_down=_rows_split,
                 w_ple_gate=_rows_split, w_ple_proj=_cols_split)
    contrib = [jnp.stack([split[k](grads[l][k]) for l in range(DEPTH)], axis=1) for k in big]
    peer = _pair_exchange(contrib)
    c_arr = core.astype(jnp.int32).reshape(1)
    parts = [_pair_sum("grad_pair_sum_" + k, a, b, c_arr) for k, a, b in zip(big, contrib, peer)]
    landed = _chip_exchange(parts)
    halves = [_chip_sum("grad_chip_sum_" + k, r) for k, r in zip(big, landed)]
    reduced = dict(zip(big, _pair_share(halves)))

    small = [k for k in names if k not in big]
    small_shapes = [(DEPTH, *grads[0][k].shape) for k in small]
    packed = _pack([jnp.stack([grads[l][k] for l in range(DEPTH)]) for k in small] + [loss_part])
    summed = _allreduce_small(packed)
    small_sum = _unpack(summed, small_shapes + [(8, 128)])
    loss = small_sum[-1][0, 0]
    for k, a in zip(small, small_sum[:-1]):
        if k in tiny:
            a = lax.dynamic_slice_in_dim(a, chip * 64, 64, axis=2)
        reduced[k] = a.reshape(wts[k].shape)

    delta_w, new_m, new_v = {}, {}, {}
    for k in names:
        shp = wts[k].shape
        as3 = (lambda a: a.reshape(shp[0], -1, shp[-1])) if len(shp) > 2 else (lambda a: a.reshape(1, shp[0], shp[1]))
        d, nm, nv = _adamw_call("adamw_" + k, as3(wts[k]), as3(reduced[k]), as3(mom[k]), as3(var[k]))
        delta_w[k], new_m[k], new_v[k] = d.reshape(shp), nm.reshape(shp), nv.reshape(shp)

    return (loss, grad_x, *[reduced[k] for k in names], *[delta_w[k] for k in names],
            *[new_m[k] for k in names], *[new_v[k] for k in names])
```

```python
import functools

import jax
import jax.numpy as jnp
from jax import lax
from jax.experimental import pallas as pl
from jax.experimental.pallas import tpu as pltpu

F32 = jnp.float32
BF = jnp.bfloat16

DEPTH = 4
D_MODEL = 1024
D_GRP = 256
HEAD_DIM = 64
N_HEADS = 4
CONF_K = 31
SC_K = 3
D_FF = 4096
D_PLE = 256
EPS = 1e-6
N_CHIPS = 4
Z_CONF, Z_QKV, Z_SC, Z_POOL, Z_F = 0, 512, 1280, 2048, 2304
Z_W = 2432
F_PAD = 128
D_IN = 2308
F_OFF = 1280

ADAM_LR, ADAM_B1, ADAM_B2, ADAM_EPS, ADAM_WD, ADAM_STEP = 0.001, 0.9, 0.999, 1e-08, 0.01, 10

VMEM_LIMIT_BYTES = 56 * 1024 * 1024
HALO = 32
NEG = -1e30
MESH_T = pl.DeviceIdType.MESH


def _cparams(sem=None):
    return pltpu.CompilerParams(dimension_semantics=sem, vmem_limit_bytes=VMEM_LIMIT_BYTES)


def _dot(a, b):
    return jnp.dot(a, b, preferred_element_type=F32)


def _dot_nt(a, b):
    return lax.dot_general(a, b, (((1,), (1,)), ((), ())), preferred_element_type=F32)


def _dot_tn(a, b):
    return lax.dot_general(a, b, (((0,), (0,)), ((), ())), preferred_element_type=F32)


def _sig(x):
    return jax.nn.sigmoid(x)


def _rms_fwd(x, g):
    r = lax.rsqrt(jnp.mean(x * x, axis=-1, keepdims=True) + EPS)
    return x * r * g


def _rms_bwd(x, g, dy):
    r = lax.rsqrt(jnp.mean(x * x, axis=-1, keepdims=True) + EPS)
    xh = x * r
    dg = jnp.sum(dy * xh, axis=0, keepdims=True)
    dxh = dy * g
    dx = r * (dxh - xh * jnp.mean(dxh * xh, axis=-1, keepdims=True))
    return dx, dg


def _back(ext, d):
    return ext if d == 0 else pltpu.roll(ext, d, 0)


def _ahead(ext, d):
    return ext if d == 0 else pltpu.roll(ext, ext.shape[0] - d, 0)


def _rows_call(name, fn, t_len, tm, rows, fulls, out_rows, out_accs=(), prevs=(), nexts=()):
    n = t_len // tm
    hb = tm // HALO
    nhb = t_len // HALO
    n_rows, n_prev, n_next, n_full = len(rows), len(prevs), len(nexts), len(fulls)
    in_specs = [pl.BlockSpec((tm, a.shape[1]), lambda i: (i, 0)) for a in rows]
    in_specs += [pl.BlockSpec((HALO, a.shape[1]), lambda i: (jnp.maximum(i * hb - 1, 0), 0)) for a in prevs]
    in_specs += [pl.BlockSpec((HALO, a.shape[1]), lambda i: (jnp.minimum((i + 1) * hb, nhb - 1), 0)) for a in nexts]
    in_specs += [pl.BlockSpec(a.shape, lambda i, nd=a.ndim: (0,) * nd) for a in fulls]
    out_shape = [jax.ShapeDtypeStruct((t_len, c), dt) for c, dt in out_rows]
    out_shape += [jax.ShapeDtypeStruct(s, F32) for s in out_accs]
    out_specs = [pl.BlockSpec((tm, c), lambda i: (i, 0)) for c, _ in out_rows]
    out_specs += [pl.BlockSpec(s, lambda i, nd=len(s): (0,) * nd) for s in out_accs]
    n_in = n_rows + n_prev + n_next + n_full
    n_ro = len(out_rows)

    def body(*refs):
        i = pl.program_id(0)
        ins, outs = refs[:n_in], refs[n_in:]
        rv = [r[...] for r in ins[:n_rows]]
        pv = [r[...] for r in ins[n_rows:n_rows + n_prev]]
        nv = [r[...] for r in ins[n_rows + n_prev:n_rows + n_prev + n_next]]
        fv = list(ins[n_rows + n_prev + n_next:])
        ro, ao = fn(i, n, rv, pv, nv, fv)
        for r, v in zip(outs[:n_ro], ro):
            r[...] = v.astype(r.dtype)
        if out_accs:
            acc = outs[n_ro:]

            @pl.when(i == 0)
            def _():
                for r in acc:
                    r[...] = jnp.zeros(r.shape, r.dtype)

            for r, v in zip(acc, ao):
                r[...] += v

    res = pl.pallas_call(
        body, name=name, grid=(n,), in_specs=in_specs, out_specs=out_specs, out_shape=out_shape,
        compiler_params=_cparams(("arbitrary",)),
    )(*rows, *prevs, *nexts, *fulls)
    return res


def _mm_tn(name, x, y, tk, tn, tt, pro=None):
    t_len, k_dim = x.shape
    n_dim = y.shape[1]

    def body(x_ref, y_ref, o_ref):
        @pl.when(pl.program_id(2) == 0)
        def _():
            o_ref[...] = jnp.zeros(o_ref.shape, o_ref.dtype)

        xv = x_ref[...]
        if pro is not None:
            xv = pro(xv)
        o_ref[...] += _dot_tn(xv.astype(BF), y_ref[...].astype(BF))

    return pl.pallas_call(
        body, name=name, grid=(k_dim // tk, n_dim // tn, t_len // tt),
        in_specs=[pl.BlockSpec((tt, tk), lambda a, b, t: (t, a)), pl.BlockSpec((tt, tn), lambda a, b, t: (t, b))],
        out_specs=pl.BlockSpec((tk, tn), lambda a, b, t: (a, b)),
        out_shape=jax.ShapeDtypeStruct((k_dim, n_dim), F32),
        compiler_params=_cparams(("parallel", "parallel", "arbitrary")),
    )(x, y)


def _mix_in_fwd(t_len, tm, h, g_pre, w_a):
    def fn(i, n, rows, prevs, nexts, fulls):
        (hv,), (g, w) = rows, fulls
        z = _dot(_rms_fwd(hv, g[...]).astype(BF), w[...])
        return [z[:, Z_CONF:Z_QKV], z[:, Z_QKV:Z_SC], z[:, Z_SC:Z_POOL], z[:, Z_POOL:Z_F], z[:, Z_F:Z_W]], []

    return _rows_call("mix_in_fwd", fn, t_len, tm, [h], [g_pre, w_a],
                      [(512, F32), (768, BF), (768, F32), (256, F32), (F_PAD, F32)])


def _mix_in_bwd(t_len, tm, dh, h, dzc, dqkv, dzs, dzp, dzf, g_pre, w_a):
    def fn(i, n, rows, prevs, nexts, fulls):
        dhv, hv, a, b, c, d, e = rows
        g, w = fulls
        dz = jnp.concatenate([a, b, c, d, e], axis=1)
        dxn = _dot_nt(dz, w[...])
        dx, dg = _rms_bwd(hv, g[...], dxn)
        xn = _rms_fwd(hv, g[...])
        return [dhv + dx, xn, dz], [dg]

    return _rows_call("mix_in_bwd", fn, t_len, tm, [dh, h, dzc, dqkv, dzs, dzp, dzf], [g_pre, w_a],
                      [(D_MODEL, F32), (D_MODEL, BF), (Z_W, BF)], [(1, D_MODEL)])


def _glu_ext(i, zc, zc_prev):
    ext = jnp.concatenate([zc_prev, zc], axis=0)
    u = ext[:, :D_GRP] * _sig(ext[:, D_GRP:])
    row = lax.broadcasted_iota(jnp.int32, u.shape, 0)
    return jnp.where((row >= HALO) | (i > 0), u, 0.0)


def _conf_fwd(t_len, tm, zc, w_dw, ln_g, ln_b, w_pw):
    def fn(i, n, rows, prevs, nexts, fulls):
        (zv,), (zp,) = rows, prevs
        wdw, lg, lb, wpw = fulls
        u = _glu_ext(i, zv, zp)
        cv = jnp.zeros((tm, D_GRP), F32)
        for k in range(CONF_K):
            cv = cv + wdw[k:k + 1, :] * _back(u, CONF_K - 1 - k)[HALO:, :]
        mu = jnp.mean(cv, axis=-1, keepdims=True)
        xc = cv - mu
        ln = xc * lax.rsqrt(jnp.mean(xc * xc, axis=-1, keepdims=True) + EPS) * lg[...] + lb[...]
        s = ln * _sig(ln)
        return [cv, _dot(s.astype(BF), wpw[...])], []

    return _rows_call("conf_fwd", fn, t_len, tm, [zc], [w_dw, ln_g, ln_b, w_pw], [(D_GRP, F32), (D_GRP, BF)], prevs=[zc])


def _conf_bwd(t_len, tm, zc, cv, dy, w_dw, ln_g, ln_b, w_pw):
    def fn(i, n, rows, prevs, nexts, fulls):
        zv, cvv, dyv = rows
        (zp,) = prevs
        cvn, dyn = nexts
        wdw, lg, lb, wpw = fulls
        cve = jnp.concatenate([cvv, cvn], axis=0)
        dye = jnp.concatenate([dyv, dyn], axis=0)
        mu = jnp.mean(cve, axis=-1, keepdims=True)
        xc = cve - mu
        rs = lax.rsqrt(jnp.mean(xc * xc, axis=-1, keepdims=True) + EPS)
        xh = xc * rs
        ln = xh * lg[...] + lb[...]
        sg = _sig(ln)
        s = ln * sg
        ds = _dot_nt(dye.astype(BF), wpw[...])
        dln = ds * (sg * (1.0 + ln * (1.0 - sg)))
        dxh = dln * lg[...]
        dcv = rs * (dxh - jnp.mean(dxh, axis=-1, keepdims=True) - xh * jnp.mean(dxh * xh, axis=-1, keepdims=True))
        row = lax.broadcasted_iota(jnp.int32, dcv.shape, 0)
        dcv = jnp.where((row < tm) | (i < n - 1), dcv, 0.0)
        d_lg = jnp.sum((dln * xh)[:tm], axis=0, keepdims=True)
        d_lb = jnp.sum(dln[:tm], axis=0, keepdims=True)
        d_wpw = _dot_tn(s[:tm].astype(BF), dyv.astype(BF))
        u = _glu_ext(i, zv, zp)
        dcv_cur = dcv[:tm]
        du = jnp.zeros((tm, D_GRP), F32)
        d_wdw = jnp.zeros((32, D_GRP), F32)
        krow = lax.broadcasted_iota(jnp.int32, (32, D_GRP), 0)
        for k in range(CONF_K):
            d = CONF_K - 1 - k
            du = du + wdw[k:k + 1, :] * _ahead(dcv, d)[:tm, :]
            tap = _back(u, d)[HALO:, :]
            d_wdw = d_wdw + jnp.where(krow == k, jnp.sum(dcv_cur * tap, axis=0, keepdims=True), 0.0)
        a, b = zv[:, :D_GRP], zv[:, D_GRP:]
        sb = _sig(b)
        dz = jnp.concatenate([du * sb, du * a * sb * (1.0 - sb)], axis=1)
        return [dz], [d_wdw, d_lg, d_lb, d_wpw]

    return _rows_call("conf_bwd", fn, t_len, tm, [zc, cv, dy], [w_dw, ln_g, ln_b, w_pw], [(512, BF)],
                      [(32, D_GRP), (1, D_GRP), (1, D_GRP), (D_GRP, D_GRP)], prevs=[zc], nexts=[cv, dy])


def _sc_ext(i, zs, zs_prev):
    ext = jnp.concatenate([zs_prev, zs], axis=0)
    e = ext[:, 2 * D_GRP:] * ext[:, :D_GRP]
    row = lax.broadcasted_iota(jnp.int32, e.shape, 0)
    return jnp.where((row >= HALO) | (i > 0), e, 0.0)


def _sconv_fwd(t_len, tm, zs, w_sc):
    def fn(i, n, rows, prevs, nexts, fulls):
        (zv,), (zp,), (w,) = rows, prevs, fulls
        e = _sc_ext(i, zv, zp)
        cv = jnp.zeros((tm, D_GRP), F32)
        for k in range(SC_K):
            cv = cv + w[k:k + 1, :] * _back(e, SC_K - 1 - k)[HALO:, :]
        return [zv[:, D_GRP:2 * D_GRP] * cv], []

    return _rows_call("sconv_fwd", fn, t_len, tm, [zs], [w_sc], [(D_GRP, BF)], prevs=[zs])


def _sconv_bwd(t_len, tm, zs, dy, w_sc):
    def fn(i, n, rows, prevs, nexts, fulls):
        zv, dyv = rows
        (zp,) = prevs
        zn, dyn = nexts
        (w,) = fulls
        e = _sc_ext(i, zv, zp)
        taps = [_back(e, SC_K - 1 - k)[HALO:, :] for k in range(SC_K)]
        cv = w[0:1, :] * taps[0] + w[1:2, :] * taps[1] + w[2:3, :] * taps[2]
        bg = zv[:, D_GRP:2 * D_GRP]
        dcv = jnp.concatenate([dyv * bg, dyn * zn[:, D_GRP:2 * D_GRP]], axis=0)
        row = lax.broadcasted_iota(jnp.int32, dcv.shape, 0)
        dcv = jnp.where((row < tm) | (i < n - 1), dcv, 0.0)
        de = jnp.zeros((tm, D_GRP), F32)
        d_w = jnp.zeros((8, D_GRP), F32)
        krow = lax.broadcasted_iota(jnp.int32, (8, D_GRP), 0)
        for k in range(SC_K):
            de = de + w[k:k + 1, :] * _ahead(dcv, SC_K - 1 - k)[:tm, :]
            d_w = d_w + jnp.where(krow == k, jnp.sum(dcv[:tm] * taps[k], axis=0, keepdims=True), 0.0)
        dz = jnp.concatenate([de * zv[:, 2 * D_GRP:], dyv * cv, de * zv[:, :D_GRP]], axis=1)
        return [dz], [d_w]

    return _rows_call("sconv_bwd", fn, t_len, tm, [zs, dy], [w_sc], [(768, BF)], [(8, D_GRP)], prevs=[zs], nexts=[zs, dy])


def _pool_window(shape):
    grp = lax.broadcasted_iota(jnp.int32, shape, 1) // 64
    return grp, jnp.where(grp == 0, 2.0, jnp.where(grp == 1, 4.0, jnp.where(grp == 2, 8.0, 16.0)))


def _pool_d(i, tm, zv, zp):
    ext = jnp.concatenate([zp, zv], axis=0)
    row = lax.broadcasted_iota(jnp.int32, ext.shape, 0)
    ext = jnp.where((row >= HALO) | (i > 0), ext, 0.0)
    s2 = ext + _back(ext, 1)
    s4 = s2 + _back(s2, 2)
    s8 = s4 + _back(s4, 4)
    s16 = s8 + _back(s8, 8)
    grp, win = _pool_window((tm, D_GRP))
    sel = jnp.where(grp == 0, s2[HALO:], jnp.where(grp == 1, s4[HALO:], jnp.where(grp == 2, s8[HALO:], s16[HALO:])))
    pos = (i * tm + lax.broadcasted_iota(jnp.int32, (tm, D_GRP), 0) + 1).astype(F32)
    return sel / jnp.minimum(pos, win) - zv


def _pool_fwd(t_len, tm, zpool, w_bd, scale):
    def fn(i, n, rows, prevs, nexts, fulls):
        (zv,), (zp,) = rows, prevs
        w, sc = fulls
        d = _pool_d(i, tm, zv, zp)
        return [_dot(d.astype(BF), w[...]) * sc[...]], []

    return _rows_call("pool_fwd", fn, t_len, tm, [zpool], [w_bd, scale], [(D_GRP, BF)], prevs=[zpool])


def _pool_bwd(t_len, tm, zpool, dy, w_bd, scale):
    def fn(i, n, rows, prevs, nexts, fulls):
        zv, dyv = rows
        (zp,) = prevs
        (dyn,) = nexts
        w, sc = fulls
        d = _pool_d(i, tm, zv, zp)
        lin = _dot(d.astype(BF), w[...])
        d_sc = jnp.sum(dyv * lin, axis=0, keepdims=True)
        dye = jnp.concatenate([dyv, dyn], axis=0) * sc[...]
        d_w = _dot_tn(d.astype(BF), dye[:tm].astype(BF))
        dd = _dot_nt(dye.astype(BF), w[...])
        row = lax.broadcasted_iota(jnp.int32, dd.shape, 0)
        dd = jnp.where((row < tm) | (i < n - 1), dd, 0.0)
        grp, win = _pool_window(dd.shape)
        pos = (i * tm + row + 1).astype(F32)
        ddc = dd / jnp.minimum(pos, win)
        f2 = ddc + _ahead(ddc, 1)
        f4 = f2 + _ahead(f2, 2)
        f8 = f4 + _ahead(f4, 4)
        f16 = f8 + _ahead(f8, 8)
        sel = jnp.where(grp == 0, f2, jnp.where(grp == 1, f4, jnp.where(grp == 2, f8, f16)))
        return [(sel - dd)[:tm]], [d_w, d_sc]

    return _rows_call("pool_bwd", fn, t_len, tm, [zpool, dy], [w_bd, scale], [(D_GRP, BF)],
                      [(D_GRP, D_GRP), (1, D_GRP)], prevs=[zpool], nexts=[dy])


def _mix_out_fwd(t_len, tm, ys, h, w_out, g_post):
    def fn(i, n, rows, prevs, nexts, fulls):
        y0, y1, y2, y3, hv = rows
        w, g = fulls
        mix = _dot(jnp.concatenate([y0, y1, y2, y3], axis=1), w[...])
        return [mix, hv + _rms_fwd(mix, g[...])], []

    return _rows_call("mix_out_fwd", fn, t_len, tm, [*ys, h], [w_out, g_post], [(D_MODEL, F32), (D_MODEL, F32)])


def _mix_out_bwd(t_len, tm, dh, mix, ys, w_out, g_post):
    def fn(i, n, rows, prevs, nexts, fulls):
        dhv, mv, y0, y1, y2, y3 = rows
        w, g = fulls
        dmix, dg = _rms_bwd(mv, g[...], dhv)
        dmb = dmix.astype(BF)
        dcat = _dot_nt(dmb, w[...])
        d_w = _dot_tn(jnp.concatenate([y0, y1, y2, y3], axis=1), dmb)
        return [dcat[:, :256], dcat[:, 256:512], dcat[:, 512:768], dcat[:, 768:]], [d_w, dg]

    return _rows_call("mix_out_bwd", fn, t_len, tm, [dh, mix, *ys], [w_out, g_post],
                      [(D_GRP, F32)] * 4, [(D_MODEL, D_MODEL), (1, D_MODEL)])


def _log_sigmoid(x):
    return jnp.minimum(x, 0.0) - jnp.log(1.0 + jnp.exp(-jnp.abs(x)))


def _fox_prep(t_len, zf, b_f):
    blk = min(256, t_len)

    def body(zf_ref, b_ref, c_ref):
        tri = (lax.broadcasted_iota(jnp.int32, (blk, blk), 0) >= lax.broadcasted_iota(jnp.int32, (blk, blk), 1)).astype(F32)

        def step(bi, carry):
            rows = pl.ds(pl.multiple_of(bi * blk, blk), blk)
            lf = _log_sigmoid(zf_ref[rows, :] + b_ref[...])
            cs = jnp.dot(tri, lf, precision=lax.Precision.HIGHEST, preferred_element_type=F32) + carry
            neg = -cs
            hi = neg.astype(BF)
            rest = neg - hi.astype(F32)
            mid = rest.astype(BF)
            c_ref[rows, :] = jnp.concatenate([hi, mid, (rest - mid.astype(F32)).astype(BF)], axis=1)
            return cs[blk - 1:blk, :]

        lax.fori_loop(0, t_len // blk, step, jnp.zeros((1, F_PAD), F32))

    return pl.pallas_call(body, name="fox_prep", out_shape=jax.ShapeDtypeStruct((t_len, 3 * F_PAD), BF),
                          compiler_params=_cparams())(zf, b_f)


def _fox_post(t_len, dc, zf, b_f):
    blk = min(256, t_len)
    nb = t_len // blk

    def body(dc_ref, zf_ref, b_ref, dz_ref, db_ref):
        tri = (lax.broadcasted_iota(jnp.int32, (blk, blk), 0) <= lax.broadcasted_iota(jnp.int32, (blk, blk), 1)).astype(F32)

        def step(s, carry):
            car, db = carry
            rows = pl.ds(pl.multiple_of((nb - 1 - s) * blk, blk), blk)
            dlf = jnp.dot(tri, dc_ref[rows, :], precision=lax.Precision.HIGHEST, preferred_element_type=F32) + car
            dz = dlf * _sig(-(zf_ref[rows, :] + b_ref[...]))
            dz_ref[rows, :] = dz.astype(dz_ref.dtype)
            return dlf[0:1, :], db + jnp.sum(dz, axis=0, keepdims=True)

        _, db = lax.fori_loop(0, nb, step, (jnp.zeros((1, F_PAD), F32), jnp.zeros((1, F_PAD), F32)))
        db_ref[...] = db

    return pl.pallas_call(body, name="fox_post",
                          out_shape=(jax.ShapeDtypeStruct((t_len, F_PAD), BF), jax.ShapeDtypeStruct((1, F_PAD), F32)),
                          compiler_params=_cparams())(dc, zf, b_f)


QK_W = 128
C_COL = HEAD_DIM
ONE_COL = HEAD_DIM + 3


def _tri_mask(tq, key_rows):
    r = lax.broadcasted_iota(jnp.int32, (tq, tq), 0)
    c = lax.broadcasted_iota(jnp.int32, (tq, tq), 1)
    return (r <= c) if key_rows else (r >= c)


def _fox_fwd(t_len, tq, qa, ka, v):
    nq = t_len // tq

    def body(q_ref, k_ref, v_ref, o_ref, lse_ref):
        i = pl.program_id(1)
        q = q_ref[0]

        def tile(j, carry, diagonal):
            m, l, acc = carry
            rows = pl.ds(pl.multiple_of(j * tq, tq), tq)
            s = _dot_nt(q, k_ref[0, rows, :])
            if diagonal:
                s = jnp.where(_tri_mask(tq, False), s, NEG)
            m_new = jnp.maximum(m, jnp.max(s, axis=-1, keepdims=True))
            alpha = jnp.exp(m - m_new)
            p = jnp.exp(s - m_new)
            l = alpha * l + jnp.sum(p, axis=-1, keepdims=True)
            acc = alpha * acc + _dot(p.astype(BF), v_ref[0, rows, :])
            return m_new, l, acc

        init = (jnp.full((tq, 1), NEG, F32), jnp.zeros((tq, 1), F32), jnp.zeros((tq, HEAD_DIM), F32))
        carry = lax.fori_loop(0, i, lambda j, c: tile(j, c, False), init)
        m, l, acc = tile(i, carry, True)
        o_ref[0] = acc / l
        lse_ref[0] = m + jnp.log(l)

    return pl.pallas_call(
        body, name="fox_fwd", grid=(N_HEADS, nq),
        in_specs=[pl.BlockSpec((1, tq, QK_W), lambda h, i: (h, i, 0)),
                  pl.BlockSpec((1, t_len, QK_W), lambda h, i: (h, 0, 0)),
                  pl.BlockSpec((1, t_len, HEAD_DIM), lambda h, i: (h, 0, 0))],
        out_specs=[pl.BlockSpec((1, tq, HEAD_DIM), lambda h, i: (h, i, 0)), pl.BlockSpec((1, tq, 1), lambda h, i: (h, i, 0))],
        out_shape=[jax.ShapeDtypeStruct((N_HEADS, t_len, HEAD_DIM), F32), jax.ShapeDtypeStruct((N_HEADS, t_len, 1), F32)],
        compiler_params=_cparams(("parallel", "arbitrary")),
    )(qa, ka, v)


def _fox_bwd(t_len, tq, qa, ka, v, do, o, lse_row):
    nq = t_len // tq

    def body(q_ref, k_ref, v_ref, do_ref, o_ref, lse_ref, dq_ref, rs_ref, dk_ref, cs_ref, dv_ref, acc_s, dl_s):
        j = pl.program_id(1)

        @pl.when(j == 0)
        def _():
            acc_s[...] = jnp.zeros(acc_s.shape, F32)
            dl_s[...] = lax.dot_general(jnp.ones((8, HEAD_DIM), F32), do_ref[0] * o_ref[0], (((1,), (1,)), ((), ())),
                                        precision=lax.Precision.HIGHEST, preferred_element_type=F32)

        kt, vt = k_ref[0], v_ref[0]

        def tile(i, carry, diagonal):
            dk, dv = carry
            rows = pl.ds(pl.multiple_of(i * tq, tq), tq)
            qt = q_ref[0, rows, :]
            dob = do_ref[0, rows, :].astype(BF)
            pt = jnp.exp(_dot_nt(kt, qt) - lse_ref[0, :, rows])
            if diagonal:
                pt = jnp.where(_tri_mask(tq, True), pt, 0.0)
            dst = (pt * (_dot_nt(vt, dob) - dl_s[0:1, rows])).astype(BF)
            acc_s[rows, :] += _dot_tn(dst, kt)
            return dk + _dot(dst, qt), dv + _dot(pt.astype(BF), dob)

        carry = tile(j, (jnp.zeros((tq, QK_W), F32), jnp.zeros((tq, HEAD_DIM), F32)), True)
        dk, dv = lax.fori_loop(j + 1, nq, lambda i, c: tile(i, c, False), carry)
        dk_ref[0] = dk[:, :HEAD_DIM]
        cs_ref[0] = dk[:, C_COL:C_COL + 1]
        dv_ref[0] = dv

        @pl.when(j == nq - 1)
        def _():
            dq_ref[0] = acc_s[:, :HEAD_DIM] * (HEAD_DIM ** -0.5)
            rs_ref[0] = acc_s[:, ONE_COL:ONE_COL + 1] * (HEAD_DIM ** -0.5)

    head = lambda w: pl.BlockSpec((1, t_len, w), lambda h, j: (h, 0, 0))
    tile_spec = lambda w: pl.BlockSpec((1, tq, w), lambda h, j: (h, j, 0))
    return pl.pallas_call(
        body, name="fox_bwd", grid=(N_HEADS, nq),
        in_specs=[head(QK_W), tile_spec(QK_W), tile_spec(HEAD_DIM), head(HEAD_DIM), head(HEAD_DIM),
                  pl.BlockSpec((1, 1, t_len), lambda h, j: (h, 0, 0))],
        out_specs=[head(HEAD_DIM), head(1), tile_spec(HEAD_DIM), tile_spec(1), tile_spec(HEAD_DIM)],
        out_shape=[jax.ShapeDtypeStruct((N_HEADS, t_len, w), F32) for w in (HEAD_DIM, 1, HEAD_DIM, 1, HEAD_DIM)],
        scratch_shapes=[pltpu.VMEM((t_len, QK_W), F32), pltpu.VMEM((8, t_len), F32)],
        compiler_params=_cparams(("parallel", "arbitrary")),
    )(qa, ka, v, do, o, lse_row)


def _to_heads(a):
    t_len = a.shape[0]
    return a.reshape(t_len, N_HEADS, HEAD_DIM).transpose(1, 0, 2)


def _attn_operands(qkv, negc):
    t_len = qkv.shape[0]
    q, k, v = (_to_heads(qkv[:, D_GRP * a:D_GRP * (a + 1)]) for a in range(3))
    terms = jnp.stack([negc[:, F_PAD * a:F_PAD * a + N_HEADS] for a in range(3)], axis=-1).transpose(1, 0, 2)
    qa = jnp.concatenate([q * jnp.asarray(HEAD_DIM ** -0.5, BF), jnp.ones((N_HEADS, t_len, 3), BF),
                          jnp.zeros((N_HEADS, t_len, QK_W - ONE_COL), BF)], axis=-1)
    ka = jnp.concatenate([k, terms, jnp.full((N_HEADS, t_len, 1), HEAD_DIM ** 0.5, BF),
                          jnp.zeros((N_HEADS, t_len, QK_W - ONE_COL - 1), BF)], axis=-1)
    return qa, ka, v


def _from_heads(a):
    t_len = a.shape[1]
    return a.transpose(1, 0, 2).reshape(t_len, N_HEADS * HEAD_DIM)


def _mlp_up_fwd(t_len, tm, h, g_pre, w_up):
    def fn(i, n, rows, prevs, nexts, fulls):
        (hv,), (g, w) = rows, fulls
        return [_dot(_rms_fwd(hv, g[...]).astype(BF), w[...])], []

    return _rows_call("mlp_up_fwd", fn, t_len, tm, [h], [g_pre, w_up], [(D_FF, F32)])[0]


def _mlp_down_fwd(t_len, tm, up, h, w_down, g_post):
    def fn(i, n, rows, prevs, nexts, fulls):
        (uv, hv), (w, g) = rows, fulls
        a = jnp.square(jnp.maximum(uv, 0.0))
        ff = _dot(a.astype(BF), w[...])
        return [ff, hv + _rms_fwd(ff, g[...])], []

    return _rows_call("mlp_down_fwd", fn, t_len, tm, [up, h], [w_down, g_post], [(D_MODEL, F32), (D_MODEL, F32)])


def _mlp_bwd_a(t_len, tm, dh, ff, up, w_down, g_post):
    def fn(i, n, rows, prevs, nexts, fulls):
        (dhv, fv, uv), (w, g) = rows, fulls
        dff, dg = _rms_bwd(fv, g[...], dhv)
        dfb = dff.astype(BF)
        dup = _dot_nt(dfb, w[...]) * (2.0 * jnp.maximum(uv, 0.0))
        return [dfb, dup], [dg]

    return _rows_call("mlp_bwd_a", fn, t_len, tm, [dh, ff, up], [w_down, g_post], [(D_MODEL, BF), (D_FF, BF)], [(1, D_MODEL)])


def _mlp_bwd_b(t_len, tm, dh, h, dup, w_up, g_pre):
    def fn(i, n, rows, prevs, nexts, fulls):
        (dhv, hv, duv), (w, g) = rows, fulls
        dhn = _dot_nt(duv, w[...])
        dx, dg = _rms_bwd(hv, g[...], dhn)
        return [dhv + dx, _rms_fwd(hv, g[...])], [dg]

    return _rows_call("mlp_bwd_b", fn, t_len, tm, [dh, h, dup], [w_up, g_pre], [(D_MODEL, F32), (D_MODEL, BF)], [(1, D_MODEL)])


def _ple_fwd(t_len, tm, h, p_i, g_pre, w_gate, w_proj, g_post):
    def fn(i, n, rows, prevs, nexts, fulls):
        (hv, pv), (g, wg, wp, gp) = rows, fulls
        gpre = _dot(_rms_fwd(hv, g[...]).astype(BF), wg[...])
        pe = _dot(pv.astype(BF), wp[...])
        return [gpre, pe, hv + _rms_fwd(pe * _sig(gpre), gp[...])], []

    return _rows_call("ple_fwd", fn, t_len, tm, [h, p_i], [g_pre, w_gate, w_proj, g_post], [(D_MODEL, F32)] * 3)


def _ple_bwd(t_len, tm, dh, h, gpre, pe, p_i, g_pre, w_gate, g_post):
    def fn(i, n, rows, prevs, nexts, fulls):
        (dhv, hv, gv, pev, pv), (g, wg, gp) = rows, fulls
        sg = _sig(gv)
        de, d_gp = _rms_bwd(pev * sg, gp[...], dhv)
        dpe = (de * sg).astype(BF)
        dgate = (de * pev * sg * (1.0 - sg)).astype(BF)
        d_wp = _dot_tn(pv.astype(BF), dpe)
        hn = _rms_fwd(hv, g[...])
        d_wg = _dot_tn(hn.astype(BF), dgate)
        dx, d_g = _rms_bwd(hv, g[...], _dot_nt(dgate, wg[...]))
        return [dhv + dx], [d_wg, d_wp, d_g, d_gp]

    return _rows_call("ple_bwd", fn, t_len, tm, [dh, h, gpre, pe, p_i], [g_pre, w_gate, g_post], [(D_MODEL, F32)],
                      [(D_MODEL, D_MODEL), (D_PLE, D_MODEL), (1, D_MODEL), (1, D_MODEL)])


def _loss_call(t_len, tm, h, target):
    def fn(i, n, rows, prevs, nexts, fulls):
        hv, tv = rows
        err = hv - tv
        part = 0.5 * jnp.sum(jnp.mean(err * err, axis=-1, keepdims=True), axis=0, keepdims=True)
        return [err * (1.0 / D_MODEL)], [jnp.broadcast_to(part, (8, 128))]

    return _rows_call("loss", fn, t_len, tm, [h, target], [], [(D_MODEL, F32)], [(8, 128)])


def _adamw_call(name, w, g, m, v):
    n_l, n_r, n_c = w.shape
    tr = 256 if n_r % 256 == 0 else n_r

    def body(w_ref, g_ref, m_ref, v_ref, d_ref, nm_ref, nv_ref):
        gv = g_ref[...]
        nm = ADAM_B1 * m_ref[...] + (1.0 - ADAM_B1) * gv
        nv = ADAM_B2 * v_ref[...] + (1.0 - ADAM_B2) * jnp.square(gv)
        m_hat = nm / (1.0 - ADAM_B1 ** ADAM_STEP)
        v_hat = nv / (1.0 - ADAM_B2 ** ADAM_STEP)
        d_ref[...] = -ADAM_LR * (m_hat / (jnp.sqrt(v_hat) + ADAM_EPS) + ADAM_WD * w_ref[...])
        nm_ref[...] = nm
        nv_ref[...] = nv

    spec = pl.BlockSpec((1, tr, n_c), lambda l, r: (l, r, 0))
    return pl.pallas_call(
        body, name=name, grid=(n_l, n_r // tr), in_specs=[spec] * 4, out_specs=[spec] * 3,
        out_shape=[jax.ShapeDtypeStruct(w.shape, F32)] * 3,
        compiler_params=_cparams(("parallel", "parallel")),
    )(w, g, m, v)


ANY = pl.BlockSpec(memory_space=pl.ANY)


def _place():
    x, y, c = lax.axis_index("x"), lax.axis_index("y"), lax.axis_index("c")
    chips = [(1 - x, y), (x, 1 - y), (1 - x, 1 - y)]
    return x, y, c, 2 * x + y, chips


def _remote(src, dst, send_sem, recv_sem, dev):
    return pltpu.make_async_remote_copy(src_ref=src, dst_ref=dst, send_sem=send_sem, recv_sem=recv_sem,
                                        device_id=dev, device_id_type=MESH_T)


def _allgather_weights(shards):
    n = len(shards)

    def body(*refs):
        ins, outs = refs[:n], refs[n:2 * n]
        send_sems, recv_sems = refs[2 * n:]
        x, y, c, q, chips = _place()
        me, sib = (x, y, c), (x, y, 1 - c)
        half, ohalf = pl.ds(2 * c, 2), pl.ds(2 * (1 - c), 2)
        sends = []
        for a in range(n):
            for j, chip in enumerate(chips):
                sends.append(_remote(ins[a].at[half], outs[a].at[j, half], send_sems.at[6 * a + j], recv_sems.at[6 * a + j], (*chip, c)))
                sends[-1].start()
        for a in range(n):
            for j in range(3):
                land = outs[a].at[j, half]
                _remote(land, land, send_sems.at[6 * a + j], recv_sems.at[6 * a + j], me).wait_recv()
                sends.append(_remote(land, land, send_sems.at[6 * a + 3 + j], recv_sems.at[6 * a + 3 + j], sib))
                sends[-1].start()
        for a in range(n):
            for j in range(3):
                land = outs[a].at[j, ohalf]
                _remote(land, land, send_sems.at[6 * a + 3 + j], recv_sems.at[6 * a + 3 + j], me).wait_recv()
        for cp in sends:
            cp.wait_send()

    return pl.pallas_call(
        body, name="allgather_weights", in_specs=[ANY] * n, out_specs=[ANY] * n,
        out_shape=[jax.ShapeDtypeStruct((3, *s.shape), s.dtype) for s in shards],
        scratch_shapes=[pltpu.SemaphoreType.DMA((6 * n,)), pltpu.SemaphoreType.DMA((6 * n,))],
    )(*shards)


def _by_chip(own, others, chip):
    by_mask = jnp.stack([own, others[1], others[0], others[2]])
    return jnp.stack([lax.dynamic_index_in_dim(by_mask, jnp.bitwise_xor(chip, r), 0, keepdims=False) for r in range(N_CHIPS)])


def _pair_exchange(grads):
    n = len(grads)

    def body(*refs):
        ins, outs = refs[:n], refs[n:2 * n]
        send_sems, recv_sems = refs[2 * n:]
        x, y, c, q, chips = _place()
        cps = [_remote(ins[a].at[pl.ds(0, N_CHIPS), pl.ds(2 * (1 - c), 2)], outs[a], send_sems.at[a], recv_sems.at[a], (x, y, 1 - c))
               for a in range(n)]
        for cp in cps:
            cp.start()
        for cp in cps:
            cp.wait()

    return pl.pallas_call(
        body, name="grad_pair_exchange", in_specs=[ANY] * n, out_specs=[ANY] * n,
        out_shape=[jax.ShapeDtypeStruct((N_CHIPS, 2, *g.shape[2:]), g.dtype) for g in grads],
        scratch_shapes=[pltpu.SemaphoreType.DMA((n,)), pltpu.SemaphoreType.DMA((n,))],
    )(*grads)


def _pair_sum(name, g, peer, c_arr):
    _, _, n_r, n_c = g.shape
    tr = 256 if n_r % 256 == 0 else n_r

    def body(c_ref, g_ref, p_ref, o_ref):
        o_ref[...] = (g_ref[...] + p_ref[...]).astype(o_ref.dtype)

    blk = (1, 1, tr, n_c)
    return pl.pallas_call(
        body, name=name,
        grid_spec=pltpu.PrefetchScalarGridSpec(
            num_scalar_prefetch=1, grid=(N_CHIPS, 2, n_r // tr),
            in_specs=[pl.BlockSpec(blk, lambda qi, li, ri, c_ref: (qi, 2 * c_ref[0] + li, ri, 0)),
                      pl.BlockSpec(blk, lambda qi, li, ri, c_ref: (qi, li, ri, 0))],
            out_specs=pl.BlockSpec(blk, lambda qi, li, ri, c_ref: (qi, li, ri, 0))),
        out_shape=jax.ShapeDtypeStruct(peer.shape, BF),
        compiler_params=_cparams(("parallel", "parallel", "parallel")),
    )(c_arr, g, peer)


def _chip_exchange(parts):
    n = len(parts)

    def body(*refs):
        ins, outs = refs[:n], refs[n:2 * n]
        send_sems, recv_sems = refs[2 * n:]
        x, y, c, q, chips = _place()
        cps = []
        for a in range(n):
            for j, (cx, cy) in enumerate(chips):
                cps.append(_remote(ins[a].at[2 * cx + cy], outs[a].at[j], send_sems.at[3 * a + j], recv_sems.at[3 * a + j], (cx, cy, c)))
                cps[-1].start()
        for cp in cps:
            cp.wait()

    return pl.pallas_call(
        body, name="grad_chip_exchange", in_specs=[ANY] * n, out_specs=[ANY] * n,
        out_shape=[jax.ShapeDtypeStruct((3, *s.shape[1:]), s.dtype) for s in parts],
        scratch_shapes=[pltpu.SemaphoreType.DMA((3 * n,)), pltpu.SemaphoreType.DMA((3 * n,))],
    )(*parts)


def _chip_sum(name, own, r, chip_arr):
    _, _, n_r, n_c = r.shape
    tr = 256 if n_r % 256 == 0 else n_r

    def body(q_ref, r0, r1, r2, r3, o_ref):
        o_ref[...] = ((r0[0].astype(F32) + r1[0].astype(F32)) + r2[0].astype(F32)) + r3[0].astype(F32)

    blk = (1, 1, tr, n_c)
    return pl.pallas_call(
        body, name=name,
        grid_spec=pltpu.PrefetchScalarGridSpec(
            num_scalar_prefetch=1, grid=(2, n_r // tr),
            in_specs=[pl.BlockSpec(blk, lambda li, ri, q_ref: (q_ref[0], li, ri, 0))]
            + [pl.BlockSpec(blk, lambda li, ri, q_ref, s=s: (s, li, ri, 0)) for s in range(3)],
            out_specs=pl.BlockSpec((1, tr, n_c), lambda li, ri, q_ref: (li, ri, 0))),
        out_shape=jax.ShapeDtypeStruct((2, n_r, n_c), F32),
        compiler_params=_cparams(("parallel", "parallel")),
    )(chip_arr, own, r, r, r)


def _pair_share(halves):
    n = len(halves)

    def body(*refs):
        ins, outs = refs[:n], refs[n:2 * n]
        send_sems, recv_sems = refs[2 * n:]
        x, y, c, q, chips = _place()
        cps = [_remote(ins[a], outs[a], send_sems.at[a], recv_sems.at[a], (x, y, 1 - c)) for a in range(n)]
        for cp in cps:
            cp.start()
        for cp in cps:
            cp.wait()

    return pl.pallas_call(
        body, name="grad_pair_share", in_specs=[ANY] * n, out_specs=[ANY] * n,
        out_shape=[jax.ShapeDtypeStruct(h.shape, h.dtype) for h in halves],
        scratch_shapes=[pltpu.SemaphoreType.DMA((n,)), pltpu.SemaphoreType.DMA((n,))],
    )(*halves)


def _allreduce_small(v):
    n_r = v.shape[0]

    def body(v_ref, o_ref, slots, send_sems, recv_sems):
        x, y, c = lax.axis_index("x"), lax.axis_index("y"), lax.axis_index("c")
        me = 4 * x + 2 * y + c
        slots[me] = v_ref[...]
        cps = []
        for r in range(1, 8):
            px = 1 - x if r & 4 else x
            py = 1 - y if r & 2 else y
            pc = 1 - c if r & 1 else c
            cps.append(_remote(v_ref, slots.at[me], send_sems.at[r - 1], recv_sems.at[r - 1], (px, py, pc)))
            cps[-1].start()
        for cp in cps:
            cp.wait()
        tot = slots[0]
        for d in range(1, 8):
            tot = tot + slots[d]
        o_ref[...] = tot

    return pl.pallas_call(
        body, name="allreduce_small",
        in_specs=[pl.BlockSpec(memory_space=pltpu.VMEM)], out_specs=pl.BlockSpec(memory_space=pltpu.VMEM),
        out_shape=jax.ShapeDtypeStruct(v.shape, F32),
        scratch_shapes=[pltpu.VMEM((8, n_r, 128), F32), pltpu.SemaphoreType.DMA((7,)), pltpu.SemaphoreType.DMA((7,))],
        compiler_params=pltpu.CompilerParams(vmem_limit_bytes=VMEM_LIMIT_BYTES),
    )(v)


def _cols_full(g, l):
    s = g[:, l]
    return s.transpose(1, 0, 2).reshape(s.shape[1], -1)


def _rows_full(g, l):
    s = g[:, l]
    return s.reshape(-1, s.shape[-1])


def _cols_split(full):
    r = full.shape[0]
    return full.reshape(r, N_CHIPS, -1).transpose(1, 0, 2)


def _rows_split(full):
    return full.reshape(N_CHIPS, -1, full.shape[-1])


def _pack(parts):
    flat = []
    for a in parts:
        f = a.reshape(-1).astype(F32)
        flat.append(jnp.pad(f, (0, (-f.shape[0]) % 1024)))
    return jnp.concatenate(flat).reshape(-1, 128)


def _unpack(buf, shapes):
    flat = buf.reshape(-1)
    out, off = [], 0
    for s in shapes:
        size = 1
        for d in s:
            size *= d
        out.append(flat[off:off + size].reshape(s))
        off += size + (-size) % 1024
    return out


def kernel(x, p, g_mix_pre, w_in, b_forget, w_conf_dw, conf_ln_g, conf_ln_b, w_conf_pw, w_sc, w_pool, pool_scale, w_out, g_mix_post, g_mlp_pre, w_up, w_down, g_mlp_post, g_ple_pre, w_ple_gate, w_ple_proj, g_ple_post, loss_target, m_g_mix_pre, m_w_in, m_b_forget, m_w_conf_dw, m_conf_ln_g, m_conf_ln_b, m_w_conf_pw, m_w_sc, m_w_pool, m_pool_scale, m_w_out, m_g_mix_post, m_g_mlp_pre, m_w_up, m_w_down, m_g_mlp_post, m_g_ple_pre, m_w_ple_gate, m_w_ple_proj, m_g_ple_post, v_g_mix_pre, v_w_in, v_b_forget, v_w_conf_dw, v_conf_ln_g, v_conf_ln_b, v_w_conf_pw, v_w_sc, v_w_pool, v_pool_scale, v_w_out, v_g_mix_post, v_g_mlp_pre, v_w_up, v_w_down, v_g_mlp_post, v_g_ple_pre, v_w_ple_gate, v_w_ple_proj, v_g_ple_post):
    names = ['g_mix_pre', 'w_in', 'b_forget', 'w_conf_dw', 'conf_ln_g', 'conf_ln_b', 'w_conf_pw', 'w_sc', 'w_pool', 'pool_scale',
             'w_out', 'g_mix_post', 'g_mlp_pre', 'w_up', 'w_down', 'g_mlp_post', 'g_ple_pre', 'w_ple_gate', 'w_ple_proj', 'g_ple_post']
    env = locals()
    wts = {k: env[k] for k in names}
    mom = {k: env["m_" + k] for k in names}
    var = {k: env["v_" + k] for k in names}

    t_len = x.shape[1]
    tm = min(256, t_len)
    tq = min(512, max(t_len // 2, 128))
    chip = 2 * lax.axis_index("x") + lax.axis_index("y")
    core = lax.axis_index("c")

    big = ['w_in', 'w_conf_pw', 'w_out', 'w_up', 'w_down', 'w_ple_gate', 'w_ple_proj']
    tiny = ['w_conf_dw', 'w_sc']
    own_shards = [wts[k].astype(BF) for k in big] + [wts[k] for k in tiny]
    gat = {k: _by_chip(own, others, chip) for k, own, others in zip(big + tiny, own_shards, _allgather_weights(own_shards))}

    def layer_weights(l):
        w_full = _cols_full(gat['w_in'], l)
        w_a = jnp.concatenate([w_full[:, :F_OFF], w_full[:, F_OFF + N_HEADS:], w_full[:, F_OFF:F_OFF + N_HEADS],
                               jnp.zeros((D_MODEL, Z_W - D_IN), BF)], axis=1)
        w_bd = jnp.zeros((D_GRP, D_GRP), F32)
        for g in range(4):
            w_bd = lax.dynamic_update_slice(w_bd, wts['w_pool'][l, g], (64 * g, 64 * g))
        row = lambda a: a[l][None, :]
        return dict(
            w_a=w_a, w_dw=jnp.pad(_cols_full(gat['w_conf_dw'], l), ((0, 1), (0, 0))), w_pw=_rows_full(gat['w_conf_pw'], l),
            w_sc=jnp.pad(_cols_full(gat['w_sc'], l), ((0, 5), (0, 0))), w_bd=w_bd.astype(BF),
            w_out=_rows_full(gat['w_out'], l), w_up=_cols_full(gat['w_up'], l), w_down=_rows_full(gat['w_down'], l),
            w_gate=_rows_full(gat['w_ple_gate'], l), w_proj=_cols_full(gat['w_ple_proj'], l),
            b_f=jnp.pad(wts['b_forget'][l], (0, F_PAD - N_HEADS))[None, :],
            ln_g=row(wts['conf_ln_g']), ln_b=row(wts['conf_ln_b']), pool_scale=row(wts['pool_scale']),
            g_mix_pre=row(wts['g_mix_pre']), g_mix_post=row(wts['g_mix_post']), g_mlp_pre=row(wts['g_mlp_pre']),
            g_mlp_post=row(wts['g_mlp_post']), g_ple_pre=row(wts['g_ple_pre']), g_ple_post=row(wts['g_ple_post']))

    lw = [layer_weights(l) for l in range(DEPTH)]

    h = x[0]
    saved = []
    for l in range(DEPTH):
        w = lw[l]
        s = dict(h0=h)
        s['zc'], s['qkv'], s['zs'], s['zp'], s['zf'] = _mix_in_fwd(t_len, tm, h, w['g_mix_pre'], w['w_a'])
        s['cv'], y_conf = _conf_fwd(t_len, tm, s['zc'], w['w_dw'], w['ln_g'], w['ln_b'], w['w_pw'])
        s['qa'], s['ka'], s['v'] = _attn_operands(s['qkv'], _fox_prep(t_len, s['zf'], w['b_f']))
        s['o'], s['lse'] = _fox_fwd(t_len, tq, s['qa'], s['ka'], s['v'])
        y_att = _from_heads(s['o']).astype(BF)
        (y_sc,) = _sconv_fwd(t_len, tm, s['zs'], w['w_sc'])
        (y_pool,) = _pool_fwd(t_len, tm, s['zp'], w['w_bd'], w['pool_scale'])
        s['ys'] = [y_conf, y_att, y_sc, y_pool]
        s['mix'], h = _mix_out_fwd(t_len, tm, s['ys'], h, w['w_out'], w['g_mix_post'])
        s['h1'] = h
        s['up'] = _mlp_up_fwd(t_len, tm, h, w['g_mlp_pre'], w['w_up'])
        s['ff'], h = _mlp_down_fwd(t_len, tm, s['up'], h, w['w_down'], w['g_mlp_post'])
        s['h2'] = h
        s['gpre'], s['pe'], h = _ple_fwd(t_len, tm, h, p[l, 0], w['g_ple_pre'], w['w_gate'], w['w_proj'], w['g_ple_post'])
        saved.append(s)

    dh, loss_part = _loss_call(t_len, tm, h, loss_target[0])

    grads = [None] * DEPTH
    for l in reversed(range(DEPTH)):
        w, s, g = lw[l], saved[l], {}
        dh, g['w_ple_gate'], g['w_ple_proj'], g['g_ple_pre'], g['g_ple_post'] = _ple_bwd(
            t_len, tm, dh, s['h2'], s['gpre'], s['pe'], p[l, 0], w['g_ple_pre'], w['w_gate'], w['g_ple_post'])
        dff, dup, g['g_mlp_post'] = _mlp_bwd_a(t_len, tm, dh, s['ff'], s['up'], w['w_down'], w['g_mlp_post'])
        dh, hn, g['g_mlp_pre'] = _mlp_bwd_b(t_len, tm, dh, s['h1'], dup, w['w_up'], w['g_mlp_pre'])
        tt = min(512, t_len)
        g['w_up'] = _mm_tn("mlp_dw_up", hn, dup, D_MODEL, 1024, tt)
        g['w_down'] = _mm_tn("mlp_dw_down", s['up'], dff, 1024, D_MODEL, tt, pro=lambda u: jnp.square(jnp.maximum(u, 0.0)))
        dy_conf, dy_att, dy_sc, dy_pool, g['w_out'], g['g_mix_post'] = _mix_out_bwd(t_len, tm, dh, s['mix'], s['ys'], w['w_out'], w['g_mix_post'])
        dzc, g['w_conf_dw'], g['conf_ln_g'], g['conf_ln_b'], g['w_conf_pw'] = _conf_bwd(
            t_len, tm, s['zc'], s['cv'], dy_conf, w['w_dw'], w['ln_g'], w['ln_b'], w['w_pw'])
        dzs, g['w_sc'] = _sconv_bwd(t_len, tm, s['zs'], dy_sc, w['w_sc'])
        dzp, d_wbd, g['pool_scale'] = _pool_bwd(t_len, tm, s['zp'], dy_pool, w['w_bd'], w['pool_scale'])
        g['w_pool'] = jnp.stack([d_wbd[64 * a:64 * (a + 1), 64 * a:64 * (a + 1)] for a in range(4)])
        do = _to_heads(dy_att)
        dq, row_sum, dk, col_sum, dv = _fox_bwd(t_len, tq, s['qa'], s['ka'], s['v'], do, s['o'],
                                                s['lse'].reshape(N_HEADS, 1, t_len))
        dc_pad = jnp.pad((row_sum - col_sum).reshape(N_HEADS, t_len).T, ((0, 0), (0, F_PAD - N_HEADS)))
        dzf, d_bf = _fox_post(t_len, dc_pad, s['zf'], w['b_f'])
        g['b_forget'] = d_bf[0, :N_HEADS]
        dqkv = jnp.concatenate([_from_heads(dq), _from_heads(dk), _from_heads(dv)], axis=1).astype(BF)
        dh, xn, dz, g['g_mix_pre'] = _mix_in_bwd(t_len, tm, dh, s['h0'], dzc, dqkv, dzs, dzp, dzf, w['g_mix_pre'], w['w_a'])
        d_wa = _mm_tn("mix_dw_in", xn, dz, D_MODEL, Z_W, tt)
        g['w_in'] = jnp.concatenate([d_wa[:, :F_OFF], d_wa[:, Z_F:Z_F + N_HEADS], d_wa[:, F_OFF:Z_F]], axis=1)
        g['w_conf_dw'] = g['w_conf_dw'][:CONF_K]
        g['w_sc'] = g['w_sc'][:SC_K]
        grads[l] = g
    grad_x = dh[None]

    split = dict(w_in=_cols_split, w_conf_pw=_rows_split, w_out=_rows_split, w_up=_cols_split, w_down=_rows_split,
                 w_ple_gate=_rows_split, w_ple_proj=_cols_split)
    contrib = [jnp.stack([split[k](grads[l][k]) for l in range(DEPTH)], axis=1) for k in big]
    peer = _pair_exchange(contrib)
    c_arr = core.astype(jnp.int32).reshape(1)
    parts = [_pair_sum("grad_pair_sum_" + k, a, b, c_arr) for k, a, b in zip(big, contrib, peer)]
    landed = _chip_exchange(parts)
    chip_arr = chip.astype(jnp.int32).reshape(1)
    halves = [_chip_sum("grad_chip_sum_" + k, own, r, chip_arr) for k, own, r in zip(big, parts, landed)]
    reduced = {}
    for k, mine, theirs in zip(big, halves, _pair_share(halves)):
        both = jnp.stack([mine, theirs])
        reduced[k] = jnp.concatenate([lax.dynamic_index_in_dim(both, core, 0, keepdims=False),
                                      lax.dynamic_index_in_dim(both, 1 - core, 0, keepdims=False)], axis=0)

    small = [k for k in names if k not in big]
    small_shapes = [(DEPTH, *grads[0][k].shape) for k in small]
    packed = _pack([jnp.stack([grads[l][k] for l in range(DEPTH)]) for k in small] + [loss_part])
    summed = _allreduce_small(packed)
    small_sum = _unpack(summed, small_shapes + [(8, 128)])
    loss = small_sum[-1][0, 0]
    for k, a in zip(small, small_sum[:-1]):
        if k in tiny:
            a = lax.dynamic_slice_in_dim(a, chip * 64, 64, axis=2)
        reduced[k] = a.reshape(wts[k].shape)

    delta_w, new_m, new_v = {}, {}, {}
    for k in names:
        shp = wts[k].shape
        as3 = (lambda a: a.reshape(shp[0], -1, shp[-1])) if len(shp) > 2 else (lambda a: a.reshape(1, shp[0], shp[1]))
        d, nm, nv = _adamw_call("adamw_" + k, as3(wts[k]), as3(reduced[k]), as3(mom[k]), as3(var[k]))
        delta_w[k], new_m[k], new_v[k] = d.reshape(shp), nm.reshape(shp), nv.reshape(shp)

    return (loss, grad_x, *[reduced[k] for k in names], *[delta_w[k] for k in names],
            *[new_m[k] for k in names], *[new_v[k] for k in names])
```

```python
import functools

import jax
import jax.numpy as jnp
from jax import lax
from jax.experimental import pallas as pl
from jax.experimental.pallas import tpu as pltpu

F32 = jnp.float32
BF = jnp.bfloat16

DEPTH = 4
D_MODEL = 1024
D_GRP = 256
HEAD_DIM = 64
N_HEADS = 4
CONF_K = 31
SC_K = 3
D_FF = 4096
D_PLE = 256
EPS = 1e-6
N_CHIPS = 4
Z_CONF, Z_QKV, Z_SC, Z_POOL, Z_F = 0, 512, 1280, 2048, 2304
Z_W = 2432
F_PAD = 128
D_IN = 2308
F_OFF = 1280

ADAM_LR, ADAM_B1, ADAM_B2, ADAM_EPS, ADAM_WD, ADAM_STEP = 0.001, 0.9, 0.999, 1e-08, 0.01, 10

VMEM_LIMIT_BYTES = 56 * 1024 * 1024
HALO = 32
NEG = -1e30
MESH_T = pl.DeviceIdType.MESH


def _cparams(sem=None):
    return pltpu.CompilerParams(dimension_semantics=sem, vmem_limit_bytes=VMEM_LIMIT_BYTES)


def _dot(a, b):
    return jnp.dot(a, b, preferred_element_type=F32)


def _dot_nt(a, b):
    return lax.dot_general(a, b, (((1,), (1,)), ((), ())), preferred_element_type=F32)


def _dot_tn(a, b):
    return lax.dot_general(a, b, (((0,), (0,)), ((), ())), preferred_element_type=F32)


def _sig(x):
    return jax.nn.sigmoid(x)


def _rms_fwd(x, g):
    r = lax.rsqrt(jnp.mean(x * x, axis=-1, keepdims=True) + EPS)
    return x * r * g


def _rms_bwd(x, g, dy):
    r = lax.rsqrt(jnp.mean(x * x, axis=-1, keepdims=True) + EPS)
    xh = x * r
    dg = jnp.sum(dy * xh, axis=0, keepdims=True)
    dxh = dy * g
    dx = r * (dxh - xh * jnp.mean(dxh * xh, axis=-1, keepdims=True))
    return dx, dg


def _back(ext, d):
    return ext if d == 0 else pltpu.roll(ext, d, 0)


def _ahead(ext, d):
    return ext if d == 0 else pltpu.roll(ext, ext.shape[0] - d, 0)


def _rows_call(name, fn, t_len, tm, rows, fulls, out_rows, out_accs=(), prevs=(), nexts=()):
    n = t_len // tm
    hb = tm // HALO
    nhb = t_len // HALO
    n_rows, n_prev, n_next, n_full = len(rows), len(prevs), len(nexts), len(fulls)
    in_specs = [pl.BlockSpec((tm, a.shape[1]), lambda i: (i, 0)) for a in rows]
    in_specs += [pl.BlockSpec((HALO, a.shape[1]), lambda i: (jnp.maximum(i * hb - 1, 0), 0)) for a in prevs]
    in_specs += [pl.BlockSpec((HALO, a.shape[1]), lambda i: (jnp.minimum((i + 1) * hb, nhb - 1), 0)) for a in nexts]
    in_specs += [pl.BlockSpec(a.shape, lambda i, nd=a.ndim: (0,) * nd) for a in fulls]
    out_shape = [jax.ShapeDtypeStruct((t_len, c), dt) for c, dt in out_rows]
    out_shape += [jax.ShapeDtypeStruct(s, F32) for s in out_accs]
    out_specs = [pl.BlockSpec((tm, c), lambda i: (i, 0)) for c, _ in out_rows]
    out_specs += [pl.BlockSpec(s, lambda i, nd=len(s): (0,) * nd) for s in out_accs]
    n_in = n_rows + n_prev + n_next + n_full
    n_ro = len(out_rows)

    def body(*refs):
        i = pl.program_id(0)
        ins, outs = refs[:n_in], refs[n_in:]
        rv = [r[...] for r in ins[:n_rows]]
        pv = [r[...] for r in ins[n_rows:n_rows + n_prev]]
        nv = [r[...] for r in ins[n_rows + n_prev:n_rows + n_prev + n_next]]
        fv = list(ins[n_rows + n_prev + n_next:])
        ro, ao = fn(i, n, rv, pv, nv, fv)
        for r, v in zip(outs[:n_ro], ro):
            r[...] = v.astype(r.dtype)
        if out_accs:
            acc = outs[n_ro:]

            @pl.when(i == 0)
            def _():
                for r in acc:
                    r[...] = jnp.zeros(r.shape, r.dtype)

            for r, v in zip(acc, ao):
                r[...] += v

    res = pl.pallas_call(
        body, name=name, grid=(n,), in_specs=in_specs, out_specs=out_specs, out_shape=out_shape,
        compiler_params=_cparams(("arbitrary",)),
    )(*rows, *prevs, *nexts, *fulls)
    return res


def _mm_tn(name, x, y, tk, tn, tt, pro=None):
    t_len, k_dim = x.shape
    n_dim = y.shape[1]

    def body(x_ref, y_ref, o_ref):
        @pl.when(pl.program_id(2) == 0)
        def _():
            o_ref[...] = jnp.zeros(o_ref.shape, o_ref.dtype)

        xv = x_ref[...]
        if pro is not None:
            xv = pro(xv)
        o_ref[...] += _dot_tn(xv.astype(BF), y_ref[...].astype(BF))

    return pl.pallas_call(
        body, name=name, grid=(k_dim // tk, n_dim // tn, t_len // tt),
        in_specs=[pl.BlockSpec((tt, tk), lambda a, b, t: (t, a)), pl.BlockSpec((tt, tn), lambda a, b, t: (t, b))],
        out_specs=pl.BlockSpec((tk, tn), lambda a, b, t: (a, b)),
        out_shape=jax.ShapeDtypeStruct((k_dim, n_dim), F32),
        compiler_params=_cparams(("parallel", "parallel", "arbitrary")),
    )(x, y)


QK_W = 128
ATT_W = N_HEADS * QK_W
C_COL = HEAD_DIM
ONE_COL = HEAD_DIM + 3


def _lane(shape):
    return lax.broadcasted_iota(jnp.int32, shape, 1)


def _head_low(a256, h):
    pair = a256[:, QK_W * (h // 2):QK_W * (h // 2 + 1)]
    return pltpu.roll(pair, HEAD_DIM, 1) if h % 2 else pair


def _heads_spread(a256):
    low = _lane((a256.shape[0], QK_W)) < HEAD_DIM
    return jnp.concatenate([jnp.where(low, _head_low(a256, h), 0.0) for h in range(N_HEADS)], axis=1)


def _heads_packed(a512):
    low = _lane((a512.shape[0], QK_W)) < HEAD_DIM
    out = []
    for pair in range(N_HEADS // 2):
        even = a512[:, QK_W * 2 * pair:QK_W * (2 * pair + 1)]
        odd = a512[:, QK_W * (2 * pair + 1):QK_W * (2 * pair + 2)]
        out.append(jnp.where(low, even, pltpu.roll(odd, HEAD_DIM, 1)))
    return jnp.concatenate(out, axis=1)


def _mix_in_fwd(t_len, tm, h, g_pre, w_a):
    def fn(i, n, rows, prevs, nexts, fulls):
        (hv,), (g, w) = rows, fulls
        z = _dot(_rms_fwd(hv, g[...]).astype(BF), w[...])
        lane = _lane((tm, QK_W))
        ones = jnp.where(lane < ONE_COL, 1.0, 0.0)
        zq = z[:, Z_QKV:Z_QKV + D_GRP]
        qa = jnp.concatenate([jnp.where(lane < HEAD_DIM, _head_low(zq, a) * (HEAD_DIM ** -0.5), ones) for a in range(N_HEADS)], axis=1)
        return [z[:, Z_CONF:Z_QKV], qa, z[:, Z_QKV + D_GRP:Z_SC], z[:, Z_SC:Z_POOL], z[:, Z_POOL:Z_F], z[:, Z_F:Z_W]], []

    return _rows_call("mix_in_fwd", fn, t_len, tm, [h], [g_pre, w_a],
                      [(512, F32), (ATT_W, BF), (2 * D_GRP, BF), (768, F32), (256, F32), (F_PAD, F32)])


def _mix_in_bwd(t_len, tm, dh, h, dzc, dqa, dka, dva, dzs, dzp, dzf, g_pre, w_a):
    def fn(i, n, rows, prevs, nexts, fulls):
        dhv, hv, a, dq, dk, dv, c, d, e = rows
        g, w = fulls
        b = jnp.concatenate([_heads_packed(dq), _heads_packed(dk), _heads_packed(dv)], axis=1).astype(BF)
        dz = jnp.concatenate([a, b, c, d, e], axis=1)
        dxn = _dot_nt(dz, w[...])
        dx, dg = _rms_bwd(hv, g[...], dxn)
        xn = _rms_fwd(hv, g[...])
        return [dhv + dx, xn, dz], [dg]

    return _rows_call("mix_in_bwd", fn, t_len, tm, [dh, h, dzc, dqa, dka, dva, dzs, dzp, dzf], [g_pre, w_a],
                      [(D_MODEL, F32), (D_MODEL, BF), (Z_W, BF)], [(1, D_MODEL)])


def _glu_ext(i, zc, zc_prev):
    ext = jnp.concatenate([zc_prev, zc], axis=0)
    u = ext[:, :D_GRP] * _sig(ext[:, D_GRP:])
    row = lax.broadcasted_iota(jnp.int32, u.shape, 0)
    return jnp.where((row >= HALO) | (i > 0), u, 0.0)


def _conf_fwd(t_len, tm, zc, w_dw, ln_g, ln_b, w_pw):
    def fn(i, n, rows, prevs, nexts, fulls):
        (zv,), (zp,) = rows, prevs
        wdw, lg, lb, wpw = fulls
        u = _glu_ext(i, zv, zp)
        cv = jnp.zeros((tm, D_GRP), F32)
        for k in range(CONF_K):
            cv = cv + wdw[k:k + 1, :] * _back(u, CONF_K - 1 - k)[HALO:, :]
        mu = jnp.mean(cv, axis=-1, keepdims=True)
        xc = cv - mu
        ln = xc * lax.rsqrt(jnp.mean(xc * xc, axis=-1, keepdims=True) + EPS) * lg[...] + lb[...]
        s = ln * _sig(ln)
        return [cv, _dot(s.astype(BF), wpw[...])], []

    return _rows_call("conf_fwd", fn, t_len, tm, [zc], [w_dw, ln_g, ln_b, w_pw], [(D_GRP, F32), (D_GRP, BF)], prevs=[zc])


def _conf_bwd(t_len, tm, zc, cv, dy, w_dw, ln_g, ln_b, w_pw):
    def fn(i, n, rows, prevs, nexts, fulls):
        zv, cvv, dyv = rows
        (zp,) = prevs
        cvn, dyn = nexts
        wdw, lg, lb, wpw = fulls
        cve = jnp.concatenate([cvv, cvn], axis=0)
        dye = jnp.concatenate([dyv, dyn], axis=0)
        mu = jnp.mean(cve, axis=-1, keepdims=True)
        xc = cve - mu
        rs = lax.rsqrt(jnp.mean(xc * xc, axis=-1, keepdims=True) + EPS)
        xh = xc * rs
        ln = xh * lg[...] + lb[...]
        sg = _sig(ln)
        s = ln * sg
        ds = _dot_nt(dye.astype(BF), wpw[...])
        dln = ds * (sg * (1.0 + ln * (1.0 - sg)))
        dxh = dln * lg[...]
        dcv = rs * (dxh - jnp.mean(dxh, axis=-1, keepdims=True) - xh * jnp.mean(dxh * xh, axis=-1, keepdims=True))
        row = lax.broadcasted_iota(jnp.int32, dcv.shape, 0)
        dcv = jnp.where((row < tm) | (i < n - 1), dcv, 0.0)
        d_lg = jnp.sum((dln * xh)[:tm], axis=0, keepdims=True)
        d_lb = jnp.sum(dln[:tm], axis=0, keepdims=True)
        d_wpw = _dot_tn(s[:tm].astype(BF), dyv.astype(BF))
        u = _glu_ext(i, zv, zp)
        dcv_cur = dcv[:tm]
        du = jnp.zeros((tm, D_GRP), F32)
        d_wdw = jnp.zeros((32, D_GRP), F32)
        krow = lax.broadcasted_iota(jnp.int32, (32, D_GRP), 0)
        for k in range(CONF_K):
            d = CONF_K - 1 - k
            du = du + wdw[k:k + 1, :] * _ahead(dcv, d)[:tm, :]
            tap = _back(u, d)[HALO:, :]
            d_wdw = d_wdw + jnp.where(krow == k, jnp.sum(dcv_cur * tap, axis=0, keepdims=True), 0.0)
        a, b = zv[:, :D_GRP], zv[:, D_GRP:]
        sb = _sig(b)
        dz = jnp.concatenate([du * sb, du * a * sb * (1.0 - sb)], axis=1)
        return [dz], [d_wdw, d_lg, d_lb, d_wpw]

    return _rows_call("conf_bwd", fn, t_len, tm, [zc, cv, dy], [w_dw, ln_g, ln_b, w_pw], [(512, BF)],
                      [(32, D_GRP), (1, D_GRP), (1, D_GRP), (D_GRP, D_GRP)], prevs=[zc], nexts=[cv, dy])


def _sc_ext(i, zs, zs_prev):
    ext = jnp.concatenate([zs_prev, zs], axis=0)
    e = ext[:, 2 * D_GRP:] * ext[:, :D_GRP]
    row = lax.broadcasted_iota(jnp.int32, e.shape, 0)
    return jnp.where((row >= HALO) | (i > 0), e, 0.0)


def _sconv_fwd(t_len, tm, zs, w_sc):
    def fn(i, n, rows, prevs, nexts, fulls):
        (zv,), (zp,), (w,) = rows, prevs, fulls
        e = _sc_ext(i, zv, zp)
        cv = jnp.zeros((tm, D_GRP), F32)
        for k in range(SC_K):
            cv = cv + w[k:k + 1, :] * _back(e, SC_K - 1 - k)[HALO:, :]
        return [zv[:, D_GRP:2 * D_GRP] * cv], []

    return _rows_call("sconv_fwd", fn, t_len, tm, [zs], [w_sc], [(D_GRP, BF)], prevs=[zs])


def _sconv_bwd(t_len, tm, zs, dy, w_sc):
    def fn(i, n, rows, prevs, nexts, fulls):
        zv, dyv = rows
        (zp,) = prevs
        zn, dyn = nexts
        (w,) = fulls
        e = _sc_ext(i, zv, zp)
        taps = [_back(e, SC_K - 1 - k)[HALO:, :] for k in range(SC_K)]
        cv = w[0:1, :] * taps[0] + w[1:2, :] * taps[1] + w[2:3, :] * taps[2]
        bg = zv[:, D_GRP:2 * D_GRP]
        dcv = jnp.concatenate([dyv * bg, dyn * zn[:, D_GRP:2 * D_GRP]], axis=0)
        row = lax.broadcasted_iota(jnp.int32, dcv.shape, 0)
        dcv = jnp.where((row < tm) | (i < n - 1), dcv, 0.0)
        de = jnp.zeros((tm, D_GRP), F32)
        d_w = jnp.zeros((8, D_GRP), F32)
        krow = lax.broadcasted_iota(jnp.int32, (8, D_GRP), 0)
        for k in range(SC_K):
            de = de + w[k:k + 1, :] * _ahead(dcv, SC_K - 1 - k)[:tm, :]
            d_w = d_w + jnp.where(krow == k, jnp.sum(dcv[:tm] * taps[k], axis=0, keepdims=True), 0.0)
        dz = jnp.concatenate([de * zv[:, 2 * D_GRP:], dyv * cv, de * zv[:, :D_GRP]], axis=1)
        return [dz], [d_w]

    return _rows_call("sconv_bwd", fn, t_len, tm, [zs, dy], [w_sc], [(768, BF)], [(8, D_GRP)], prevs=[zs], nexts=[zs, dy])


def _pool_window(shape):
    grp = lax.broadcasted_iota(jnp.int32, shape, 1) // 64
    return grp, jnp.where(grp == 0, 2.0, jnp.where(grp == 1, 4.0, jnp.where(grp == 2, 8.0, 16.0)))


def _pool_d(i, tm, zv, zp):
    ext = jnp.concatenate([zp, zv], axis=0)
    row = lax.broadcasted_iota(jnp.int32, ext.shape, 0)
    ext = jnp.where((row >= HALO) | (i > 0), ext, 0.0)
    s2 = ext + _back(ext, 1)
    s4 = s2 + _back(s2, 2)
    s8 = s4 + _back(s4, 4)
    s16 = s8 + _back(s8, 8)
    grp, win = _pool_window((tm, D_GRP))
    sel = jnp.where(grp == 0, s2[HALO:], jnp.where(grp == 1, s4[HALO:], jnp.where(grp == 2, s8[HALO:], s16[HALO:])))
    pos = (i * tm + lax.broadcasted_iota(jnp.int32, (tm, D_GRP), 0) + 1).astype(F32)
    return sel / jnp.minimum(pos, win) - zv


def _pool_fwd(t_len, tm, zpool, w_bd, scale):
    def fn(i, n, rows, prevs, nexts, fulls):
        (zv,), (zp,) = rows, prevs
        w, sc = fulls
        d = _pool_d(i, tm, zv, zp)
        return [_dot(d.astype(BF), w[...]) * sc[...]], []

    return _rows_call("pool_fwd", fn, t_len, tm, [zpool], [w_bd, scale], [(D_GRP, BF)], prevs=[zpool])


def _pool_bwd(t_len, tm, zpool, dy, w_bd, scale):
    def fn(i, n, rows, prevs, nexts, fulls):
        zv, dyv = rows
        (zp,) = prevs
        (dyn,) = nexts
        w, sc = fulls
        d = _pool_d(i, tm, zv, zp)
        lin = _dot(d.astype(BF), w[...])
        d_sc = jnp.sum(dyv * lin, axis=0, keepdims=True)
        dye = jnp.concatenate([dyv, dyn], axis=0) * sc[...]
        d_w = _dot_tn(d.astype(BF), dye[:tm].astype(BF))
        dd = _dot_nt(dye.astype(BF), w[...])
        row = lax.broadcasted_iota(jnp.int32, dd.shape, 0)
        dd = jnp.where((row < tm) | (i < n - 1), dd, 0.0)
        grp, win = _pool_window(dd.shape)
        pos = (i * tm + row + 1).astype(F32)
        ddc = dd / jnp.minimum(pos, win)
        f2 = ddc + _ahead(ddc, 1)
        f4 = f2 + _ahead(f2, 2)
        f8 = f4 + _ahead(f4, 4)
        f16 = f8 + _ahead(f8, 8)
        sel = jnp.where(grp == 0, f2, jnp.where(grp == 1, f4, jnp.where(grp == 2, f8, f16)))
        return [(sel - dd)[:tm]], [d_w, d_sc]

    return _rows_call("pool_bwd", fn, t_len, tm, [zpool, dy], [w_bd, scale], [(D_GRP, BF)],
                      [(D_GRP, D_GRP), (1, D_GRP)], prevs=[zpool], nexts=[dy])


def _mix_cat(y_conf, o_att, y_sc, y_pool):
    return jnp.concatenate([y_conf, _heads_packed(o_att).astype(BF), y_sc, y_pool], axis=1)


def _mix_out_fwd(t_len, tm, ys, h, w_out, g_post):
    def fn(i, n, rows, prevs, nexts, fulls):
        y0, y1, y2, y3, hv = rows
        w, g = fulls
        mix = _dot(_mix_cat(y0, y1, y2, y3), w[...])
        return [mix, hv + _rms_fwd(mix, g[...])], []

    return _rows_call("mix_out_fwd", fn, t_len, tm, [*ys, h], [w_out, g_post], [(D_MODEL, F32), (D_MODEL, F32)])


def _mix_out_bwd(t_len, tm, dh, mix, ys, w_out, g_post):
    def fn(i, n, rows, prevs, nexts, fulls):
        dhv, mv, y0, y1, y2, y3 = rows
        w, g = fulls
        dmix, dg = _rms_bwd(mv, g[...], dhv)
        dmb = dmix.astype(BF)
        dcat = _dot_nt(dmb, w[...])
        d_w = _dot_tn(_mix_cat(y0, y1, y2, y3), dmb)
        return [dcat[:, :256], _heads_spread(dcat[:, 256:512]), dcat[:, 512:768], dcat[:, 768:]], [d_w, dg]

    return _rows_call("mix_out_bwd", fn, t_len, tm, [dh, mix, *ys], [w_out, g_post],
                      [(D_GRP, F32), (ATT_W, F32), (D_GRP, F32), (D_GRP, F32)], [(D_MODEL, D_MODEL), (1, D_MODEL)])


def _log_sigmoid(x):
    return jnp.minimum(x, 0.0) - jnp.log(1.0 + jnp.exp(-jnp.abs(x)))


SCAN_BLK = 256


def _fox_prep(t_len, zf, kv, b_f):
    blk = min(SCAN_BLK, t_len)

    def body(zf_ref, kv_ref, b_ref, ka_ref, va_ref, carry_s):
        @pl.when(pl.program_id(0) == 0)
        def _():
            carry_s[...] = jnp.zeros(carry_s.shape, F32)

        tri = (lax.broadcasted_iota(jnp.int32, (blk, blk), 0) >= lax.broadcasted_iota(jnp.int32, (blk, blk), 1)).astype(F32)
        lf = _log_sigmoid(zf_ref[...] + b_ref[...])
        cs = jnp.dot(tri, lf, precision=lax.Precision.HIGHEST, preferred_element_type=F32) + carry_s[...]
        carry_s[...] = cs[blk - 1:blk, :]
        neg = -cs
        hi = neg.astype(BF).astype(F32)
        mid = (neg - hi).astype(BF).astype(F32)
        lo = ((neg - hi) - mid).astype(BF).astype(F32)
        kvv = kv_ref[...].astype(F32)
        lane = _lane((blk, QK_W))
        ka, va = [], []
        for a in range(N_HEADS):
            terms = jnp.where(lane == C_COL, hi[:, a:a + 1], jnp.where(lane == C_COL + 1, mid[:, a:a + 1], jnp.where(
                lane == C_COL + 2, lo[:, a:a + 1], jnp.where(lane == ONE_COL, HEAD_DIM ** 0.5, 0.0))))
            ka.append(jnp.where(lane < HEAD_DIM, _head_low(kvv[:, :D_GRP], a), terms))
            va.append(jnp.where(lane < HEAD_DIM, _head_low(kvv[:, D_GRP:], a), 0.0))
        ka_ref[...] = jnp.concatenate(ka, axis=1).astype(BF)
        va_ref[...] = jnp.concatenate(va, axis=1).astype(BF)

    row = lambda w: pl.BlockSpec((blk, w), lambda i: (i, 0))
    return pl.pallas_call(
        body, name="fox_prep", grid=(t_len // blk,),
        in_specs=[row(F_PAD), row(2 * D_GRP), pl.BlockSpec((1, F_PAD), lambda i: (0, 0))],
        out_specs=[row(ATT_W), row(ATT_W)],
        out_shape=[jax.ShapeDtypeStruct((t_len, ATT_W), BF)] * 2,
        scratch_shapes=[pltpu.VMEM((1, F_PAD), F32)],
        compiler_params=_cparams(("arbitrary",)),
    )(zf, kv, b_f)


def _fox_post(t_len, dqa, dka, zf, b_f):
    blk = min(SCAN_BLK, t_len)
    nb = t_len // blk

    def body(dq_ref, dk_ref, zf_ref, b_ref, dz_ref, db_ref, carry_s):
        @pl.when(pl.program_id(0) == 0)
        def _():
            carry_s[...] = jnp.zeros(carry_s.shape, F32)
            db_ref[...] = jnp.zeros(db_ref.shape, F32)

        tri = (lax.broadcasted_iota(jnp.int32, (blk, blk), 0) <= lax.broadcasted_iota(jnp.int32, (blk, blk), 1)).astype(F32)
        lane = _lane((blk, F_PAD))
        dc = jnp.zeros((blk, F_PAD), F32)
        for a in range(N_HEADS):
            head = slice(QK_W * a, QK_W * (a + 1))
            col = dq_ref[:, head][:, ONE_COL:ONE_COL + 1] - dk_ref[:, head][:, C_COL:C_COL + 1]
            dc = jnp.where(lane == a, col, dc)
        dlf = jnp.dot(tri, dc, precision=lax.Precision.HIGHEST, preferred_element_type=F32) + carry_s[...]
        carry_s[...] = dlf[0:1, :]
        dz = dlf * _sig(-(zf_ref[...] + b_ref[...]))
        dz_ref[...] = dz.astype(dz_ref.dtype)
        db_ref[...] += jnp.sum(dz, axis=0, keepdims=True)

    row = lambda w: pl.BlockSpec((blk, w), lambda i: (nb - 1 - i, 0))
    one = pl.BlockSpec((1, F_PAD), lambda i: (0, 0))
    return pl.pallas_call(
        body, name="fox_post", grid=(nb,),
        in_specs=[row(ATT_W), row(ATT_W), row(F_PAD), one], out_specs=[row(F_PAD), one],
        out_shape=(jax.ShapeDtypeStruct((t_len, F_PAD), BF), jax.ShapeDtypeStruct((1, F_PAD), F32)),
        scratch_shapes=[pltpu.VMEM((1, F_PAD), F32)],
        compiler_params=_cparams(("arbitrary",)),
    )(dqa, dka, zf, b_f)


def _tri_mask(tq, key_rows):
    r = lax.broadcasted_iota(jnp.int32, (tq, tq), 0)
    c = lax.broadcasted_iota(jnp.int32, (tq, tq), 1)
    return (r <= c) if key_rows else (r >= c)


LSE_COL = HEAD_DIM


def _fox_fwd(t_len, tq, qa, ka, va):
    nq = t_len // tq

    def body(q_ref, k_ref, v_ref, o_ref):
        i = pl.program_id(1)
        q = q_ref[...]

        def tile(j, carry, diagonal):
            m, l, acc = carry
            rows = pl.ds(pl.multiple_of(j * tq, tq), tq)
            s = _dot_nt(q, k_ref[rows, :])
            if diagonal:
                s = jnp.where(_tri_mask(tq, False), s, NEG)
            m_new = jnp.maximum(m, jnp.max(s, axis=-1, keepdims=True))
            alpha = jnp.exp(m - m_new)
            p = jnp.exp(s - m_new)
            l = alpha * l + jnp.sum(p, axis=-1, keepdims=True)
            acc = alpha * acc + _dot(p.astype(BF), v_ref[rows, :])
            return m_new, l, acc

        init = (jnp.full((tq, 1), NEG, F32), jnp.zeros((tq, 1), F32), jnp.zeros((tq, QK_W), F32))
        carry = lax.fori_loop(0, i, lambda j, c: tile(j, c, False), init)
        m, l, acc = tile(i, carry, True)
        o_ref[...] = jnp.where(_lane((tq, QK_W)) == LSE_COL, m + jnp.log(l), acc / l)

    tile_spec = pl.BlockSpec((tq, QK_W), lambda h, i: (i, h))
    head_spec = pl.BlockSpec((t_len, QK_W), lambda h, i: (0, h))
    return pl.pallas_call(
        body, name="fox_fwd", grid=(N_HEADS, nq),
        in_specs=[tile_spec, head_spec, head_spec], out_specs=tile_spec,
        out_shape=jax.ShapeDtypeStruct((t_len, ATT_W), F32),
        compiler_params=_cparams(("parallel", "arbitrary")),
    )(qa, ka, va)


def _fox_bwd(t_len, tq, qa, ka, va, do, o):
    nq = t_len // tq

    def body(q_ref, k_ref, v_ref, do_ref, o_ref, dq_ref, dk_ref, dv_ref, acc_s, dl_s, lse_s):
        j = pl.program_id(1)

        @pl.when(j == 0)
        def _():
            acc_s[...] = jnp.zeros(acc_s.shape, F32)
            row_of = lambda sel, a: lax.dot_general(sel, a, (((1,), (1,)), ((), ())), precision=lax.Precision.HIGHEST,
                                                    preferred_element_type=F32)
            dl_s[...] = row_of(jnp.ones((8, QK_W), F32), do_ref[...] * o_ref[...])
            lse_s[...] = row_of(jnp.where(_lane((8, QK_W)) == LSE_COL, 1.0, 0.0), o_ref[...])

        kt, vt = k_ref[...], v_ref[...]

        def tile(i, carry, diagonal):
            dk, dv = carry
            rows = pl.ds(pl.multiple_of(i * tq, tq), tq)
            qt = q_ref[rows, :]
            dob = do_ref[rows, :].astype(BF)
            pt = jnp.exp(_dot_nt(kt, qt) - lse_s[0:1, rows])
            if diagonal:
                pt = jnp.where(_tri_mask(tq, True), pt, 0.0)
            dst = (pt * (_dot_nt(vt, dob) - dl_s[0:1, rows])).astype(BF)
            acc_s[rows, :] += _dot_tn(dst, kt)
            return dk + _dot(dst, qt), dv + _dot(pt.astype(BF), dob)

        carry = tile(j, (jnp.zeros((tq, QK_W), F32), jnp.zeros((tq, QK_W), F32)), True)
        dk, dv = lax.fori_loop(j + 1, nq, lambda i, c: tile(i, c, False), carry)
        dk_ref[...] = dk
        dv_ref[...] = dv

        @pl.when(j == nq - 1)
        def _():
            dq_ref[...] = acc_s[...] * (HEAD_DIM ** -0.5)

    head_spec = pl.BlockSpec((t_len, QK_W), lambda h, j: (0, h))
    tile_spec = pl.BlockSpec((tq, QK_W), lambda h, j: (j, h))
    return pl.pallas_call(
        body, name="fox_bwd", grid=(N_HEADS, nq),
        in_specs=[head_spec, tile_spec, tile_spec, head_spec, head_spec],
        out_specs=[head_spec, tile_spec, tile_spec],
        out_shape=[jax.ShapeDtypeStruct((t_len, ATT_W), F32)] * 3,
        scratch_shapes=[pltpu.VMEM((t_len, QK_W), F32), pltpu.VMEM((8, t_len), F32), pltpu.VMEM((8, t_len), F32)],
        compiler_params=_cparams(("parallel", "arbitrary")),
    )(qa, ka, va, do, o)


def _mlp_up_fwd(t_len, tm, h, g_pre, w_up):
    def fn(i, n, rows, prevs, nexts, fulls):
        (hv,), (g, w) = rows, fulls
        return [_dot(_rms_fwd(hv, g[...]).astype(BF), w[...])], []

    return _rows_call("mlp_up_fwd", fn, t_len, tm, [h], [g_pre, w_up], [(D_FF, F32)])[0]


def _mlp_down_fwd(t_len, tm, up, h, w_down, g_post):
    def fn(i, n, rows, prevs, nexts, fulls):
        (uv, hv), (w, g) = rows, fulls
        a = jnp.square(jnp.maximum(uv, 0.0))
        ff = _dot(a.astype(BF), w[...])
        return [ff, hv + _rms_fwd(ff, g[...])], []

    return _rows_call("mlp_down_fwd", fn, t_len, tm, [up, h], [w_down, g_post], [(D_MODEL, F32), (D_MODEL, F32)])


def _mlp_bwd_a(t_len, tm, dh, ff, up, w_down, g_post):
    def fn(i, n, rows, prevs, nexts, fulls):
        (dhv, fv, uv), (w, g) = rows, fulls
        dff, dg = _rms_bwd(fv, g[...], dhv)
        dfb = dff.astype(BF)
        dup = _dot_nt(dfb, w[...]) * (2.0 * jnp.maximum(uv, 0.0))
        return [dfb, dup], [dg]

    return _rows_call("mlp_bwd_a", fn, t_len, tm, [dh, ff, up], [w_down, g_post], [(D_MODEL, BF), (D_FF, BF)], [(1, D_MODEL)])


def _mlp_bwd_b(t_len, tm, dh, h, dup, w_up, g_pre):
    def fn(i, n, rows, prevs, nexts, fulls):
        (dhv, hv, duv), (w, g) = rows, fulls
        dhn = _dot_nt(duv, w[...])
        dx, dg = _rms_bwd(hv, g[...], dhn)
        return [dhv + dx, _rms_fwd(hv, g[...])], [dg]

    return _rows_call("mlp_bwd_b", fn, t_len, tm, [dh, h, dup], [w_up, g_pre], [(D_MODEL, F32), (D_MODEL, BF)], [(1, D_MODEL)])


def _ple_fwd(t_len, tm, h, p_i, g_pre, w_gate, w_proj, g_post):
    def fn(i, n, rows, prevs, nexts, fulls):
        (hv, pv), (g, wg, wp, gp) = rows, fulls
        gpre = _dot(_rms_fwd(hv, g[...]).astype(BF), wg[...])
        pe = _dot(pv.astype(BF), wp[...])
        return [gpre, pe, hv + _rms_fwd(pe * _sig(gpre), gp[...])], []

    return _rows_call("ple_fwd", fn, t_len, tm, [h, p_i], [g_pre, w_gate, w_proj, g_post], [(D_MODEL, F32)] * 3)


def _ple_bwd(t_len, tm, dh, h, gpre, pe, p_i, g_pre, w_gate, g_post):
    def fn(i, n, rows, prevs, nexts, fulls):
        (dhv, hv, gv, pev, pv), (g, wg, gp) = rows, fulls
        sg = _sig(gv)
        de, d_gp = _rms_bwd(pev * sg, gp[...], dhv)
        dpe = (de * sg).astype(BF)
        dgate = (de * pev * sg * (1.0 - sg)).astype(BF)
        d_wp = _dot_tn(pv.astype(BF), dpe)
        hn = _rms_fwd(hv, g[...])
        d_wg = _dot_tn(hn.astype(BF), dgate)
        dx, d_g = _rms_bwd(hv, g[...], _dot_nt(dgate, wg[...]))
        return [dhv + dx], [d_wg, d_wp, d_g, d_gp]

    return _rows_call("ple_bwd", fn, t_len, tm, [dh, h, gpre, pe, p_i], [g_pre, w_gate, g_post], [(D_MODEL, F32)],
                      [(D_MODEL, D_MODEL), (D_PLE, D_MODEL), (1, D_MODEL), (1, D_MODEL)])


def _loss_call(t_len, tm, h, target):
    def fn(i, n, rows, prevs, nexts, fulls):
        hv, tv = rows
        err = hv - tv
        part = 0.5 * jnp.sum(jnp.mean(err * err, axis=-1, keepdims=True), axis=0, keepdims=True)
        return [err * (1.0 / D_MODEL)], [jnp.broadcast_to(part, (8, 128))]

    return _rows_call("loss", fn, t_len, tm, [h, target], [], [(D_MODEL, F32)], [(8, 128)])


def _adamw_call(name, w, g, m, v):
    n_l, n_r, n_c = w.shape
    tr = 256 if n_r % 256 == 0 else n_r

    def body(w_ref, g_ref, m_ref, v_ref, d_ref, nm_ref, nv_ref):
        gv = g_ref[...]
        nm = ADAM_B1 * m_ref[...] + (1.0 - ADAM_B1) * gv
        nv = ADAM_B2 * v_ref[...] + (1.0 - ADAM_B2) * jnp.square(gv)
        m_hat = nm / (1.0 - ADAM_B1 ** ADAM_STEP)
        v_hat = nv / (1.0 - ADAM_B2 ** ADAM_STEP)
        d_ref[...] = -ADAM_LR * (m_hat / (jnp.sqrt(v_hat) + ADAM_EPS) + ADAM_WD * w_ref[...])
        nm_ref[...] = nm
        nv_ref[...] = nv

    spec = pl.BlockSpec((1, tr, n_c), lambda l, r: (l, r, 0))
    return pl.pallas_call(
        body, name=name, grid=(n_l, n_r // tr), in_specs=[spec] * 4, out_specs=[spec] * 3,
        out_shape=[jax.ShapeDtypeStruct(w.shape, F32)] * 3,
        compiler_params=_cparams(("parallel", "parallel")),
    )(w, g, m, v)


ANY = pl.BlockSpec(memory_space=pl.ANY)


def _place():
    x, y, c = lax.axis_index("x"), lax.axis_index("y"), lax.axis_index("c")
    chips = [(1 - x, y), (x, 1 - y), (1 - x, 1 - y)]
    return x, y, c, 2 * x + y, chips


def _remote(src, dst, send_sem, recv_sem, dev):
    return pltpu.make_async_remote_copy(src_ref=src, dst_ref=dst, send_sem=send_sem, recv_sem=recv_sem,
                                        device_id=dev, device_id_type=MESH_T)


def _allgather_weights(shards):
    n = len(shards)

    def body(*refs):
        ins, outs = refs[:n], refs[n:2 * n]
        send_sems, recv_sems = refs[2 * n:]
        x, y, c, q, chips = _place()
        me, sib = (x, y, c), (x, y, 1 - c)
        half, ohalf = pl.ds(2 * c, 2), pl.ds(2 * (1 - c), 2)
        sends = []
        for a in range(n):
            for j, chip in enumerate(chips):
                sends.append(_remote(ins[a].at[half], outs[a].at[j, half], send_sems.at[6 * a + j], recv_sems.at[6 * a + j], (*chip, c)))
                sends[-1].start()
        for a in range(n):
            for j in range(3):
                land = outs[a].at[j, half]
                _remote(land, land, send_sems.at[6 * a + j], recv_sems.at[6 * a + j], me).wait_recv()
                sends.append(_remote(land, land, send_sems.at[6 * a + 3 + j], recv_sems.at[6 * a + 3 + j], sib))
                sends[-1].start()
        for a in range(n):
            for j in range(3):
                land = outs[a].at[j, ohalf]
                _remote(land, land, send_sems.at[6 * a + 3 + j], recv_sems.at[6 * a + 3 + j], me).wait_recv()
        for cp in sends:
            cp.wait_send()

    return pl.pallas_call(
        body, name="allgather_weights", in_specs=[ANY] * n, out_specs=[ANY] * n,
        out_shape=[jax.ShapeDtypeStruct((3, *s.shape), s.dtype) for s in shards],
        scratch_shapes=[pltpu.SemaphoreType.DMA((6 * n,)), pltpu.SemaphoreType.DMA((6 * n,))],
    )(*shards)


def _by_chip(own, others, chip):
    by_mask = jnp.stack([own, others[1], others[0], others[2]])
    return jnp.stack([lax.dynamic_index_in_dim(by_mask, jnp.bitwise_xor(chip, r), 0, keepdims=False) for r in range(N_CHIPS)])


def _pair_exchange(grads):
    n = len(grads)

    def body(*refs):
        ins, outs = refs[:n], refs[n:2 * n]
        send_sems, recv_sems = refs[2 * n:]
        x, y, c, q, chips = _place()
        cps = [_remote(ins[a].at[pl.ds(0, N_CHIPS), pl.ds(2 * (1 - c), 2)], outs[a], send_sems.at[a], recv_sems.at[a], (x, y, 1 - c))
               for a in range(n)]
        for cp in cps:
            cp.start()
        for cp in cps:
            cp.wait()

    return pl.pallas_call(
        body, name="grad_pair_exchange", in_specs=[ANY] * n, out_specs=[ANY] * n,
        out_shape=[jax.ShapeDtypeStruct((N_CHIPS, 2, *g.shape[2:]), g.dtype) for g in grads],
        scratch_shapes=[pltpu.SemaphoreType.DMA((n,)), pltpu.SemaphoreType.DMA((n,))],
    )(*grads)


def _pair_sum(name, g, peer, c_arr):
    _, _, n_r, n_c = g.shape
    tr = 256 if n_r % 256 == 0 else n_r

    def body(c_ref, g_ref, p_ref, o_ref):
        o_ref[...] = (g_ref[...] + p_ref[...]).astype(o_ref.dtype)

    blk = (1, 1, tr, n_c)
    return pl.pallas_call(
        body, name=name,
        grid_spec=pltpu.PrefetchScalarGridSpec(
            num_scalar_prefetch=1, grid=(N_CHIPS, 2, n_r // tr),
            in_specs=[pl.BlockSpec(blk, lambda qi, li, ri, c_ref: (qi, 2 * c_ref[0] + li, ri, 0)),
                      pl.BlockSpec(blk, lambda qi, li, ri, c_ref: (qi, li, ri, 0))],
            out_specs=pl.BlockSpec(blk, lambda qi, li, ri, c_ref: (qi, li, ri, 0))),
        out_shape=jax.ShapeDtypeStruct(peer.shape, BF),
        compiler_params=_cparams(("parallel", "parallel", "parallel")),
    )(c_arr, g, peer)


def _chip_exchange(parts):
    n = len(parts)

    def body(*refs):
        ins, outs = refs[:n], refs[n:2 * n]
        send_sems, recv_sems = refs[2 * n:]
        x, y, c, q, chips = _place()
        cps = []
        for a in range(n):
            for j, (cx, cy) in enumerate(chips):
                cps.append(_remote(ins[a].at[2 * cx + cy], outs[a].at[j], send_sems.at[3 * a + j], recv_sems.at[3 * a + j], (cx, cy, c)))
                cps[-1].start()
        for cp in cps:
            cp.wait()

    return pl.pallas_call(
        body, name="grad_chip_exchange", in_specs=[ANY] * n, out_specs=[ANY] * n,
        out_shape=[jax.ShapeDtypeStruct((3, *s.shape[1:]), s.dtype) for s in parts],
        scratch_shapes=[pltpu.SemaphoreType.DMA((3 * n,)), pltpu.SemaphoreType.DMA((3 * n,))],
    )(*parts)


def _chip_sum(name, own, r, chip_arr):
    _, _, n_r, n_c = r.shape
    tr = 256 if n_r % 256 == 0 else n_r

    def body(q_ref, r0, r1, r2, r3, o_ref):
        o_ref[...] = ((r0[0].astype(F32) + r1[0].astype(F32)) + r2[0].astype(F32)) + r3[0].astype(F32)

    blk = (1, 1, tr, n_c)
    return pl.pallas_call(
        body, name=name,
        grid_spec=pltpu.PrefetchScalarGridSpec(
            num_scalar_prefetch=1, grid=(2, n_r // tr),
            in_specs=[pl.BlockSpec(blk, lambda li, ri, q_ref: (q_ref[0], li, ri, 0))]
            + [pl.BlockSpec(blk, lambda li, ri, q_ref, s=s: (s, li, ri, 0)) for s in range(3)],
            out_specs=pl.BlockSpec((1, tr, n_c), lambda li, ri, q_ref: (li, ri, 0))),
        out_shape=jax.ShapeDtypeStruct((2, n_r, n_c), F32),
        compiler_params=_cparams(("parallel", "parallel")),
    )(chip_arr, own, r, r, r)


def _pair_share(halves):
    n = len(halves)

    def body(*refs):
        ins, outs = refs[:n], refs[n:2 * n]
        send_sems, recv_sems = refs[2 * n:]
        x, y, c, q, chips = _place()
        cps = [_remote(ins[a], outs[a], send_sems.at[a], recv_sems.at[a], (x, y, 1 - c)) for a in range(n)]
        for cp in cps:
            cp.start()
        for cp in cps:
            cp.wait()

    return pl.pallas_call(
        body, name="grad_pair_share", in_specs=[ANY] * n, out_specs=[ANY] * n,
        out_shape=[jax.ShapeDtypeStruct(h.shape, h.dtype) for h in halves],
        scratch_shapes=[pltpu.SemaphoreType.DMA((n,)), pltpu.SemaphoreType.DMA((n,))],
    )(*halves)


def _allreduce_small(v):
    n_r = v.shape[0]

    def body(v_ref, o_ref, slots, send_sems, recv_sems):
        x, y, c = lax.axis_index("x"), lax.axis_index("y"), lax.axis_index("c")
        me = 4 * x + 2 * y + c
        slots[me] = v_ref[...]
        cps = []
        for r in range(1, 8):
            px = 1 - x if r & 4 else x
            py = 1 - y if r & 2 else y
            pc = 1 - c if r & 1 else c
            cps.append(_remote(v_ref, slots.at[me], send_sems.at[r - 1], recv_sems.at[r - 1], (px, py, pc)))
            cps[-1].start()
        for cp in cps:
            cp.wait()
        tot = slots[0]
        for d in range(1, 8):
            tot = tot + slots[d]
        o_ref[...] = tot

    return pl.pallas_call(
        body, name="allreduce_small",
        in_specs=[pl.BlockSpec(memory_space=pltpu.VMEM)], out_specs=pl.BlockSpec(memory_space=pltpu.VMEM),
        out_shape=jax.ShapeDtypeStruct(v.shape, F32),
        scratch_shapes=[pltpu.VMEM((8, n_r, 128), F32), pltpu.SemaphoreType.DMA((7,)), pltpu.SemaphoreType.DMA((7,))],
        compiler_params=pltpu.CompilerParams(vmem_limit_bytes=VMEM_LIMIT_BYTES),
    )(v)


def _cols_full(g, l):
    s = g[:, l]
    return s.transpose(1, 0, 2).reshape(s.shape[1], -1)


def _rows_full(g, l):
    s = g[:, l]
    return s.reshape(-1, s.shape[-1])


def _cols_split(full):
    r = full.shape[0]
    return full.reshape(r, N_CHIPS, -1).transpose(1, 0, 2)


def _rows_split(full):
    return full.reshape(N_CHIPS, -1, full.shape[-1])


def _pack(parts):
    flat = []
    for a in parts:
        f = a.reshape(-1).astype(F32)
        flat.append(jnp.pad(f, (0, (-f.shape[0]) % 1024)))
    return jnp.concatenate(flat).reshape(-1, 128)


def _unpack(buf, shapes):
    flat = buf.reshape(-1)
    out, off = [], 0
    for s in shapes:
        size = 1
        for d in s:
            size *= d
        out.append(flat[off:off + size].reshape(s))
        off += size + (-size) % 1024
    return out


def kernel(x, p, g_mix_pre, w_in, b_forget, w_conf_dw, conf_ln_g, conf_ln_b, w_conf_pw, w_sc, w_pool, pool_scale, w_out, g_mix_post, g_mlp_pre, w_up, w_down, g_mlp_post, g_ple_pre, w_ple_gate, w_ple_proj, g_ple_post, loss_target, m_g_mix_pre, m_w_in, m_b_forget, m_w_conf_dw, m_conf_ln_g, m_conf_ln_b, m_w_conf_pw, m_w_sc, m_w_pool, m_pool_scale, m_w_out, m_g_mix_post, m_g_mlp_pre, m_w_up, m_w_down, m_g_mlp_post, m_g_ple_pre, m_w_ple_gate, m_w_ple_proj, m_g_ple_post, v_g_mix_pre, v_w_in, v_b_forget, v_w_conf_dw, v_conf_ln_g, v_conf_ln_b, v_w_conf_pw, v_w_sc, v_w_pool, v_pool_scale, v_w_out, v_g_mix_post, v_g_mlp_pre, v_w_up, v_w_down, v_g_mlp_post, v_g_ple_pre, v_w_ple_gate, v_w_ple_proj, v_g_ple_post):
    names = ['g_mix_pre', 'w_in', 'b_forget', 'w_conf_dw', 'conf_ln_g', 'conf_ln_b', 'w_conf_pw', 'w_sc', 'w_pool', 'pool_scale',
             'w_out', 'g_mix_post', 'g_mlp_pre', 'w_up', 'w_down', 'g_mlp_post', 'g_ple_pre', 'w_ple_gate', 'w_ple_proj', 'g_ple_post']
    env = locals()
    wts = {k: env[k] for k in names}
    mom = {k: env["m_" + k] for k in names}
    var = {k: env["v_" + k] for k in names}

    t_len = x.shape[1]
    tm = min(256, t_len)
    tq = min(512, max(t_len // 2, 128))
    chip = 2 * lax.axis_index("x") + lax.axis_index("y")
    core = lax.axis_index("c")

    big = ['w_in', 'w_conf_pw', 'w_out', 'w_up', 'w_down', 'w_ple_gate', 'w_ple_proj']
    tiny = ['w_conf_dw', 'w_sc']
    own_shards = [wts[k].astype(BF) for k in big] + [wts[k] for k in tiny]
    gat = {k: _by_chip(own, others, chip) for k, own, others in zip(big + tiny, own_shards, _allgather_weights(own_shards))}

    def layer_weights(l):
        w_full = _cols_full(gat['w_in'], l)
        w_a = jnp.concatenate([w_full[:, :F_OFF], w_full[:, F_OFF + N_HEADS:], w_full[:, F_OFF:F_OFF + N_HEADS],
                               jnp.zeros((D_MODEL, Z_W - D_IN), BF)], axis=1)
        w_bd = jnp.zeros((D_GRP, D_GRP), F32)
        for g in range(4):
            w_bd = lax.dynamic_update_slice(w_bd, wts['w_pool'][l, g], (64 * g, 64 * g))
        row = lambda a: a[l][None, :]
        return dict(
            w_a=w_a, w_dw=jnp.pad(_cols_full(gat['w_conf_dw'], l), ((0, 1), (0, 0))), w_pw=_rows_full(gat['w_conf_pw'], l),
            w_sc=jnp.pad(_cols_full(gat['w_sc'], l), ((0, 5), (0, 0))), w_bd=w_bd.astype(BF),
            w_out=_rows_full(gat['w_out'], l), w_up=_cols_full(gat['w_up'], l), w_down=_rows_full(gat['w_down'], l),
            w_gate=_rows_full(gat['w_ple_gate'], l), w_proj=_cols_full(gat['w_ple_proj'], l),
            b_f=jnp.pad(wts['b_forget'][l], (0, F_PAD - N_HEADS))[None, :],
            ln_g=row(wts['conf_ln_g']), ln_b=row(wts['conf_ln_b']), pool_scale=row(wts['pool_scale']),
            g_mix_pre=row(wts['g_mix_pre']), g_mix_post=row(wts['g_mix_post']), g_mlp_pre=row(wts['g_mlp_pre']),
            g_mlp_post=row(wts['g_mlp_post']), g_ple_pre=row(wts['g_ple_pre']), g_ple_post=row(wts['g_ple_post']))

    lw = [layer_weights(l) for l in range(DEPTH)]

    h = x[0]
    saved = []
    for l in range(DEPTH):
        w = lw[l]
        s = dict(h0=h)
        s['zc'], s['qa'], kv, s['zs'], s['zp'], s['zf'] = _mix_in_fwd(t_len, tm, h, w['g_mix_pre'], w['w_a'])
        s['cv'], y_conf = _conf_fwd(t_len, tm, s['zc'], w['w_dw'], w['ln_g'], w['ln_b'], w['w_pw'])
        s['ka'], s['va'] = _fox_prep(t_len, s['zf'], kv, w['b_f'])
        s['o'] = _fox_fwd(t_len, tq, s['qa'], s['ka'], s['va'])
        (y_sc,) = _sconv_fwd(t_len, tm, s['zs'], w['w_sc'])
        (y_pool,) = _pool_fwd(t_len, tm, s['zp'], w['w_bd'], w['pool_scale'])
        s['ys'] = [y_conf, s['o'], y_sc, y_pool]
        s['mix'], h = _mix_out_fwd(t_len, tm, s['ys'], h, w['w_out'], w['g_mix_post'])
        s['h1'] = h
        s['up'] = _mlp_up_fwd(t_len, tm, h, w['g_mlp_pre'], w['w_up'])
        s['ff'], h = _mlp_down_fwd(t_len, tm, s['up'], h, w['w_down'], w['g_mlp_post'])
        s['h2'] = h
        s['gpre'], s['pe'], h = _ple_fwd(t_len, tm, h, p[l, 0], w['g_ple_pre'], w['w_gate'], w['w_proj'], w['g_ple_post'])
        saved.append(s)

    dh, loss_part = _loss_call(t_len, tm, h, loss_target[0])

    grads = [None] * DEPTH
    for l in reversed(range(DEPTH)):
        w, s, g = lw[l], saved[l], {}
        dh, g['w_ple_gate'], g['w_ple_proj'], g['g_ple_pre'], g['g_ple_post'] = _ple_bwd(
            t_len, tm, dh, s['h2'], s['gpre'], s['pe'], p[l, 0], w['g_ple_pre'], w['w_gate'], w['g_ple_post'])
        dff, dup, g['g_mlp_post'] = _mlp_bwd_a(t_len, tm, dh, s['ff'], s['up'], w['w_down'], w['g_mlp_post'])
        dh, hn, g['g_mlp_pre'] = _mlp_bwd_b(t_len, tm, dh, s['h1'], dup, w['w_up'], w['g_mlp_pre'])
        tt = min(512, t_len)
        g['w_up'] = _mm_tn("mlp_dw_up", hn, dup, D_MODEL, 1024, tt)
        g['w_down'] = _mm_tn("mlp_dw_down", s['up'], dff, 1024, D_MODEL, tt, pro=lambda u: jnp.square(jnp.maximum(u, 0.0)))
        dy_conf, dy_att, dy_sc, dy_pool, g['w_out'], g['g_mix_post'] = _mix_out_bwd(t_len, tm, dh, s['mix'], s['ys'], w['w_out'], w['g_mix_post'])
        dzc, g['w_conf_dw'], g['conf_ln_g'], g['conf_ln_b'], g['w_conf_pw'] = _conf_bwd(
            t_len, tm, s['zc'], s['cv'], dy_conf, w['w_dw'], w['ln_g'], w['ln_b'], w['w_pw'])
        dzs, g['w_sc'] = _sconv_bwd(t_len, tm, s['zs'], dy_sc, w['w_sc'])
        dzp, d_wbd, g['pool_scale'] = _pool_bwd(t_len, tm, s['zp'], dy_pool, w['w_bd'], w['pool_scale'])
        g['w_pool'] = jnp.stack([d_wbd[64 * a:64 * (a + 1), 64 * a:64 * (a + 1)] for a in range(4)])
        dqa, dka, dva = _fox_bwd(t_len, tq, s['qa'], s['ka'], s['va'], dy_att, s['o'])
        dzf, d_bf = _fox_post(t_len, dqa, dka, s['zf'], w['b_f'])
        g['b_forget'] = d_bf[0, :N_HEADS]
        dh, xn, dz, g['g_mix_pre'] = _mix_in_bwd(t_len, tm, dh, s['h0'], dzc, dqa, dka, dva, dzs, dzp, dzf, w['g_mix_pre'], w['w_a'])
        d_wa = _mm_tn("mix_dw_in", xn, dz, D_MODEL, Z_W, tt)
        g['w_in'] = jnp.concatenate([d_wa[:, :F_OFF], d_wa[:, Z_F:Z_F + N_HEADS], d_wa[:, F_OFF:Z_F]], axis=1)
        g['w_conf_dw'] = g['w_conf_dw'][:CONF_K]
        g['w_sc'] = g['w_sc'][:SC_K]
        grads[l] = g
    grad_x = dh[None]

    split = dict(w_in=_cols_split, w_conf_pw=_rows_split, w_out=_rows_split, w_up=_cols_split, w_down=_rows_split,
                 w_ple_gate=_rows_split, w_ple_proj=_cols_split)
    contrib = [jnp.stack([split[k](grads[l][k]) for l in range(DEPTH)], axis=1) for k in big]
    peer = _pair_exchange(contrib)
    c_arr = core.astype(jnp.int32).reshape(1)
    parts = [_pair_sum("grad_pair_sum_" + k, a, b, c_arr) for k, a, b in zip(big, contrib, peer)]
    landed = _chip_exchange(parts)
    chip_arr = chip.astype(jnp.int32).reshape(1)
    halves = [_chip_sum("grad_chip_sum_" + k, own, r, chip_arr) for k, own, r in zip(big, parts, landed)]
    reduced = {}
    for k, mine, theirs in zip(big, halves, _pair_share(halves)):
        both = jnp.stack([mine, theirs])
        reduced[k] = jnp.concatenate([lax.dynamic_index_in_dim(both, core, 0, keepdims=False),
                                      lax.dynamic_index_in_dim(both, 1 - core, 0, keepdims=False)], axis=0)

    small = [k for k in names if k not in big]
    small_shapes = [(DEPTH, *grads[0][k].shape) for k in small]
    packed = _pack([jnp.stack([grads[l][k] for l in range(DEPTH)]) for k in small] + [loss_part])
    summed = _allreduce_small(packed)
    small_sum = _unpack(summed, small_shapes + [(8, 128)])
    loss = small_sum[-1][0, 0]
    for k, a in zip(small, small_sum[:-1]):
        if k in tiny:
            a = lax.dynamic_slice_in_dim(a, chip * 64, 64, axis=2)
        reduced[k] = a.reshape(wts[k].shape)

    delta_w, new_m, new_v = {}, {}, {}
    for k in names:
        shp = wts[k].shape
        as3 = (lambda a: a.reshape(shp[0], -1, shp[-1])) if len(shp) > 2 else (lambda a: a.reshape(1, shp[0], shp[1]))
        d, nm, nv = _adamw_call("adamw_" + k, as3(wts[k]), as3(reduced[k]), as3(mom[k]), as3(var[k]))
        delta_w[k], new_m[k], new_v[k] = d.reshape(shp), nm.reshape(shp), nv.reshape(shp)

    return (loss, grad_x, *[reduced[k] for k in names], *[delta_w[k] for k in names],
            *[new_m[k] for k in names], *[new_v[k] for k in names])
```

```python
import functools

import jax
import jax.numpy as jnp
from jax import lax
from jax.experimental import pallas as pl
from jax.experimental.pallas import tpu as pltpu

F32 = jnp.float32
BF = jnp.bfloat16

DEPTH = 4
D_MODEL = 1024
D_GRP = 256
HEAD_DIM = 64
N_HEADS = 4
CONF_K = 31
SC_K = 3
D_FF = 4096
D_PLE = 256
EPS = 1e-6
N_CHIPS = 4
Z_CONF, Z_QKV, Z_SC, Z_POOL, Z_F = 0, 512, 1280, 2048, 2304
Z_W = 2432
F_PAD = 128
D_IN = 2308
F_OFF = 1280

ADAM_LR, ADAM_B1, ADAM_B2, ADAM_EPS, ADAM_WD, ADAM_STEP = 0.001, 0.9, 0.999, 1e-08, 0.01, 10

VMEM_LIMIT_BYTES = 56 * 1024 * 1024
HALO = 32
NEG = -1e30
MESH_T = pl.DeviceIdType.MESH


def _cparams(sem=None):
    return pltpu.CompilerParams(dimension_semantics=sem, vmem_limit_bytes=VMEM_LIMIT_BYTES)


def _dot(a, b):
    return jnp.dot(a, b, preferred_element_type=F32)


def _dot_nt(a, b):
    return lax.dot_general(a, b, (((1,), (1,)), ((), ())), preferred_element_type=F32)


def _dot_tn(a, b):
    return lax.dot_general(a, b, (((0,), (0,)), ((), ())), preferred_element_type=F32)


def _sig(x):
    return jax.nn.sigmoid(x)


def _rms_fwd(x, g):
    r = lax.rsqrt(jnp.mean(x * x, axis=-1, keepdims=True) + EPS)
    return x * r * g


def _rms_bwd(x, g, dy):
    r = lax.rsqrt(jnp.mean(x * x, axis=-1, keepdims=True) + EPS)
    xh = x * r
    dg = jnp.sum(dy * xh, axis=0, keepdims=True)
    dxh = dy * g
    dx = r * (dxh - xh * jnp.mean(dxh * xh, axis=-1, keepdims=True))
    return dx, dg


def _back(ext, d):
    return ext if d == 0 else pltpu.roll(ext, d, 0)


def _ahead(ext, d):
    return ext if d == 0 else pltpu.roll(ext, ext.shape[0] - d, 0)


def _rows_call(name, fn, t_len, tm, rows, fulls, out_rows, out_accs=(), prevs=(), nexts=()):
    n = t_len // tm
    hb = tm // HALO
    nhb = t_len // HALO
    n_rows, n_prev, n_next, n_full = len(rows), len(prevs), len(nexts), len(fulls)
    in_specs = [pl.BlockSpec((tm, a.shape[1]), lambda i: (i, 0)) for a in rows]
    in_specs += [pl.BlockSpec((HALO, a.shape[1]), lambda i: (jnp.maximum(i * hb - 1, 0), 0)) for a in prevs]
    in_specs += [pl.BlockSpec((HALO, a.shape[1]), lambda i: (jnp.minimum((i + 1) * hb, nhb - 1), 0)) for a in nexts]
    in_specs += [pl.BlockSpec(a.shape, lambda i, nd=a.ndim: (0,) * nd) for a in fulls]
    out_shape = [jax.ShapeDtypeStruct((t_len, c), dt) for c, dt in out_rows]
    out_shape += [jax.ShapeDtypeStruct(s, F32) for s in out_accs]
    out_specs = [pl.BlockSpec((tm, c), lambda i: (i, 0)) for c, _ in out_rows]
    out_specs += [pl.BlockSpec(s, lambda i, nd=len(s): (0,) * nd) for s in out_accs]
    n_in = n_rows + n_prev + n_next + n_full
    n_ro = len(out_rows)

    def body(*refs):
        i = pl.program_id(0)
        ins, outs = refs[:n_in], refs[n_in:]
        rv = [r[...] for r in ins[:n_rows]]
        pv = [r[...] for r in ins[n_rows:n_rows + n_prev]]
        nv = [r[...] for r in ins[n_rows + n_prev:n_rows + n_prev + n_next]]
        fv = list(ins[n_rows + n_prev + n_next:])
        ro, ao = fn(i, n, rv, pv, nv, fv)
        for r, v in zip(outs[:n_ro], ro):
            r[...] = v.astype(r.dtype)
        if out_accs:
            acc = outs[n_ro:]

            @pl.when(i == 0)
            def _():
                for r in acc:
                    r[...] = jnp.zeros(r.shape, r.dtype)

            for r, v in zip(acc, ao):
                r[...] += v

    res = pl.pallas_call(
        body, name=name, grid=(n,), in_specs=in_specs, out_specs=out_specs, out_shape=out_shape,
        compiler_params=_cparams(("arbitrary",)),
    )(*rows, *prevs, *nexts, *fulls)
    return res


def _mm_tn(name, x, y, tk, tn, tt, pro=None):
    t_len, k_dim = x.shape
    n_dim = y.shape[1]

    def body(x_ref, y_ref, o_ref):
        @pl.when(pl.program_id(2) == 0)
        def _():
            o_ref[...] = jnp.zeros(o_ref.shape, o_ref.dtype)

        xv = x_ref[...]
        if pro is not None:
            xv = pro(xv)
        o_ref[...] += _dot_tn(xv.astype(BF), y_ref[...].astype(BF))

    return pl.pallas_call(
        body, name=name, grid=(k_dim // tk, n_dim // tn, t_len // tt),
        in_specs=[pl.BlockSpec((tt, tk), lambda a, b, t: (t, a)), pl.BlockSpec((tt, tn), lambda a, b, t: (t, b))],
        out_specs=pl.BlockSpec((tk, tn), lambda a, b, t: (a, b)),
        out_shape=jax.ShapeDtypeStruct((k_dim, n_dim), F32),
        compiler_params=_cparams(("parallel", "parallel", "arbitrary")),
    )(x, y)


QK_W = 128
ATT_W = N_HEADS * QK_W
C_COL = HEAD_DIM
ONE_COL = HEAD_DIM + 3


def _lane(shape):
    return lax.broadcasted_iota(jnp.int32, shape, 1)


def _head_low(a256, h):
    pair = a256[:, QK_W * (h // 2):QK_W * (h // 2 + 1)]
    return pltpu.roll(pair, HEAD_DIM, 1) if h % 2 else pair


def _heads_spread(a256):
    low = _lane((a256.shape[0], QK_W)) < HEAD_DIM
    return jnp.concatenate([jnp.where(low, _head_low(a256, h), 0.0) for h in range(N_HEADS)], axis=1)


def _heads_packed(a512):
    low = _lane((a512.shape[0], QK_W)) < HEAD_DIM
    out = []
    for pair in range(N_HEADS // 2):
        even = a512[:, QK_W * 2 * pair:QK_W * (2 * pair + 1)]
        odd = a512[:, QK_W * (2 * pair + 1):QK_W * (2 * pair + 2)]
        out.append(jnp.where(low, even, pltpu.roll(odd, HEAD_DIM, 1)))
    return jnp.concatenate(out, axis=1)


def _mix_in_fwd(t_len, tm, h, g_pre, w_a):
    def fn(i, n, rows, prevs, nexts, fulls):
        (hv,), (g, w) = rows, fulls
        z = _dot(_rms_fwd(hv, g[...]).astype(BF), w[...])
        lane = _lane((tm, QK_W))
        ones = jnp.where(lane < ONE_COL, 1.0, 0.0)
        zq = z[:, Z_QKV:Z_QKV + D_GRP]
        qa = jnp.concatenate([jnp.where(lane < HEAD_DIM, _head_low(zq, a) * (HEAD_DIM ** -0.5), ones) for a in range(N_HEADS)], axis=1)
        return [z[:, Z_CONF:Z_QKV], qa, z[:, Z_QKV + D_GRP:Z_SC], z[:, Z_SC:Z_POOL], z[:, Z_POOL:Z_F], z[:, Z_F:Z_W]], []

    return _rows_call("mix_in_fwd", fn, t_len, tm, [h], [g_pre, w_a],
                      [(512, F32), (ATT_W, BF), (2 * D_GRP, BF), (768, F32), (256, F32), (F_PAD, F32)])


def _mix_in_bwd(t_len, tm, dh, h, dzc, dqa, dka, dva, dzs, dzp, dzf, g_pre, w_a):
    def fn(i, n, rows, prevs, nexts, fulls):
        dhv, hv, a, dq, dk, dv, c, d, e = rows
        g, w = fulls
        b = jnp.concatenate([_heads_packed(dq), _heads_packed(dk), _heads_packed(dv)], axis=1).astype(BF)
        dz = jnp.concatenate([a, b, c, d, e], axis=1)
        dxn = _dot_nt(dz, w[...])
        dx, dg = _rms_bwd(hv, g[...], dxn)
        xn = _rms_fwd(hv, g[...])
        return [dhv + dx, xn, dz], [dg]

    return _rows_call("mix_in_bwd", fn, t_len, tm, [dh, h, dzc, dqa, dka, dva, dzs, dzp, dzf], [g_pre, w_a],
                      [(D_MODEL, F32), (D_MODEL, BF), (Z_W, BF)], [(1, D_MODEL)])


def _glu_ext(i, zc, zc_prev):
    ext = jnp.concatenate([zc_prev, zc], axis=0)
    u = ext[:, :D_GRP] * _sig(ext[:, D_GRP:])
    row = lax.broadcasted_iota(jnp.int32, u.shape, 0)
    return jnp.where((row >= HALO) | (i > 0), u, 0.0)


def _conf_fwd(t_len, tm, zc, w_dw, ln_g, ln_b, w_pw):
    def fn(i, n, rows, prevs, nexts, fulls):
        (zv,), (zp,) = rows, prevs
        wdw, lg, lb, wpw = fulls
        u = _glu_ext(i, zv, zp)
        cv = jnp.zeros((tm, D_GRP), F32)
        for k in range(CONF_K):
            cv = cv + wdw[k:k + 1, :] * _back(u, CONF_K - 1 - k)[HALO:, :]
        mu = jnp.mean(cv, axis=-1, keepdims=True)
        xc = cv - mu
        ln = xc * lax.rsqrt(jnp.mean(xc * xc, axis=-1, keepdims=True) + EPS) * lg[...] + lb[...]
        s = ln * _sig(ln)
        return [cv, _dot(s.astype(BF), wpw[...])], []

    return _rows_call("conf_fwd", fn, t_len, tm, [zc], [w_dw, ln_g, ln_b, w_pw], [(D_GRP, F32), (D_GRP, BF)], prevs=[zc])


def _conf_bwd(t_len, tm, zc, cv, dy, w_dw, ln_g, ln_b, w_pw):
    def fn(i, n, rows, prevs, nexts, fulls):
        zv, cvv, dyv = rows
        (zp,) = prevs
        cvn, dyn = nexts
        wdw, lg, lb, wpw = fulls
        cve = jnp.concatenate([cvv, cvn], axis=0)
        dye = jnp.concatenate([dyv, dyn], axis=0)
        mu = jnp.mean(cve, axis=-1, keepdims=True)
        xc = cve - mu
        rs = lax.rsqrt(jnp.mean(xc * xc, axis=-1, keepdims=True) + EPS)
        xh = xc * rs
        ln = xh * lg[...] + lb[...]
        sg = _sig(ln)
        s = ln * sg
        ds = _dot_nt(dye.astype(BF), wpw[...])
        dln = ds * (sg * (1.0 + ln * (1.0 - sg)))
        dxh = dln * lg[...]
        dcv = rs * (dxh - jnp.mean(dxh, axis=-1, keepdims=True) - xh * jnp.mean(dxh * xh, axis=-1, keepdims=True))
        row = lax.broadcasted_iota(jnp.int32, dcv.shape, 0)
        dcv = jnp.where((row < tm) | (i < n - 1), dcv, 0.0)
        d_lg = jnp.sum((dln * xh)[:tm], axis=0, keepdims=True)
        d_lb = jnp.sum(dln[:tm], axis=0, keepdims=True)
        d_wpw = _dot_tn(s[:tm].astype(BF), dyv.astype(BF))
        u = _glu_ext(i, zv, zp)
        dcv_cur = dcv[:tm]
        du = jnp.zeros((tm, D_GRP), F32)
        d_wdw = jnp.zeros((32, D_GRP), F32)
        krow = lax.broadcasted_iota(jnp.int32, (32, D_GRP), 0)
        for k in range(CONF_K):
            d = CONF_K - 1 - k
            du = du + wdw[k:k + 1, :] * _ahead(dcv, d)[:tm, :]
            tap = _back(u, d)[HALO:, :]
            d_wdw = d_wdw + jnp.where(krow == k, jnp.sum(dcv_cur * tap, axis=0, keepdims=True), 0.0)
        a, b = zv[:, :D_GRP], zv[:, D_GRP:]
        sb = _sig(b)
        dz = jnp.concatenate([du * sb, du * a * sb * (1.0 - sb)], axis=1)
        return [dz], [d_wdw, d_lg, d_lb, d_wpw]

    return _rows_call("conf_bwd", fn, t_len, tm, [zc, cv, dy], [w_dw, ln_g, ln_b, w_pw], [(512, BF)],
                      [(32, D_GRP), (1, D_GRP), (1, D_GRP), (D_GRP, D_GRP)], prevs=[zc], nexts=[cv, dy])


def _sc_ext(i, zs, zs_prev):
    ext = jnp.concatenate([zs_prev, zs], axis=0)
    e = ext[:, 2 * D_GRP:] * ext[:, :D_GRP]
    row = lax.broadcasted_iota(jnp.int32, e.shape, 0)
    return jnp.where((row >= HALO) | (i > 0), e, 0.0)


def _sconv_fwd(t_len, tm, zs, w_sc):
    def fn(i, n, rows, prevs, nexts, fulls):
        (zv,), (zp,), (w,) = rows, prevs, fulls
        e = _sc_ext(i, zv, zp)
        cv = jnp.zeros((tm, D_GRP), F32)
        for k in range(SC_K):
            cv = cv + w[k:k + 1, :] * _back(e, SC_K - 1 - k)[HALO:, :]
        return [zv[:, D_GRP:2 * D_GRP] * cv], []

    return _rows_call("sconv_fwd", fn, t_len, tm, [zs], [w_sc], [(D_GRP, BF)], prevs=[zs])


def _sconv_bwd(t_len, tm, zs, dy, w_sc):
    def fn(i, n, rows, prevs, nexts, fulls):
        zv, dyv = rows
        (zp,) = prevs
        zn, dyn = nexts
        (w,) = fulls
        e = _sc_ext(i, zv, zp)
        taps = [_back(e, SC_K - 1 - k)[HALO:, :] for k in range(SC_K)]
        cv = w[0:1, :] * taps[0] + w[1:2, :] * taps[1] + w[2:3, :] * taps[2]
        bg = zv[:, D_GRP:2 * D_GRP]
        dcv = jnp.concatenate([dyv * bg, dyn * zn[:, D_GRP:2 * D_GRP]], axis=0)
        row = lax.broadcasted_iota(jnp.int32, dcv.shape, 0)
        dcv = jnp.where((row < tm) | (i < n - 1), dcv, 0.0)
        de = jnp.zeros((tm, D_GRP), F32)
        d_w = jnp.zeros((8, D_GRP), F32)
        krow = lax.broadcasted_iota(jnp.int32, (8, D_GRP), 0)
        for k in range(SC_K):
            de = de + w[k:k + 1, :] * _ahead(dcv, SC_K - 1 - k)[:tm, :]
            d_w = d_w + jnp.where(krow == k, jnp.sum(dcv[:tm] * taps[k], axis=0, keepdims=True), 0.0)
        dz = jnp.concatenate([de * zv[:, 2 * D_GRP:], dyv * cv, de * zv[:, :D_GRP]], axis=1)
        return [dz], [d_w]

    return _rows_call("sconv_bwd", fn, t_len, tm, [zs, dy], [w_sc], [(768, BF)], [(8, D_GRP)], prevs=[zs], nexts=[zs, dy])


def _pool_window(shape):
    grp = lax.broadcasted_iota(jnp.int32, shape, 1) // 64
    return grp, jnp.where(grp == 0, 2.0, jnp.where(grp == 1, 4.0, jnp.where(grp == 2, 8.0, 16.0)))


def _pool_d(i, tm, zv, zp):
    ext = jnp.concatenate([zp, zv], axis=0)
    row = lax.broadcasted_iota(jnp.int32, ext.shape, 0)
    ext = jnp.where((row >= HALO) | (i > 0), ext, 0.0)
    s2 = ext + _back(ext, 1)
    s4 = s2 + _back(s2, 2)
    s8 = s4 + _back(s4, 4)
    s16 = s8 + _back(s8, 8)
    grp, win = _pool_window((tm, D_GRP))
    sel = jnp.where(grp == 0, s2[HALO:], jnp.where(grp == 1, s4[HALO:], jnp.where(grp == 2, s8[HALO:], s16[HALO:])))
    pos = (i * tm + lax.broadcasted_iota(jnp.int32, (tm, D_GRP), 0) + 1).astype(F32)
    return sel / jnp.minimum(pos, win) - zv


def _pool_fwd(t_len, tm, zpool, w_bd, scale):
    def fn(i, n, rows, prevs, nexts, fulls):
        (zv,), (zp,) = rows, prevs
        w, sc = fulls
        d = _pool_d(i, tm, zv, zp)
        return [_dot(d.astype(BF), w[...]) * sc[...]], []

    return _rows_call("pool_fwd", fn, t_len, tm, [zpool], [w_bd, scale], [(D_GRP, BF)], prevs=[zpool])


def _pool_bwd(t_len, tm, zpool, dy, w_bd, scale):
    def fn(i, n, rows, prevs, nexts, fulls):
        zv, dyv = rows
        (zp,) = prevs
        (dyn,) = nexts
        w, sc = fulls
        d = _pool_d(i, tm, zv, zp)
        lin = _dot(d.astype(BF), w[...])
        d_sc = jnp.sum(dyv * lin, axis=0, keepdims=True)
        dye = jnp.concatenate([dyv, dyn], axis=0) * sc[...]
        d_w = _dot_tn(d.astype(BF), dye[:tm].astype(BF))
        dd = _dot_nt(dye.astype(BF), w[...])
        row = lax.broadcasted_iota(jnp.int32, dd.shape, 0)
        dd = jnp.where((row < tm) | (i < n - 1), dd, 0.0)
        grp, win = _pool_window(dd.shape)
        pos = (i * tm + row + 1).astype(F32)
        ddc = dd / jnp.minimum(pos, win)
        f2 = ddc + _ahead(ddc, 1)
        f4 = f2 + _ahead(f2, 2)
        f8 = f4 + _ahead(f4, 4)
        f16 = f8 + _ahead(f8, 8)
        sel = jnp.where(grp == 0, f2, jnp.where(grp == 1, f4, jnp.where(grp == 2, f8, f16)))
        return [(sel - dd)[:tm]], [d_w, d_sc]

    return _rows_call("pool_bwd", fn, t_len, tm, [zpool, dy], [w_bd, scale], [(D_GRP, BF)],
                      [(D_GRP, D_GRP), (1, D_GRP)], prevs=[zpool], nexts=[dy])


def _mix_cat(y_conf, o_att, y_sc, y_pool):
    return jnp.concatenate([y_conf, _heads_packed(o_att).astype(BF), y_sc, y_pool], axis=1)


def _mix_out_fwd(t_len, tm, ys, h, w_out, g_post):
    def fn(i, n, rows, prevs, nexts, fulls):
        y0, y1, y2, y3, hv = rows
        w, g = fulls
        mix = _dot(_mix_cat(y0, y1, y2, y3), w[...])
        return [mix, hv + _rms_fwd(mix, g[...])], []

    return _rows_call("mix_out_fwd", fn, t_len, tm, [*ys, h], [w_out, g_post], [(D_MODEL, F32), (D_MODEL, F32)])


def _mix_out_bwd(t_len, tm, dh, mix, ys, w_out, g_post):
    def fn(i, n, rows, prevs, nexts, fulls):
        dhv, mv, y0, y1, y2, y3 = rows
        w, g = fulls
        dmix, dg = _rms_bwd(mv, g[...], dhv)
        dmb = dmix.astype(BF)
        dcat = _dot_nt(dmb, w[...])
        d_w = _dot_tn(_mix_cat(y0, y1, y2, y3), dmb)
        return [dcat[:, :256], _heads_spread(dcat[:, 256:512]), dcat[:, 512:768], dcat[:, 768:]], [d_w, dg]

    return _rows_call("mix_out_bwd", fn, t_len, tm, [dh, mix, *ys], [w_out, g_post],
                      [(D_GRP, F32), (ATT_W, F32), (D_GRP, F32), (D_GRP, F32)], [(D_MODEL, D_MODEL), (1, D_MODEL)])


def _log_sigmoid(x):
    return jnp.minimum(x, 0.0) - jnp.log(1.0 + jnp.exp(-jnp.abs(x)))


SCAN_BLK = 256


def _fox_prep(t_len, zf, kv, b_f):
    blk = min(SCAN_BLK, t_len)

    def body(zf_ref, kv_ref, b_ref, ka_ref, va_ref, carry_s):
        @pl.when(pl.program_id(0) == 0)
        def _():
            carry_s[...] = jnp.zeros(carry_s.shape, F32)

        tri = (lax.broadcasted_iota(jnp.int32, (blk, blk), 0) >= lax.broadcasted_iota(jnp.int32, (blk, blk), 1)).astype(F32)
        lf = _log_sigmoid(zf_ref[...] + b_ref[...])
        cs = jnp.dot(tri, lf, precision=lax.Precision.HIGHEST, preferred_element_type=F32) + carry_s[...]
        carry_s[...] = cs[blk - 1:blk, :]
        neg = -cs
        hi = neg.astype(BF).astype(F32)
        mid = (neg - hi).astype(BF).astype(F32)
        lo = ((neg - hi) - mid).astype(BF).astype(F32)
        kvv = kv_ref[...].astype(F32)
        lane = _lane((blk, QK_W))
        ka, va = [], []
        for a in range(N_HEADS):
            terms = jnp.where(lane == C_COL, hi[:, a:a + 1], jnp.where(lane == C_COL + 1, mid[:, a:a + 1], jnp.where(
                lane == C_COL + 2, lo[:, a:a + 1], jnp.where(lane == ONE_COL, HEAD_DIM ** 0.5, 0.0))))
            ka.append(jnp.where(lane < HEAD_DIM, _head_low(kvv[:, :D_GRP], a), terms))
            va.append(jnp.where(lane < HEAD_DIM, _head_low(kvv[:, D_GRP:], a), 0.0))
        ka_ref[...] = jnp.concatenate(ka, axis=1).astype(BF)
        va_ref[...] = jnp.concatenate(va, axis=1).astype(BF)

    row = lambda w: pl.BlockSpec((blk, w), lambda i: (i, 0))
    return pl.pallas_call(
        body, name="fox_prep", grid=(t_len // blk,),
        in_specs=[row(F_PAD), row(2 * D_GRP), pl.BlockSpec((1, F_PAD), lambda i: (0, 0))],
        out_specs=[row(ATT_W), row(ATT_W)],
        out_shape=[jax.ShapeDtypeStruct((t_len, ATT_W), BF)] * 2,
        scratch_shapes=[pltpu.VMEM((1, F_PAD), F32)],
        compiler_params=_cparams(("arbitrary",)),
    )(zf, kv, b_f)


def _fox_post(t_len, dqa, dka, zf, b_f):
    blk = min(SCAN_BLK, t_len)
    nb = t_len // blk

    def body(dq_ref, dk_ref, zf_ref, b_ref, dz_ref, db_ref, carry_s):
        @pl.when(pl.program_id(0) == 0)
        def _():
            carry_s[...] = jnp.zeros(carry_s.shape, F32)
            db_ref[...] = jnp.zeros(db_ref.shape, F32)

        tri = (lax.broadcasted_iota(jnp.int32, (blk, blk), 0) <= lax.broadcasted_iota(jnp.int32, (blk, blk), 1)).astype(F32)
        lane = _lane((blk, F_PAD))
        dc = jnp.zeros((blk, F_PAD), F32)
        for a in range(N_HEADS):
            head = slice(QK_W * a, QK_W * (a + 1))
            col = dq_ref[:, head][:, ONE_COL:ONE_COL + 1] - dk_ref[:, head][:, C_COL:C_COL + 1]
            dc = jnp.where(lane == a, col, dc)
        dlf = jnp.dot(tri, dc, precision=lax.Precision.HIGHEST, preferred_element_type=F32) + carry_s[...]
        carry_s[...] = dlf[0:1, :]
        dz = dlf * _sig(-(zf_ref[...] + b_ref[...]))
        dz_ref[...] = dz.astype(dz_ref.dtype)
        db_ref[...] += jnp.sum(dz, axis=0, keepdims=True)

    row = lambda w: pl.BlockSpec((blk, w), lambda i: (nb - 1 - i, 0))
    one = pl.BlockSpec((1, F_PAD), lambda i: (0, 0))
    return pl.pallas_call(
        body, name="fox_post", grid=(nb,),
        in_specs=[row(ATT_W), row(ATT_W), row(F_PAD), one], out_specs=[row(F_PAD), one],
        out_shape=(jax.ShapeDtypeStruct((t_len, F_PAD), BF), jax.ShapeDtypeStruct((1, F_PAD), F32)),
        scratch_shapes=[pltpu.VMEM((1, F_PAD), F32)],
        compiler_params=_cparams(("arbitrary",)),
    )(dqa, dka, zf, b_f)


def _tri_mask(tq, key_rows):
    r = lax.broadcasted_iota(jnp.int32, (tq, tq), 0)
    c = lax.broadcasted_iota(jnp.int32, (tq, tq), 1)
    return (r <= c) if key_rows else (r >= c)


LSE_COL = HEAD_DIM


def _fox_fwd(t_len, tq, qa, ka, va, gather=None):
    nq = t_len // tq
    plan, shards = gather if gather else (None, [])
    n = len(shards)

    def body(q_ref, k_ref, v_ref, *rest):
        ins, o_ref, outs, sems = rest[:n], rest[n], rest[n + 1:2 * n + 1], rest[2 * n + 1:]
        h, i = pl.program_id(0), pl.program_id(1)
        if plan:
            @pl.when((h == 0) & (i == 0))
            def _():
                plan.start(ins, outs, *sems)

        q = q_ref[...]

        def tile(j, carry, diagonal):
            m, l, acc = carry
            rows = pl.ds(pl.multiple_of(j * tq, tq), tq)
            s = _dot_nt(q, k_ref[rows, :])
            if diagonal:
                s = jnp.where(_tri_mask(tq, False), s, NEG)
            m_new = jnp.maximum(m, jnp.max(s, axis=-1, keepdims=True))
            alpha = jnp.exp(m - m_new)
            p = jnp.exp(s - m_new)
            l = alpha * l + jnp.sum(p, axis=-1, keepdims=True)
            acc = alpha * acc + _dot(p.astype(BF), v_ref[rows, :])
            return m_new, l, acc

        init = (jnp.full((tq, 1), NEG, F32), jnp.zeros((tq, 1), F32), jnp.zeros((tq, QK_W), F32))
        carry = lax.fori_loop(0, i, lambda j, c: tile(j, c, False), init)
        m, l, acc = tile(i, carry, True)
        o_ref[...] = jnp.where(_lane((tq, QK_W)) == LSE_COL, m + jnp.log(l), acc / l)
        if plan:
            @pl.when((h == N_HEADS - 1) & (i == nq - 1))
            def _():
                plan.finish(ins, outs, *sems)

    tile_spec = pl.BlockSpec((tq, QK_W), lambda h, i: (i, h))
    head_spec = pl.BlockSpec((t_len, QK_W), lambda h, i: (0, h))
    return pl.pallas_call(
        body, name="fox_fwd_gather" if plan else "fox_fwd", grid=(N_HEADS, nq),
        in_specs=[tile_spec, head_spec, head_spec] + [ANY] * n, out_specs=[tile_spec] + [ANY] * n,
        out_shape=[jax.ShapeDtypeStruct((t_len, ATT_W), F32)] + (plan.out_shape if plan else []),
        scratch_shapes=plan.scratch if plan else [],
        compiler_params=_cparams(("arbitrary", "arbitrary")),
    )(qa, ka, va, *shards)


def _fox_bwd(t_len, tq, qa, ka, va, do, o):
    nq = t_len // tq

    def body(q_ref, k_ref, v_ref, do_ref, o_ref, dq_ref, dk_ref, dv_ref, acc_s, dl_s, lse_s):
        j = pl.program_id(1)

        @pl.when(j == 0)
        def _():
            acc_s[...] = jnp.zeros(acc_s.shape, F32)
            row_of = lambda sel, a: lax.dot_general(sel, a, (((1,), (1,)), ((), ())), precision=lax.Precision.HIGHEST,
                                                    preferred_element_type=F32)
            dl_s[...] = row_of(jnp.ones((8, QK_W), F32), do_ref[...] * o_ref[...])
            lse_s[...] = row_of(jnp.where(_lane((8, QK_W)) == LSE_COL, 1.0, 0.0), o_ref[...])

        kt, vt = k_ref[...], v_ref[...]

        def tile(i, carry, diagonal):
            dk, dv = carry
            rows = pl.ds(pl.multiple_of(i * tq, tq), tq)
            qt = q_ref[rows, :]
            dob = do_ref[rows, :].astype(BF)
            pt = jnp.exp(_dot_nt(kt, qt) - lse_s[0:1, rows])
            if diagonal:
                pt = jnp.where(_tri_mask(tq, True), pt, 0.0)
            dst = (pt * (_dot_nt(vt, dob) - dl_s[0:1, rows])).astype(BF)
            acc_s[rows, :] += _dot_tn(dst, kt)
            return dk + _dot(dst, qt), dv + _dot(pt.astype(BF), dob)

        carry = tile(j, (jnp.zeros((tq, QK_W), F32), jnp.zeros((tq, QK_W), F32)), True)
        dk, dv = lax.fori_loop(j + 1, nq, lambda i, c: tile(i, c, False), carry)
        dk_ref[...] = dk
        dv_ref[...] = dv

        @pl.when(j == nq - 1)
        def _():
            dq_ref[...] = acc_s[...] * (HEAD_DIM ** -0.5)

    head_spec = pl.BlockSpec((t_len, QK_W), lambda h, j: (0, h))
    tile_spec = pl.BlockSpec((tq, QK_W), lambda h, j: (j, h))
    return pl.pallas_call(
        body, name="fox_bwd", grid=(N_HEADS, nq),
        in_specs=[head_spec, tile_spec, tile_spec, head_spec, head_spec],
        out_specs=[head_spec, tile_spec, tile_spec],
        out_shape=[jax.ShapeDtypeStruct((t_len, ATT_W), F32)] * 3,
        scratch_shapes=[pltpu.VMEM((t_len, QK_W), F32), pltpu.VMEM((8, t_len), F32), pltpu.VMEM((8, t_len), F32)],
        compiler_params=_cparams(("parallel", "arbitrary")),
    )(qa, ka, va, do, o)


def _mlp_up_fwd(t_len, tm, h, g_pre, w_up):
    def fn(i, n, rows, prevs, nexts, fulls):
        (hv,), (g, w) = rows, fulls
        return [_dot(_rms_fwd(hv, g[...]).astype(BF), w[...])], []

    return _rows_call("mlp_up_fwd", fn, t_len, tm, [h], [g_pre, w_up], [(D_FF, F32)])[0]


def _mlp_down_fwd(t_len, tm, up, h, w_down, g_post):
    def fn(i, n, rows, prevs, nexts, fulls):
        (uv, hv), (w, g) = rows, fulls
        a = jnp.square(jnp.maximum(uv, 0.0))
        ff = _dot(a.astype(BF), w[...])
        return [ff, hv + _rms_fwd(ff, g[...])], []

    return _rows_call("mlp_down_fwd", fn, t_len, tm, [up, h], [w_down, g_post], [(D_MODEL, F32), (D_MODEL, F32)])


def _mlp_bwd_a(t_len, tm, dh, ff, up, w_down, g_post):
    def fn(i, n, rows, prevs, nexts, fulls):
        (dhv, fv, uv), (w, g) = rows, fulls
        dff, dg = _rms_bwd(fv, g[...], dhv)
        dfb = dff.astype(BF)
        dup = _dot_nt(dfb, w[...]) * (2.0 * jnp.maximum(uv, 0.0))
        return [dfb, dup], [dg]

    return _rows_call("mlp_bwd_a", fn, t_len, tm, [dh, ff, up], [w_down, g_post], [(D_MODEL, BF), (D_FF, BF)], [(1, D_MODEL)])


def _mlp_bwd_b(t_len, tm, dh, h, dup, w_up, g_pre):
    def fn(i, n, rows, prevs, nexts, fulls):
        (dhv, hv, duv), (w, g) = rows, fulls
        dhn = _dot_nt(duv, w[...])
        dx, dg = _rms_bwd(hv, g[...], dhn)
        return [dhv + dx, _rms_fwd(hv, g[...])], [dg]

    return _rows_call("mlp_bwd_b", fn, t_len, tm, [dh, h, dup], [w_up, g_pre], [(D_MODEL, F32), (D_MODEL, BF)], [(1, D_MODEL)])


def _ple_fwd(t_len, tm, h, p_i, g_pre, w_gate, w_proj, g_post):
    def fn(i, n, rows, prevs, nexts, fulls):
        (hv, pv), (g, wg, wp, gp) = rows, fulls
        gpre = _dot(_rms_fwd(hv, g[...]).astype(BF), wg[...])
        pe = _dot(pv.astype(BF), wp[...])
        return [gpre, pe, hv + _rms_fwd(pe * _sig(gpre), gp[...])], []

    return _rows_call("ple_fwd", fn, t_len, tm, [h, p_i], [g_pre, w_gate, w_proj, g_post], [(D_MODEL, F32)] * 3)


def _ple_bwd(t_len, tm, dh, h, gpre, pe, p_i, g_pre, w_gate, g_post):
    def fn(i, n, rows, prevs, nexts, fulls):
        (dhv, hv, gv, pev, pv), (g, wg, gp) = rows, fulls
        sg = _sig(gv)
        de, d_gp = _rms_bwd(pev * sg, gp[...], dhv)
        dpe = (de * sg).astype(BF)
        dgate = (de * pev * sg * (1.0 - sg)).astype(BF)
        d_wp = _dot_tn(pv.astype(BF), dpe)
        hn = _rms_fwd(hv, g[...])
        d_wg = _dot_tn(hn.astype(BF), dgate)
        dx, d_g = _rms_bwd(hv, g[...], _dot_nt(dgate, wg[...]))
        return [dhv + dx], [d_wg, d_wp, d_g, d_gp]

    return _rows_call("ple_bwd", fn, t_len, tm, [dh, h, gpre, pe, p_i], [g_pre, w_gate, g_post], [(D_MODEL, F32)],
                      [(D_MODEL, D_MODEL), (D_PLE, D_MODEL), (1, D_MODEL), (1, D_MODEL)])


def _loss_call(t_len, tm, h, target):
    def fn(i, n, rows, prevs, nexts, fulls):
        hv, tv = rows
        err = hv - tv
        part = 0.5 * jnp.sum(jnp.mean(err * err, axis=-1, keepdims=True), axis=0, keepdims=True)
        return [err * (1.0 / D_MODEL)], [jnp.broadcast_to(part, (8, 128))]

    return _rows_call("loss", fn, t_len, tm, [h, target], [], [(D_MODEL, F32)], [(8, 128)])


def _adamw_call(name, w, g, m, v):
    n_l, n_r, n_c = w.shape
    tr = 256 if n_r % 256 == 0 else n_r

    def body(w_ref, g_ref, m_ref, v_ref, d_ref, nm_ref, nv_ref):
        gv = g_ref[...]
        nm = ADAM_B1 * m_ref[...] + (1.0 - ADAM_B1) * gv
        nv = ADAM_B2 * v_ref[...] + (1.0 - ADAM_B2) * jnp.square(gv)
        m_hat = nm / (1.0 - ADAM_B1 ** ADAM_STEP)
        v_hat = nv / (1.0 - ADAM_B2 ** ADAM_STEP)
        d_ref[...] = -ADAM_LR * (m_hat / (jnp.sqrt(v_hat) + ADAM_EPS) + ADAM_WD * w_ref[...])
        nm_ref[...] = nm
        nv_ref[...] = nv

    spec = pl.BlockSpec((1, tr, n_c), lambda l, r: (l, r, 0))
    return pl.pallas_call(
        body, name=name, grid=(n_l, n_r // tr), in_specs=[spec] * 4, out_specs=[spec] * 3,
        out_shape=[jax.ShapeDtypeStruct(w.shape, F32)] * 3,
        compiler_params=_cparams(("parallel", "parallel")),
    )(w, g, m, v)


ANY = pl.BlockSpec(memory_space=pl.ANY)


def _place():
    x, y, c = lax.axis_index("x"), lax.axis_index("y"), lax.axis_index("c")
    chips = [(1 - x, y), (x, 1 - y), (1 - x, 1 - y)]
    return x, y, c, 2 * x + y, chips


def _remote(src, dst, send_sem, recv_sem, dev):
    return pltpu.make_async_remote_copy(src_ref=src, dst_ref=dst, send_sem=send_sem, recv_sem=recv_sem,
                                        device_id=dev, device_id_type=MESH_T)


class _GatherPlan:
    def __init__(self, shards, lo, nl):
        self.shapes = [s.shape for s in shards]
        self.lo, self.nl, self.n = lo, nl, len(shards)
        self.split = [s.shape[1] % 32 == 0 for s in shards]
        self.out_shape = [jax.ShapeDtypeStruct((3, nl, *s.shape[1:]), s.dtype) for s in shards]
        self.scratch = [pltpu.SemaphoreType.DMA((6 * self.n,)), pltpu.SemaphoreType.DMA((6 * self.n,))]

    def _views(self, ins, outs, a, c):
        lay, all_l = pl.ds(self.lo, self.nl), pl.ds(0, self.nl)
        if not self.split[a]:
            return ins[a].at[lay], (lambda j: outs[a].at[j]), None
        hr = self.shapes[a][1] // 2
        mine, other = pl.ds(c * hr, hr), pl.ds((1 - c) * hr, hr)
        return ins[a].at[lay, mine], (lambda j: outs[a].at[j, all_l, mine]), (lambda j: outs[a].at[j, all_l, other])

    def _ici(self, ins, outs, ssem, rsem):
        x, y, c, q, chips = _place()
        cps = []
        for a in range(self.n):
            src, land, _ = self._views(ins, outs, a, c)
            for j, chip in enumerate(chips):
                cps.append(_remote(src, land(j), ssem.at[6 * a + j], rsem.at[6 * a + j], (*chip, c)))
        return cps

    def start(self, ins, outs, ssem, rsem):
        for cp in self._ici(ins, outs, ssem, rsem):
            cp.start()

    def finish(self, ins, outs, ssem, rsem):
        x, y, c, q, chips = _place()
        sib = (x, y, 1 - c)
        cps = self._ici(ins, outs, ssem, rsem)
        for a in range(self.n):
            _, land, _ = self._views(ins, outs, a, c)
            for j in range(3):
                cps[3 * a + j].wait_recv()
                if self.split[a]:
                    cps.append(_remote(land(j), land(j), ssem.at[6 * a + 3 + j], rsem.at[6 * a + 3 + j], sib))
                    cps[-1].start()
        for a in range(self.n):
            _, _, other = self._views(ins, outs, a, c)
            for j in range(3):
                if self.split[a]:
                    _remote(other(j), other(j), ssem.at[6 * a + 3 + j], rsem.at[6 * a + 3 + j], sib).wait_recv()
        for cp in cps:
            cp.wait_send()


def _allgather_weights(shards, lo, nl):
    plan = _GatherPlan(shards, lo, nl)
    n = plan.n

    def body(*refs):
        ins, outs, sems = refs[:n], refs[n:2 * n], refs[2 * n:]
        plan.start(ins, outs, *sems)
        plan.finish(ins, outs, *sems)

    return pl.pallas_call(body, name="allgather_weights", in_specs=[ANY] * n, out_specs=[ANY] * n,
                          out_shape=plan.out_shape, scratch_shapes=plan.scratch)(*shards)


def _by_chip(own, others, chip):
    by_mask = jnp.stack([own, others[1], others[0], others[2]])
    return jnp.stack([lax.dynamic_index_in_dim(by_mask, jnp.bitwise_xor(chip, r), 0, keepdims=False) for r in range(N_CHIPS)])


def _pair_exchange(grads):
    n = len(grads)

    def body(*refs):
        ins, outs = refs[:n], refs[n:2 * n]
        send_sems, recv_sems = refs[2 * n:]
        x, y, c, q, chips = _place()
        cps = [_remote(ins[a].at[pl.ds(0, N_CHIPS), pl.ds(2 * (1 - c), 2)], outs[a], send_sems.at[a], recv_sems.at[a], (x, y, 1 - c))
               for a in range(n)]
        for cp in cps:
            cp.start()
        for cp in cps:
            cp.wait()

    return pl.pallas_call(
        body, name="grad_pair_exchange", in_specs=[ANY] * n, out_specs=[ANY] * n,
        out_shape=[jax.ShapeDtypeStruct((N_CHIPS, 2, *g.shape[2:]), g.dtype) for g in grads],
        scratch_shapes=[pltpu.SemaphoreType.DMA((n,)), pltpu.SemaphoreType.DMA((n,))],
    )(*grads)


def _pair_sum(name, g, peer, c_arr):
    _, _, n_r, n_c = g.shape
    tr = 256 if n_r % 256 == 0 else n_r

    def body(c_ref, g_ref, p_ref, o_ref):
        o_ref[...] = (g_ref[...] + p_ref[...]).astype(o_ref.dtype)

    blk = (1, 1, tr, n_c)
    return pl.pallas_call(
        body, name=name,
        grid_spec=pltpu.PrefetchScalarGridSpec(
            num_scalar_prefetch=1, grid=(N_CHIPS, 2, n_r // tr),
            in_specs=[pl.BlockSpec(blk, lambda qi, li, ri, c_ref: (qi, 2 * c_ref[0] + li, ri, 0)),
                      pl.BlockSpec(blk, lambda qi, li, ri, c_ref: (qi, li, ri, 0))],
            out_specs=pl.BlockSpec(blk, lambda qi, li, ri, c_ref: (qi, li, ri, 0))),
        out_shape=jax.ShapeDtypeStruct(peer.shape, BF),
        compiler_params=_cparams(("parallel", "parallel", "parallel")),
    )(c_arr, g, peer)


def _chip_exchange(parts):
    n = len(parts)

    def body(*refs):
        ins, outs = refs[:n], refs[n:2 * n]
        send_sems, recv_sems = refs[2 * n:]
        x, y, c, q, chips = _place()
        cps = []
        for a in range(n):
            for j, (cx, cy) in enumerate(chips):
                cps.append(_remote(ins[a].at[2 * cx + cy], outs[a].at[j], send_sems.at[3 * a + j], recv_sems.at[3 * a + j], (cx, cy, c)))
                cps[-1].start()
        for cp in cps:
            cp.wait()

    return pl.pallas_call(
        body, name="grad_chip_exchange", in_specs=[ANY] * n, out_specs=[ANY] * n,
        out_shape=[jax.ShapeDtypeStruct((3, *s.shape[1:]), s.dtype) for s in parts],
        scratch_shapes=[pltpu.SemaphoreType.DMA((3 * n,)), pltpu.SemaphoreType.DMA((3 * n,))],
    )(*parts)


def _chip_sum(name, own, r, chip_arr):
    _, _, n_r, n_c = r.shape
    tr = 256 if n_r % 256 == 0 else n_r

    def body(q_ref, r0, r1, r2, r3, o_ref):
        o_ref[...] = ((r0[0].astype(F32) + r1[0].astype(F32)) + r2[0].astype(F32)) + r3[0].astype(F32)

    blk = (1, 1, tr, n_c)
    return pl.pallas_call(
        body, name=name,
        grid_spec=pltpu.PrefetchScalarGridSpec(
            num_scalar_prefetch=1, grid=(2, n_r // tr),
            in_specs=[pl.BlockSpec(blk, lambda li, ri, q_ref: (q_ref[0], li, ri, 0))]
            + [pl.BlockSpec(blk, lambda li, ri, q_ref, s=s: (s, li, ri, 0)) for s in range(3)],
            out_specs=pl.BlockSpec((1, tr, n_c), lambda li, ri, q_ref: (li, ri, 0))),
        out_shape=jax.ShapeDtypeStruct((2, n_r, n_c), F32),
        compiler_params=_cparams(("parallel", "parallel")),
    )(chip_arr, own, r, r, r)


def _pair_share(halves):
    n = len(halves)

    def body(*refs):
        ins, outs = refs[:n], refs[n:2 * n]
        send_sems, recv_sems = refs[2 * n:]
        x, y, c, q, chips = _place()
        cps = [_remote(ins[a], outs[a], send_sems.at[a], recv_sems.at[a], (x, y, 1 - c)) for a in range(n)]
        for cp in cps:
            cp.start()
        for cp in cps:
            cp.wait()

    return pl.pallas_call(
        body, name="grad_pair_share", in_specs=[ANY] * n, out_specs=[ANY] * n,
        out_shape=[jax.ShapeDtypeStruct(h.shape, h.dtype) for h in halves],
        scratch_shapes=[pltpu.SemaphoreType.DMA((n,)), pltpu.SemaphoreType.DMA((n,))],
    )(*halves)


def _allreduce_small(v):
    n_r = v.shape[0]

    def body(v_ref, o_ref, slots, send_sems, recv_sems):
        x, y, c = lax.axis_index("x"), lax.axis_index("y"), lax.axis_index("c")
        me = 4 * x + 2 * y + c
        slots[me] = v_ref[...]
        cps = []
        for r in range(1, 8):
            px = 1 - x if r & 4 else x
            py = 1 - y if r & 2 else y
            pc = 1 - c if r & 1 else c
            cps.append(_remote(v_ref, slots.at[me], send_sems.at[r - 1], recv_sems.at[r - 1], (px, py, pc)))
            cps[-1].start()
        for cp in cps:
            cp.wait()
        tot = slots[0]
        for d in range(1, 8):
            tot = tot + slots[d]
        o_ref[...] = tot

    return pl.pallas_call(
        body, name="allreduce_small",
        in_specs=[pl.BlockSpec(memory_space=pltpu.VMEM)], out_specs=pl.BlockSpec(memory_space=pltpu.VMEM),
        out_shape=jax.ShapeDtypeStruct(v.shape, F32),
        scratch_shapes=[pltpu.VMEM((8, n_r, 128), F32), pltpu.SemaphoreType.DMA((7,)), pltpu.SemaphoreType.DMA((7,))],
        compiler_params=pltpu.CompilerParams(vmem_limit_bytes=VMEM_LIMIT_BYTES),
    )(v)


def _cols_full(g, l):
    s = g[:, l]
    return s.transpose(1, 0, 2).reshape(s.shape[1], -1)


def _rows_full(g, l):
    s = g[:, l]
    return s.reshape(-1, s.shape[-1])


def _cols_split(full):
    r = full.shape[0]
    return full.reshape(r, N_CHIPS, -1).transpose(1, 0, 2)


def _rows_split(full):
    return full.reshape(N_CHIPS, -1, full.shape[-1])


def _pack(parts):
    flat = []
    for a in parts:
        f = a.reshape(-1).astype(F32)
        flat.append(jnp.pad(f, (0, (-f.shape[0]) % 1024)))
    return jnp.concatenate(flat).reshape(-1, 128)


def _unpack(buf, shapes):
    flat = buf.reshape(-1)
    out, off = [], 0
    for s in shapes:
        size = 1
        for d in s:
            size *= d
        out.append(flat[off:off + size].reshape(s))
        off += size + (-size) % 1024
    return out


def kernel(x, p, g_mix_pre, w_in, b_forget, w_conf_dw, conf_ln_g, conf_ln_b, w_conf_pw, w_sc, w_pool, pool_scale, w_out, g_mix_post, g_mlp_pre, w_up, w_down, g_mlp_post, g_ple_pre, w_ple_gate, w_ple_proj, g_ple_post, loss_target, m_g_mix_pre, m_w_in, m_b_forget, m_w_conf_dw, m_conf_ln_g, m_conf_ln_b, m_w_conf_pw, m_w_sc, m_w_pool, m_pool_scale, m_w_out, m_g_mix_post, m_g_mlp_pre, m_w_up, m_w_down, m_g_mlp_post, m_g_ple_pre, m_w_ple_gate, m_w_ple_proj, m_g_ple_post, v_g_mix_pre, v_w_in, v_b_forget, v_w_conf_dw, v_conf_ln_g, v_conf_ln_b, v_w_conf_pw, v_w_sc, v_w_pool, v_pool_scale, v_w_out, v_g_mix_post, v_g_mlp_pre, v_w_up, v_w_down, v_g_mlp_post, v_g_ple_pre, v_w_ple_gate, v_w_ple_proj, v_g_ple_post):
    names = ['g_mix_pre', 'w_in', 'b_forget', 'w_conf_dw', 'conf_ln_g', 'conf_ln_b', 'w_conf_pw', 'w_sc', 'w_pool', 'pool_scale',
             'w_out', 'g_mix_post', 'g_mlp_pre', 'w_up', 'w_down', 'g_mlp_post', 'g_ple_pre', 'w_ple_gate', 'w_ple_proj', 'g_ple_post']
    env = locals()
    wts = {k: env[k] for k in names}
    mom = {k: env["m_" + k] for k in names}
    var = {k: env["v_" + k] for k in names}

    t_len = x.shape[1]
    tm = min(256, t_len)
    tq = min(512, max(t_len // 2, 128))
    chip = 2 * lax.axis_index("x") + lax.axis_index("y")
    core = lax.axis_index("c")

    big = ['w_in', 'w_conf_pw', 'w_out', 'w_up', 'w_down', 'w_ple_gate', 'w_ple_proj']
    tiny = ['w_conf_dw', 'w_sc']
    own_shards = [wts[k].astype(BF) for k in big] + [wts[k] for k in tiny]

    def chip_order(others, lo, nl):
        return {k: _by_chip(own[lo:lo + nl], oth, chip) for k, own, oth in zip(big + tiny, own_shards, others)}

    def layer_weights(gat, li, l):
        w_full = _cols_full(gat['w_in'], li)
        w_a = jnp.concatenate([w_full[:, :F_OFF], w_full[:, F_OFF + N_HEADS:], w_full[:, F_OFF:F_OFF + N_HEADS],
                               jnp.zeros((D_MODEL, Z_W - D_IN), BF)], axis=1)
        w_bd = jnp.zeros((D_GRP, D_GRP), F32)
        for g in range(4):
            w_bd = lax.dynamic_update_slice(w_bd, wts['w_pool'][l, g], (64 * g, 64 * g))
        row = lambda a: a[l][None, :]
        return dict(
            w_a=w_a, w_dw=jnp.pad(_cols_full(gat['w_conf_dw'], li), ((0, 1), (0, 0))), w_pw=_rows_full(gat['w_conf_pw'], li),
            w_sc=jnp.pad(_cols_full(gat['w_sc'], li), ((0, 5), (0, 0))), w_bd=w_bd.astype(BF),
            w_out=_rows_full(gat['w_out'], li), w_up=_cols_full(gat['w_up'], li), w_down=_rows_full(gat['w_down'], li),
            w_gate=_rows_full(gat['w_ple_gate'], li), w_proj=_cols_full(gat['w_ple_proj'], li),
            b_f=jnp.pad(wts['b_forget'][l], (0, F_PAD - N_HEADS))[None, :],
            ln_g=row(wts['conf_ln_g']), ln_b=row(wts['conf_ln_b']), pool_scale=row(wts['pool_scale']),
            g_mix_pre=row(wts['g_mix_pre']), g_mix_post=row(wts['g_mix_post']), g_mlp_pre=row(wts['g_mlp_pre']),
            g_mlp_post=row(wts['g_mlp_post']), g_ple_pre=row(wts['g_ple_pre']), g_ple_post=row(wts['g_ple_post']))

    lw = [layer_weights(chip_order(_allgather_weights(own_shards, 0, 1), 0, 1), 0, 0)]

    h = x[0]
    saved = []
    for l in range(DEPTH):
        w = lw[l]
        s = dict(h0=h)
        s['zc'], s['qa'], kv, s['zs'], s['zp'], s['zf'] = _mix_in_fwd(t_len, tm, h, w['g_mix_pre'], w['w_a'])
        s['cv'], y_conf = _conf_fwd(t_len, tm, s['zc'], w['w_dw'], w['ln_g'], w['ln_b'], w['w_pw'])
        s['ka'], s['va'] = _fox_prep(t_len, s['zf'], kv, w['b_f'])
        if l == 0:
            s['o'], *others = _fox_fwd(t_len, tq, s['qa'], s['ka'], s['va'],
                                       gather=(_GatherPlan(own_shards, 1, DEPTH - 1), own_shards))
            rest = chip_order(others, 1, DEPTH - 1)
            lw += [layer_weights(rest, li, li + 1) for li in range(DEPTH - 1)]
        else:
            (s['o'],) = _fox_fwd(t_len, tq, s['qa'], s['ka'], s['va'])
        (y_sc,) = _sconv_fwd(t_len, tm, s['zs'], w['w_sc'])
        (y_pool,) = _pool_fwd(t_len, tm, s['zp'], w['w_bd'], w['pool_scale'])
        s['ys'] = [y_conf, s['o'], y_sc, y_pool]
        s['mix'], h = _mix_out_fwd(t_len, tm, s['ys'], h, w['w_out'], w['g_mix_post'])
        s['h1'] = h
        s['up'] = _mlp_up_fwd(t_len, tm, h, w['g_mlp_pre'], w['w_up'])
        s['ff'], h = _mlp_down_fwd(t_len, tm, s['up'], h, w['w_down'], w['g_mlp_post'])
        s['h2'] = h
        s['gpre'], s['pe'], h = _ple_fwd(t_len, tm, h, p[l, 0], w['g_ple_pre'], w['w_gate'], w['w_proj'], w['g_ple_post'])
        saved.append(s)

    dh, loss_part = _loss_call(t_len, tm, h, loss_target[0])

    grads = [None] * DEPTH
    for l in reversed(range(DEPTH)):
        w, s, g = lw[l], saved[l], {}
        dh, g['w_ple_gate'], g['w_ple_proj'], g['g_ple_pre'], g['g_ple_post'] = _ple_bwd(
            t_len, tm, dh, s['h2'], s['gpre'], s['pe'], p[l, 0], w['g_ple_pre'], w['w_gate'], w['g_ple_post'])
        dff, dup, g['g_mlp_post'] = _mlp_bwd_a(t_len, tm, dh, s['ff'], s['up'], w['w_down'], w['g_mlp_post'])
        dh, hn, g['g_mlp_pre'] = _mlp_bwd_b(t_len, tm, dh, s['h1'], dup, w['w_up'], w['g_mlp_pre'])
        tt = min(512, t_len)
        g['w_up'] = _mm_tn("mlp_dw_up", hn, dup, D_MODEL, 1024, tt)
        g['w_down'] = _mm_tn("mlp_dw_down", s['up'], dff, 1024, D_MODEL, tt, pro=lambda u: jnp.square(jnp.maximum(u, 0.0)))
        dy_conf, dy_att, dy_sc, dy_pool, g['w_out'], g['g_mix_post'] = _mix_out_bwd(t_len, tm, dh, s['mix'], s['ys'], w['w_out'], w['g_mix_post'])
        dzc, g['w_conf_dw'], g['conf_ln_g'], g['conf_ln_b'], g['w_conf_pw'] = _conf_bwd(
            t_len, tm, s['zc'], s['cv'], dy_conf, w['w_dw'], w['ln_g'], w['ln_b'], w['w_pw'])
        dzs, g['w_sc'] = _sconv_bwd(t_len, tm, s['zs'], dy_sc, w['w_sc'])
        dzp, d_wbd, g['pool_scale'] = _pool_bwd(t_len, tm, s['zp'], dy_pool, w['w_bd'], w['pool_scale'])
        g['w_pool'] = jnp.stack([d_wbd[64 * a:64 * (a + 1), 64 * a:64 * (a + 1)] for a in range(4)])
        dqa, dka, dva = _fox_bwd(t_len, tq, s['qa'], s['ka'], s['va'], dy_att, s['o'])
        dzf, d_bf = _fox_post(t_len, dqa, dka, s['zf'], w['b_f'])
        g['b_forget'] = d_bf[0, :N_HEADS]
        dh, xn, dz, g['g_mix_pre'] = _mix_in_bwd(t_len, tm, dh, s['h0'], dzc, dqa, dka, dva, dzs, dzp, dzf, w['g_mix_pre'], w['w_a'])
        d_wa = _mm_tn("mix_dw_in", xn, dz, D_MODEL, Z_W, tt)
        g['w_in'] = jnp.concatenate([d_wa[:, :F_OFF], d_wa[:, Z_F:Z_F + N_HEADS], d_wa[:, F_OFF:Z_F]], axis=1)
        g['w_conf_dw'] = g['w_conf_dw'][:CONF_K]
        g['w_sc'] = g['w_sc'][:SC_K]
        grads[l] = g
    grad_x = dh[None]

    split = dict(w_in=_cols_split, w_conf_pw=_rows_split, w_out=_rows_split, w_up=_cols_split, w_down=_rows_split,
                 w_ple_gate=_rows_split, w_ple_proj=_cols_split)
    contrib = [jnp.stack([split[k](grads[l][k]) for l in range(DEPTH)], axis=1) for k in big]
    peer = _pair_exchange(contrib)
    c_arr = core.astype(jnp.int32).reshape(1)
    parts = [_pair_sum("grad_pair_sum_" + k, a, b, c_arr) for k, a, b in zip(big, contrib, peer)]
    landed = _chip_exchange(parts)
    chip_arr = chip.astype(jnp.int32).reshape(1)
    halves = [_chip_sum("grad_chip_sum_" + k, own, r, chip_arr) for k, own, r in zip(big, parts, landed)]
    reduced = {}
    for k, mine, theirs in zip(big, halves, _pair_share(halves)):
        both = jnp.stack([mine, theirs])
        reduced[k] = jnp.concatenate([lax.dynamic_index_in_dim(both, core, 0, keepdims=False),
                                      lax.dynamic_index_in_dim(both, 1 - core, 0, keepdims=False)], axis=0)

    small = [k for k in names if k not in big]
    small_shapes = [(DEPTH, *grads[0][k].shape) for k in small]
    packed = _pack([jnp.stack([grads[l][k] for l in range(DEPTH)]) for k in small] + [loss_part])
    summed = _allreduce_small(packed)
    small_sum = _unpack(summed, small_shapes + [(8, 128)])
    loss = small_sum[-1][0, 0]
    for k, a in zip(small, small_sum[:-1]):
        if k in tiny:
            a = lax.dynamic_slice_in_dim(a, chip * 64, 64, axis=2)
        reduced[k] = a.reshape(wts[k].shape)

    delta_w, new_m, new_v = {}, {}, {}
    for k in names:
        shp = wts[k].shape
        as3 = (lambda a: a.reshape(shp[0], -1, shp[-1])) if len(shp) > 2 else (lambda a: a.reshape(1, shp[0], shp[1]))
        d, nm, nv = _adamw_call("adamw_" + k, as3(wts[k]), as3(reduced[k]), as3(mom[k]), as3(var[k]))
        delta_w[k], new_m[k], new_v[k] = d.reshape(shp), nm.reshape(shp), nv.reshape(shp)

    return (loss, grad_x, *[reduced[k] for k in names], *[delta_w[k] for k in names],
            *[new_m[k] for k in names], *[new_v[k] for k in names])
```

```python
import functools

import jax
import jax.numpy as jnp
from jax import lax
from jax.experimental import pallas as pl
from jax.experimental.pallas import tpu as pltpu

F32 = jnp.float32
BF = jnp.bfloat16

DEPTH = 4
D_MODEL = 1024
D_GRP = 256
HEAD_DIM = 64
N_HEADS = 4
CONF_K = 31
SC_K = 3
D_FF = 4096
D_PLE = 256
EPS = 1e-6
N_CHIPS = 4
Z_CONF, Z_QKV, Z_SC, Z_POOL, Z_F = 0, 512, 1280, 2048, 2304
Z_W = 2432
F_PAD = 128
D_IN = 2308
F_OFF = 1280

ADAM_LR, ADAM_B1, ADAM_B2, ADAM_EPS, ADAM_WD, ADAM_STEP = 0.001, 0.9, 0.999, 1e-08, 0.01, 10

VMEM_LIMIT_BYTES = 56 * 1024 * 1024
HALO = 32
NEG = -1e30
MESH_T = pl.DeviceIdType.MESH


def _cparams(sem=None):
    return pltpu.CompilerParams(dimension_semantics=sem, vmem_limit_bytes=VMEM_LIMIT_BYTES)


def _dot(a, b):
    return jnp.dot(a, b, preferred_element_type=F32)


def _dot_nt(a, b):
    return lax.dot_general(a, b, (((1,), (1,)), ((), ())), preferred_element_type=F32)


def _dot_tn(a, b):
    return lax.dot_general(a, b, (((0,), (0,)), ((), ())), preferred_element_type=F32)


def _sig(x):
    return jax.nn.sigmoid(x)


def _rms_fwd(x, g):
    r = lax.rsqrt(jnp.mean(x * x, axis=-1, keepdims=True) + EPS)
    return x * r * g


def _rms_bwd(x, g, dy):
    r = lax.rsqrt(jnp.mean(x * x, axis=-1, keepdims=True) + EPS)
    xh = x * r
    dg = jnp.sum(dy * xh, axis=0, keepdims=True)
    dxh = dy * g
    dx = r * (dxh - xh * jnp.mean(dxh * xh, axis=-1, keepdims=True))
    return dx, dg


def _back(ext, d):
    return ext if d == 0 else pltpu.roll(ext, d, 0)


def _ahead(ext, d):
    return ext if d == 0 else pltpu.roll(ext, ext.shape[0] - d, 0)


def _rows_call(name, fn, t_len, tm, rows, fulls, out_rows, out_accs=(), prevs=(), nexts=(), comm=None):
    plan, comm_in = comm if comm else (None, [])
    n_comm = len(comm_in)
    n = t_len // tm
    hb = tm // HALO
    nhb = t_len // HALO
    n_rows, n_prev, n_next, n_full = len(rows), len(prevs), len(nexts), len(fulls)
    in_specs = [pl.BlockSpec((tm, a.shape[1]), lambda i: (i, 0)) for a in rows]
    in_specs += [pl.BlockSpec((HALO, a.shape[1]), lambda i: (jnp.maximum(i * hb - 1, 0), 0)) for a in prevs]
    in_specs += [pl.BlockSpec((HALO, a.shape[1]), lambda i: (jnp.minimum((i + 1) * hb, nhb - 1), 0)) for a in nexts]
    in_specs += [pl.BlockSpec(a.shape, lambda i, nd=a.ndim: (0,) * nd) for a in fulls]
    out_shape = [jax.ShapeDtypeStruct((t_len, c), dt) for c, dt in out_rows]
    out_shape += [jax.ShapeDtypeStruct(s, F32) for s in out_accs]
    out_specs = [pl.BlockSpec((tm, c), lambda i: (i, 0)) for c, _ in out_rows]
    out_specs += [pl.BlockSpec(s, lambda i, nd=len(s): (0,) * nd) for s in out_accs]
    n_in = n_rows + n_prev + n_next + n_full
    n_ro = len(out_rows)

    n_out = n_ro + len(out_accs)

    def body(*refs):
        i = pl.program_id(0)
        ins, outs = refs[:n_in], refs[n_in + n_comm:n_in + n_comm + n_out]
        c_ins, c_outs, sems = refs[n_in:n_in + n_comm], refs[n_in + n_comm + n_out:n_in + 2 * n_comm + n_out], refs[n_in + 2 * n_comm + n_out:]
        if plan:
            @pl.when(i == 0)
            def _():
                plan.start(c_ins, c_outs, *sems)

        rv = [r[...] for r in ins[:n_rows]]
        pv = [r[...] for r in ins[n_rows:n_rows + n_prev]]
        nv = [r[...] for r in ins[n_rows + n_prev:n_rows + n_prev + n_next]]
        fv = list(ins[n_rows + n_prev + n_next:])
        ro, ao = fn(i, n, rv, pv, nv, fv)
        for r, v in zip(outs[:n_ro], ro):
            r[...] = v.astype(r.dtype)
        if out_accs:
            acc = outs[n_ro:]

            @pl.when(i == 0)
            def _():
                for r in acc:
                    r[...] = jnp.zeros(r.shape, r.dtype)

            for r, v in zip(acc, ao):
                r[...] += v
        if plan:
            @pl.when(i == n - 1)
            def _():
                plan.finish(c_ins, c_outs, *sems)

    res = pl.pallas_call(
        body, name=name, grid=(n,), in_specs=in_specs + [ANY] * n_comm, out_specs=out_specs + [ANY] * n_comm,
        out_shape=out_shape + (plan.out_shape if plan else []), scratch_shapes=plan.scratch if plan else [],
        compiler_params=_cparams(("arbitrary",)),
    )(*rows, *prevs, *nexts, *fulls, *comm_in)
    return res


def _mm_tn(name, x, y, tk, tn, tt, pro=None):
    t_len, k_dim = x.shape
    n_dim = y.shape[1]

    def body(x_ref, y_ref, o_ref):
        @pl.when(pl.program_id(2) == 0)
        def _():
            o_ref[...] = jnp.zeros(o_ref.shape, o_ref.dtype)

        xv = x_ref[...]
        if pro is not None:
            xv = pro(xv)
        o_ref[...] += _dot_tn(xv.astype(BF), y_ref[...].astype(BF))

    return pl.pallas_call(
        body, name=name, grid=(k_dim // tk, n_dim // tn, t_len // tt),
        in_specs=[pl.BlockSpec((tt, tk), lambda a, b, t: (t, a)), pl.BlockSpec((tt, tn), lambda a, b, t: (t, b))],
        out_specs=pl.BlockSpec((tk, tn), lambda a, b, t: (a, b)),
        out_shape=jax.ShapeDtypeStruct((k_dim, n_dim), F32),
        compiler_params=_cparams(("parallel", "parallel", "arbitrary")),
    )(x, y)


QK_W = 128
ATT_W = N_HEADS * QK_W
C_COL = HEAD_DIM
ONE_COL = HEAD_DIM + 3


def _lane(shape):
    return lax.broadcasted_iota(jnp.int32, shape, 1)


def _head_low(a256, h):
    pair = a256[:, QK_W * (h // 2):QK_W * (h // 2 + 1)]
    return pltpu.roll(pair, HEAD_DIM, 1) if h % 2 else pair


def _heads_spread(a256):
    low = _lane((a256.shape[0], QK_W)) < HEAD_DIM
    return jnp.concatenate([jnp.where(low, _head_low(a256, h), 0.0) for h in range(N_HEADS)], axis=1)


def _heads_packed(a512):
    low = _lane((a512.shape[0], QK_W)) < HEAD_DIM
    out = []
    for pair in range(N_HEADS // 2):
        even = a512[:, QK_W * 2 * pair:QK_W * (2 * pair + 1)]
        odd = a512[:, QK_W * (2 * pair + 1):QK_W * (2 * pair + 2)]
        out.append(jnp.where(low, even, pltpu.roll(odd, HEAD_DIM, 1)))
    return jnp.concatenate(out, axis=1)


def _mix_in_fwd(t_len, tm, h, g_pre, w_a):
    def fn(i, n, rows, prevs, nexts, fulls):
        (hv,), (g, w) = rows, fulls
        z = _dot(_rms_fwd(hv, g[...]).astype(BF), w[...])
        lane = _lane((tm, QK_W))
        ones = jnp.where(lane < ONE_COL, 1.0, 0.0)
        zq = z[:, Z_QKV:Z_QKV + D_GRP]
        qa = jnp.concatenate([jnp.where(lane < HEAD_DIM, _head_low(zq, a) * (HEAD_DIM ** -0.5), ones) for a in range(N_HEADS)], axis=1)
        return [z[:, Z_CONF:Z_QKV], qa, z[:, Z_QKV + D_GRP:Z_SC], z[:, Z_SC:Z_POOL], z[:, Z_POOL:Z_F], z[:, Z_F:Z_W]], []

    return _rows_call("mix_in_fwd", fn, t_len, tm, [h], [g_pre, w_a],
                      [(512, F32), (ATT_W, BF), (2 * D_GRP, BF), (768, F32), (256, F32), (F_PAD, F32)])


def _mix_in_bwd(t_len, tm, dh, h, dzc, dqa, dka, dva, dzs, dzp, dzf, g_pre, w_a):
    def fn(i, n, rows, prevs, nexts, fulls):
        dhv, hv, a, dq, dk, dv, c, d, e = rows
        g, w = fulls
        b = jnp.concatenate([_heads_packed(dq), _heads_packed(dk), _heads_packed(dv)], axis=1).astype(BF)
        dz = jnp.concatenate([a, b, c, d, e], axis=1)
        dxn = _dot_nt(dz, w[...])
        dx, dg = _rms_bwd(hv, g[...], dxn)
        xn = _rms_fwd(hv, g[...])
        return [dhv + dx, xn, dz], [dg]

    return _rows_call("mix_in_bwd", fn, t_len, tm, [dh, h, dzc, dqa, dka, dva, dzs, dzp, dzf], [g_pre, w_a],
                      [(D_MODEL, F32), (D_MODEL, BF), (Z_W, BF)], [(1, D_MODEL)])


def _glu_ext(i, zc, zc_prev):
    ext = jnp.concatenate([zc_prev, zc], axis=0)
    u = ext[:, :D_GRP] * _sig(ext[:, D_GRP:])
    row = lax.broadcasted_iota(jnp.int32, u.shape, 0)
    return jnp.where((row >= HALO) | (i > 0), u, 0.0)


def _conf_fwd(t_len, tm, zc, w_dw, ln_g, ln_b, w_pw):
    def fn(i, n, rows, prevs, nexts, fulls):
        (zv,), (zp,) = rows, prevs
        wdw, lg, lb, wpw = fulls
        u = _glu_ext(i, zv, zp)
        cv = jnp.zeros((tm, D_GRP), F32)
        for k in range(CONF_K):
            cv = cv + wdw[k:k + 1, :] * _back(u, CONF_K - 1 - k)[HALO:, :]
        mu = jnp.mean(cv, axis=-1, keepdims=True)
        xc = cv - mu
        ln = xc * lax.rsqrt(jnp.mean(xc * xc, axis=-1, keepdims=True) + EPS) * lg[...] + lb[...]
        s = ln * _sig(ln)
        return [cv, _dot(s.astype(BF), wpw[...])], []

    return _rows_call("conf_fwd", fn, t_len, tm, [zc], [w_dw, ln_g, ln_b, w_pw], [(D_GRP, F32), (D_GRP, BF)], prevs=[zc])


def _conf_bwd(t_len, tm, zc, cv, dy, w_dw, ln_g, ln_b, w_pw):
    def fn(i, n, rows, prevs, nexts, fulls):
        zv, cvv, dyv = rows
        (zp,) = prevs
        cvn, dyn = nexts
        wdw, lg, lb, wpw = fulls
        cve = jnp.concatenate([cvv, cvn], axis=0)
        dye = jnp.concatenate([dyv, dyn], axis=0)
        mu = jnp.mean(cve, axis=-1, keepdims=True)
        xc = cve - mu
        rs = lax.rsqrt(jnp.mean(xc * xc, axis=-1, keepdims=True) + EPS)
        xh = xc * rs
        ln = xh * lg[...] + lb[...]
        sg = _sig(ln)
        s = ln * sg
        ds = _dot_nt(dye.astype(BF), wpw[...])
        dln = ds * (sg * (1.0 + ln * (1.0 - sg)))
        dxh = dln * lg[...]
        dcv = rs * (dxh - jnp.mean(dxh, axis=-1, keepdims=True) - xh * jnp.mean(dxh * xh, axis=-1, keepdims=True))
        row = lax.broadcasted_iota(jnp.int32, dcv.shape, 0)
        dcv = jnp.where((row < tm) | (i < n - 1), dcv, 0.0)
        d_lg = jnp.sum((dln * xh)[:tm], axis=0, keepdims=True)
        d_lb = jnp.sum(dln[:tm], axis=0, keepdims=True)
        d_wpw = _dot_tn(s[:tm].astype(BF), dyv.astype(BF))
        u = _glu_ext(i, zv, zp)
        dcv_cur = dcv[:tm]
        du = jnp.zeros((tm, D_GRP), F32)
        d_wdw = jnp.zeros((32, D_GRP), F32)
        krow = lax.broadcasted_iota(jnp.int32, (32, D_GRP), 0)
        for k in range(CONF_K):
            d = CONF_K - 1 - k
            du = du + wdw[k:k + 1, :] * _ahead(dcv, d)[:tm, :]
            tap = _back(u, d)[HALO:, :]
            d_wdw = d_wdw + jnp.where(krow == k, jnp.sum(dcv_cur * tap, axis=0, keepdims=True), 0.0)
        a, b = zv[:, :D_GRP], zv[:, D_GRP:]
        sb = _sig(b)
        dz = jnp.concatenate([du * sb, du * a * sb * (1.0 - sb)], axis=1)
        return [dz], [d_wdw, d_lg, d_lb, d_wpw]

    return _rows_call("conf_bwd", fn, t_len, tm, [zc, cv, dy], [w_dw, ln_g, ln_b, w_pw], [(512, BF)],
                      [(32, D_GRP), (1, D_GRP), (1, D_GRP), (D_GRP, D_GRP)], prevs=[zc], nexts=[cv, dy])


def _sc_ext(i, zs, zs_prev):
    ext = jnp.concatenate([zs_prev, zs], axis=0)
    e = ext[:, 2 * D_GRP:] * ext[:, :D_GRP]
    row = lax.broadcasted_iota(jnp.int32, e.shape, 0)
    return jnp.where((row >= HALO) | (i > 0), e, 0.0)


def _sconv_fwd(t_len, tm, zs, w_sc):
    def fn(i, n, rows, prevs, nexts, fulls):
        (zv,), (zp,), (w,) = rows, prevs, fulls
        e = _sc_ext(i, zv, zp)
        cv = jnp.zeros((tm, D_GRP), F32)
        for k in range(SC_K):
            cv = cv + w[k:k + 1, :] * _back(e, SC_K - 1 - k)[HALO:, :]
        return [zv[:, D_GRP:2 * D_GRP] * cv], []

    return _rows_call("sconv_fwd", fn, t_len, tm, [zs], [w_sc], [(D_GRP, BF)], prevs=[zs])


def _sconv_bwd(t_len, tm, zs, dy, w_sc):
    def fn(i, n, rows, prevs, nexts, fulls):
        zv, dyv = rows
        (zp,) = prevs
        zn, dyn = nexts
        (w,) = fulls
        e = _sc_ext(i, zv, zp)
        taps = [_back(e, SC_K - 1 - k)[HALO:, :] for k in range(SC_K)]
        cv = w[0:1, :] * taps[0] + w[1:2, :] * taps[1] + w[2:3, :] * taps[2]
        bg = zv[:, D_GRP:2 * D_GRP]
        dcv = jnp.concatenate([dyv * bg, dyn * zn[:, D_GRP:2 * D_GRP]], axis=0)
        row = lax.broadcasted_iota(jnp.int32, dcv.shape, 0)
        dcv = jnp.where((row < tm) | (i < n - 1), dcv, 0.0)
        de = jnp.zeros((tm, D_GRP), F32)
        d_w = jnp.zeros((8, D_GRP), F32)
        krow = lax.broadcasted_iota(jnp.int32, (8, D_GRP), 0)
        for k in range(SC_K):
            de = de + w[k:k + 1, :] * _ahead(dcv, SC_K - 1 - k)[:tm, :]
            d_w = d_w + jnp.where(krow == k, jnp.sum(dcv[:tm] * taps[k], axis=0, keepdims=True), 0.0)
        dz = jnp.concatenate([de * zv[:, 2 * D_GRP:], dyv * cv, de * zv[:, :D_GRP]], axis=1)
        return [dz], [d_w]

    return _rows_call("sconv_bwd", fn, t_len, tm, [zs, dy], [w_sc], [(768, BF)], [(8, D_GRP)], prevs=[zs], nexts=[zs, dy])


def _pool_window(shape):
    grp = lax.broadcasted_iota(jnp.int32, shape, 1) // 64
    return grp, jnp.where(grp == 0, 2.0, jnp.where(grp == 1, 4.0, jnp.where(grp == 2, 8.0, 16.0)))


def _pool_d(i, tm, zv, zp):
    ext = jnp.concatenate([zp, zv], axis=0)
    row = lax.broadcasted_iota(jnp.int32, ext.shape, 0)
    ext = jnp.where((row >= HALO) | (i > 0), ext, 0.0)
    s2 = ext + _back(ext, 1)
    s4 = s2 + _back(s2, 2)
    s8 = s4 + _back(s4, 4)
    s16 = s8 + _back(s8, 8)
    grp, win = _pool_window((tm, D_GRP))
    sel = jnp.where(grp == 0, s2[HALO:], jnp.where(grp == 1, s4[HALO:], jnp.where(grp == 2, s8[HALO:], s16[HALO:])))
    pos = (i * tm + lax.broadcasted_iota(jnp.int32, (tm, D_GRP), 0) + 1).astype(F32)
    return sel / jnp.minimum(pos, win) - zv


def _pool_fwd(t_len, tm, zpool, w_bd, scale):
    def fn(i, n, rows, prevs, nexts, fulls):
        (zv,), (zp,) = rows, prevs
        w, sc = fulls
        d = _pool_d(i, tm, zv, zp)
        return [_dot(d.astype(BF), w[...]) * sc[...]], []

    return _rows_call("pool_fwd", fn, t_len, tm, [zpool], [w_bd, scale], [(D_GRP, BF)], prevs=[zpool])


def _pool_bwd(t_len, tm, zpool, dy, w_bd, scale):
    def fn(i, n, rows, prevs, nexts, fulls):
        zv, dyv = rows
        (zp,) = prevs
        (dyn,) = nexts
        w, sc = fulls
        d = _pool_d(i, tm, zv, zp)
        lin = _dot(d.astype(BF), w[...])
        d_sc = jnp.sum(dyv * lin, axis=0, keepdims=True)
        dye = jnp.concatenate([dyv, dyn], axis=0) * sc[...]
        d_w = _dot_tn(d.astype(BF), dye[:tm].astype(BF))
        dd = _dot_nt(dye.astype(BF), w[...])
        row = lax.broadcasted_iota(jnp.int32, dd.shape, 0)
        dd = jnp.where((row < tm) | (i < n - 1), dd, 0.0)
        grp, win = _pool_window(dd.shape)
        pos = (i * tm + row + 1).astype(F32)
        ddc = dd / jnp.minimum(pos, win)
        f2 = ddc + _ahead(ddc, 1)
        f4 = f2 + _ahead(f2, 2)
        f8 = f4 + _ahead(f4, 4)
        f16 = f8 + _ahead(f8, 8)
        sel = jnp.where(grp == 0, f2, jnp.where(grp == 1, f4, jnp.where(grp == 2, f8, f16)))
        return [(sel - dd)[:tm]], [d_w, d_sc]

    return _rows_call("pool_bwd", fn, t_len, tm, [zpool, dy], [w_bd, scale], [(D_GRP, BF)],
                      [(D_GRP, D_GRP), (1, D_GRP)], prevs=[zpool], nexts=[dy])


def _mix_cat(y_conf, o_att, y_sc, y_pool):
    return jnp.concatenate([y_conf, _heads_packed(o_att).astype(BF), y_sc, y_pool], axis=1)


def _mix_out_fwd(t_len, tm, ys, h, w_out, g_post):
    def fn(i, n, rows, prevs, nexts, fulls):
        y0, y1, y2, y3, hv = rows
        w, g = fulls
        mix = _dot(_mix_cat(y0, y1, y2, y3), w[...])
        return [mix, hv + _rms_fwd(mix, g[...])], []

    return _rows_call("mix_out_fwd", fn, t_len, tm, [*ys, h], [w_out, g_post], [(D_MODEL, F32), (D_MODEL, F32)])


def _mix_out_bwd(t_len, tm, dh, mix, ys, w_out, g_post):
    def fn(i, n, rows, prevs, nexts, fulls):
        dhv, mv, y0, y1, y2, y3 = rows
        w, g = fulls
        dmix, dg = _rms_bwd(mv, g[...], dhv)
        dmb = dmix.astype(BF)
        dcat = _dot_nt(dmb, w[...])
        d_w = _dot_tn(_mix_cat(y0, y1, y2, y3), dmb)
        return [dcat[:, :256], _heads_spread(dcat[:, 256:512]), dcat[:, 512:768], dcat[:, 768:]], [d_w, dg]

    return _rows_call("mix_out_bwd", fn, t_len, tm, [dh, mix, *ys], [w_out, g_post],
                      [(D_GRP, F32), (ATT_W, F32), (D_GRP, F32), (D_GRP, F32)], [(D_MODEL, D_MODEL), (1, D_MODEL)])


def _log_sigmoid(x):
    return jnp.minimum(x, 0.0) - jnp.log(1.0 + jnp.exp(-jnp.abs(x)))


SCAN_BLK = 256


def _fox_prep(t_len, zf, kv, b_f):
    blk = min(SCAN_BLK, t_len)

    def body(zf_ref, kv_ref, b_ref, ka_ref, va_ref, carry_s):
        @pl.when(pl.program_id(0) == 0)
        def _():
            carry_s[...] = jnp.zeros(carry_s.shape, F32)

        tri = (lax.broadcasted_iota(jnp.int32, (blk, blk), 0) >= lax.broadcasted_iota(jnp.int32, (blk, blk), 1)).astype(F32)
        lf = _log_sigmoid(zf_ref[...] + b_ref[...])
        cs = jnp.dot(tri, lf, precision=lax.Precision.HIGHEST, preferred_element_type=F32) + carry_s[...]
        carry_s[...] = cs[blk - 1:blk, :]
        neg = -cs
        hi = neg.astype(BF).astype(F32)
        mid = (neg - hi).astype(BF).astype(F32)
        lo = ((neg - hi) - mid).astype(BF).astype(F32)
        kvv = kv_ref[...].astype(F32)
        lane = _lane((blk, QK_W))
        ka, va = [], []
        for a in range(N_HEADS):
            terms = jnp.where(lane == C_COL, hi[:, a:a + 1], jnp.where(lane == C_COL + 1, mid[:, a:a + 1], jnp.where(
                lane == C_COL + 2, lo[:, a:a + 1], jnp.where(lane == ONE_COL, HEAD_DIM ** 0.5, 0.0))))
            ka.append(jnp.where(lane < HEAD_DIM, _head_low(kvv[:, :D_GRP], a), terms))
            va.append(jnp.where(lane < HEAD_DIM, _head_low(kvv[:, D_GRP:], a), 0.0))
        ka_ref[...] = jnp.concatenate(ka, axis=1).astype(BF)
        va_ref[...] = jnp.concatenate(va, axis=1).astype(BF)

    row = lambda w: pl.BlockSpec((blk, w), lambda i: (i, 0))
    return pl.pallas_call(
        body, name="fox_prep", grid=(t_len // blk,),
        in_specs=[row(F_PAD), row(2 * D_GRP), pl.BlockSpec((1, F_PAD), lambda i: (0, 0))],
        out_specs=[row(ATT_W), row(ATT_W)],
        out_shape=[jax.ShapeDtypeStruct((t_len, ATT_W), BF)] * 2,
        scratch_shapes=[pltpu.VMEM((1, F_PAD), F32)],
        compiler_params=_cparams(("arbitrary",)),
    )(zf, kv, b_f)


def _fox_post(t_len, dqa, dka, zf, b_f):
    blk = min(SCAN_BLK, t_len)
    nb = t_len // blk

    def body(dq_ref, dk_ref, zf_ref, b_ref, dz_ref, db_ref, carry_s):
        @pl.when(pl.program_id(0) == 0)
        def _():
            carry_s[...] = jnp.zeros(carry_s.shape, F32)
            db_ref[...] = jnp.zeros(db_ref.shape, F32)

        tri = (lax.broadcasted_iota(jnp.int32, (blk, blk), 0) <= lax.broadcasted_iota(jnp.int32, (blk, blk), 1)).astype(F32)
        lane = _lane((blk, F_PAD))
        dc = jnp.zeros((blk, F_PAD), F32)
        for a in range(N_HEADS):
            head = slice(QK_W * a, QK_W * (a + 1))
            col = dq_ref[:, head][:, ONE_COL:ONE_COL + 1] - dk_ref[:, head][:, C_COL:C_COL + 1]
            dc = jnp.where(lane == a, col, dc)
        dlf = jnp.dot(tri, dc, precision=lax.Precision.HIGHEST, preferred_element_type=F32) + carry_s[...]
        carry_s[...] = dlf[0:1, :]
        dz = dlf * _sig(-(zf_ref[...] + b_ref[...]))
        dz_ref[...] = dz.astype(dz_ref.dtype)
        db_ref[...] += jnp.sum(dz, axis=0, keepdims=True)

    row = lambda w: pl.BlockSpec((blk, w), lambda i: (nb - 1 - i, 0))
    one = pl.BlockSpec((1, F_PAD), lambda i: (0, 0))
    return pl.pallas_call(
        body, name="fox_post", grid=(nb,),
        in_specs=[row(ATT_W), row(ATT_W), row(F_PAD), one], out_specs=[row(F_PAD), one],
        out_shape=(jax.ShapeDtypeStruct((t_len, F_PAD), BF), jax.ShapeDtypeStruct((1, F_PAD), F32)),
        scratch_shapes=[pltpu.VMEM((1, F_PAD), F32)],
        compiler_params=_cparams(("arbitrary",)),
    )(dqa, dka, zf, b_f)


def _tri_mask(tq, key_rows):
    r = lax.broadcasted_iota(jnp.int32, (tq, tq), 0)
    c = lax.broadcasted_iota(jnp.int32, (tq, tq), 1)
    return (r <= c) if key_rows else (r >= c)


LSE_COL = HEAD_DIM


def _fox_fwd(t_len, tq, qa, ka, va, gather=None):
    nq = t_len // tq
    plan, shards = gather if gather else (None, [])
    n = len(shards)

    def body(q_ref, k_ref, v_ref, *rest):
        ins, o_ref, outs, sems = rest[:n], rest[n], rest[n + 1:2 * n + 1], rest[2 * n + 1:]
        h, i = pl.program_id(0), pl.program_id(1)
        if plan:
            @pl.when((h == 0) & (i == 0))
            def _():
                plan.start(ins, outs, *sems)

        q = q_ref[...]

        def tile(j, carry, diagonal):
            m, l, acc = carry
            rows = pl.ds(pl.multiple_of(j * tq, tq), tq)
            s = _dot_nt(q, k_ref[rows, :])
            if diagonal:
                s = jnp.where(_tri_mask(tq, False), s, NEG)
            m_new = jnp.maximum(m, jnp.max(s, axis=-1, keepdims=True))
            alpha = jnp.exp(m - m_new)
            p = jnp.exp(s - m_new)
            l = alpha * l + jnp.sum(p, axis=-1, keepdims=True)
            acc = alpha * acc + _dot(p.astype(BF), v_ref[rows, :])
            return m_new, l, acc

        init = (jnp.full((tq, 1), NEG, F32), jnp.zeros((tq, 1), F32), jnp.zeros((tq, QK_W), F32))
        carry = lax.fori_loop(0, i, lambda j, c: tile(j, c, False), init)
        m, l, acc = tile(i, carry, True)
        o_ref[...] = jnp.where(_lane((tq, QK_W)) == LSE_COL, m + jnp.log(l), acc / l)
        if plan:
            @pl.when((h == N_HEADS - 1) & (i == nq - 1))
            def _():
                plan.finish(ins, outs, *sems)

    tile_spec = pl.BlockSpec((tq, QK_W), lambda h, i: (i, h))
    head_spec = pl.BlockSpec((t_len, QK_W), lambda h, i: (0, h))
    return pl.pallas_call(
        body, name="fox_fwd_gather" if plan else "fox_fwd", grid=(N_HEADS, nq),
        in_specs=[tile_spec, head_spec, head_spec] + [ANY] * n, out_specs=[tile_spec] + [ANY] * n,
        out_shape=[jax.ShapeDtypeStruct((t_len, ATT_W), F32)] + (plan.out_shape if plan else []),
        scratch_shapes=plan.scratch if plan else [],
        compiler_params=_cparams(("arbitrary", "arbitrary")),
    )(qa, ka, va, *shards)


def _fox_bwd(t_len, tq, qa, ka, va, do, o, comm=None):
    nq = t_len // tq
    plan, comm_in = comm if comm else (None, [])
    n = len(comm_in)

    def body(q_ref, k_ref, v_ref, do_ref, o_ref, *rest):
        c_ins, (dq_ref, dk_ref, dv_ref), c_outs = rest[:n], rest[n:n + 3], rest[n + 3:2 * n + 3]
        acc_s, dl_s, lse_s = rest[2 * n + 3:2 * n + 6]
        sems = rest[2 * n + 6:]
        hd, j = pl.program_id(0), pl.program_id(1)
        if plan:
            @pl.when((hd == 0) & (j == 0))
            def _():
                plan.start(c_ins, c_outs, *sems)

        @pl.when(j == 0)
        def _():
            acc_s[...] = jnp.zeros(acc_s.shape, F32)
            row_of = lambda sel, a: lax.dot_general(sel, a, (((1,), (1,)), ((), ())), precision=lax.Precision.HIGHEST,
                                                    preferred_element_type=F32)
            dl_s[...] = row_of(jnp.ones((8, QK_W), F32), do_ref[...] * o_ref[...])
            lse_s[...] = row_of(jnp.where(_lane((8, QK_W)) == LSE_COL, 1.0, 0.0), o_ref[...])

        kt, vt = k_ref[...], v_ref[...]

        def tile(i, carry, diagonal):
            dk, dv = carry
            rows = pl.ds(pl.multiple_of(i * tq, tq), tq)
            qt = q_ref[rows, :]
            dob = do_ref[rows, :].astype(BF)
            pt = jnp.exp(_dot_nt(kt, qt) - lse_s[0:1, rows])
            if diagonal:
                pt = jnp.where(_tri_mask(tq, True), pt, 0.0)
            dst = (pt * (_dot_nt(vt, dob) - dl_s[0:1, rows])).astype(BF)
            acc_s[rows, :] += _dot_tn(dst, kt)
            return dk + _dot(dst, qt), dv + _dot(pt.astype(BF), dob)

        carry = tile(j, (jnp.zeros((tq, QK_W), F32), jnp.zeros((tq, QK_W), F32)), True)
        dk, dv = lax.fori_loop(j + 1, nq, lambda i, c: tile(i, c, False), carry)
        dk_ref[...] = dk
        dv_ref[...] = dv

        @pl.when(j == nq - 1)
        def _():
            dq_ref[...] = acc_s[...] * (HEAD_DIM ** -0.5)

        if plan:
            @pl.when((hd == N_HEADS - 1) & (j == nq - 1))
            def _():
                plan.finish(c_ins, c_outs, *sems)

    head_spec = pl.BlockSpec((t_len, QK_W), lambda h, j: (0, h))
    tile_spec = pl.BlockSpec((tq, QK_W), lambda h, j: (j, h))
    return pl.pallas_call(
        body, name="fox_bwd_exchange" if plan else "fox_bwd", grid=(N_HEADS, nq),
        in_specs=[head_spec, tile_spec, tile_spec, head_spec, head_spec] + [ANY] * n,
        out_specs=[head_spec, tile_spec, tile_spec] + [ANY] * n,
        out_shape=[jax.ShapeDtypeStruct((t_len, ATT_W), F32)] * 3 + (plan.out_shape if plan else []),
        scratch_shapes=[pltpu.VMEM((t_len, QK_W), F32), pltpu.VMEM((8, t_len), F32), pltpu.VMEM((8, t_len), F32)]
        + (plan.scratch if plan else []),
        compiler_params=_cparams(("arbitrary", "arbitrary")),
    )(qa, ka, va, do, o, *comm_in)


def _mlp_up_fwd(t_len, tm, h, g_pre, w_up):
    def fn(i, n, rows, prevs, nexts, fulls):
        (hv,), (g, w) = rows, fulls
        return [_dot(_rms_fwd(hv, g[...]).astype(BF), w[...])], []

    return _rows_call("mlp_up_fwd", fn, t_len, tm, [h], [g_pre, w_up], [(D_FF, F32)])[0]


def _mlp_down_fwd(t_len, tm, up, h, w_down, g_post):
    def fn(i, n, rows, prevs, nexts, fulls):
        (uv, hv), (w, g) = rows, fulls
        a = jnp.square(jnp.maximum(uv, 0.0))
        ff = _dot(a.astype(BF), w[...])
        return [ff, hv + _rms_fwd(ff, g[...])], []

    return _rows_call("mlp_down_fwd", fn, t_len, tm, [up, h], [w_down, g_post], [(D_MODEL, F32), (D_MODEL, F32)])


def _mlp_bwd_a(t_len, tm, dh, ff, up, w_down, g_post):
    def fn(i, n, rows, prevs, nexts, fulls):
        (dhv, fv, uv), (w, g) = rows, fulls
        dff, dg = _rms_bwd(fv, g[...], dhv)
        dfb = dff.astype(BF)
        dup = _dot_nt(dfb, w[...]) * (2.0 * jnp.maximum(uv, 0.0))
        return [dfb, dup], [dg]

    return _rows_call("mlp_bwd_a", fn, t_len, tm, [dh, ff, up], [w_down, g_post], [(D_MODEL, BF), (D_FF, BF)], [(1, D_MODEL)])


def _mlp_bwd_b(t_len, tm, dh, h, dup, w_up, g_pre):
    def fn(i, n, rows, prevs, nexts, fulls):
        (dhv, hv, duv), (w, g) = rows, fulls
        dhn = _dot_nt(duv, w[...])
        dx, dg = _rms_bwd(hv, g[...], dhn)
        return [dhv + dx, _rms_fwd(hv, g[...])], [dg]

    return _rows_call("mlp_bwd_b", fn, t_len, tm, [dh, h, dup], [w_up, g_pre], [(D_MODEL, F32), (D_MODEL, BF)], [(1, D_MODEL)])


def _ple_fwd(t_len, tm, h, p_i, g_pre, w_gate, w_proj, g_post):
    def fn(i, n, rows, prevs, nexts, fulls):
        (hv, pv), (g, wg, wp, gp) = rows, fulls
        gpre = _dot(_rms_fwd(hv, g[...]).astype(BF), wg[...])
        pe = _dot(pv.astype(BF), wp[...])
        return [gpre, pe, hv + _rms_fwd(pe * _sig(gpre), gp[...])], []

    return _rows_call("ple_fwd", fn, t_len, tm, [h, p_i], [g_pre, w_gate, w_proj, g_post], [(D_MODEL, F32)] * 3)


def _ple_bwd(t_len, tm, dh, h, gpre, pe, p_i, g_pre, w_gate, g_post, comm=None):
    def fn(i, n, rows, prevs, nexts, fulls):
        (dhv, hv, gv, pev, pv), (g, wg, gp) = rows, fulls
        sg = _sig(gv)
        de, d_gp = _rms_bwd(pev * sg, gp[...], dhv)
        dpe = (de * sg).astype(BF)
        dgate = (de * pev * sg * (1.0 - sg)).astype(BF)
        d_wp = _dot_tn(pv.astype(BF), dpe)
        hn = _rms_fwd(hv, g[...])
        d_wg = _dot_tn(hn.astype(BF), dgate)
        dx, d_g = _rms_bwd(hv, g[...], _dot_nt(dgate, wg[...]))
        return [dhv + dx], [d_wg, d_wp, d_g, d_gp]

    return _rows_call("ple_bwd_exchange" if comm else "ple_bwd", fn, t_len, tm, [dh, h, gpre, pe, p_i], [g_pre, w_gate, g_post],
                      [(D_MODEL, F32)], [(D_MODEL, D_MODEL), (D_PLE, D_MODEL), (1, D_MODEL), (1, D_MODEL)], comm=comm)


def _loss_call(t_len, tm, h, target):
    def fn(i, n, rows, prevs, nexts, fulls):
        hv, tv = rows
        err = hv - tv
        part = 0.5 * jnp.sum(jnp.mean(err * err, axis=-1, keepdims=True), axis=0, keepdims=True)
        return [err * (1.0 / D_MODEL)], [jnp.broadcast_to(part, (8, 128))]

    return _rows_call("loss", fn, t_len, tm, [h, target], [], [(D_MODEL, F32)], [(8, 128)])


def _adamw_call(name, w, g, m, v):
    n_l, n_r, n_c = w.shape
    tr = 256 if n_r % 256 == 0 else n_r

    def body(w_ref, g_ref, m_ref, v_ref, d_ref, nm_ref, nv_ref):
        gv = g_ref[...]
        nm = ADAM_B1 * m_ref[...] + (1.0 - ADAM_B1) * gv
        nv = ADAM_B2 * v_ref[...] + (1.0 - ADAM_B2) * jnp.square(gv)
        m_hat = nm / (1.0 - ADAM_B1 ** ADAM_STEP)
        v_hat = nv / (1.0 - ADAM_B2 ** ADAM_STEP)
        d_ref[...] = -ADAM_LR * (m_hat / (jnp.sqrt(v_hat) + ADAM_EPS) + ADAM_WD * w_ref[...])
        nm_ref[...] = nm
        nv_ref[...] = nv

    spec = pl.BlockSpec((1, tr, n_c), lambda l, r: (l, r, 0))
    return pl.pallas_call(
        body, name=name, grid=(n_l, n_r // tr), in_specs=[spec] * 4, out_specs=[spec] * 3,
        out_shape=[jax.ShapeDtypeStruct(w.shape, F32)] * 3,
        compiler_params=_cparams(("parallel", "parallel")),
    )(w, g, m, v)


ANY = pl.BlockSpec(memory_space=pl.ANY)


def _place():
    x, y, c = lax.axis_index("x"), lax.axis_index("y"), lax.axis_index("c")
    chips = [(1 - x, y), (x, 1 - y), (1 - x, 1 - y)]
    return x, y, c, 2 * x + y, chips


def _remote(src, dst, send_sem, recv_sem, dev):
    return pltpu.make_async_remote_copy(src_ref=src, dst_ref=dst, send_sem=send_sem, recv_sem=recv_sem,
                                        device_id=dev, device_id_type=MESH_T)


class _GatherPlan:
    def __init__(self, shards, lo, nl):
        self.shapes = [s.shape for s in shards]
        self.lo, self.nl, self.n = lo, nl, len(shards)
        self.split = [s.shape[1] % 32 == 0 for s in shards]
        self.out_shape = [jax.ShapeDtypeStruct((3, nl, *s.shape[1:]), s.dtype) for s in shards]
        self.scratch = [pltpu.SemaphoreType.DMA((6 * self.n,)), pltpu.SemaphoreType.DMA((6 * self.n,))]

    def _views(self, ins, outs, a, c):
        lay, all_l = pl.ds(self.lo, self.nl), pl.ds(0, self.nl)
        if not self.split[a]:
            return ins[a].at[lay], (lambda j: outs[a].at[j]), None
        hr = self.shapes[a][1] // 2
        mine, other = pl.ds(c * hr, hr), pl.ds((1 - c) * hr, hr)
        return ins[a].at[lay, mine], (lambda j: outs[a].at[j, all_l, mine]), (lambda j: outs[a].at[j, all_l, other])

    def _ici(self, ins, outs, ssem, rsem):
        x, y, c, q, chips = _place()
        cps = []
        for a in range(self.n):
            src, land, _ = self._views(ins, outs, a, c)
            for j, chip in enumerate(chips):
                cps.append(_remote(src, land(j), ssem.at[6 * a + j], rsem.at[6 * a + j], (*chip, c)))
        return cps

    def start(self, ins, outs, ssem, rsem):
        for cp in self._ici(ins, outs, ssem, rsem):
            cp.start()

    def finish(self, ins, outs, ssem, rsem):
        x, y, c, q, chips = _place()
        sib = (x, y, 1 - c)
        cps = self._ici(ins, outs, ssem, rsem)
        for a in range(self.n):
            _, land, _ = self._views(ins, outs, a, c)
            for j in range(3):
                cps[3 * a + j].wait_recv()
                if self.split[a]:
                    cps.append(_remote(land(j), land(j), ssem.at[6 * a + 3 + j], rsem.at[6 * a + 3 + j], sib))
                    cps[-1].start()
        for a in range(self.n):
            _, _, other = self._views(ins, outs, a, c)
            for j in range(3):
                if self.split[a]:
                    _remote(other(j), other(j), ssem.at[6 * a + 3 + j], rsem.at[6 * a + 3 + j], sib).wait_recv()
        for cp in cps:
            cp.wait_send()


def _allgather_weights(shards, lo, nl):
    plan = _GatherPlan(shards, lo, nl)
    n = plan.n

    def body(*refs):
        ins, outs, sems = refs[:n], refs[n:2 * n], refs[2 * n:]
        plan.start(ins, outs, *sems)
        plan.finish(ins, outs, *sems)

    return pl.pallas_call(body, name="allgather_weights", in_specs=[ANY] * n, out_specs=[ANY] * n,
                          out_shape=plan.out_shape, scratch_shapes=plan.scratch)(*shards)


def _by_chip(own, others, chip):
    by_mask = jnp.stack([own, others[1], others[0], others[2]])
    return jnp.stack([lax.dynamic_index_in_dim(by_mask, jnp.bitwise_xor(chip, r), 0, keepdims=False) for r in range(N_CHIPS)])


class _PairPlan:
    def __init__(self, grads):
        self.n = len(grads)
        self.half = [g.shape[2] // 2 for g in grads]
        self.layers = [g.shape[1] for g in grads]
        self.out_shape = [jax.ShapeDtypeStruct((N_CHIPS, g.shape[1], g.shape[2] // 2, g.shape[3]), g.dtype) for g in grads]
        self.scratch = [pltpu.SemaphoreType.DMA((self.n,)), pltpu.SemaphoreType.DMA((self.n,))]

    def _copies(self, ins, outs, ssem, rsem):
        x, y, c, q, chips = _place()
        return [_remote(ins[a].at[pl.ds(0, N_CHIPS), pl.ds(0, self.layers[a]), pl.ds((1 - c) * self.half[a], self.half[a])],
                        outs[a], ssem.at[a], rsem.at[a], (x, y, 1 - c)) for a in range(self.n)]

    def start(self, ins, outs, ssem, rsem):
        for cp in self._copies(ins, outs, ssem, rsem):
            cp.start()

    def finish(self, ins, outs, ssem, rsem):
        for cp in self._copies(ins, outs, ssem, rsem):
            cp.wait()


class _ChipPlan:
    def __init__(self, parts):
        self.n = len(parts)
        self.out_shape = [jax.ShapeDtypeStruct((3, *s.shape[1:]), s.dtype) for s in parts]
        self.scratch = [pltpu.SemaphoreType.DMA((3 * self.n,)), pltpu.SemaphoreType.DMA((3 * self.n,))]

    def _copies(self, ins, outs, ssem, rsem):
        x, y, c, q, chips = _place()
        return [_remote(ins[a].at[2 * cx + cy], outs[a].at[j], ssem.at[3 * a + j], rsem.at[3 * a + j], (cx, cy, c))
                for a in range(self.n) for j, (cx, cy) in enumerate(chips)]

    def start(self, ins, outs, ssem, rsem):
        for cp in self._copies(ins, outs, ssem, rsem):
            cp.start()

    def finish(self, ins, outs, ssem, rsem):
        for cp in self._copies(ins, outs, ssem, rsem):
            cp.wait()


def _exchange_call(name, plan, arrays):
    n = plan.n

    def body(*refs):
        ins, outs, sems = refs[:n], refs[n:2 * n], refs[2 * n:]
        plan.start(ins, outs, *sems)
        plan.finish(ins, outs, *sems)

    return pl.pallas_call(body, name=name, in_specs=[ANY] * n, out_specs=[ANY] * n,
                          out_shape=plan.out_shape, scratch_shapes=plan.scratch)(*arrays)


def _pair_sum(name, g, peer, c_arr):
    _, n_l, half, n_c = peer.shape
    tr = 256 if half % 256 == 0 else half
    nb = half // tr

    def body(c_ref, g_ref, p_ref, o_ref):
        o_ref[...] = (g_ref[...] + p_ref[...]).astype(o_ref.dtype)

    blk = (1, 1, tr, n_c)
    return pl.pallas_call(
        body, name=name,
        grid_spec=pltpu.PrefetchScalarGridSpec(
            num_scalar_prefetch=1, grid=(N_CHIPS, n_l, nb),
            in_specs=[pl.BlockSpec(blk, lambda qi, li, ri, c_ref: (qi, li, c_ref[0] * nb + ri, 0)),
                      pl.BlockSpec(blk, lambda qi, li, ri, c_ref: (qi, li, ri, 0))],
            out_specs=pl.BlockSpec(blk, lambda qi, li, ri, c_ref: (qi, li, ri, 0))),
        out_shape=jax.ShapeDtypeStruct(peer.shape, BF),
        compiler_params=_cparams(("parallel", "parallel", "parallel")),
    )(c_arr, g, peer)


def _chip_sum(name, own, r, chip_arr):
    _, n_l, n_r, n_c = r.shape
    tr = 256 if n_r % 256 == 0 else n_r

    def body(q_ref, r0, r1, r2, r3, o_ref):
        o_ref[...] = ((r0[0].astype(F32) + r1[0].astype(F32)) + r2[0].astype(F32)) + r3[0].astype(F32)

    blk = (1, 1, tr, n_c)
    return pl.pallas_call(
        body, name=name,
        grid_spec=pltpu.PrefetchScalarGridSpec(
            num_scalar_prefetch=1, grid=(n_l, n_r // tr),
            in_specs=[pl.BlockSpec(blk, lambda li, ri, q_ref: (q_ref[0], li, ri, 0))]
            + [pl.BlockSpec(blk, lambda li, ri, q_ref, s=s: (s, li, ri, 0)) for s in range(3)],
            out_specs=pl.BlockSpec((1, tr, n_c), lambda li, ri, q_ref: (li, ri, 0))),
        out_shape=jax.ShapeDtypeStruct((n_l, n_r, n_c), F32),
        compiler_params=_cparams(("parallel", "parallel")),
    )(chip_arr, own, r, r, r)


def _pair_share(halves):
    n = len(halves)

    def body(*refs):
        ins, outs = refs[:n], refs[n:2 * n]
        send_sems, recv_sems = refs[2 * n:]
        x, y, c, q, chips = _place()
        cps = [_remote(ins[a], outs[a], send_sems.at[a], recv_sems.at[a], (x, y, 1 - c)) for a in range(n)]
        for cp in cps:
            cp.start()
        for cp in cps:
            cp.wait()

    return pl.pallas_call(
        body, name="grad_pair_share", in_specs=[ANY] * n, out_specs=[ANY] * n,
        out_shape=[jax.ShapeDtypeStruct(h.shape, h.dtype) for h in halves],
        scratch_shapes=[pltpu.SemaphoreType.DMA((n,)), pltpu.SemaphoreType.DMA((n,))],
    )(*halves)


def _allreduce_small(v):
    n_r = v.shape[0]

    def body(v_ref, o_ref, slots, send_sems, recv_sems):
        x, y, c = lax.axis_index("x"), lax.axis_index("y"), lax.axis_index("c")
        me = 4 * x + 2 * y + c
        slots[me] = v_ref[...]
        cps = []
        for r in range(1, 8):
            px = 1 - x if r & 4 else x
            py = 1 - y if r & 2 else y
            pc = 1 - c if r & 1 else c
            cps.append(_remote(v_ref, slots.at[me], send_sems.at[r - 1], recv_sems.at[r - 1], (px, py, pc)))
            cps[-1].start()
        for cp in cps:
            cp.wait()
        tot = slots[0]
        for d in range(1, 8):
            tot = tot + slots[d]
        o_ref[...] = tot

    return pl.pallas_call(
        body, name="allreduce_small",
        in_specs=[pl.BlockSpec(memory_space=pltpu.VMEM)], out_specs=pl.BlockSpec(memory_space=pltpu.VMEM),
        out_shape=jax.ShapeDtypeStruct(v.shape, F32),
        scratch_shapes=[pltpu.VMEM((8, n_r, 128), F32), pltpu.SemaphoreType.DMA((7,)), pltpu.SemaphoreType.DMA((7,))],
        compiler_params=pltpu.CompilerParams(vmem_limit_bytes=VMEM_LIMIT_BYTES),
    )(v)


def _cols_full(g, l):
    s = g[:, l]
    return s.transpose(1, 0, 2).reshape(s.shape[1], -1)


def _rows_full(g, l):
    s = g[:, l]
    return s.reshape(-1, s.shape[-1])


def _cols_split(full):
    r = full.shape[0]
    return full.reshape(r, N_CHIPS, -1).transpose(1, 0, 2)


def _rows_split(full):
    return full.reshape(N_CHIPS, -1, full.shape[-1])


def _pack(parts):
    flat = []
    for a in parts:
        f = a.reshape(-1).astype(F32)
        flat.append(jnp.pad(f, (0, (-f.shape[0]) % 1024)))
    return jnp.concatenate(flat).reshape(-1, 128)


def _unpack(buf, shapes):
    flat = buf.reshape(-1)
    out, off = [], 0
    for s in shapes:
        size = 1
        for d in s:
            size *= d
        out.append(flat[off:off + size].reshape(s))
        off += size + (-size) % 1024
    return out


def kernel(x, p, g_mix_pre, w_in, b_forget, w_conf_dw, conf_ln_g, conf_ln_b, w_conf_pw, w_sc, w_pool, pool_scale, w_out, g_mix_post, g_mlp_pre, w_up, w_down, g_mlp_post, g_ple_pre, w_ple_gate, w_ple_proj, g_ple_post, loss_target, m_g_mix_pre, m_w_in, m_b_forget, m_w_conf_dw, m_conf_ln_g, m_conf_ln_b, m_w_conf_pw, m_w_sc, m_w_pool, m_pool_scale, m_w_out, m_g_mix_post, m_g_mlp_pre, m_w_up, m_w_down, m_g_mlp_post, m_g_ple_pre, m_w_ple_gate, m_w_ple_proj, m_g_ple_post, v_g_mix_pre, v_w_in, v_b_forget, v_w_conf_dw, v_conf_ln_g, v_conf_ln_b, v_w_conf_pw, v_w_sc, v_w_pool, v_pool_scale, v_w_out, v_g_mix_post, v_g_mlp_pre, v_w_up, v_w_down, v_g_mlp_post, v_g_ple_pre, v_w_ple_gate, v_w_ple_proj, v_g_ple_post):
    names = ['g_mix_pre', 'w_in', 'b_forget', 'w_conf_dw', 'conf_ln_g', 'conf_ln_b', 'w_conf_pw', 'w_sc', 'w_pool', 'pool_scale',
             'w_out', 'g_mix_post', 'g_mlp_pre', 'w_up', 'w_down', 'g_mlp_post', 'g_ple_pre', 'w_ple_gate', 'w_ple_proj', 'g_ple_post']
    env = locals()
    wts = {k: env[k] for k in names}
    mom = {k: env["m_" + k] for k in names}
    var = {k: env["v_" + k] for k in names}

    t_len = x.shape[1]
    tm = min(256, t_len)
    tq = min(512, max(t_len // 2, 128))
    chip = 2 * lax.axis_index("x") + lax.axis_index("y")
    core = lax.axis_index("c")

    big = ['w_in', 'w_conf_pw', 'w_out', 'w_up', 'w_down', 'w_ple_gate', 'w_ple_proj']
    tiny = ['w_conf_dw', 'w_sc']
    own_shards = [wts[k].astype(BF) for k in big] + [wts[k] for k in tiny]

    def chip_order(others, lo, nl):
        return {k: _by_chip(own[lo:lo + nl], oth, chip) for k, own, oth in zip(big + tiny, own_shards, others)}

    def layer_weights(gat, li, l):
        w_full = _cols_full(gat['w_in'], li)
        w_a = jnp.concatenate([w_full[:, :F_OFF], w_full[:, F_OFF + N_HEADS:], w_full[:, F_OFF:F_OFF + N_HEADS],
                               jnp.zeros((D_MODEL, Z_W - D_IN), BF)], axis=1)
        w_bd = jnp.zeros((D_GRP, D_GRP), F32)
        for g in range(4):
            w_bd = lax.dynamic_update_slice(w_bd, wts['w_pool'][l, g], (64 * g, 64 * g))
        row = lambda a: a[l][None, :]
        return dict(
            w_a=w_a, w_dw=jnp.pad(_cols_full(gat['w_conf_dw'], li), ((0, 1), (0, 0))), w_pw=_rows_full(gat['w_conf_pw'], li),
            w_sc=jnp.pad(_cols_full(gat['w_sc'], li), ((0, 5), (0, 0))), w_bd=w_bd.astype(BF),
            w_out=_rows_full(gat['w_out'], li), w_up=_cols_full(gat['w_up'], li), w_down=_rows_full(gat['w_down'], li),
            w_gate=_rows_full(gat['w_ple_gate'], li), w_proj=_cols_full(gat['w_ple_proj'], li),
            b_f=jnp.pad(wts['b_forget'][l], (0, F_PAD - N_HEADS))[None, :],
            ln_g=row(wts['conf_ln_g']), ln_b=row(wts['conf_ln_b']), pool_scale=row(wts['pool_scale']),
            g_mix_pre=row(wts['g_mix_pre']), g_mix_post=row(wts['g_mix_post']), g_mlp_pre=row(wts['g_mlp_pre']),
            g_mlp_post=row(wts['g_mlp_post']), g_ple_pre=row(wts['g_ple_pre']), g_ple_post=row(wts['g_ple_post']))

    lw = [layer_weights(chip_order(_allgather_weights(own_shards, 0, 1), 0, 1), 0, 0)]

    h = x[0]
    saved = []
    for l in range(DEPTH):
        w = lw[l]
        s = dict(h0=h)
        s['zc'], s['qa'], kv, s['zs'], s['zp'], s['zf'] = _mix_in_fwd(t_len, tm, h, w['g_mix_pre'], w['w_a'])
        s['cv'], y_conf = _conf_fwd(t_len, tm, s['zc'], w['w_dw'], w['ln_g'], w['ln_b'], w['w_pw'])
        s['ka'], s['va'] = _fox_prep(t_len, s['zf'], kv, w['b_f'])
        if l == 0:
            s['o'], *others = _fox_fwd(t_len, tq, s['qa'], s['ka'], s['va'],
                                       gather=(_GatherPlan(own_shards, 1, DEPTH - 1), own_shards))
            rest = chip_order(others, 1, DEPTH - 1)
            lw += [layer_weights(rest, li, li + 1) for li in range(DEPTH - 1)]
        else:
            (s['o'],) = _fox_fwd(t_len, tq, s['qa'], s['ka'], s['va'])
        (y_sc,) = _sconv_fwd(t_len, tm, s['zs'], w['w_sc'])
        (y_pool,) = _pool_fwd(t_len, tm, s['zp'], w['w_bd'], w['pool_scale'])
        s['ys'] = [y_conf, s['o'], y_sc, y_pool]
        s['mix'], h = _mix_out_fwd(t_len, tm, s['ys'], h, w['w_out'], w['g_mix_post'])
        s['h1'] = h
        s['up'] = _mlp_up_fwd(t_len, tm, h, w['g_mlp_pre'], w['w_up'])
        s['ff'], h = _mlp_down_fwd(t_len, tm, s['up'], h, w['w_down'], w['g_mlp_post'])
        s['h2'] = h
        s['gpre'], s['pe'], h = _ple_fwd(t_len, tm, h, p[l, 0], w['g_ple_pre'], w['w_gate'], w['w_proj'], w['g_ple_post'])
        saved.append(s)

    dh, loss_part = _loss_call(t_len, tm, h, loss_target[0])

    grads = [None] * DEPTH
    split = dict(w_in=_cols_split, w_conf_pw=_rows_split, w_out=_rows_split, w_up=_cols_split, w_down=_rows_split,
                 w_ple_gate=_rows_split, w_ple_proj=_cols_split)
    c_arr = core.astype(jnp.int32).reshape(1)
    chip_arr = chip.astype(jnp.int32).reshape(1)

    def contrib_of(layers):
        return [jnp.stack([split[k](grads[l][k]) for l in layers], axis=1) for k in big]

    def pair_sums(tag, contrib, peer):
        return [_pair_sum(f"grad_pair_sum_{tag}_{k}", a, b, c_arr) for k, a, b in zip(big, contrib, peer)]

    def finish_reduce(tag, parts, landed):
        halves = [_chip_sum(f"grad_chip_sum_{tag}_{k}", own, r, chip_arr) for k, own, r in zip(big, parts, landed)]
        full = []
        for mine, theirs in zip(halves, _pair_share(halves)):
            both = jnp.stack([mine, theirs])
            full.append(jnp.concatenate([lax.dynamic_index_in_dim(both, core, 0, keepdims=False),
                                         lax.dynamic_index_in_dim(both, 1 - core, 0, keepdims=False)], axis=1))
        return full

    for l in reversed(range(DEPTH)):
        w, s, g = lw[l], saved[l], {}
        contrib13 = contrib_of(range(1, DEPTH)) if l == 0 else []
        dh, g['w_ple_gate'], g['w_ple_proj'], g['g_ple_pre'], g['g_ple_post'], *peer13 = _ple_bwd(
            t_len, tm, dh, s['h2'], s['gpre'], s['pe'], p[l, 0], w['g_ple_pre'], w['w_gate'], w['g_ple_post'],
            comm=(_PairPlan(contrib13), contrib13) if l == 0 else None)
        parts13 = pair_sums("l13", contrib13, peer13) if l == 0 else []
        dff, dup, g['g_mlp_post'] = _mlp_bwd_a(t_len, tm, dh, s['ff'], s['up'], w['w_down'], w['g_mlp_post'])
        dh, hn, g['g_mlp_pre'] = _mlp_bwd_b(t_len, tm, dh, s['h1'], dup, w['w_up'], w['g_mlp_pre'])
        tt = min(512, t_len)
        g['w_up'] = _mm_tn("mlp_dw_up", hn, dup, D_MODEL, 1024, tt)
        g['w_down'] = _mm_tn("mlp_dw_down", s['up'], dff, 1024, D_MODEL, tt, pro=lambda u: jnp.square(jnp.maximum(u, 0.0)))
        dy_conf, dy_att, dy_sc, dy_pool, g['w_out'], g['g_mix_post'] = _mix_out_bwd(t_len, tm, dh, s['mix'], s['ys'], w['w_out'], w['g_mix_post'])
        dzc, g['w_conf_dw'], g['conf_ln_g'], g['conf_ln_b'], g['w_conf_pw'] = _conf_bwd(
            t_len, tm, s['zc'], s['cv'], dy_conf, w['w_dw'], w['ln_g'], w['ln_b'], w['w_pw'])
        dzs, g['w_sc'] = _sconv_bwd(t_len, tm, s['zs'], dy_sc, w['w_sc'])
        dzp, d_wbd, g['pool_scale'] = _pool_bwd(t_len, tm, s['zp'], dy_pool, w['w_bd'], w['pool_scale'])
        g['w_pool'] = jnp.stack([d_wbd[64 * a:64 * (a + 1), 64 * a:64 * (a + 1)] for a in range(4)])
        dqa, dka, dva, *landed13 = _fox_bwd(t_len, tq, s['qa'], s['ka'], s['va'], dy_att, s['o'],
                                            comm=(_ChipPlan(parts13), parts13) if l == 0 else None)
        dzf, d_bf = _fox_post(t_len, dqa, dka, s['zf'], w['b_f'])
        g['b_forget'] = d_bf[0, :N_HEADS]
        dh, xn, dz, g['g_mix_pre'] = _mix_in_bwd(t_len, tm, dh, s['h0'], dzc, dqa, dka, dva, dzs, dzp, dzf, w['g_mix_pre'], w['w_a'])
        d_wa = _mm_tn("mix_dw_in", xn, dz, D_MODEL, Z_W, tt)
        g['w_in'] = jnp.concatenate([d_wa[:, :F_OFF], d_wa[:, Z_F:Z_F + N_HEADS], d_wa[:, F_OFF:Z_F]], axis=1)
        g['w_conf_dw'] = g['w_conf_dw'][:CONF_K]
        g['w_sc'] = g['w_sc'][:SC_K]
        grads[l] = g
    grad_x = dh[None]

    contrib0 = contrib_of([0])
    parts0 = pair_sums("l0", contrib0, _exchange_call("grad_pair_exchange", _PairPlan(contrib0), contrib0))
    first = finish_reduce("l0", parts0, _exchange_call("grad_chip_exchange", _ChipPlan(parts0), parts0))
    later = finish_reduce("l13", parts13, landed13)
    reduced = {k: jnp.concatenate([a, b], axis=0) for k, a, b in zip(big, first, later)}

    small = [k for k in names if k not in big]
    small_shapes = [(DEPTH, *grads[0][k].shape) for k in small]
    packed = _pack([jnp.stack([grads[l][k] for l in range(DEPTH)]) for k in small] + [loss_part])
    summed = _allreduce_small(packed)
    small_sum = _unpack(summed, small_shapes + [(8, 128)])
    loss = small_sum[-1][0, 0]
    for k, a in zip(small, small_sum[:-1]):
        if k in tiny:
            a = lax.dynamic_slice_in_dim(a, chip * 64, 64, axis=2)
        reduced[k] = a.reshape(wts[k].shape)

    delta_w, new_m, new_v = {}, {}, {}
    for k in names:
        shp = wts[k].shape
        as3 = (lambda a: a.reshape(shp[0], -1, shp[-1])) if len(shp) > 2 else (lambda a: a.reshape(1, shp[0], shp[1]))
        d, nm, nv = _adamw_call("adamw_" + k, as3(wts[k]), as3(reduced[k]), as3(mom[k]), as3(var[k]))
        delta_w[k], new_m[k], new_v[k] = d.reshape(shp), nm.reshape(shp), nv.reshape(shp)

    return (loss, grad_x, *[reduced[k] for k in names], *[delta_w[k] for k in names],
            *[new_m[k] for k in names], *[new_v[k] for k in names])
```

```python
import functools

import jax
import jax.numpy as jnp
from jax import lax
from jax.experimental import pallas as pl
from jax.experimental.pallas import tpu as pltpu

F32 = jnp.float32
BF = jnp.bfloat16

DEPTH = 4
D_MODEL = 1024
D_GRP = 256
HEAD_DIM = 64
N_HEADS = 4
CONF_K = 31
SC_K = 3
D_FF = 4096
D_PLE = 256
EPS = 1e-6
N_CHIPS = 4
Z_CONF, Z_QKV, Z_SC, Z_POOL, Z_F = 0, 512, 1280, 2048, 2304
Z_W = 2432
F_PAD = 128
D_IN = 2308
F_OFF = 1280

ADAM_LR, ADAM_B1, ADAM_B2, ADAM_EPS, ADAM_WD, ADAM_STEP = 0.001, 0.9, 0.999, 1e-08, 0.01, 10

VMEM_LIMIT_BYTES = 56 * 1024 * 1024
HALO = 32
NEG = -1e30
MESH_T = pl.DeviceIdType.MESH


def _cparams(sem=None):
    return pltpu.CompilerParams(dimension_semantics=sem, vmem_limit_bytes=VMEM_LIMIT_BYTES)


def _dot(a, b):
    return jnp.dot(a, b, preferred_element_type=F32)


def _dot_nt(a, b):
    return lax.dot_general(a, b, (((1,), (1,)), ((), ())), preferred_element_type=F32)


def _dot_tn(a, b):
    return lax.dot_general(a, b, (((0,), (0,)), ((), ())), preferred_element_type=F32)


def _sig(x):
    return jax.nn.sigmoid(x)


def _rms_fwd(x, g):
    r = lax.rsqrt(jnp.mean(x * x, axis=-1, keepdims=True) + EPS)
    return x * r * g


def _rms_bwd(x, g, dy):
    r = lax.rsqrt(jnp.mean(x * x, axis=-1, keepdims=True) + EPS)
    xh = x * r
    dg = jnp.sum(dy * xh, axis=0, keepdims=True)
    dxh = dy * g
    dx = r * (dxh - xh * jnp.mean(dxh * xh, axis=-1, keepdims=True))
    return dx, dg


def _back(ext, d):
    return ext if d == 0 else pltpu.roll(ext, d, 0)


def _ahead(ext, d):
    return ext if d == 0 else pltpu.roll(ext, ext.shape[0] - d, 0)


def _rows_call(name, fn, t_len, tm, rows, fulls, out_rows, out_accs=(), prevs=(), nexts=(), comm=None):
    plan, comm_in = comm if comm else (None, [])
    n_comm = len(comm_in)
    n = t_len // tm
    hb = tm // HALO
    nhb = t_len // HALO
    n_rows, n_prev, n_next, n_full = len(rows), len(prevs), len(nexts), len(fulls)
    in_specs = [pl.BlockSpec((tm, a.shape[1]), lambda i: (i, 0)) for a in rows]
    in_specs += [pl.BlockSpec((HALO, a.shape[1]), lambda i: (jnp.maximum(i * hb - 1, 0), 0)) for a in prevs]
    in_specs += [pl.BlockSpec((HALO, a.shape[1]), lambda i: (jnp.minimum((i + 1) * hb, nhb - 1), 0)) for a in nexts]
    in_specs += [pl.BlockSpec(a.shape, lambda i, nd=a.ndim: (0,) * nd) for a in fulls]
    out_shape = [jax.ShapeDtypeStruct((t_len, c), dt) for c, dt in out_rows]
    out_shape += [jax.ShapeDtypeStruct(s, F32) for s in out_accs]
    out_specs = [pl.BlockSpec((tm, c), lambda i: (i, 0)) for c, _ in out_rows]
    out_specs += [pl.BlockSpec(s, lambda i, nd=len(s): (0,) * nd) for s in out_accs]
    n_in = n_rows + n_prev + n_next + n_full
    n_ro = len(out_rows)

    n_out = n_ro + len(out_accs)

    def body(*refs):
        i = pl.program_id(0)
        ins, outs = refs[:n_in], refs[n_in + n_comm:n_in + n_comm + n_out]
        c_ins, c_outs, sems = refs[n_in:n_in + n_comm], refs[n_in + n_comm + n_out:n_in + 2 * n_comm + n_out], refs[n_in + 2 * n_comm + n_out:]
        if plan:
            @pl.when(i == 0)
            def _():
                plan.start(c_ins, c_outs, *sems)

        rv = [r[...] for r in ins[:n_rows]]
        pv = [r[...] for r in ins[n_rows:n_rows + n_prev]]
        nv = [r[...] for r in ins[n_rows + n_prev:n_rows + n_prev + n_next]]
        fv = list(ins[n_rows + n_prev + n_next:])
        ro, ao = fn(i, n, rv, pv, nv, fv)
        for r, v in zip(outs[:n_ro], ro):
            r[...] = v.astype(r.dtype)
        if out_accs:
            acc = outs[n_ro:]

            @pl.when(i == 0)
            def _():
                for r in acc:
                    r[...] = jnp.zeros(r.shape, r.dtype)

            for r, v in zip(acc, ao):
                r[...] += v
        if plan:
            @pl.when(i == n - 1)
            def _():
                plan.finish(c_ins, c_outs, *sems)

    res = pl.pallas_call(
        body, name=name, grid=(n,), in_specs=in_specs + [ANY] * n_comm, out_specs=out_specs + [ANY] * n_comm,
        out_shape=out_shape + (plan.out_shape if plan else []), scratch_shapes=plan.scratch if plan else [],
        compiler_params=_cparams(("arbitrary",)),
    )(*rows, *prevs, *nexts, *fulls, *comm_in)
    return res


def _mm_tn(name, x, y, tk, tn, tt, pro=None):
    t_len, k_dim = x.shape
    n_dim = y.shape[1]

    def body(x_ref, y_ref, o_ref):
        @pl.when(pl.program_id(2) == 0)
        def _():
            o_ref[...] = jnp.zeros(o_ref.shape, o_ref.dtype)

        xv = x_ref[...]
        if pro is not None:
            xv = pro(xv)
        o_ref[...] += _dot_tn(xv.astype(BF), y_ref[...].astype(BF))

    return pl.pallas_call(
        body, name=name, grid=(k_dim // tk, n_dim // tn, t_len // tt),
        in_specs=[pl.BlockSpec((tt, tk), lambda a, b, t: (t, a)), pl.BlockSpec((tt, tn), lambda a, b, t: (t, b))],
        out_specs=pl.BlockSpec((tk, tn), lambda a, b, t: (a, b)),
        out_shape=jax.ShapeDtypeStruct((k_dim, n_dim), F32),
        compiler_params=_cparams(("parallel", "parallel", "arbitrary")),
    )(x, y)


QK_W = 128
ATT_W = N_HEADS * QK_W
C_COL = HEAD_DIM
ONE_COL = HEAD_DIM + 3


def _lane(shape):
    return lax.broadcasted_iota(jnp.int32, shape, 1)


def _head_low(a256, h):
    pair = a256[:, QK_W * (h // 2):QK_W * (h // 2 + 1)]
    return pltpu.roll(pair, HEAD_DIM, 1) if h % 2 else pair


def _heads_spread(a256):
    low = _lane((a256.shape[0], QK_W)) < HEAD_DIM
    return jnp.concatenate([jnp.where(low, _head_low(a256, h), 0.0) for h in range(N_HEADS)], axis=1)


def _heads_packed(a512):
    low = _lane((a512.shape[0], QK_W)) < HEAD_DIM
    out = []
    for pair in range(N_HEADS // 2):
        even = a512[:, QK_W * 2 * pair:QK_W * (2 * pair + 1)]
        odd = a512[:, QK_W * (2 * pair + 1):QK_W * (2 * pair + 2)]
        out.append(jnp.where(low, even, pltpu.roll(odd, HEAD_DIM, 1)))
    return jnp.concatenate(out, axis=1)


def _mix_in_fwd(t_len, tm, h, g_pre, w_a):
    def fn(i, n, rows, prevs, nexts, fulls):
        (hv,), (g, w) = rows, fulls
        z = _dot(_rms_fwd(hv, g[...]).astype(BF), w[...])
        lane = _lane((tm, QK_W))
        ones = jnp.where(lane < ONE_COL, 1.0, 0.0)
        zq = z[:, Z_QKV:Z_QKV + D_GRP]
        qa = jnp.concatenate([jnp.where(lane < HEAD_DIM, _head_low(zq, a) * (HEAD_DIM ** -0.5), ones) for a in range(N_HEADS)], axis=1)
        return [z[:, Z_CONF:Z_QKV], qa, z[:, Z_QKV + D_GRP:Z_SC], z[:, Z_SC:Z_POOL], z[:, Z_POOL:Z_F], z[:, Z_F:Z_W]], []

    return _rows_call("mix_in_fwd", fn, t_len, tm, [h], [g_pre, w_a],
                      [(512, F32), (ATT_W, BF), (2 * D_GRP, BF), (768, F32), (256, F32), (F_PAD, F32)])


def _mix_in_bwd(t_len, tm, dh, h, dzc, dqa, dka, dva, dzs, dzp, dzf, g_pre, w_a):
    def fn(i, n, rows, prevs, nexts, fulls):
        dhv, hv, a, dq, dk, dv, c, d, e = rows
        g, w = fulls
        b = jnp.concatenate([_heads_packed(dq), _heads_packed(dk), _heads_packed(dv)], axis=1).astype(BF)
        dz = jnp.concatenate([a, b, c, d, e], axis=1)
        dxn = _dot_nt(dz, w[...])
        dx, dg = _rms_bwd(hv, g[...], dxn)
        xn = _rms_fwd(hv, g[...])
        return [dhv + dx, xn, dz], [dg]

    return _rows_call("mix_in_bwd", fn, t_len, tm, [dh, h, dzc, dqa, dka, dva, dzs, dzp, dzf], [g_pre, w_a],
                      [(D_MODEL, F32), (D_MODEL, BF), (Z_W, BF)], [(1, D_MODEL)])


def _glu_ext(i, zc, zc_prev):
    ext = jnp.concatenate([zc_prev, zc], axis=0)
    u = ext[:, :D_GRP] * _sig(ext[:, D_GRP:])
    row = lax.broadcasted_iota(jnp.int32, u.shape, 0)
    return jnp.where((row >= HALO) | (i > 0), u, 0.0)


def _conf_fwd(t_len, tm, zc, w_dw, ln_g, ln_b, w_pw):
    def fn(i, n, rows, prevs, nexts, fulls):
        (zv,), (zp,) = rows, prevs
        wdw, lg, lb, wpw = fulls
        u = _glu_ext(i, zv, zp)
        cv = jnp.zeros((tm, D_GRP), F32)
        for k in range(CONF_K):
            cv = cv + wdw[k:k + 1, :] * _back(u, CONF_K - 1 - k)[HALO:, :]
        mu = jnp.mean(cv, axis=-1, keepdims=True)
        xc = cv - mu
        ln = xc * lax.rsqrt(jnp.mean(xc * xc, axis=-1, keepdims=True) + EPS) * lg[...] + lb[...]
        s = ln * _sig(ln)
        return [cv, _dot(s.astype(BF), wpw[...])], []

    return _rows_call("conf_fwd", fn, t_len, tm, [zc], [w_dw, ln_g, ln_b, w_pw], [(D_GRP, F32), (D_GRP, BF)], prevs=[zc])


def _conf_bwd(t_len, tm, zc, cv, dy, w_dw, ln_g, ln_b, w_pw):
    def fn(i, n, rows, prevs, nexts, fulls):
        zv, cvv, dyv = rows
        (zp,) = prevs
        cvn, dyn = nexts
        wdw, lg, lb, wpw = fulls
        cve = jnp.concatenate([cvv, cvn], axis=0)
        dye = jnp.concatenate([dyv, dyn], axis=0)
        mu = jnp.mean(cve, axis=-1, keepdims=True)
        xc = cve - mu
        rs = lax.rsqrt(jnp.mean(xc * xc, axis=-1, keepdims=True) + EPS)
        xh = xc * rs
        ln = xh * lg[...] + lb[...]
        sg = _sig(ln)
        s = ln * sg
        ds = _dot_nt(dye.astype(BF), wpw[...])
        dln = ds * (sg * (1.0 + ln * (1.0 - sg)))
        dxh = dln * lg[...]
        dcv = rs * (dxh - jnp.mean(dxh, axis=-1, keepdims=True) - xh * jnp.mean(dxh * xh, axis=-1, keepdims=True))
        row = lax.broadcasted_iota(jnp.int32, dcv.shape, 0)
        dcv = jnp.where((row < tm) | (i < n - 1), dcv, 0.0)
        d_lg = jnp.sum((dln * xh)[:tm], axis=0, keepdims=True)
        d_lb = jnp.sum(dln[:tm], axis=0, keepdims=True)
        d_wpw = _dot_tn(s[:tm].astype(BF), dyv.astype(BF))
        u = _glu_ext(i, zv, zp)
        dcv_cur = dcv[:tm]
        du = jnp.zeros((tm, D_GRP), F32)
        d_wdw = jnp.zeros((32, D_GRP), F32)
        krow = lax.broadcasted_iota(jnp.int32, (32, D_GRP), 0)
        for k in range(CONF_K):
            d = CONF_K - 1 - k
            du = du + wdw[k:k + 1, :] * _ahead(dcv, d)[:tm, :]
            tap = _back(u, d)[HALO:, :]
            d_wdw = d_wdw + jnp.where(krow == k, jnp.sum(dcv_cur * tap, axis=0, keepdims=True), 0.0)
        a, b = zv[:, :D_GRP], zv[:, D_GRP:]
        sb = _sig(b)
        dz = jnp.concatenate([du * sb, du * a * sb * (1.0 - sb)], axis=1)
        return [dz], [d_wdw, d_lg, d_lb, d_wpw]

    return _rows_call("conf_bwd", fn, t_len, tm, [zc, cv, dy], [w_dw, ln_g, ln_b, w_pw], [(512, BF)],
                      [(32, D_GRP), (1, D_GRP), (1, D_GRP), (D_GRP, D_GRP)], prevs=[zc], nexts=[cv, dy])


def _sc_ext(i, zs, zs_prev):
    ext = jnp.concatenate([zs_prev, zs], axis=0)
    e = ext[:, 2 * D_GRP:] * ext[:, :D_GRP]
    row = lax.broadcasted_iota(jnp.int32, e.shape, 0)
    return jnp.where((row >= HALO) | (i > 0), e, 0.0)


def _sconv_fwd(t_len, tm, zs, w_sc):
    def fn(i, n, rows, prevs, nexts, fulls):
        (zv,), (zp,), (w,) = rows, prevs, fulls
        e = _sc_ext(i, zv, zp)
        cv = jnp.zeros((tm, D_GRP), F32)
        for k in range(SC_K):
            cv = cv + w[k:k + 1, :] * _back(e, SC_K - 1 - k)[HALO:, :]
        return [zv[:, D_GRP:2 * D_GRP] * cv], []

    return _rows_call("sconv_fwd", fn, t_len, tm, [zs], [w_sc], [(D_GRP, BF)], prevs=[zs])


def _sconv_bwd(t_len, tm, zs, dy, w_sc):
    def fn(i, n, rows, prevs, nexts, fulls):
        zv, dyv = rows
        (zp,) = prevs
        zn, dyn = nexts
        (w,) = fulls
        e = _sc_ext(i, zv, zp)
        taps = [_back(e, SC_K - 1 - k)[HALO:, :] for k in range(SC_K)]
        cv = w[0:1, :] * taps[0] + w[1:2, :] * taps[1] + w[2:3, :] * taps[2]
        bg = zv[:, D_GRP:2 * D_GRP]
        dcv = jnp.concatenate([dyv * bg, dyn * zn[:, D_GRP:2 * D_GRP]], axis=0)
        row = lax.broadcasted_iota(jnp.int32, dcv.shape, 0)
        dcv = jnp.where((row < tm) | (i < n - 1), dcv, 0.0)
        de = jnp.zeros((tm, D_GRP), F32)
        d_w = jnp.zeros((8, D_GRP), F32)
        krow = lax.broadcasted_iota(jnp.int32, (8, D_GRP), 0)
        for k in range(SC_K):
            de = de + w[k:k + 1, :] * _ahead(dcv, SC_K - 1 - k)[:tm, :]
            d_w = d_w + jnp.where(krow == k, jnp.sum(dcv[:tm] * taps[k], axis=0, keepdims=True), 0.0)
        dz = jnp.concatenate([de * zv[:, 2 * D_GRP:], dyv * cv, de * zv[:, :D_GRP]], axis=1)
        return [dz], [d_w]

    return _rows_call("sconv_bwd", fn, t_len, tm, [zs, dy], [w_sc], [(768, BF)], [(8, D_GRP)], prevs=[zs], nexts=[zs, dy])


def _pool_window(shape):
    grp = lax.broadcasted_iota(jnp.int32, shape, 1) // 64
    return grp, jnp.where(grp == 0, 2.0, jnp.where(grp == 1, 4.0, jnp.where(grp == 2, 8.0, 16.0)))


def _pool_d(i, tm, zv, zp):
    ext = jnp.concatenate([zp, zv], axis=0)
    row = lax.broadcasted_iota(jnp.int32, ext.shape, 0)
    ext = jnp.where((row >= HALO) | (i > 0), ext, 0.0)
    s2 = ext + _back(ext, 1)
    s4 = s2 + _back(s2, 2)
    s8 = s4 + _back(s4, 4)
    s16 = s8 + _back(s8, 8)
    grp, win = _pool_window((tm, D_GRP))
    sel = jnp.where(grp == 0, s2[HALO:], jnp.where(grp == 1, s4[HALO:], jnp.where(grp == 2, s8[HALO:], s16[HALO:])))
    pos = (i * tm + lax.broadcasted_iota(jnp.int32, (tm, D_GRP), 0) + 1).astype(F32)
    return sel / jnp.minimum(pos, win) - zv


def _pool_fwd(t_len, tm, zpool, w_bd, scale):
    def fn(i, n, rows, prevs, nexts, fulls):
        (zv,), (zp,) = rows, prevs
        w, sc = fulls
        d = _pool_d(i, tm, zv, zp)
        return [_dot(d.astype(BF), w[...]) * sc[...]], []

    return _rows_call("pool_fwd", fn, t_len, tm, [zpool], [w_bd, scale], [(D_GRP, BF)], prevs=[zpool])


def _pool_bwd(t_len, tm, zpool, dy, w_bd, scale):
    def fn(i, n, rows, prevs, nexts, fulls):
        zv, dyv = rows
        (zp,) = prevs
        (dyn,) = nexts
        w, sc = fulls
        d = _pool_d(i, tm, zv, zp)
        lin = _dot(d.astype(BF), w[...])
        d_sc = jnp.sum(dyv * lin, axis=0, keepdims=True)
        dye = jnp.concatenate([dyv, dyn], axis=0) * sc[...]
        d_w = _dot_tn(d.astype(BF), dye[:tm].astype(BF))
        dd = _dot_nt(dye.astype(BF), w[...])
        row = lax.broadcasted_iota(jnp.int32, dd.shape, 0)
        dd = jnp.where((row < tm) | (i < n - 1), dd, 0.0)
        grp, win = _pool_window(dd.shape)
        pos = (i * tm + row + 1).astype(F32)
        ddc = dd / jnp.minimum(pos, win)
        f2 = ddc + _ahead(ddc, 1)
        f4 = f2 + _ahead(f2, 2)
        f8 = f4 + _ahead(f4, 4)
        f16 = f8 + _ahead(f8, 8)
        sel = jnp.where(grp == 0, f2, jnp.where(grp == 1, f4, jnp.where(grp == 2, f8, f16)))
        return [(sel - dd)[:tm]], [d_w, d_sc]

    return _rows_call("pool_bwd", fn, t_len, tm, [zpool, dy], [w_bd, scale], [(D_GRP, BF)],
                      [(D_GRP, D_GRP), (1, D_GRP)], prevs=[zpool], nexts=[dy])


def _mix_cat(y_conf, o_att, y_sc, y_pool):
    return jnp.concatenate([y_conf, _heads_packed(o_att).astype(BF), y_sc, y_pool], axis=1)


def _mix_out_fwd(t_len, tm, ys, h, w_out, g_post):
    def fn(i, n, rows, prevs, nexts, fulls):
        y0, y1, y2, y3, hv = rows
        w, g = fulls
        mix = _dot(_mix_cat(y0, y1, y2, y3), w[...])
        return [mix, hv + _rms_fwd(mix, g[...])], []

    return _rows_call("mix_out_fwd", fn, t_len, tm, [*ys, h], [w_out, g_post], [(D_MODEL, F32), (D_MODEL, F32)])


def _mix_out_bwd(t_len, tm, dh, mix, ys, w_out, g_post):
    def fn(i, n, rows, prevs, nexts, fulls):
        dhv, mv, y0, y1, y2, y3 = rows
        w, g = fulls
        dmix, dg = _rms_bwd(mv, g[...], dhv)
        dmb = dmix.astype(BF)
        dcat = _dot_nt(dmb, w[...])
        d_w = _dot_tn(_mix_cat(y0, y1, y2, y3), dmb)
        return [dcat[:, :256], _heads_spread(dcat[:, 256:512]), dcat[:, 512:768], dcat[:, 768:]], [d_w, dg]

    return _rows_call("mix_out_bwd", fn, t_len, tm, [dh, mix, *ys], [w_out, g_post],
                      [(D_GRP, F32), (ATT_W, F32), (D_GRP, F32), (D_GRP, F32)], [(D_MODEL, D_MODEL), (1, D_MODEL)])


def _log_sigmoid(x):
    return jnp.minimum(x, 0.0) - jnp.log(1.0 + jnp.exp(-jnp.abs(x)))


SCAN_BLK = 256


def _fox_prep(t_len, zf, kv, b_f):
    blk = min(SCAN_BLK, t_len)

    def body(zf_ref, kv_ref, b_ref, ka_ref, va_ref, carry_s):
        @pl.when(pl.program_id(0) == 0)
        def _():
            carry_s[...] = jnp.zeros(carry_s.shape, F32)

        tri = (lax.broadcasted_iota(jnp.int32, (blk, blk), 0) >= lax.broadcasted_iota(jnp.int32, (blk, blk), 1)).astype(F32)
        lf = _log_sigmoid(zf_ref[...] + b_ref[...])
        cs = jnp.dot(tri, lf, precision=lax.Precision.HIGHEST, preferred_element_type=F32) + carry_s[...]
        carry_s[...] = cs[blk - 1:blk, :]
        neg = -cs
        hi = neg.astype(BF).astype(F32)
        mid = (neg - hi).astype(BF).astype(F32)
        lo = ((neg - hi) - mid).astype(BF).astype(F32)
        kvv = kv_ref[...].astype(F32)
        lane = _lane((blk, QK_W))
        ka, va = [], []
        for a in range(N_HEADS):
            terms = jnp.where(lane == C_COL, hi[:, a:a + 1], jnp.where(lane == C_COL + 1, mid[:, a:a + 1], jnp.where(
                lane == C_COL + 2, lo[:, a:a + 1], jnp.where(lane == ONE_COL, HEAD_DIM ** 0.5, 0.0))))
            ka.append(jnp.where(lane < HEAD_DIM, _head_low(kvv[:, :D_GRP], a), terms))
            va.append(jnp.where(lane < HEAD_DIM, _head_low(kvv[:, D_GRP:], a), 0.0))
        ka_ref[...] = jnp.concatenate(ka, axis=1).astype(BF)
        va_ref[...] = jnp.concatenate(va, axis=1).astype(BF)

    row = lambda w: pl.BlockSpec((blk, w), lambda i: (i, 0))
    return pl.pallas_call(
        body, name="fox_prep", grid=(t_len // blk,),
        in_specs=[row(F_PAD), row(2 * D_GRP), pl.BlockSpec((1, F_PAD), lambda i: (0, 0))],
        out_specs=[row(ATT_W), row(ATT_W)],
        out_shape=[jax.ShapeDtypeStruct((t_len, ATT_W), BF)] * 2,
        scratch_shapes=[pltpu.VMEM((1, F_PAD), F32)],
        compiler_params=_cparams(("arbitrary",)),
    )(zf, kv, b_f)


def _fox_post(t_len, dqa, dka, zf, b_f):
    blk = min(SCAN_BLK, t_len)
    nb = t_len // blk

    def body(dq_ref, dk_ref, zf_ref, b_ref, dz_ref, db_ref, carry_s):
        @pl.when(pl.program_id(0) == 0)
        def _():
            carry_s[...] = jnp.zeros(carry_s.shape, F32)
            db_ref[...] = jnp.zeros(db_ref.shape, F32)

        tri = (lax.broadcasted_iota(jnp.int32, (blk, blk), 0) <= lax.broadcasted_iota(jnp.int32, (blk, blk), 1)).astype(F32)
        lane = _lane((blk, F_PAD))
        dc = jnp.zeros((blk, F_PAD), F32)
        for a in range(N_HEADS):
            head = slice(QK_W * a, QK_W * (a + 1))
            col = dq_ref[:, head][:, ONE_COL:ONE_COL + 1] - dk_ref[:, head][:, C_COL:C_COL + 1]
            dc = jnp.where(lane == a, col, dc)
        dlf = jnp.dot(tri, dc, precision=lax.Precision.HIGHEST, preferred_element_type=F32) + carry_s[...]
        carry_s[...] = dlf[0:1, :]
        dz = dlf * _sig(-(zf_ref[...] + b_ref[...]))
        dz_ref[...] = dz.astype(dz_ref.dtype)
        db_ref[...] += jnp.sum(dz, axis=0, keepdims=True)

    row = lambda w: pl.BlockSpec((blk, w), lambda i: (nb - 1 - i, 0))
    one = pl.BlockSpec((1, F_PAD), lambda i: (0, 0))
    return pl.pallas_call(
        body, name="fox_post", grid=(nb,),
        in_specs=[row(ATT_W), row(ATT_W), row(F_PAD), one], out_specs=[row(F_PAD), one],
        out_shape=(jax.ShapeDtypeStruct((t_len, F_PAD), BF), jax.ShapeDtypeStruct((1, F_PAD), F32)),
        scratch_shapes=[pltpu.VMEM((1, F_PAD), F32)],
        compiler_params=_cparams(("arbitrary",)),
    )(dqa, dka, zf, b_f)


def _tri_mask(tq, key_rows):
    r = lax.broadcasted_iota(jnp.int32, (tq, tq), 0)
    c = lax.broadcasted_iota(jnp.int32, (tq, tq), 1)
    return (r <= c) if key_rows else (r >= c)


LSE_COL = HEAD_DIM


def _fox_fwd(t_len, tq, qa, ka, va, gather=None):
    nq = t_len // tq
    plan, shards = gather if gather else (None, [])
    n = len(shards)

    def body(q_ref, k_ref, v_ref, *rest):
        ins, o_ref, outs, sems = rest[:n], rest[n], rest[n + 1:2 * n + 1], rest[2 * n + 1:]
        h, i = pl.program_id(0), pl.program_id(1)
        if plan:
            @pl.when((h == 0) & (i == 0))
            def _():
                plan.start(ins, outs, *sems)

        q = q_ref[...]

        def tile(j, carry, diagonal):
            m, l, acc = carry
            rows = pl.ds(pl.multiple_of(j * tq, tq), tq)
            s = _dot_nt(q, k_ref[rows, :])
            if diagonal:
                s = jnp.where(_tri_mask(tq, False), s, NEG)
            m_new = jnp.maximum(m, jnp.max(s, axis=-1, keepdims=True))
            alpha = jnp.exp(m - m_new)
            p = jnp.exp(s - m_new)
            l = alpha * l + jnp.sum(p, axis=-1, keepdims=True)
            acc = alpha * acc + _dot(p.astype(BF), v_ref[rows, :])
            return m_new, l, acc

        init = (jnp.full((tq, 1), NEG, F32), jnp.zeros((tq, 1), F32), jnp.zeros((tq, QK_W), F32))
        carry = lax.fori_loop(0, i, lambda j, c: tile(j, c, False), init)
        m, l, acc = tile(i, carry, True)
        o_ref[...] = jnp.where(_lane((tq, QK_W)) == LSE_COL, m + jnp.log(l), acc / l)
        if plan:
            @pl.when((h == N_HEADS - 1) & (i == nq - 1))
            def _():
                plan.finish(ins, outs, *sems)

    tile_spec = pl.BlockSpec((tq, QK_W), lambda h, i: (i, h))
    head_spec = pl.BlockSpec((t_len, QK_W), lambda h, i: (0, h))
    return pl.pallas_call(
        body, name="fox_fwd_gather" if plan else "fox_fwd", grid=(N_HEADS, nq),
        in_specs=[tile_spec, head_spec, head_spec] + [ANY] * n, out_specs=[tile_spec] + [ANY] * n,
        out_shape=[jax.ShapeDtypeStruct((t_len, ATT_W), F32)] + (plan.out_shape if plan else []),
        scratch_shapes=plan.scratch if plan else [],
        compiler_params=_cparams(("arbitrary", "arbitrary")),
    )(qa, ka, va, *shards)


def _fox_bwd(t_len, tq, qa, ka, va, do, o, comm=None):
    nq = t_len // tq
    plan, comm_in = comm if comm else (None, [])
    n = len(comm_in)

    def body(q_ref, k_ref, v_ref, do_ref, o_ref, *rest):
        c_ins, (dq_ref, dk_ref, dv_ref), c_outs = rest[:n], rest[n:n + 3], rest[n + 3:2 * n + 3]
        acc_s, dl_s, lse_s = rest[2 * n + 3:2 * n + 6]
        sems = rest[2 * n + 6:]
        hd, j = pl.program_id(0), pl.program_id(1)
        if plan:
            @pl.when((hd == 0) & (j == 0))
            def _():
                plan.start(c_ins, c_outs, *sems)

        @pl.when(j == 0)
        def _():
            acc_s[...] = jnp.zeros(acc_s.shape, F32)
            row_of = lambda sel, a: lax.dot_general(sel, a, (((1,), (1,)), ((), ())), precision=lax.Precision.HIGHEST,
                                                    preferred_element_type=F32)
            dl_s[...] = row_of(jnp.ones((8, QK_W), F32), do_ref[...] * o_ref[...])
            lse_s[...] = row_of(jnp.where(_lane((8, QK_W)) == LSE_COL, 1.0, 0.0), o_ref[...])

        kt, vt = k_ref[...], v_ref[...]

        def tile(i, carry, diagonal):
            dk, dv = carry
            rows = pl.ds(pl.multiple_of(i * tq, tq), tq)
            qt = q_ref[rows, :]
            dob = do_ref[rows, :].astype(BF)
            pt = jnp.exp(_dot_nt(kt, qt) - lse_s[0:1, rows])
            if diagonal:
                pt = jnp.where(_tri_mask(tq, True), pt, 0.0)
            dst = (pt * (_dot_nt(vt, dob) - dl_s[0:1, rows])).astype(BF)
            acc_s[rows, :] += _dot_tn(dst, kt)
            return dk + _dot(dst, qt), dv + _dot(pt.astype(BF), dob)

        carry = tile(j, (jnp.zeros((tq, QK_W), F32), jnp.zeros((tq, QK_W), F32)), True)
        dk, dv = lax.fori_loop(j + 1, nq, lambda i, c: tile(i, c, False), carry)
        dk_ref[...] = dk
        dv_ref[...] = dv

        @pl.when(j == nq - 1)
        def _():
            dq_ref[...] = acc_s[...] * (HEAD_DIM ** -0.5)

        if plan:
            @pl.when((hd == N_HEADS - 1) & (j == nq - 1))
            def _():
                plan.finish(c_ins, c_outs, *sems)

    head_spec = pl.BlockSpec((t_len, QK_W), lambda h, j: (0, h))
    tile_spec = pl.BlockSpec((tq, QK_W), lambda h, j: (j, h))
    return pl.pallas_call(
        body, name="fox_bwd_exchange" if plan else "fox_bwd", grid=(N_HEADS, nq),
        in_specs=[head_spec, tile_spec, tile_spec, head_spec, head_spec] + [ANY] * n,
        out_specs=[head_spec, tile_spec, tile_spec] + [ANY] * n,
        out_shape=[jax.ShapeDtypeStruct((t_len, ATT_W), F32)] * 3 + (plan.out_shape if plan else []),
        scratch_shapes=[pltpu.VMEM((t_len, QK_W), F32), pltpu.VMEM((8, t_len), F32), pltpu.VMEM((8, t_len), F32)]
        + (plan.scratch if plan else []),
        compiler_params=_cparams(("arbitrary", "arbitrary")),
    )(qa, ka, va, do, o, *comm_in)


def _mlp_up_fwd(t_len, tm, h, g_pre, w_up):
    def fn(i, n, rows, prevs, nexts, fulls):
        (hv,), (g, w) = rows, fulls
        return [_dot(_rms_fwd(hv, g[...]).astype(BF), w[...])], []

    return _rows_call("mlp_up_fwd", fn, t_len, tm, [h], [g_pre, w_up], [(D_FF, F32)])[0]


def _mlp_down_fwd(t_len, tm, up, h, w_down, g_post):
    def fn(i, n, rows, prevs, nexts, fulls):
        (uv, hv), (w, g) = rows, fulls
        a = jnp.square(jnp.maximum(uv, 0.0))
        ff = _dot(a.astype(BF), w[...])
        return [ff, hv + _rms_fwd(ff, g[...])], []

    return _rows_call("mlp_down_fwd", fn, t_len, tm, [up, h], [w_down, g_post], [(D_MODEL, F32), (D_MODEL, F32)])


def _mlp_bwd_a(t_len, tm, dh, ff, up, w_down, g_post):
    def fn(i, n, rows, prevs, nexts, fulls):
        (dhv, fv, uv), (w, g) = rows, fulls
        dff, dg = _rms_bwd(fv, g[...], dhv)
        dfb = dff.astype(BF)
        dup = _dot_nt(dfb, w[...]) * (2.0 * jnp.maximum(uv, 0.0))
        return [dfb, dup], [dg]

    return _rows_call("mlp_bwd_a", fn, t_len, tm, [dh, ff, up], [w_down, g_post], [(D_MODEL, BF), (D_FF, BF)], [(1, D_MODEL)])


def _mlp_bwd_b(t_len, tm, dh, h, dup, w_up, g_pre):
    def fn(i, n, rows, prevs, nexts, fulls):
        (dhv, hv, duv), (w, g) = rows, fulls
        dhn = _dot_nt(duv, w[...])
        dx, dg = _rms_bwd(hv, g[...], dhn)
        return [dhv + dx, _rms_fwd(hv, g[...])], [dg]

    return _rows_call("mlp_bwd_b", fn, t_len, tm, [dh, h, dup], [w_up, g_pre], [(D_MODEL, F32), (D_MODEL, BF)], [(1, D_MODEL)])


def _ple_fwd(t_len, tm, h, p_i, g_pre, w_gate, w_proj, g_post):
    def fn(i, n, rows, prevs, nexts, fulls):
        (hv, pv), (g, wg, wp, gp) = rows, fulls
        gpre = _dot(_rms_fwd(hv, g[...]).astype(BF), wg[...])
        pe = _dot(pv.astype(BF), wp[...])
        return [gpre, pe, hv + _rms_fwd(pe * _sig(gpre), gp[...])], []

    return _rows_call("ple_fwd", fn, t_len, tm, [h, p_i], [g_pre, w_gate, w_proj, g_post], [(D_MODEL, F32)] * 3)


def _ple_bwd(t_len, tm, dh, h, gpre, pe, p_i, g_pre, w_gate, g_post, comm=None):
    def fn(i, n, rows, prevs, nexts, fulls):
        (dhv, hv, gv, pev, pv), (g, wg, gp) = rows, fulls
        sg = _sig(gv)
        de, d_gp = _rms_bwd(pev * sg, gp[...], dhv)
        dpe = (de * sg).astype(BF)
        dgate = (de * pev * sg * (1.0 - sg)).astype(BF)
        d_wp = _dot_tn(pv.astype(BF), dpe)
        hn = _rms_fwd(hv, g[...])
        d_wg = _dot_tn(hn.astype(BF), dgate)
        dx, d_g = _rms_bwd(hv, g[...], _dot_nt(dgate, wg[...]))
        return [dhv + dx], [d_wg, d_wp, d_g, d_gp]

    return _rows_call("ple_bwd_exchange" if comm else "ple_bwd", fn, t_len, tm, [dh, h, gpre, pe, p_i], [g_pre, w_gate, g_post],
                      [(D_MODEL, F32)], [(D_MODEL, D_MODEL), (D_PLE, D_MODEL), (1, D_MODEL), (1, D_MODEL)], comm=comm)


def _loss_call(t_len, tm, h, target):
    def fn(i, n, rows, prevs, nexts, fulls):
        hv, tv = rows
        err = hv - tv
        part = 0.5 * jnp.sum(jnp.mean(err * err, axis=-1, keepdims=True), axis=0, keepdims=True)
        return [err * (1.0 / D_MODEL)], [jnp.broadcast_to(part, (8, 128))]

    return _rows_call("loss", fn, t_len, tm, [h, target], [], [(D_MODEL, F32)], [(8, 128)])


def _adamw_call(name, w, g, m, v):
    n_l, n_r, n_c = w.shape
    tr = 256 if n_r % 256 == 0 else n_r

    def body(w_ref, g_ref, m_ref, v_ref, d_ref, nm_ref, nv_ref):
        gv = g_ref[...]
        nm = ADAM_B1 * m_ref[...] + (1.0 - ADAM_B1) * gv
        nv = ADAM_B2 * v_ref[...] + (1.0 - ADAM_B2) * jnp.square(gv)
        m_hat = nm / (1.0 - ADAM_B1 ** ADAM_STEP)
        v_hat = nv / (1.0 - ADAM_B2 ** ADAM_STEP)
        d_ref[...] = -ADAM_LR * (m_hat / (jnp.sqrt(v_hat) + ADAM_EPS) + ADAM_WD * w_ref[...])
        nm_ref[...] = nm
        nv_ref[...] = nv

    spec = pl.BlockSpec((1, tr, n_c), lambda l, r: (l, r, 0))
    return pl.pallas_call(
        body, name=name, grid=(n_l, n_r // tr), in_specs=[spec] * 4, out_specs=[spec] * 3,
        out_shape=[jax.ShapeDtypeStruct(w.shape, F32)] * 3,
        compiler_params=_cparams(("parallel", "parallel")),
    )(w, g, m, v)


ANY = pl.BlockSpec(memory_space=pl.ANY)


def _place():
    x, y, c = lax.axis_index("x"), lax.axis_index("y"), lax.axis_index("c")
    chips = [(1 - x, y), (x, 1 - y), (1 - x, 1 - y)]
    return x, y, c, 2 * x + y, chips


def _remote(src, dst, send_sem, recv_sem, dev):
    return pltpu.make_async_remote_copy(src_ref=src, dst_ref=dst, send_sem=send_sem, recv_sem=recv_sem,
                                        device_id=dev, device_id_type=MESH_T)


class _GatherPlan:
    def __init__(self, shards, lo, nl):
        self.shapes = [s.shape for s in shards]
        self.lo, self.nl, self.n = lo, nl, len(shards)
        self.split = [s.shape[1] % 32 == 0 for s in shards]
        self.out_shape = [jax.ShapeDtypeStruct((3, nl, *s.shape[1:]), s.dtype) for s in shards]
        self.scratch = [pltpu.SemaphoreType.DMA((6 * self.n,)), pltpu.SemaphoreType.DMA((6 * self.n,))]

    def _views(self, ins, outs, a, c):
        lay, all_l = pl.ds(self.lo, self.nl), pl.ds(0, self.nl)
        if not self.split[a]:
            return ins[a].at[lay], (lambda j: outs[a].at[j]), None
        hr = self.shapes[a][1] // 2
        mine, other = pl.ds(c * hr, hr), pl.ds((1 - c) * hr, hr)
        return ins[a].at[lay, mine], (lambda j: outs[a].at[j, all_l, mine]), (lambda j: outs[a].at[j, all_l, other])

    def _ici(self, ins, outs, ssem, rsem):
        x, y, c, q, chips = _place()
        cps = []
        for a in range(self.n):
            src, land, _ = self._views(ins, outs, a, c)
            for j, chip in enumerate(chips):
                cps.append(_remote(src, land(j), ssem.at[6 * a + j], rsem.at[6 * a + j], (*chip, c)))
        return cps

    def start(self, ins, outs, ssem, rsem):
        for cp in self._ici(ins, outs, ssem, rsem):
            cp.start()

    def finish(self, ins, outs, ssem, rsem):
        x, y, c, q, chips = _place()
        sib = (x, y, 1 - c)
        cps = self._ici(ins, outs, ssem, rsem)
        for a in range(self.n):
            _, land, _ = self._views(ins, outs, a, c)
            for j in range(3):
                cps[3 * a + j].wait_recv()
                if self.split[a]:
                    cps.append(_remote(land(j), land(j), ssem.at[6 * a + 3 + j], rsem.at[6 * a + 3 + j], sib))
                    cps[-1].start()
        for a in range(self.n):
            _, _, other = self._views(ins, outs, a, c)
            for j in range(3):
                if self.split[a]:
                    _remote(other(j), other(j), ssem.at[6 * a + 3 + j], rsem.at[6 * a + 3 + j], sib).wait_recv()
        for cp in cps:
            cp.wait_send()


def _allgather_weights(shards, lo, nl):
    plan = _GatherPlan(shards, lo, nl)
    n = plan.n

    def body(*refs):
        ins, outs, sems = refs[:n], refs[n:2 * n], refs[2 * n:]
        plan.start(ins, outs, *sems)
        plan.finish(ins, outs, *sems)

    return pl.pallas_call(body, name="allgather_weights", in_specs=[ANY] * n, out_specs=[ANY] * n,
                          out_shape=plan.out_shape, scratch_shapes=plan.scratch)(*shards)


def _by_chip(own, others, chip):
    by_mask = jnp.stack([own, others[1], others[0], others[2]])
    return jnp.stack([lax.dynamic_index_in_dim(by_mask, jnp.bitwise_xor(chip, r), 0, keepdims=False) for r in range(N_CHIPS)])


class _PairPlan:
    def __init__(self, grads):
        self.n = len(grads)
        self.half = [g.shape[2] // 2 for g in grads]
        self.layers = [g.shape[1] for g in grads]
        self.out_shape = [jax.ShapeDtypeStruct((N_CHIPS, g.shape[1], g.shape[2] // 2, g.shape[3]), g.dtype) for g in grads]
        self.scratch = [pltpu.SemaphoreType.DMA((self.n,)), pltpu.SemaphoreType.DMA((self.n,))]

    def _copies(self, ins, outs, ssem, rsem):
        x, y, c, q, chips = _place()
        return [_remote(ins[a].at[pl.ds(0, N_CHIPS), pl.ds(0, self.layers[a]), pl.ds((1 - c) * self.half[a], self.half[a])],
                        outs[a], ssem.at[a], rsem.at[a], (x, y, 1 - c)) for a in range(self.n)]

    def start(self, ins, outs, ssem, rsem):
        for cp in self._copies(ins, outs, ssem, rsem):
            cp.start()

    def finish(self, ins, outs, ssem, rsem):
        for cp in self._copies(ins, outs, ssem, rsem):
            cp.wait()


class _ChipPlan:
    def __init__(self, parts):
        self.n = len(parts)
        self.out_shape = [jax.ShapeDtypeStruct((3, *s.shape[1:]), s.dtype) for s in parts]
        self.scratch = [pltpu.SemaphoreType.DMA((3 * self.n,)), pltpu.SemaphoreType.DMA((3 * self.n,))]

    def _copies(self, ins, outs, ssem, rsem):
        x, y, c, q, chips = _place()
        return [_remote(ins[a].at[2 * cx + cy], outs[a].at[j], ssem.at[3 * a + j], rsem.at[3 * a + j], (cx, cy, c))
                for a in range(self.n) for j, (cx, cy) in enumerate(chips)]

    def start(self, ins, outs, ssem, rsem):
        for cp in self._copies(ins, outs, ssem, rsem):
            cp.start()

    def finish(self, ins, outs, ssem, rsem):
        for cp in self._copies(ins, outs, ssem, rsem):
            cp.wait()


def _exchange_call(name, plan, arrays):
    n = plan.n

    def body(*refs):
        ins, outs, sems = refs[:n], refs[n:2 * n], refs[2 * n:]
        plan.start(ins, outs, *sems)
        plan.finish(ins, outs, *sems)

    return pl.pallas_call(body, name=name, in_specs=[ANY] * n, out_specs=[ANY] * n,
                          out_shape=plan.out_shape, scratch_shapes=plan.scratch)(*arrays)


def _pair_sum(name, g, peer, c_arr):
    _, n_l, half, n_c = peer.shape
    tr = 256 if half % 256 == 0 else half
    nb = half // tr

    def body(c_ref, g_ref, p_ref, o_ref):
        o_ref[...] = (g_ref[...] + p_ref[...]).astype(o_ref.dtype)

    blk = (1, 1, tr, n_c)
    return pl.pallas_call(
        body, name=name,
        grid_spec=pltpu.PrefetchScalarGridSpec(
            num_scalar_prefetch=1, grid=(N_CHIPS, n_l, nb),
            in_specs=[pl.BlockSpec(blk, lambda qi, li, ri, c_ref: (qi, li, c_ref[0] * nb + ri, 0)),
                      pl.BlockSpec(blk, lambda qi, li, ri, c_ref: (qi, li, ri, 0))],
            out_specs=pl.BlockSpec(blk, lambda qi, li, ri, c_ref: (qi, li, ri, 0))),
        out_shape=jax.ShapeDtypeStruct(peer.shape, BF),
        compiler_params=_cparams(("parallel", "parallel", "parallel")),
    )(c_arr, g, peer)


def _chip_sum(name, own, r, chip_arr):
    _, n_l, n_r, n_c = r.shape
    tr = 256 if n_r % 256 == 0 else n_r

    def body(q_ref, r0, r1, r2, r3, o_ref):
        o_ref[...] = ((r0[0].astype(F32) + r1[0].astype(F32)) + r2[0].astype(F32)) + r3[0].astype(F32)

    blk = (1, 1, tr, n_c)
    return pl.pallas_call(
        body, name=name,
        grid_spec=pltpu.PrefetchScalarGridSpec(
            num_scalar_prefetch=1, grid=(n_l, n_r // tr),
            in_specs=[pl.BlockSpec(blk, lambda li, ri, q_ref: (q_ref[0], li, ri, 0))]
            + [pl.BlockSpec(blk, lambda li, ri, q_ref, s=s: (s, li, ri, 0)) for s in range(3)],
            out_specs=pl.BlockSpec((1, tr, n_c), lambda li, ri, q_ref: (li, ri, 0))),
        out_shape=jax.ShapeDtypeStruct((n_l, n_r, n_c), F32),
        compiler_params=_cparams(("parallel", "parallel")),
    )(chip_arr, own, r, r, r)


def _pair_share(halves):
    n = len(halves)

    def body(*refs):
        ins, outs = refs[:n], refs[n:2 * n]
        send_sems, recv_sems = refs[2 * n:]
        x, y, c, q, chips = _place()
        cps = [_remote(ins[a], outs[a], send_sems.at[a], recv_sems.at[a], (x, y, 1 - c)) for a in range(n)]
        for cp in cps:
            cp.start()
        for cp in cps:
            cp.wait()

    return pl.pallas_call(
        body, name="grad_pair_share", in_specs=[ANY] * n, out_specs=[ANY] * n,
        out_shape=[jax.ShapeDtypeStruct(h.shape, h.dtype) for h in halves],
        scratch_shapes=[pltpu.SemaphoreType.DMA((n,)), pltpu.SemaphoreType.DMA((n,))],
    )(*halves)


def _allreduce_small(v):
    n_r = v.shape[0]

    def body(v_ref, o_ref, slots, send_sems, recv_sems):
        x, y, c = lax.axis_index("x"), lax.axis_index("y"), lax.axis_index("c")
        me = 4 * x + 2 * y + c
        slots[me] = v_ref[...]
        cps = []
        for r in range(1, 8):
            px = 1 - x if r & 4 else x
            py = 1 - y if r & 2 else y
            pc = 1 - c if r & 1 else c
            cps.append(_remote(v_ref, slots.at[me], send_sems.at[r - 1], recv_sems.at[r - 1], (px, py, pc)))
            cps[-1].start()
        for cp in cps:
            cp.wait()
        tot = slots[0]
        for d in range(1, 8):
            tot = tot + slots[d]
        o_ref[...] = tot

    return pl.pallas_call(
        body, name="allreduce_small",
        in_specs=[pl.BlockSpec(memory_space=pltpu.VMEM)], out_specs=pl.BlockSpec(memory_space=pltpu.VMEM),
        out_shape=jax.ShapeDtypeStruct(v.shape, F32),
        scratch_shapes=[pltpu.VMEM((8, n_r, 128), F32), pltpu.SemaphoreType.DMA((7,)), pltpu.SemaphoreType.DMA((7,))],
        compiler_params=pltpu.CompilerParams(vmem_limit_bytes=VMEM_LIMIT_BYTES),
    )(v)


def _cols_full(g, l):
    s = g[:, l]
    return s.transpose(1, 0, 2).reshape(s.shape[1], -1)


def _rows_full(g, l):
    s = g[:, l]
    return s.reshape(-1, s.shape[-1])


def _cols_split(full):
    r = full.shape[0]
    return full.reshape(r, N_CHIPS, -1).transpose(1, 0, 2)


def _rows_split(full):
    return full.reshape(N_CHIPS, -1, full.shape[-1])


def _pack(parts):
    flat = []
    for a in parts:
        f = a.reshape(-1).astype(F32)
        flat.append(jnp.pad(f, (0, (-f.shape[0]) % 1024)))
    return jnp.concatenate(flat).reshape(-1, 128)


def _unpack(buf, shapes):
    flat = buf.reshape(-1)
    out, off = [], 0
    for s in shapes:
        size = 1
        for d in s:
            size *= d
        out.append(flat[off:off + size].reshape(s))
        off += size + (-size) % 1024
    return out


def kernel(x, p, g_mix_pre, w_in, b_forget, w_conf_dw, conf_ln_g, conf_ln_b, w_conf_pw, w_sc, w_pool, pool_scale, w_out, g_mix_post, g_mlp_pre, w_up, w_down, g_mlp_post, g_ple_pre, w_ple_gate, w_ple_proj, g_ple_post, loss_target, m_g_mix_pre, m_w_in, m_b_forget, m_w_conf_dw, m_conf_ln_g, m_conf_ln_b, m_w_conf_pw, m_w_sc, m_w_pool, m_pool_scale, m_w_out, m_g_mix_post, m_g_mlp_pre, m_w_up, m_w_down, m_g_mlp_post, m_g_ple_pre, m_w_ple_gate, m_w_ple_proj, m_g_ple_post, v_g_mix_pre, v_w_in, v_b_forget, v_w_conf_dw, v_conf_ln_g, v_conf_ln_b, v_w_conf_pw, v_w_sc, v_w_pool, v_pool_scale, v_w_out, v_g_mix_post, v_g_mlp_pre, v_w_up, v_w_down, v_g_mlp_post, v_g_ple_pre, v_w_ple_gate, v_w_ple_proj, v_g_ple_post):
    names = ['g_mix_pre', 'w_in', 'b_forget', 'w_conf_dw', 'conf_ln_g', 'conf_ln_b', 'w_conf_pw', 'w_sc', 'w_pool', 'pool_scale',
             'w_out', 'g_mix_post', 'g_mlp_pre', 'w_up', 'w_down', 'g_mlp_post', 'g_ple_pre', 'w_ple_gate', 'w_ple_proj', 'g_ple_post']
    env = locals()
    wts = {k: env[k] for k in names}
    mom = {k: env["m_" + k] for k in names}
    var = {k: env["v_" + k] for k in names}

    t_len = x.shape[1]
    tm = min(256, t_len)
    tb = min(512, t_len)
    tq = min(512, max(t_len // 2, 128))
    tt = min(1024, t_len)
    chip = 2 * lax.axis_index("x") + lax.axis_index("y")
    core = lax.axis_index("c")

    big = ['w_in', 'w_conf_pw', 'w_out', 'w_up', 'w_down', 'w_ple_gate', 'w_ple_proj']
    tiny = ['w_conf_dw', 'w_sc']
    own_shards = [wts[k].astype(BF) for k in big] + [wts[k] for k in tiny]

    def chip_order(others, lo, nl):
        return {k: _by_chip(own[lo:lo + nl], oth, chip) for k, own, oth in zip(big + tiny, own_shards, others)}

    def layer_weights(gat, li, l):
        w_full = _cols_full(gat['w_in'], li)
        w_a = jnp.concatenate([w_full[:, :F_OFF], w_full[:, F_OFF + N_HEADS:], w_full[:, F_OFF:F_OFF + N_HEADS],
                               jnp.zeros((D_MODEL, Z_W - D_IN), BF)], axis=1)
        w_bd = jnp.zeros((D_GRP, D_GRP), F32)
        for g in range(4):
            w_bd = lax.dynamic_update_slice(w_bd, wts['w_pool'][l, g], (64 * g, 64 * g))
        row = lambda a: a[l][None, :]
        return dict(
            w_a=w_a, w_dw=jnp.pad(_cols_full(gat['w_conf_dw'], li), ((0, 1), (0, 0))), w_pw=_rows_full(gat['w_conf_pw'], li),
            w_sc=jnp.pad(_cols_full(gat['w_sc'], li), ((0, 5), (0, 0))), w_bd=w_bd.astype(BF),
            w_out=_rows_full(gat['w_out'], li), w_up=_cols_full(gat['w_up'], li), w_down=_rows_full(gat['w_down'], li),
            w_gate=_rows_full(gat['w_ple_gate'], li), w_proj=_cols_full(gat['w_ple_proj'], li),
            b_f=jnp.pad(wts['b_forget'][l], (0, F_PAD - N_HEADS))[None, :],
            ln_g=row(wts['conf_ln_g']), ln_b=row(wts['conf_ln_b']), pool_scale=row(wts['pool_scale']),
            g_mix_pre=row(wts['g_mix_pre']), g_mix_post=row(wts['g_mix_post']), g_mlp_pre=row(wts['g_mlp_pre']),
            g_mlp_post=row(wts['g_mlp_post']), g_ple_pre=row(wts['g_ple_pre']), g_ple_post=row(wts['g_ple_post']))

    lw = [layer_weights(chip_order(_allgather_weights(own_shards, 0, 1), 0, 1), 0, 0)]

    h = x[0]
    saved = []
    for l in range(DEPTH):
        w = lw[l]
        s = dict(h0=h)
        s['zc'], s['qa'], kv, s['zs'], s['zp'], s['zf'] = _mix_in_fwd(t_len, tb, h, w['g_mix_pre'], w['w_a'])
        s['cv'], y_conf = _conf_fwd(t_len, tm, s['zc'], w['w_dw'], w['ln_g'], w['ln_b'], w['w_pw'])
        s['ka'], s['va'] = _fox_prep(t_len, s['zf'], kv, w['b_f'])
        if l == 0:
            s['o'], *others = _fox_fwd(t_len, tq, s['qa'], s['ka'], s['va'],
                                       gather=(_GatherPlan(own_shards, 1, DEPTH - 1), own_shards))
            rest = chip_order(others, 1, DEPTH - 1)
            lw += [layer_weights(rest, li, li + 1) for li in range(DEPTH - 1)]
        else:
            (s['o'],) = _fox_fwd(t_len, tq, s['qa'], s['ka'], s['va'])
        (y_sc,) = _sconv_fwd(t_len, tm, s['zs'], w['w_sc'])
        (y_pool,) = _pool_fwd(t_len, tm, s['zp'], w['w_bd'], w['pool_scale'])
        s['ys'] = [y_conf, s['o'], y_sc, y_pool]
        s['mix'], h = _mix_out_fwd(t_len, tb, s['ys'], h, w['w_out'], w['g_mix_post'])
        s['h1'] = h
        s['up'] = _mlp_up_fwd(t_len, tb, h, w['g_mlp_pre'], w['w_up'])
        s['ff'], h = _mlp_down_fwd(t_len, tm, s['up'], h, w['w_down'], w['g_mlp_post'])
        s['h2'] = h
        s['gpre'], s['pe'], h = _ple_fwd(t_len, tb, h, p[l, 0], w['g_ple_pre'], w['w_gate'], w['w_proj'], w['g_ple_post'])
        saved.append(s)

    dh, loss_part = _loss_call(t_len, tm, h, loss_target[0])

    grads = [None] * DEPTH
    split = dict(w_in=_cols_split, w_conf_pw=_rows_split, w_out=_rows_split, w_up=_cols_split, w_down=_rows_split,
                 w_ple_gate=_rows_split, w_ple_proj=_cols_split)
    c_arr = core.astype(jnp.int32).reshape(1)
    chip_arr = chip.astype(jnp.int32).reshape(1)

    def contrib_of(layers):
        return [jnp.stack([split[k](grads[l][k]) for l in layers], axis=1) for k in big]

    def pair_sums(tag, contrib, peer):
        return [_pair_sum(f"grad_pair_sum_{tag}_{k}", a, b, c_arr) for k, a, b in zip(big, contrib, peer)]

    def finish_reduce(tag, parts, landed):
        halves = [_chip_sum(f"grad_chip_sum_{tag}_{k}", own, r, chip_arr) for k, own, r in zip(big, parts, landed)]
        full = []
        for mine, theirs in zip(halves, _pair_share(halves)):
            both = jnp.stack([mine, theirs])
            full.append(jnp.concatenate([lax.dynamic_index_in_dim(both, core, 0, keepdims=False),
                                         lax.dynamic_index_in_dim(both, 1 - core, 0, keepdims=False)], axis=1))
        return full

    for l in reversed(range(DEPTH)):
        w, s, g = lw[l], saved[l], {}
        contrib13 = contrib_of(range(1, DEPTH)) if l == 0 else []
        dh, g['w_ple_gate'], g['w_ple_proj'], g['g_ple_pre'], g['g_ple_post'], *peer13 = _ple_bwd(
            t_len, tm, dh, s['h2'], s['gpre'], s['pe'], p[l, 0], w['g_ple_pre'], w['w_gate'], w['g_ple_post'],
            comm=(_PairPlan(contrib13), contrib13) if l == 0 else None)
        parts13 = pair_sums("l13", contrib13, peer13) if l == 0 else []
        dff, dup, g['g_mlp_post'] = _mlp_bwd_a(t_len, tm, dh, s['ff'], s['up'], w['w_down'], w['g_mlp_post'])
        dh, hn, g['g_mlp_pre'] = _mlp_bwd_b(t_len, tb, dh, s['h1'], dup, w['w_up'], w['g_mlp_pre'])
        g['w_up'] = _mm_tn("mlp_dw_up", hn, dup, D_MODEL, 1024, tt)
        g['w_down'] = _mm_tn("mlp_dw_down", s['up'], dff, 1024, D_MODEL, tt, pro=lambda u: jnp.square(jnp.maximum(u, 0.0)))
        dy_conf, dy_att, dy_sc, dy_pool, g['w_out'], g['g_mix_post'] = _mix_out_bwd(t_len, tb, dh, s['mix'], s['ys'], w['w_out'], w['g_mix_post'])
        dzc, g['w_conf_dw'], g['conf_ln_g'], g['conf_ln_b'], g['w_conf_pw'] = _conf_bwd(
            t_len, tm, s['zc'], s['cv'], dy_conf, w['w_dw'], w['ln_g'], w['ln_b'], w['w_pw'])
        dzs, g['w_sc'] = _sconv_bwd(t_len, tm, s['zs'], dy_sc, w['w_sc'])
        dzp, d_wbd, g['pool_scale'] = _pool_bwd(t_len, tm, s['zp'], dy_pool, w['w_bd'], w['pool_scale'])
        g['w_pool'] = jnp.stack([d_wbd[64 * a:64 * (a + 1), 64 * a:64 * (a + 1)] for a in range(4)])
        dqa, dka, dva, *landed13 = _fox_bwd(t_len, tq, s['qa'], s['ka'], s['va'], dy_att, s['o'],
                                            comm=(_ChipPlan(parts13), parts13) if l == 0 else None)
        dzf, d_bf = _fox_post(t_len, dqa, dka, s['zf'], w['b_f'])
        g['b_forget'] = d_bf[0, :N_HEADS]
        dh, xn, dz, g['g_mix_pre'] = _mix_in_bwd(t_len, tm, dh, s['h0'], dzc, dqa, dka, dva, dzs, dzp, dzf, w['g_mix_pre'], w['w_a'])
        d_wa = _mm_tn("mix_dw_in", xn, dz, D_MODEL, Z_W, tt)
        g['w_in'] = jnp.concatenate([d_wa[:, :F_OFF], d_wa[:, Z_F:Z_F + N_HEADS], d_wa[:, F_OFF:Z_F]], axis=1)
        g['w_conf_dw'] = g['w_conf_dw'][:CONF_K]
        g['w_sc'] = g['w_sc'][:SC_K]
        grads[l] = g
    grad_x = dh[None]

    contrib0 = contrib_of([0])
    parts0 = pair_sums("l0", contrib0, _exchange_call("grad_pair_exchange", _PairPlan(contrib0), contrib0))
    first = finish_reduce("l0", parts0, _exchange_call("grad_chip_exchange", _ChipPlan(parts0), parts0))
    later = finish_reduce("l13", parts13, landed13)
    reduced = {k: jnp.concatenate([a, b], axis=0) for k, a, b in zip(big, first, later)}

    small = [k for k in names if k not in big]
    small_shapes = [(DEPTH, *grads[0][k].shape) for k in small]
    packed = _pack([jnp.stack([grads[l][k] for l in range(DEPTH)]) for k in small] + [loss_part])
    summed = _allreduce_small(packed)
    small_sum = _unpack(summed, small_shapes + [(8, 128)])
    loss = small_sum[-1][0, 0]
    for k, a in zip(small, small_sum[:-1]):
        if k in tiny:
            a = lax.dynamic_slice_in_dim(a, chip * 64, 64, axis=2)
        reduced[k] = a.reshape(wts[k].shape)

    delta_w, new_m, new_v = {}, {}, {}
    for k in names:
        shp = wts[k].shape
        as3 = (lambda a: a.reshape(shp[0], -1, shp[-1])) if len(shp) > 2 else (lambda a: a.reshape(1, shp[0], shp[1]))
        d, nm, nv = _adamw_call("adamw_" + k, as3(wts[k]), as3(reduced[k]), as3(mom[k]), as3(var[k]))
        delta_w[k], new_m[k], new_v[k] = d.reshape(shp), nm.reshape(shp), nv.reshape(shp)

    return (loss, grad_x, *[reduced[k] for k in names], *[delta_w[k] for k in names],
            *[new_m[k] for k in names], *[new_v[k] for k in names])
```

```python
import functools

import jax
import jax.numpy as jnp
from jax import lax
from jax.experimental import pallas as pl
from jax.experimental.pallas import tpu as pltpu

F32 = jnp.float32
BF = jnp.bfloat16

DEPTH = 4
D_MODEL = 1024
D_GRP = 256
HEAD_DIM = 64
N_HEADS = 4
CONF_K = 31
SC_K = 3
D_FF = 4096
D_PLE = 256
EPS = 1e-6
N_CHIPS = 4
Z_CONF, Z_QKV, Z_SC, Z_POOL, Z_F = 0, 512, 1280, 2048, 2304
Z_W = 2432
F_PAD = 128
D_IN = 2308
F_OFF = 1280

ADAM_LR, ADAM_B1, ADAM_B2, ADAM_EPS, ADAM_WD, ADAM_STEP = 0.001, 0.9, 0.999, 1e-08, 0.01, 10

VMEM_LIMIT_BYTES = 56 * 1024 * 1024
HALO = 32
NEG = -1e30
MESH_T = pl.DeviceIdType.MESH


def _cparams(sem=None):
    return pltpu.CompilerParams(dimension_semantics=sem, vmem_limit_bytes=VMEM_LIMIT_BYTES)


def _dot(a, b):
    return jnp.dot(a, b, preferred_element_type=F32)


def _dot_nt(a, b):
    return lax.dot_general(a, b, (((1,), (1,)), ((), ())), preferred_element_type=F32)


def _dot_tn(a, b):
    return lax.dot_general(a, b, (((0,), (0,)), ((), ())), preferred_element_type=F32)


def _sig(x):
    return jax.nn.sigmoid(x)


def _rms_fwd(x, g):
    r = lax.rsqrt(jnp.mean(x * x, axis=-1, keepdims=True) + EPS)
    return x * r * g


def _rms_bwd(x, g, dy):
    r = lax.rsqrt(jnp.mean(x * x, axis=-1, keepdims=True) + EPS)
    xh = x * r
    dg = jnp.sum(dy * xh, axis=0, keepdims=True)
    dxh = dy * g
    dx = r * (dxh - xh * jnp.mean(dxh * xh, axis=-1, keepdims=True))
    return dx, dg


def _back(ext, d):
    return ext if d == 0 else pltpu.roll(ext, d, 0)


def _ahead(ext, d):
    return ext if d == 0 else pltpu.roll(ext, ext.shape[0] - d, 0)


def _rows_call(name, fn, t_len, tm, rows, fulls, out_rows, out_accs=(), prevs=(), nexts=(), comm=None):
    plan, comm_in = comm if comm else (None, [])
    n_comm = len(comm_in)
    n = t_len // tm
    hb = tm // HALO
    nhb = t_len // HALO
    n_rows, n_prev, n_next, n_full = len(rows), len(prevs), len(nexts), len(fulls)
    in_specs = [pl.BlockSpec((tm, a.shape[1]), lambda i: (i, 0)) for a in rows]
    in_specs += [pl.BlockSpec((HALO, a.shape[1]), lambda i: (jnp.maximum(i * hb - 1, 0), 0)) for a in prevs]
    in_specs += [pl.BlockSpec((HALO, a.shape[1]), lambda i: (jnp.minimum((i + 1) * hb, nhb - 1), 0)) for a in nexts]
    in_specs += [pl.BlockSpec(a.shape, lambda i, nd=a.ndim: (0,) * nd, pipeline_mode=pl.Buffered(1)) for a in fulls]
    out_shape = [jax.ShapeDtypeStruct((t_len, c), dt) for c, dt in out_rows]
    out_shape += [jax.ShapeDtypeStruct(s, F32) for s in out_accs]
    out_specs = [pl.BlockSpec((tm, c), lambda i: (i, 0)) for c, _ in out_rows]
    out_specs += [pl.BlockSpec(s, lambda i, nd=len(s): (0,) * nd) for s in out_accs]
    n_in = n_rows + n_prev + n_next + n_full
    n_ro = len(out_rows)

    n_out = n_ro + len(out_accs)

    def body(*refs):
        i = pl.program_id(0)
        ins, outs = refs[:n_in], refs[n_in + n_comm:n_in + n_comm + n_out]
        c_ins, c_outs, sems = refs[n_in:n_in + n_comm], refs[n_in + n_comm + n_out:n_in + 2 * n_comm + n_out], refs[n_in + 2 * n_comm + n_out:]
        if plan:
            @pl.when(i == 0)
            def _():
                plan.start(c_ins, c_outs, *sems)

        rv = [r[...] for r in ins[:n_rows]]
        pv = [r[...] for r in ins[n_rows:n_rows + n_prev]]
        nv = [r[...] for r in ins[n_rows + n_prev:n_rows + n_prev + n_next]]
        fv = list(ins[n_rows + n_prev + n_next:])
        ro, ao = fn(i, n, rv, pv, nv, fv)
        for r, v in zip(outs[:n_ro], ro):
            r[...] = v.astype(r.dtype)
        if out_accs:
            acc = outs[n_ro:]

            @pl.when(i == 0)
            def _():
                for r in acc:
                    r[...] = jnp.zeros(r.shape, r.dtype)

            for r, v in zip(acc, ao):
                r[...] += v
        if plan:
            @pl.when(i == n - 1)
            def _():
                plan.finish(c_ins, c_outs, *sems)

    res = pl.pallas_call(
        body, name=name, grid=(n,), in_specs=in_specs + [ANY] * n_comm, out_specs=out_specs + [ANY] * n_comm,
        out_shape=out_shape + (plan.out_shape if plan else []), scratch_shapes=plan.scratch if plan else [],
        compiler_params=_cparams(("arbitrary",)),
    )(*rows, *prevs, *nexts, *fulls, *comm_in)
    return res


def _mm_tn(name, x, y, tk, tn, tt, pro=None):
    t_len, k_dim = x.shape
    n_dim = y.shape[1]

    def body(x_ref, y_ref, o_ref):
        @pl.when(pl.program_id(2) == 0)
        def _():
            o_ref[...] = jnp.zeros(o_ref.shape, o_ref.dtype)

        xv = x_ref[...]
        if pro is not None:
            xv = pro(xv)
        o_ref[...] += _dot_tn(xv.astype(BF), y_ref[...].astype(BF))

    return pl.pallas_call(
        body, name=name, grid=(k_dim // tk, n_dim // tn, t_len // tt),
        in_specs=[pl.BlockSpec((tt, tk), lambda a, b, t: (t, a)), pl.BlockSpec((tt, tn), lambda a, b, t: (t, b))],
        out_specs=pl.BlockSpec((tk, tn), lambda a, b, t: (a, b)),
        out_shape=jax.ShapeDtypeStruct((k_dim, n_dim), F32),
        compiler_params=_cparams(("parallel", "parallel", "arbitrary")),
    )(x, y)


QK_W = 128
ATT_W = N_HEADS * QK_W
C_COL = HEAD_DIM
ONE_COL = HEAD_DIM + 3


def _lane(shape):
    return lax.broadcasted_iota(jnp.int32, shape, 1)


def _head_low(a256, h):
    pair = a256[:, QK_W * (h // 2):QK_W * (h // 2 + 1)]
    return pltpu.roll(pair, HEAD_DIM, 1) if h % 2 else pair


def _heads_spread(a256):
    low = _lane((a256.shape[0], QK_W)) < HEAD_DIM
    return jnp.concatenate([jnp.where(low, _head_low(a256, h), 0.0) for h in range(N_HEADS)], axis=1)


def _heads_packed(a512):
    low = _lane((a512.shape[0], QK_W)) < HEAD_DIM
    out = []
    for pair in range(N_HEADS // 2):
        even = a512[:, QK_W * 2 * pair:QK_W * (2 * pair + 1)]
        odd = a512[:, QK_W * (2 * pair + 1):QK_W * (2 * pair + 2)]
        out.append(jnp.where(low, even, pltpu.roll(odd, HEAD_DIM, 1)))
    return jnp.concatenate(out, axis=1)


def _mix_in_fwd(t_len, tm, h, g_pre, w_a):
    def fn(i, n, rows, prevs, nexts, fulls):
        (hv,), (g, w) = rows, fulls
        z = _dot(_rms_fwd(hv, g[...]).astype(BF), w[...])
        lane = _lane((tm, QK_W))
        ones = jnp.where(lane < ONE_COL, 1.0, 0.0)
        zq = z[:, Z_QKV:Z_QKV + D_GRP]
        qa = jnp.concatenate([jnp.where(lane < HEAD_DIM, _head_low(zq, a) * (HEAD_DIM ** -0.5), ones) for a in range(N_HEADS)], axis=1)
        return [z[:, Z_CONF:Z_QKV], qa, z[:, Z_QKV + D_GRP:Z_SC], z[:, Z_SC:Z_POOL], z[:, Z_POOL:Z_F], z[:, Z_F:Z_W]], []

    return _rows_call("mix_in_fwd", fn, t_len, tm, [h], [g_pre, w_a],
                      [(512, F32), (ATT_W, BF), (2 * D_GRP, BF), (768, F32), (256, F32), (F_PAD, F32)])


def _mix_in_bwd(t_len, tm, dh, h, dzc, dqa, dka, dva, dzs, dzp, dzf, g_pre, w_a):
    def fn(i, n, rows, prevs, nexts, fulls):
        dhv, hv, a, dq, dk, dv, c, d, e = rows
        g, w = fulls
        b = jnp.concatenate([_heads_packed(dq), _heads_packed(dk), _heads_packed(dv)], axis=1).astype(BF)
        dz = jnp.concatenate([a, b, c, d, e], axis=1)
        dxn = _dot_nt(dz, w[...])
        dx, dg = _rms_bwd(hv, g[...], dxn)
        xn = _rms_fwd(hv, g[...])
        return [dhv + dx, xn, dz], [dg]

    return _rows_call("mix_in_bwd", fn, t_len, tm, [dh, h, dzc, dqa, dka, dva, dzs, dzp, dzf], [g_pre, w_a],
                      [(D_MODEL, F32), (D_MODEL, BF), (Z_W, BF)], [(1, D_MODEL)])


def _glu_ext(i, zc, zc_prev):
    ext = jnp.concatenate([zc_prev, zc], axis=0)
    u = ext[:, :D_GRP] * _sig(ext[:, D_GRP:])
    row = lax.broadcasted_iota(jnp.int32, u.shape, 0)
    return jnp.where((row >= HALO) | (i > 0), u, 0.0)


def _conf_fwd(t_len, tm, zc, w_dw, ln_g, ln_b, w_pw):
    def fn(i, n, rows, prevs, nexts, fulls):
        (zv,), (zp,) = rows, prevs
        wdw, lg, lb, wpw = fulls
        u = _glu_ext(i, zv, zp)
        cv = jnp.zeros((tm, D_GRP), F32)
        for k in range(CONF_K):
            cv = cv + wdw[k:k + 1, :] * _back(u, CONF_K - 1 - k)[HALO:, :]
        mu = jnp.mean(cv, axis=-1, keepdims=True)
        xc = cv - mu
        ln = xc * lax.rsqrt(jnp.mean(xc * xc, axis=-1, keepdims=True) + EPS) * lg[...] + lb[...]
        s = ln * _sig(ln)
        return [cv, _dot(s.astype(BF), wpw[...])], []

    return _rows_call("conf_fwd", fn, t_len, tm, [zc], [w_dw, ln_g, ln_b, w_pw], [(D_GRP, F32), (D_GRP, BF)], prevs=[zc])


def _conf_bwd(t_len, tm, zc, cv, dy, w_dw, ln_g, ln_b, w_pw):
    def fn(i, n, rows, prevs, nexts, fulls):
        zv, cvv, dyv = rows
        (zp,) = prevs
        cvn, dyn = nexts
        wdw, lg, lb, wpw = fulls
        cve = jnp.concatenate([cvv, cvn], axis=0)
        dye = jnp.concatenate([dyv, dyn], axis=0)
        mu = jnp.mean(cve, axis=-1, keepdims=True)
        xc = cve - mu
        rs = lax.rsqrt(jnp.mean(xc * xc, axis=-1, keepdims=True) + EPS)
        xh = xc * rs
        ln = xh * lg[...] + lb[...]
        sg = _sig(ln)
        s = ln * sg
        ds = _dot_nt(dye.astype(BF), wpw[...])
        dln = ds * (sg * (1.0 + ln * (1.0 - sg)))
        dxh = dln * lg[...]
        dcv = rs * (dxh - jnp.mean(dxh, axis=-1, keepdims=True) - xh * jnp.mean(dxh * xh, axis=-1, keepdims=True))
        row = lax.broadcasted_iota(jnp.int32, dcv.shape, 0)
        dcv = jnp.where((row < tm) | (i < n - 1), dcv, 0.0)
        d_lg = jnp.sum((dln * xh)[:tm], axis=0, keepdims=True)
        d_lb = jnp.sum(dln[:tm], axis=0, keepdims=True)
        d_wpw = _dot_tn(s[:tm].astype(BF), dyv.astype(BF))
        u = _glu_ext(i, zv, zp)
        dcv_cur = dcv[:tm]
        du = jnp.zeros((tm, D_GRP), F32)
        d_wdw = jnp.zeros((32, D_GRP), F32)
        krow = lax.broadcasted_iota(jnp.int32, (32, D_GRP), 0)
        for k in range(CONF_K):
            d = CONF_K - 1 - k
            du = du + wdw[k:k + 1, :] * _ahead(dcv, d)[:tm, :]
            tap = _back(u, d)[HALO:, :]
            d_wdw = d_wdw + jnp.where(krow == k, jnp.sum(dcv_cur * tap, axis=0, keepdims=True), 0.0)
        a, b = zv[:, :D_GRP], zv[:, D_GRP:]
        sb = _sig(b)
        dz = jnp.concatenate([du * sb, du * a * sb * (1.0 - sb)], axis=1)
        return [dz], [d_wdw, d_lg, d_lb, d_wpw]

    return _rows_call("conf_bwd", fn, t_len, tm, [zc, cv, dy], [w_dw, ln_g, ln_b, w_pw], [(512, BF)],
                      [(32, D_GRP), (1, D_GRP), (1, D_GRP), (D_GRP, D_GRP)], prevs=[zc], nexts=[cv, dy])


def _sc_ext(i, zs, zs_prev):
    ext = jnp.concatenate([zs_prev, zs], axis=0)
    e = ext[:, 2 * D_GRP:] * ext[:, :D_GRP]
    row = lax.broadcasted_iota(jnp.int32, e.shape, 0)
    return jnp.where((row >= HALO) | (i > 0), e, 0.0)


def _sconv_fwd(t_len, tm, zs, w_sc):
    def fn(i, n, rows, prevs, nexts, fulls):
        (zv,), (zp,), (w,) = rows, prevs, fulls
        e = _sc_ext(i, zv, zp)
        cv = jnp.zeros((tm, D_GRP), F32)
        for k in range(SC_K):
            cv = cv + w[k:k + 1, :] * _back(e, SC_K - 1 - k)[HALO:, :]
        return [zv[:, D_GRP:2 * D_GRP] * cv], []

    return _rows_call("sconv_fwd", fn, t_len, tm, [zs], [w_sc], [(D_GRP, BF)], prevs=[zs])


def _sconv_bwd(t_len, tm, zs, dy, w_sc):
    def fn(i, n, rows, prevs, nexts, fulls):
        zv, dyv = rows
        (zp,) = prevs
        zn, dyn = nexts
        (w,) = fulls
        e = _sc_ext(i, zv, zp)
        taps = [_back(e, SC_K - 1 - k)[HALO:, :] for k in range(SC_K)]
        cv = w[0:1, :] * taps[0] + w[1:2, :] * taps[1] + w[2:3, :] * taps[2]
        bg = zv[:, D_GRP:2 * D_GRP]
        dcv = jnp.concatenate([dyv * bg, dyn * zn[:, D_GRP:2 * D_GRP]], axis=0)
        row = lax.broadcasted_iota(jnp.int32, dcv.shape, 0)
        dcv = jnp.where((row < tm) | (i < n - 1), dcv, 0.0)
        de = jnp.zeros((tm, D_GRP), F32)
        d_w = jnp.zeros((8, D_GRP), F32)
        krow = lax.broadcasted_iota(jnp.int32, (8, D_GRP), 0)
        for k in range(SC_K):
            de = de + w[k:k + 1, :] * _ahead(dcv, SC_K - 1 - k)[:tm, :]
            d_w = d_w + jnp.where(krow == k, jnp.sum(dcv[:tm] * taps[k], axis=0, keepdims=True), 0.0)
        dz = jnp.concatenate([de * zv[:, 2 * D_GRP:], dyv * cv, de * zv[:, :D_GRP]], axis=1)
        return [dz], [d_w]

    return _rows_call("sconv_bwd", fn, t_len, tm, [zs, dy], [w_sc], [(768, BF)], [(8, D_GRP)], prevs=[zs], nexts=[zs, dy])


def _pool_window(shape):
    grp = lax.broadcasted_iota(jnp.int32, shape, 1) // 64
    return grp, jnp.where(grp == 0, 2.0, jnp.where(grp == 1, 4.0, jnp.where(grp == 2, 8.0, 16.0)))


def _pool_d(i, tm, zv, zp):
    ext = jnp.concatenate([zp, zv], axis=0)
    row = lax.broadcasted_iota(jnp.int32, ext.shape, 0)
    ext = jnp.where((row >= HALO) | (i > 0), ext, 0.0)
    s2 = ext + _back(ext, 1)
    s4 = s2 + _back(s2, 2)
    s8 = s4 + _back(s4, 4)
    s16 = s8 + _back(s8, 8)
    grp, win = _pool_window((tm, D_GRP))
    sel = jnp.where(grp == 0, s2[HALO:], jnp.where(grp == 1, s4[HALO:], jnp.where(grp == 2, s8[HALO:], s16[HALO:])))
    pos = (i * tm + lax.broadcasted_iota(jnp.int32, (tm, D_GRP), 0) + 1).astype(F32)
    return sel / jnp.minimum(pos, win) - zv


def _pool_fwd(t_len, tm, zpool, w_bd, scale):
    def fn(i, n, rows, prevs, nexts, fulls):
        (zv,), (zp,) = rows, prevs
        w, sc = fulls
        d = _pool_d(i, tm, zv, zp)
        return [_dot(d.astype(BF), w[...]) * sc[...]], []

    return _rows_call("pool_fwd", fn, t_len, tm, [zpool], [w_bd, scale], [(D_GRP, BF)], prevs=[zpool])


def _pool_bwd(t_len, tm, zpool, dy, w_bd, scale):
    def fn(i, n, rows, prevs, nexts, fulls):
        zv, dyv = rows
        (zp,) = prevs
        (dyn,) = nexts
        w, sc = fulls
        d = _pool_d(i, tm, zv, zp)
        lin = _dot(d.astype(BF), w[...])
        d_sc = jnp.sum(dyv * lin, axis=0, keepdims=True)
        dye = jnp.concatenate([dyv, dyn], axis=0) * sc[...]
        d_w = _dot_tn(d.astype(BF), dye[:tm].astype(BF))
        dd = _dot_nt(dye.astype(BF), w[...])
        row = lax.broadcasted_iota(jnp.int32, dd.shape, 0)
        dd = jnp.where((row < tm) | (i < n - 1), dd, 0.0)
        grp, win = _pool_window(dd.shape)
        pos = (i * tm + row + 1).astype(F32)
        ddc = dd / jnp.minimum(pos, win)
        f2 = ddc + _ahead(ddc, 1)
        f4 = f2 + _ahead(f2, 2)
        f8 = f4 + _ahead(f4, 4)
        f16 = f8 + _ahead(f8, 8)
        sel = jnp.where(grp == 0, f2, jnp.where(grp == 1, f4, jnp.where(grp == 2, f8, f16)))
        return [(sel - dd)[:tm]], [d_w, d_sc]

    return _rows_call("pool_bwd", fn, t_len, tm, [zpool, dy], [w_bd, scale], [(D_GRP, BF)],
                      [(D_GRP, D_GRP), (1, D_GRP)], prevs=[zpool], nexts=[dy])


def _mix_cat(y_conf, o_att, y_sc, y_pool):
    return jnp.concatenate([y_conf, _heads_packed(o_att).astype(BF), y_sc, y_pool], axis=1)


def _mix_out_fwd(t_len, tm, ys, h, w_out, g_post):
    def fn(i, n, rows, prevs, nexts, fulls):
        y0, y1, y2, y3, hv = rows
        w, g = fulls
        mix = _dot(_mix_cat(y0, y1, y2, y3), w[...])
        return [mix, hv + _rms_fwd(mix, g[...])], []

    return _rows_call("mix_out_fwd", fn, t_len, tm, [*ys, h], [w_out, g_post], [(D_MODEL, F32), (D_MODEL, F32)])


def _mix_out_bwd(t_len, tm, dh, mix, ys, w_out, g_post):
    def fn(i, n, rows, prevs, nexts, fulls):
        dhv, mv, y0, y1, y2, y3 = rows
        w, g = fulls
        dmix, dg = _rms_bwd(mv, g[...], dhv)
        dmb = dmix.astype(BF)
        dcat = _dot_nt(dmb, w[...])
        d_w = _dot_tn(_mix_cat(y0, y1, y2, y3), dmb)
        return [dcat[:, :256], _heads_spread(dcat[:, 256:512]), dcat[:, 512:768], dcat[:, 768:]], [d_w, dg]

    return _rows_call("mix_out_bwd", fn, t_len, tm, [dh, mix, *ys], [w_out, g_post],
                      [(D_GRP, F32), (ATT_W, F32), (D_GRP, F32), (D_GRP, F32)], [(D_MODEL, D_MODEL), (1, D_MODEL)])


def _log_sigmoid(x):
    return jnp.minimum(x, 0.0) - jnp.log(1.0 + jnp.exp(-jnp.abs(x)))


SCAN_BLK = 256


def _fox_prep(t_len, zf, kv, b_f):
    blk = min(SCAN_BLK, t_len)

    def body(zf_ref, kv_ref, b_ref, ka_ref, va_ref, carry_s):
        @pl.when(pl.program_id(0) == 0)
        def _():
            carry_s[...] = jnp.zeros(carry_s.shape, F32)

        tri = (lax.broadcasted_iota(jnp.int32, (blk, blk), 0) >= lax.broadcasted_iota(jnp.int32, (blk, blk), 1)).astype(F32)
        lf = _log_sigmoid(zf_ref[...] + b_ref[...])
        cs = jnp.dot(tri, lf, precision=lax.Precision.HIGHEST, preferred_element_type=F32) + carry_s[...]
        carry_s[...] = cs[blk - 1:blk, :]
        neg = -cs
        hi = neg.astype(BF).astype(F32)
        mid = (neg - hi).astype(BF).astype(F32)
        lo = ((neg - hi) - mid).astype(BF).astype(F32)
        kvv = kv_ref[...].astype(F32)
        lane = _lane((blk, QK_W))
        ka, va = [], []
        for a in range(N_HEADS):
            terms = jnp.where(lane == C_COL, hi[:, a:a + 1], jnp.where(lane == C_COL + 1, mid[:, a:a + 1], jnp.where(
                lane == C_COL + 2, lo[:, a:a + 1], jnp.where(lane == ONE_COL, HEAD_DIM ** 0.5, 0.0))))
            ka.append(jnp.where(lane < HEAD_DIM, _head_low(kvv[:, :D_GRP], a), terms))
            va.append(jnp.where(lane < HEAD_DIM, _head_low(kvv[:, D_GRP:], a), 0.0))
        ka_ref[...] = jnp.concatenate(ka, axis=1).astype(BF)
        va_ref[...] = jnp.concatenate(va, axis=1).astype(BF)

    row = lambda w: pl.BlockSpec((blk, w), lambda i: (i, 0))
    return pl.pallas_call(
        body, name="fox_prep", grid=(t_len // blk,),
        in_specs=[row(F_PAD), row(2 * D_GRP), pl.BlockSpec((1, F_PAD), lambda i: (0, 0))],
        out_specs=[row(ATT_W), row(ATT_W)],
        out_shape=[jax.ShapeDtypeStruct((t_len, ATT_W), BF)] * 2,
        scratch_shapes=[pltpu.VMEM((1, F_PAD), F32)],
        compiler_params=_cparams(("arbitrary",)),
    )(zf, kv, b_f)


def _fox_post(t_len, dqa, dka, zf, b_f):
    blk = min(SCAN_BLK, t_len)
    nb = t_len // blk

    def body(dq_ref, dk_ref, zf_ref, b_ref, dz_ref, db_ref, carry_s):
        @pl.when(pl.program_id(0) == 0)
        def _():
            carry_s[...] = jnp.zeros(carry_s.shape, F32)
            db_ref[...] = jnp.zeros(db_ref.shape, F32)

        tri = (lax.broadcasted_iota(jnp.int32, (blk, blk), 0) <= lax.broadcasted_iota(jnp.int32, (blk, blk), 1)).astype(F32)
        lane = _lane((blk, F_PAD))
        dc = jnp.zeros((blk, F_PAD), F32)
        for a in range(N_HEADS):
            head = slice(QK_W * a, QK_W * (a + 1))
            col = dq_ref[:, head][:, ONE_COL:ONE_COL + 1] - dk_ref[:, head][:, C_COL:C_COL + 1]
            dc = jnp.where(lane == a, col, dc)
        dlf = jnp.dot(tri, dc, precision=lax.Precision.HIGHEST, preferred_element_type=F32) + carry_s[...]
        carry_s[...] = dlf[0:1, :]
        dz = dlf * _sig(-(zf_ref[...] + b_ref[...]))
        dz_ref[...] = dz.astype(dz_ref.dtype)
        db_ref[...] += jnp.sum(dz, axis=0, keepdims=True)

    row = lambda w: pl.BlockSpec((blk, w), lambda i: (nb - 1 - i, 0))
    one = pl.BlockSpec((1, F_PAD), lambda i: (0, 0))
    return pl.pallas_call(
        body, name="fox_post", grid=(nb,),
        in_specs=[row(ATT_W), row(ATT_W), row(F_PAD), one], out_specs=[row(F_PAD), one],
        out_shape=(jax.ShapeDtypeStruct((t_len, F_PAD), BF), jax.ShapeDtypeStruct((1, F_PAD), F32)),
        scratch_shapes=[pltpu.VMEM((1, F_PAD), F32)],
        compiler_params=_cparams(("arbitrary",)),
    )(dqa, dka, zf, b_f)


def _tri_mask(tq, key_rows):
    r = lax.broadcasted_iota(jnp.int32, (tq, tq), 0)
    c = lax.broadcasted_iota(jnp.int32, (tq, tq), 1)
    return (r <= c) if key_rows else (r >= c)


LSE_COL = HEAD_DIM


def _fox_fwd(t_len, tq, qa, ka, va, gather=None):
    nq = t_len // tq
    plan, shards = gather if gather else (None, [])
    n = len(shards)

    def body(q_ref, k_ref, v_ref, *rest):
        ins, o_ref, outs, sems = rest[:n], rest[n], rest[n + 1:2 * n + 1], rest[2 * n + 1:]
        h, i = pl.program_id(0), pl.program_id(1)
        if plan:
            @pl.when((h == 0) & (i == 0))
            def _():
                plan.start(ins, outs, *sems)

        q = q_ref[...]

        def tile(j, carry, diagonal):
            m, l, acc = carry
            rows = pl.ds(pl.multiple_of(j * tq, tq), tq)
            s = _dot_nt(q, k_ref[rows, :])
            if diagonal:
                s = jnp.where(_tri_mask(tq, False), s, NEG)
            m_new = jnp.maximum(m, jnp.max(s, axis=-1, keepdims=True))
            alpha = jnp.exp(m - m_new)
            p = jnp.exp(s - m_new)
            l = alpha * l + jnp.sum(p, axis=-1, keepdims=True)
            acc = alpha * acc + _dot(p.astype(BF), v_ref[rows, :])
            return m_new, l, acc

        init = (jnp.full((tq, 1), NEG, F32), jnp.zeros((tq, 1), F32), jnp.zeros((tq, QK_W), F32))
        carry = lax.fori_loop(0, i, lambda j, c: tile(j, c, False), init)
        m, l, acc = tile(i, carry, True)
        o_ref[...] = jnp.where(_lane((tq, QK_W)) == LSE_COL, m + jnp.log(l), acc / l)
        if plan:
            @pl.when((h == N_HEADS - 1) & (i == nq - 1))
            def _():
                plan.finish(ins, outs, *sems)

    tile_spec = pl.BlockSpec((tq, QK_W), lambda h, i: (i, h))
    head_spec = pl.BlockSpec((t_len, QK_W), lambda h, i: (0, h))
    return pl.pallas_call(
        body, name="fox_fwd_gather" if plan else "fox_fwd", grid=(N_HEADS, nq),
        in_specs=[tile_spec, head_spec, head_spec] + [ANY] * n, out_specs=[tile_spec] + [ANY] * n,
        out_shape=[jax.ShapeDtypeStruct((t_len, ATT_W), F32)] + (plan.out_shape if plan else []),
        scratch_shapes=plan.scratch if plan else [],
        compiler_params=_cparams(("arbitrary", "arbitrary")),
    )(qa, ka, va, *shards)


def _fox_bwd(t_len, tq, qa, ka, va, do, o, comm=None):
    nq = t_len // tq
    plan, comm_in = comm if comm else (None, [])
    n = len(comm_in)

    def body(q_ref, k_ref, v_ref, do_ref, o_ref, *rest):
        c_ins, (dq_ref, dk_ref, dv_ref), c_outs = rest[:n], rest[n:n + 3], rest[n + 3:2 * n + 3]
        acc_s, dl_s, lse_s = rest[2 * n + 3:2 * n + 6]
        sems = rest[2 * n + 6:]
        hd, j = pl.program_id(0), pl.program_id(1)
        if plan:
            @pl.when((hd == 0) & (j == 0))
            def _():
                plan.start(c_ins, c_outs, *sems)

        @pl.when(j == 0)
        def _():
            acc_s[...] = jnp.zeros(acc_s.shape, F32)
            row_of = lambda sel, a: lax.dot_general(sel, a, (((1,), (1,)), ((), ())), precision=lax.Precision.HIGHEST,
                                                    preferred_element_type=F32)
            dl_s[...] = row_of(jnp.ones((8, QK_W), F32), do_ref[...] * o_ref[...])
            lse_s[...] = row_of(jnp.where(_lane((8, QK_W)) == LSE_COL, 1.0, 0.0), o_ref[...])

        kt, vt = k_ref[...], v_ref[...]

        def tile(i, carry, diagonal):
            dk, dv = carry
            rows = pl.ds(pl.multiple_of(i * tq, tq), tq)
            qt = q_ref[rows, :]
            dob = do_ref[rows, :].astype(BF)
            pt = jnp.exp(_dot_nt(kt, qt) - lse_s[0:1, rows])
            if diagonal:
                pt = jnp.where(_tri_mask(tq, True), pt, 0.0)
            dst = (pt * (_dot_nt(vt, dob) - dl_s[0:1, rows])).astype(BF)
            acc_s[rows, :] += _dot_tn(dst, kt)
            return dk + _dot(dst, qt), dv + _dot(pt.astype(BF), dob)

        carry = tile(j, (jnp.zeros((tq, QK_W), F32), jnp.zeros((tq, QK_W), F32)), True)
        dk, dv = lax.fori_loop(j + 1, nq, lambda i, c: tile(i, c, False), carry)
        dk_ref[...] = dk
        dv_ref[...] = dv

        @pl.when(j == nq - 1)
        def _():
            dq_ref[...] = acc_s[...] * (HEAD_DIM ** -0.5)

        if plan:
            @pl.when((hd == N_HEADS - 1) & (j == nq - 1))
            def _():
                plan.finish(c_ins, c_outs, *sems)

    head_spec = pl.BlockSpec((t_len, QK_W), lambda h, j: (0, h))
    tile_spec = pl.BlockSpec((tq, QK_W), lambda h, j: (j, h))
    return pl.pallas_call(
        body, name="fox_bwd_exchange" if plan else "fox_bwd", grid=(N_HEADS, nq),
        in_specs=[head_spec, tile_spec, tile_spec, head_spec, head_spec] + [ANY] * n,
        out_specs=[head_spec, tile_spec, tile_spec] + [ANY] * n,
        out_shape=[jax.ShapeDtypeStruct((t_len, ATT_W), F32)] * 3 + (plan.out_shape if plan else []),
        scratch_shapes=[pltpu.VMEM((t_len, QK_W), F32), pltpu.VMEM((8, t_len), F32), pltpu.VMEM((8, t_len), F32)]
        + (plan.scratch if plan else []),
        compiler_params=_cparams(("arbitrary", "arbitrary")),
    )(qa, ka, va, do, o, *comm_in)


def _mlp_up_fwd(t_len, tm, h, g_pre, w_up):
    def fn(i, n, rows, prevs, nexts, fulls):
        (hv,), (g, w) = rows, fulls
        return [_dot(_rms_fwd(hv, g[...]).astype(BF), w[...])], []

    return _rows_call("mlp_up_fwd", fn, t_len, tm, [h], [g_pre, w_up], [(D_FF, F32)])[0]


def _mlp_down_fwd(t_len, tm, up, h, w_down, g_post):
    def fn(i, n, rows, prevs, nexts, fulls):
        (uv, hv), (w, g) = rows, fulls
        a = jnp.square(jnp.maximum(uv, 0.0))
        ff = _dot(a.astype(BF), w[...])
        return [ff, hv + _rms_fwd(ff, g[...])], []

    return _rows_call("mlp_down_fwd", fn, t_len, tm, [up, h], [w_down, g_post], [(D_MODEL, F32), (D_MODEL, F32)])


def _mlp_bwd_a(t_len, tm, dh, ff, up, w_down, g_post):
    def fn(i, n, rows, prevs, nexts, fulls):
        (dhv, fv, uv), (w, g) = rows, fulls
        dff, dg = _rms_bwd(fv, g[...], dhv)
        dfb = dff.astype(BF)
        dup = _dot_nt(dfb, w[...]) * (2.0 * jnp.maximum(uv, 0.0))
        return [dfb, dup], [dg]

    return _rows_call("mlp_bwd_a", fn, t_len, tm, [dh, ff, up], [w_down, g_post], [(D_MODEL, BF), (D_FF, BF)], [(1, D_MODEL)])


def _mlp_bwd_b(t_len, tm, dh, h, dup, w_up, g_pre):
    def fn(i, n, rows, prevs, nexts, fulls):
        (dhv, hv, duv), (w, g) = rows, fulls
        dhn = _dot_nt(duv, w[...])
        dx, dg = _rms_bwd(hv, g[...], dhn)
        return [dhv + dx, _rms_fwd(hv, g[...])], [dg]

    return _rows_call("mlp_bwd_b", fn, t_len, tm, [dh, h, dup], [w_up, g_pre], [(D_MODEL, F32), (D_MODEL, BF)], [(1, D_MODEL)])


def _ple_fwd(t_len, tm, h, p_i, g_pre, w_gate, w_proj, g_post):
    def fn(i, n, rows, prevs, nexts, fulls):
        (hv, pv), (g, wg, wp, gp) = rows, fulls
        gpre = _dot(_rms_fwd(hv, g[...]).astype(BF), wg[...])
        pe = _dot(pv.astype(BF), wp[...])
        return [gpre, pe, hv + _rms_fwd(pe * _sig(gpre), gp[...])], []

    return _rows_call("ple_fwd", fn, t_len, tm, [h, p_i], [g_pre, w_gate, w_proj, g_post], [(D_MODEL, F32)] * 3)


def _ple_bwd(t_len, tm, dh, h, gpre, pe, p_i, g_pre, w_gate, g_post, comm=None):
    def fn(i, n, rows, prevs, nexts, fulls):
        (dhv, hv, gv, pev, pv), (g, wg, gp) = rows, fulls
        sg = _sig(gv)
        de, d_gp = _rms_bwd(pev * sg, gp[...], dhv)
        dpe = (de * sg).astype(BF)
        dgate = (de * pev * sg * (1.0 - sg)).astype(BF)
        d_wp = _dot_tn(pv.astype(BF), dpe)
        hn = _rms_fwd(hv, g[...])
        d_wg = _dot_tn(hn.astype(BF), dgate)
        dx, d_g = _rms_bwd(hv, g[...], _dot_nt(dgate, wg[...]))
        return [dhv + dx], [d_wg, d_wp, d_g, d_gp]

    return _rows_call("ple_bwd_exchange" if comm else "ple_bwd", fn, t_len, tm, [dh, h, gpre, pe, p_i], [g_pre, w_gate, g_post],
                      [(D_MODEL, F32)], [(D_MODEL, D_MODEL), (D_PLE, D_MODEL), (1, D_MODEL), (1, D_MODEL)], comm=comm)


def _loss_call(t_len, tm, h, target):
    def fn(i, n, rows, prevs, nexts, fulls):
        hv, tv = rows
        err = hv - tv
        part = 0.5 * jnp.sum(jnp.mean(err * err, axis=-1, keepdims=True), axis=0, keepdims=True)
        return [err * (1.0 / D_MODEL)], [jnp.broadcast_to(part, (8, 128))]

    return _rows_call("loss", fn, t_len, tm, [h, target], [], [(D_MODEL, F32)], [(8, 128)])


def _adamw_call(name, w, g, m, v):
    n_l, n_r, n_c = w.shape
    tr = 256 if n_r % 256 == 0 else n_r

    def body(w_ref, g_ref, m_ref, v_ref, d_ref, nm_ref, nv_ref):
        gv = g_ref[...]
        nm = ADAM_B1 * m_ref[...] + (1.0 - ADAM_B1) * gv
        nv = ADAM_B2 * v_ref[...] + (1.0 - ADAM_B2) * jnp.square(gv)
        m_hat = nm / (1.0 - ADAM_B1 ** ADAM_STEP)
        v_hat = nv / (1.0 - ADAM_B2 ** ADAM_STEP)
        d_ref[...] = -ADAM_LR * (m_hat / (jnp.sqrt(v_hat) + ADAM_EPS) + ADAM_WD * w_ref[...])
        nm_ref[...] = nm
        nv_ref[...] = nv

    spec = pl.BlockSpec((1, tr, n_c), lambda l, r: (l, r, 0))
    return pl.pallas_call(
        body, name=name, grid=(n_l, n_r // tr), in_specs=[spec] * 4, out_specs=[spec] * 3,
        out_shape=[jax.ShapeDtypeStruct(w.shape, F32)] * 3,
        compiler_params=_cparams(("parallel", "parallel")),
    )(w, g, m, v)


ANY = pl.BlockSpec(memory_space=pl.ANY)


def _place():
    x, y, c = lax.axis_index("x"), lax.axis_index("y"), lax.axis_index("c")
    chips = [(1 - x, y), (x, 1 - y), (1 - x, 1 - y)]
    return x, y, c, 2 * x + y, chips


def _remote(src, dst, send_sem, recv_sem, dev):
    return pltpu.make_async_remote_copy(src_ref=src, dst_ref=dst, send_sem=send_sem, recv_sem=recv_sem,
                                        device_id=dev, device_id_type=MESH_T)


class _GatherPlan:
    def __init__(self, shards, lo, nl):
        self.shapes = [s.shape for s in shards]
        self.lo, self.nl, self.n = lo, nl, len(shards)
        self.split = [s.shape[1] % 32 == 0 for s in shards]
        self.out_shape = [jax.ShapeDtypeStruct((3, nl, *s.shape[1:]), s.dtype) for s in shards]
        self.scratch = [pltpu.SemaphoreType.DMA((6 * self.n,)), pltpu.SemaphoreType.DMA((6 * self.n,))]

    def _views(self, ins, outs, a, c):
        lay, all_l = pl.ds(self.lo, self.nl), pl.ds(0, self.nl)
        if not self.split[a]:
            return ins[a].at[lay], (lambda j: outs[a].at[j]), None
        hr = self.shapes[a][1] // 2
        mine, other = pl.ds(c * hr, hr), pl.ds((1 - c) * hr, hr)
        return ins[a].at[lay, mine], (lambda j: outs[a].at[j, all_l, mine]), (lambda j: outs[a].at[j, all_l, other])

    def _ici(self, ins, outs, ssem, rsem):
        x, y, c, q, chips = _place()
        cps = []
        for a in range(self.n):
            src, land, _ = self._views(ins, outs, a, c)
            for j, chip in enumerate(chips):
                cps.append(_remote(src, land(j), ssem.at[6 * a + j], rsem.at[6 * a + j], (*chip, c)))
        return cps

    def start(self, ins, outs, ssem, rsem):
        for cp in self._ici(ins, outs, ssem, rsem):
            cp.start()

    def finish(self, ins, outs, ssem, rsem):
        x, y, c, q, chips = _place()
        sib = (x, y, 1 - c)
        cps = self._ici(ins, outs, ssem, rsem)
        for a in range(self.n):
            _, land, _ = self._views(ins, outs, a, c)
            for j in range(3):
                cps[3 * a + j].wait_recv()
                if self.split[a]:
                    cps.append(_remote(land(j), land(j), ssem.at[6 * a + 3 + j], rsem.at[6 * a + 3 + j], sib))
                    cps[-1].start()
        for a in range(self.n):
            _, _, other = self._views(ins, outs, a, c)
            for j in range(3):
                if self.split[a]:
                    _remote(other(j), other(j), ssem.at[6 * a + 3 + j], rsem.at[6 * a + 3 + j], sib).wait_recv()
        for cp in cps:
            cp.wait_send()


def _allgather_weights(shards, lo, nl):
    plan = _GatherPlan(shards, lo, nl)
    n = plan.n

    def body(*refs):
        ins, outs, sems = refs[:n], refs[n:2 * n], refs[2 * n:]
        plan.start(ins, outs, *sems)
        plan.finish(ins, outs, *sems)

    return pl.pallas_call(body, name="allgather_weights", in_specs=[ANY] * n, out_specs=[ANY] * n,
                          out_shape=plan.out_shape, scratch_shapes=plan.scratch)(*shards)


def _by_chip(own, others, chip):
    by_mask = jnp.stack([own, others[1], others[0], others[2]])
    return jnp.stack([lax.dynamic_index_in_dim(by_mask, jnp.bitwise_xor(chip, r), 0, keepdims=False) for r in range(N_CHIPS)])


class _PairPlan:
    def __init__(self, grads):
        self.n = len(grads)
        self.half = [g.shape[2] // 2 for g in grads]
        self.layers = [g.shape[1] for g in grads]
        self.out_shape = [jax.ShapeDtypeStruct((N_CHIPS, g.shape[1], g.shape[2] // 2, g.shape[3]), g.dtype) for g in grads]
        self.scratch = [pltpu.SemaphoreType.DMA((self.n,)), pltpu.SemaphoreType.DMA((self.n,))]

    def _copies(self, ins, outs, ssem, rsem):
        x, y, c, q, chips = _place()
        return [_remote(ins[a].at[pl.ds(0, N_CHIPS), pl.ds(0, self.layers[a]), pl.ds((1 - c) * self.half[a], self.half[a])],
                        outs[a], ssem.at[a], rsem.at[a], (x, y, 1 - c)) for a in range(self.n)]

    def start(self, ins, outs, ssem, rsem):
        for cp in self._copies(ins, outs, ssem, rsem):
            cp.start()

    def finish(self, ins, outs, ssem, rsem):
        for cp in self._copies(ins, outs, ssem, rsem):
            cp.wait()


class _ChipPlan:
    def __init__(self, parts):
        self.n = len(parts)
        self.out_shape = [jax.ShapeDtypeStruct((3, *s.shape[1:]), s.dtype) for s in parts]
        self.scratch = [pltpu.SemaphoreType.DMA((3 * self.n,)), pltpu.SemaphoreType.DMA((3 * self.n,))]

    def _copies(self, ins, outs, ssem, rsem):
        x, y, c, q, chips = _place()
        return [_remote(ins[a].at[2 * cx + cy], outs[a].at[j], ssem.at[3 * a + j], rsem.at[3 * a + j], (cx, cy, c))
                for a in range(self.n) for j, (cx, cy) in enumerate(chips)]

    def start(self, ins, outs, ssem, rsem):
        for cp in self._copies(ins, outs, ssem, rsem):
            cp.start()

    def finish(self, ins, outs, ssem, rsem):
        for cp in self._copies(ins, outs, ssem, rsem):
            cp.wait()


def _exchange_call(name, plan, arrays):
    n = plan.n

    def body(*refs):
        ins, outs, sems = refs[:n], refs[n:2 * n], refs[2 * n:]
        plan.start(ins, outs, *sems)
        plan.finish(ins, outs, *sems)

    return pl.pallas_call(body, name=name, in_specs=[ANY] * n, out_specs=[ANY] * n,
                          out_shape=plan.out_shape, scratch_shapes=plan.scratch)(*arrays)


def _pair_sum(name, g, peer, c_arr):
    _, n_l, half, n_c = peer.shape
    tr = 256 if half % 256 == 0 else half
    nb = half // tr

    def body(c_ref, g_ref, p_ref, o_ref):
        o_ref[...] = (g_ref[...] + p_ref[...]).astype(o_ref.dtype)

    blk = (1, 1, tr, n_c)
    return pl.pallas_call(
        body, name=name,
        grid_spec=pltpu.PrefetchScalarGridSpec(
            num_scalar_prefetch=1, grid=(N_CHIPS, n_l, nb),
            in_specs=[pl.BlockSpec(blk, lambda qi, li, ri, c_ref: (qi, li, c_ref[0] * nb + ri, 0)),
                      pl.BlockSpec(blk, lambda qi, li, ri, c_ref: (qi, li, ri, 0))],
            out_specs=pl.BlockSpec(blk, lambda qi, li, ri, c_ref: (qi, li, ri, 0))),
        out_shape=jax.ShapeDtypeStruct(peer.shape, BF),
        compiler_params=_cparams(("parallel", "parallel", "parallel")),
    )(c_arr, g, peer)


def _chip_sum(name, own, r, chip_arr):
    _, n_l, n_r, n_c = r.shape
    tr = 256 if n_r % 256 == 0 else n_r

    def body(q_ref, r0, r1, r2, r3, o_ref):
        o_ref[...] = ((r0[0].astype(F32) + r1[0].astype(F32)) + r2[0].astype(F32)) + r3[0].astype(F32)

    blk = (1, 1, tr, n_c)
    return pl.pallas_call(
        body, name=name,
        grid_spec=pltpu.PrefetchScalarGridSpec(
            num_scalar_prefetch=1, grid=(n_l, n_r // tr),
            in_specs=[pl.BlockSpec(blk, lambda li, ri, q_ref: (q_ref[0], li, ri, 0))]
            + [pl.BlockSpec(blk, lambda li, ri, q_ref, s=s: (s, li, ri, 0)) for s in range(3)],
            out_specs=pl.BlockSpec((1, tr, n_c), lambda li, ri, q_ref: (li, ri, 0))),
        out_shape=jax.ShapeDtypeStruct((n_l, n_r, n_c), F32),
        compiler_params=_cparams(("parallel", "parallel")),
    )(chip_arr, own, r, r, r)


def _pair_share(halves):
    n = len(halves)

    def body(*refs):
        ins, outs = refs[:n], refs[n:2 * n]
        send_sems, recv_sems = refs[2 * n:]
        x, y, c, q, chips = _place()
        cps = [_remote(ins[a], outs[a], send_sems.at[a], recv_sems.at[a], (x, y, 1 - c)) for a in range(n)]
        for cp in cps:
            cp.start()
        for cp in cps:
            cp.wait()

    return pl.pallas_call(
        body, name="grad_pair_share", in_specs=[ANY] * n, out_specs=[ANY] * n,
        out_shape=[jax.ShapeDtypeStruct(h.shape, h.dtype) for h in halves],
        scratch_shapes=[pltpu.SemaphoreType.DMA((n,)), pltpu.SemaphoreType.DMA((n,))],
    )(*halves)


def _allreduce_small(v):
    n_r = v.shape[0]

    def body(v_ref, o_ref, slots, send_sems, recv_sems):
        x, y, c = lax.axis_index("x"), lax.axis_index("y"), lax.axis_index("c")
        me = 4 * x + 2 * y + c
        slots[me] = v_ref[...]
        cps = []
        for r in range(1, 8):
            px = 1 - x if r & 4 else x
            py = 1 - y if r & 2 else y
            pc = 1 - c if r & 1 else c
            cps.append(_remote(v_ref, slots.at[me], send_sems.at[r - 1], recv_sems.at[r - 1], (px, py, pc)))
            cps[-1].start()
        for cp in cps:
            cp.wait()
        tot = slots[0]
        for d in range(1, 8):
            tot = tot + slots[d]
        o_ref[...] = tot

    return pl.pallas_call(
        body, name="allreduce_small",
        in_specs=[pl.BlockSpec(memory_space=pltpu.VMEM)], out_specs=pl.BlockSpec(memory_space=pltpu.VMEM),
        out_shape=jax.ShapeDtypeStruct(v.shape, F32),
        scratch_shapes=[pltpu.VMEM((8, n_r, 128), F32), pltpu.SemaphoreType.DMA((7,)), pltpu.SemaphoreType.DMA((7,))],
        compiler_params=pltpu.CompilerParams(vmem_limit_bytes=VMEM_LIMIT_BYTES),
    )(v)


def _cols_full(g, l):
    s = g[:, l]
    return s.transpose(1, 0, 2).reshape(s.shape[1], -1)


def _rows_full(g, l):
    s = g[:, l]
    return s.reshape(-1, s.shape[-1])


def _cols_split(full):
    r = full.shape[0]
    return full.reshape(r, N_CHIPS, -1).transpose(1, 0, 2)


def _rows_split(full):
    return full.reshape(N_CHIPS, -1, full.shape[-1])


def _pack(parts):
    flat = []
    for a in parts:
        f = a.reshape(-1).astype(F32)
        flat.append(jnp.pad(f, (0, (-f.shape[0]) % 1024)))
    return jnp.concatenate(flat).reshape(-1, 128)


def _unpack(buf, shapes):
    flat = buf.reshape(-1)
    out, off = [], 0
    for s in shapes:
        size = 1
        for d in s:
            size *= d
        out.append(flat[off:off + size].reshape(s))
        off += size + (-size) % 1024
    return out


def kernel(x, p, g_mix_pre, w_in, b_forget, w_conf_dw, conf_ln_g, conf_ln_b, w_conf_pw, w_sc, w_pool, pool_scale, w_out, g_mix_post, g_mlp_pre, w_up, w_down, g_mlp_post, g_ple_pre, w_ple_gate, w_ple_proj, g_ple_post, loss_target, m_g_mix_pre, m_w_in, m_b_forget, m_w_conf_dw, m_conf_ln_g, m_conf_ln_b, m_w_conf_pw, m_w_sc, m_w_pool, m_pool_scale, m_w_out, m_g_mix_post, m_g_mlp_pre, m_w_up, m_w_down, m_g_mlp_post, m_g_ple_pre, m_w_ple_gate, m_w_ple_proj, m_g_ple_post, v_g_mix_pre, v_w_in, v_b_forget, v_w_conf_dw, v_conf_ln_g, v_conf_ln_b, v_w_conf_pw, v_w_sc, v_w_pool, v_pool_scale, v_w_out, v_g_mix_post, v_g_mlp_pre, v_w_up, v_w_down, v_g_mlp_post, v_g_ple_pre, v_w_ple_gate, v_w_ple_proj, v_g_ple_post):
    names = ['g_mix_pre', 'w_in', 'b_forget', 'w_conf_dw', 'conf_ln_g', 'conf_ln_b', 'w_conf_pw', 'w_sc', 'w_pool', 'pool_scale',
             'w_out', 'g_mix_post', 'g_mlp_pre', 'w_up', 'w_down', 'g_mlp_post', 'g_ple_pre', 'w_ple_gate', 'w_ple_proj', 'g_ple_post']
    env = locals()
    wts = {k: env[k] for k in names}
    mom = {k: env["m_" + k] for k in names}
    var = {k: env["v_" + k] for k in names}

    t_len = x.shape[1]
    tm = min(256, t_len)
    tb = min(512, t_len)
    tq = min(512, max(t_len // 2, 128))
    tt = min(1024, t_len)
    chip = 2 * lax.axis_index("x") + lax.axis_index("y")
    core = lax.axis_index("c")

    big = ['w_in', 'w_conf_pw', 'w_out', 'w_up', 'w_down', 'w_ple_gate', 'w_ple_proj']
    tiny = ['w_conf_dw', 'w_sc']
    own_shards = [wts[k].astype(BF) for k in big] + [wts[k] for k in tiny]

    def chip_order(others, lo, nl):
        return {k: _by_chip(own[lo:lo + nl], oth, chip) for k, own, oth in zip(big + tiny, own_shards, others)}

    def layer_weights(gat, li, l):
        w_full = _cols_full(gat['w_in'], li)
        w_a = jnp.concatenate([w_full[:, :F_OFF], w_full[:, F_OFF + N_HEADS:], w_full[:, F_OFF:F_OFF + N_HEADS],
                               jnp.zeros((D_MODEL, Z_W - D_IN), BF)], axis=1)
        w_bd = jnp.zeros((D_GRP, D_GRP), F32)
        for g in range(4):
            w_bd = lax.dynamic_update_slice(w_bd, wts['w_pool'][l, g], (64 * g, 64 * g))
        row = lambda a: a[l][None, :]
        return dict(
            w_a=w_a, w_dw=jnp.pad(_cols_full(gat['w_conf_dw'], li), ((0, 1), (0, 0))), w_pw=_rows_full(gat['w_conf_pw'], li),
            w_sc=jnp.pad(_cols_full(gat['w_sc'], li), ((0, 5), (0, 0))), w_bd=w_bd.astype(BF),
            w_out=_rows_full(gat['w_out'], li), w_up=_cols_full(gat['w_up'], li), w_down=_rows_full(gat['w_down'], li),
            w_gate=_rows_full(gat['w_ple_gate'], li), w_proj=_cols_full(gat['w_ple_proj'], li),
            b_f=jnp.pad(wts['b_forget'][l], (0, F_PAD - N_HEADS))[None, :],
            ln_g=row(wts['conf_ln_g']), ln_b=row(wts['conf_ln_b']), pool_scale=row(wts['pool_scale']),
            g_mix_pre=row(wts['g_mix_pre']), g_mix_post=row(wts['g_mix_post']), g_mlp_pre=row(wts['g_mlp_pre']),
            g_mlp_post=row(wts['g_mlp_post']), g_ple_pre=row(wts['g_ple_pre']), g_ple_post=row(wts['g_ple_post']))

    lw = [layer_weights(chip_order(_allgather_weights(own_shards, 0, 1), 0, 1), 0, 0)]

    h = x[0]
    saved = []
    for l in range(DEPTH):
        w = lw[l]
        s = dict(h0=h)
        s['zc'], s['qa'], kv, s['zs'], s['zp'], s['zf'] = _mix_in_fwd(t_len, tb, h, w['g_mix_pre'], w['w_a'])
        s['cv'], y_conf = _conf_fwd(t_len, tm, s['zc'], w['w_dw'], w['ln_g'], w['ln_b'], w['w_pw'])
        s['ka'], s['va'] = _fox_prep(t_len, s['zf'], kv, w['b_f'])
        if l == 0:
            s['o'], *others = _fox_fwd(t_len, tq, s['qa'], s['ka'], s['va'],
                                       gather=(_GatherPlan(own_shards, 1, DEPTH - 1), own_shards))
            rest = chip_order(others, 1, DEPTH - 1)
            lw += [layer_weights(rest, li, li + 1) for li in range(DEPTH - 1)]
        else:
            (s['o'],) = _fox_fwd(t_len, tq, s['qa'], s['ka'], s['va'])
        (y_sc,) = _sconv_fwd(t_len, tm, s['zs'], w['w_sc'])
        (y_pool,) = _pool_fwd(t_len, tm, s['zp'], w['w_bd'], w['pool_scale'])
        s['ys'] = [y_conf, s['o'], y_sc, y_pool]
        s['mix'], h = _mix_out_fwd(t_len, tb, s['ys'], h, w['w_out'], w['g_mix_post'])
        s['h1'] = h
        s['up'] = _mlp_up_fwd(t_len, tb, h, w['g_mlp_pre'], w['w_up'])
        s['ff'], h = _mlp_down_fwd(t_len, tb, s['up'], h, w['w_down'], w['g_mlp_post'])
        s['h2'] = h
        s['gpre'], s['pe'], h = _ple_fwd(t_len, tb, h, p[l, 0], w['g_ple_pre'], w['w_gate'], w['w_proj'], w['g_ple_post'])
        saved.append(s)

    dh, loss_part = _loss_call(t_len, tm, h, loss_target[0])

    grads = [None] * DEPTH
    split = dict(w_in=_cols_split, w_conf_pw=_rows_split, w_out=_rows_split, w_up=_cols_split, w_down=_rows_split,
                 w_ple_gate=_rows_split, w_ple_proj=_cols_split)
    c_arr = core.astype(jnp.int32).reshape(1)
    chip_arr = chip.astype(jnp.int32).reshape(1)

    def contrib_of(layers):
        return [jnp.stack([split[k](grads[l][k]) for l in layers], axis=1) for k in big]

    def pair_sums(tag, contrib, peer):
        return [_pair_sum(f"grad_pair_sum_{tag}_{k}", a, b, c_arr) for k, a, b in zip(big, contrib, peer)]

    def finish_reduce(tag, parts, landed):
        halves = [_chip_sum(f"grad_chip_sum_{tag}_{k}", own, r, chip_arr) for k, own, r in zip(big, parts, landed)]
        full = []
        for mine, theirs in zip(halves, _pair_share(halves)):
            both = jnp.stack([mine, theirs])
            full.append(jnp.concatenate([lax.dynamic_index_in_dim(both, core, 0, keepdims=False),
                                         lax.dynamic_index_in_dim(both, 1 - core, 0, keepdims=False)], axis=1))
        return full

    for l in reversed(range(DEPTH)):
        w, s, g = lw[l], saved[l], {}
        contrib13 = contrib_of(range(1, DEPTH)) if l == 0 else []
        dh, g['w_ple_gate'], g['w_ple_proj'], g['g_ple_pre'], g['g_ple_post'], *peer13 = _ple_bwd(
            t_len, tb, dh, s['h2'], s['gpre'], s['pe'], p[l, 0], w['g_ple_pre'], w['w_gate'], w['g_ple_post'],
            comm=(_PairPlan(contrib13), contrib13) if l == 0 else None)
        parts13 = pair_sums("l13", contrib13, peer13) if l == 0 else []
        dff, dup, g['g_mlp_post'] = _mlp_bwd_a(t_len, tm, dh, s['ff'], s['up'], w['w_down'], w['g_mlp_post'])
        dh, hn, g['g_mlp_pre'] = _mlp_bwd_b(t_len, tb, dh, s['h1'], dup, w['w_up'], w['g_mlp_pre'])
        g['w_up'] = _mm_tn("mlp_dw_up", hn, dup, D_MODEL, 1024, tt)
        g['w_down'] = _mm_tn("mlp_dw_down", s['up'], dff, 1024, D_MODEL, tt, pro=lambda u: jnp.square(jnp.maximum(u, 0.0)))
        dy_conf, dy_att, dy_sc, dy_pool, g['w_out'], g['g_mix_post'] = _mix_out_bwd(t_len, tb, dh, s['mix'], s['ys'], w['w_out'], w['g_mix_post'])
        dzc, g['w_conf_dw'], g['conf_ln_g'], g['conf_ln_b'], g['w_conf_pw'] = _conf_bwd(
            t_len, tm, s['zc'], s['cv'], dy_conf, w['w_dw'], w['ln_g'], w['ln_b'], w['w_pw'])
        dzs, g['w_sc'] = _sconv_bwd(t_len, tm, s['zs'], dy_sc, w['w_sc'])
        dzp, d_wbd, g['pool_scale'] = _pool_bwd(t_len, tm, s['zp'], dy_pool, w['w_bd'], w['pool_scale'])
        g['w_pool'] = jnp.stack([d_wbd[64 * a:64 * (a + 1), 64 * a:64 * (a + 1)] for a in range(4)])
        dqa, dka, dva, *landed13 = _fox_bwd(t_len, tq, s['qa'], s['ka'], s['va'], dy_att, s['o'],
                                            comm=(_ChipPlan(parts13), parts13) if l == 0 else None)
        dzf, d_bf = _fox_post(t_len, dqa, dka, s['zf'], w['b_f'])
        g['b_forget'] = d_bf[0, :N_HEADS]
        dh, xn, dz, g['g_mix_pre'] = _mix_in_bwd(t_len, tb, dh, s['h0'], dzc, dqa, dka, dva, dzs, dzp, dzf, w['g_mix_pre'], w['w_a'])
        d_wa = _mm_tn("mix_dw_in", xn, dz, D_MODEL, Z_W, tt)
        g['w_in'] = jnp.concatenate([d_wa[:, :F_OFF], d_wa[:, Z_F:Z_F + N_HEADS], d_wa[:, F_OFF:Z_F]], axis=1)
        g['w_conf_dw'] = g['w_conf_dw'][:CONF_K]
        g['w_sc'] = g['w_sc'][:SC_K]
        grads[l] = g
    grad_x = dh[None]

    contrib0 = contrib_of([0])
    parts0 = pair_sums("l0", contrib0, _exchange_call("grad_pair_exchange", _PairPlan(contrib0), contrib0))
    first = finish_reduce("l0", parts0, _exchange_call("grad_chip_exchange", _ChipPlan(parts0), parts0))
    later = finish_reduce("l13", parts13, landed13)
    reduced = {k: jnp.concatenate([a, b], axis=0) for k, a, b in zip(big, first, later)}

    small = [k for k in names if k not in big]
    small_shapes = [(DEPTH, *grads[0][k].shape) for k in small]
    packed = _pack([jnp.stack([grads[l][k] for l in range(DEPTH)]) for k in small] + [loss_part])
    summed = _allreduce_small(packed)
    small_sum = _unpack(summed, small_shapes + [(8, 128)])
    loss = small_sum[-1][0, 0]
    for k, a in zip(small, small_sum[:-1]):
        if k in tiny:
            a = lax.dynamic_slice_in_dim(a, chip * 64, 64, axis=2)
        reduced[k] = a.reshape(wts[k].shape)

    delta_w, new_m, new_v = {}, {}, {}
    for k in names:
        shp = wts[k].shape
        as3 = (lambda a: a.reshape(shp[0], -1, shp[-1])) if len(shp) > 2 else (lambda a: a.reshape(1, shp[0], shp[1]))
        d, nm, nv = _adamw_call("adamw_" + k, as3(wts[k]), as3(reduced[k]), as3(mom[k]), as3(var[k]))
        delta_w[k], new_m[k], new_v[k] = d.reshape(shp), nm.reshape(shp), nv.reshape(shp)

    return (loss, grad_x, *[reduced[k] for k in names], *[delta_w[k] for k in names],
            *[new_m[k] for k in names], *[new_v[k] for k in names])
```

```python
import functools

import jax
import jax.numpy as jnp
from jax import lax
from jax.experimental import pallas as pl
from jax.experimental.pallas import tpu as pltpu

F32 = jnp.float32
BF = jnp.bfloat16

DEPTH = 4
D_MODEL = 1024
D_GRP = 256
HEAD_DIM = 64
N_HEADS = 4
CONF_K = 31
SC_K = 3
D_FF = 4096
D_PLE = 256
EPS = 1e-6
N_CHIPS = 4
Z_CONF, Z_QKV, Z_SC, Z_POOL, Z_F = 0, 512, 1280, 2048, 2304
Z_W = 2432
F_PAD = 128
D_IN = 2308
F_OFF = 1280

ADAM_LR, ADAM_B1, ADAM_B2, ADAM_EPS, ADAM_WD, ADAM_STEP = 0.001, 0.9, 0.999, 1e-08, 0.01, 10

VMEM_LIMIT_BYTES = 56 * 1024 * 1024
HALO = 32
NEG = -1e30
MESH_T = pl.DeviceIdType.MESH


def _cparams(sem=None):
    return pltpu.CompilerParams(dimension_semantics=sem, vmem_limit_bytes=VMEM_LIMIT_BYTES)


def _dot(a, b):
    return jnp.dot(a, b, preferred_element_type=F32)


def _dot_nt(a, b):
    return lax.dot_general(a, b, (((1,), (1,)), ((), ())), preferred_element_type=F32)


def _dot_tn(a, b):
    return lax.dot_general(a, b, (((0,), (0,)), ((), ())), preferred_element_type=F32)


def _sig(x):
    return jax.nn.sigmoid(x)


def _rms_fwd(x, g):
    r = lax.rsqrt(jnp.mean(x * x, axis=-1, keepdims=True) + EPS)
    return x * r * g


def _rms_bwd(x, g, dy):
    r = lax.rsqrt(jnp.mean(x * x, axis=-1, keepdims=True) + EPS)
    xh = x * r
    dg = jnp.sum(dy * xh, axis=0, keepdims=True)
    dxh = dy * g
    dx = r * (dxh - xh * jnp.mean(dxh * xh, axis=-1, keepdims=True))
    return dx, dg


def _back(ext, d):
    return ext if d == 0 else pltpu.roll(ext, d, 0)


def _ahead(ext, d):
    return ext if d == 0 else pltpu.roll(ext, ext.shape[0] - d, 0)


def _rows_call(name, fn, t_len, tm, rows, fulls, out_rows, out_accs=(), prevs=(), nexts=(), comm=None):
    plan, comm_in = comm if comm else (None, [])
    n_comm = len(comm_in)
    n = t_len // tm
    hb = tm // HALO
    nhb = t_len // HALO
    n_rows, n_prev, n_next, n_full = len(rows), len(prevs), len(nexts), len(fulls)
    in_specs = [pl.BlockSpec((tm, a.shape[1]), lambda i: (i, 0)) for a in rows]
    in_specs += [pl.BlockSpec((HALO, a.shape[1]), lambda i: (jnp.maximum(i * hb - 1, 0), 0)) for a in prevs]
    in_specs += [pl.BlockSpec((HALO, a.shape[1]), lambda i: (jnp.minimum((i + 1) * hb, nhb - 1), 0)) for a in nexts]
    in_specs += [pl.BlockSpec(a.shape, lambda i, nd=a.ndim: (0,) * nd, pipeline_mode=pl.Buffered(1)) for a in fulls]
    out_shape = [jax.ShapeDtypeStruct((t_len, c), dt) for c, dt in out_rows]
    out_shape += [jax.ShapeDtypeStruct(s, F32) for s in out_accs]
    out_specs = [pl.BlockSpec((tm, c), lambda i: (i, 0)) for c, _ in out_rows]
    out_specs += [pl.BlockSpec(s, lambda i, nd=len(s): (0,) * nd) for s in out_accs]
    n_in = n_rows + n_prev + n_next + n_full
    n_ro = len(out_rows)

    n_out = n_ro + len(out_accs)

    def body(*refs):
        i = pl.program_id(0)
        ins, outs = refs[:n_in], refs[n_in + n_comm:n_in + n_comm + n_out]
        c_ins, c_outs, sems = refs[n_in:n_in + n_comm], refs[n_in + n_comm + n_out:n_in + 2 * n_comm + n_out], refs[n_in + 2 * n_comm + n_out:]
        if plan:
            @pl.when(i == 0)
            def _():
                plan.start(c_ins, c_outs, *sems)

        rv = [r[...] for r in ins[:n_rows]]
        pv = [r[...] for r in ins[n_rows:n_rows + n_prev]]
        nv = [r[...] for r in ins[n_rows + n_prev:n_rows + n_prev + n_next]]
        fv = list(ins[n_rows + n_prev + n_next:])
        ro, ao = fn(i, n, rv, pv, nv, fv)
        for r, v in zip(outs[:n_ro], ro):
            r[...] = v.astype(r.dtype)
        if out_accs:
            acc = outs[n_ro:]

            @pl.when(i == 0)
            def _():
                for r in acc:
                    r[...] = jnp.zeros(r.shape, r.dtype)

            for r, v in zip(acc, ao):
                r[...] += v
        if plan:
            @pl.when(i == n - 1)
            def _():
                plan.finish(c_ins, c_outs, *sems)

    res = pl.pallas_call(
        body, name=name, grid=(n,), in_specs=in_specs + [ANY] * n_comm, out_specs=out_specs + [ANY] * n_comm,
        out_shape=out_shape + (plan.out_shape if plan else []), scratch_shapes=plan.scratch if plan else [],
        compiler_params=_cparams(("arbitrary",)),
    )(*rows, *prevs, *nexts, *fulls, *comm_in)
    return res


def _mm_tn(name, x, y, tk, tn, tt, pro=None):
    t_len, k_dim = x.shape
    n_dim = y.shape[1]

    def body(x_ref, y_ref, o_ref):
        @pl.when(pl.program_id(2) == 0)
        def _():
            o_ref[...] = jnp.zeros(o_ref.shape, o_ref.dtype)

        xv = x_ref[...]
        if pro is not None:
            xv = pro(xv)
        o_ref[...] += _dot_tn(xv.astype(BF), y_ref[...].astype(BF))

    return pl.pallas_call(
        body, name=name, grid=(k_dim // tk, n_dim // tn, t_len // tt),
        in_specs=[pl.BlockSpec((tt, tk), lambda a, b, t: (t, a)), pl.BlockSpec((tt, tn), lambda a, b, t: (t, b))],
        out_specs=pl.BlockSpec((tk, tn), lambda a, b, t: (a, b)),
        out_shape=jax.ShapeDtypeStruct((k_dim, n_dim), F32),
        compiler_params=_cparams(("parallel", "parallel", "arbitrary")),
    )(x, y)


QK_W = 128
ATT_W = N_HEADS * QK_W
C_COL = HEAD_DIM
ONE_COL = HEAD_DIM + 3


def _lane(shape):
    return lax.broadcasted_iota(jnp.int32, shape, 1)


def _head_low(a256, h):
    pair = a256[:, QK_W * (h // 2):QK_W * (h // 2 + 1)]
    return pltpu.roll(pair, HEAD_DIM, 1) if h % 2 else pair


def _heads_spread(a256):
    low = _lane((a256.shape[0], QK_W)) < HEAD_DIM
    return jnp.concatenate([jnp.where(low, _head_low(a256, h), 0.0) for h in range(N_HEADS)], axis=1)


def _heads_packed(a512):
    low = _lane((a512.shape[0], QK_W)) < HEAD_DIM
    out = []
    for pair in range(N_HEADS // 2):
        even = a512[:, QK_W * 2 * pair:QK_W * (2 * pair + 1)]
        odd = a512[:, QK_W * (2 * pair + 1):QK_W * (2 * pair + 2)]
        out.append(jnp.where(low, even, pltpu.roll(odd, HEAD_DIM, 1)))
    return jnp.concatenate(out, axis=1)


def _mix_in_fwd(t_len, tm, h, g_pre, w_a):
    def fn(i, n, rows, prevs, nexts, fulls):
        (hv,), (g, w) = rows, fulls
        z = _dot(_rms_fwd(hv, g[...]).astype(BF), w[...])
        lane = _lane((tm, QK_W))
        ones = jnp.where(lane < ONE_COL, 1.0, 0.0)
        zq = z[:, Z_QKV:Z_QKV + D_GRP]
        qa = jnp.concatenate([jnp.where(lane < HEAD_DIM, _head_low(zq, a) * (HEAD_DIM ** -0.5), ones) for a in range(N_HEADS)], axis=1)
        return [z[:, Z_CONF:Z_QKV], qa, z[:, Z_QKV + D_GRP:Z_SC], z[:, Z_SC:Z_POOL], z[:, Z_POOL:Z_F], z[:, Z_F:Z_W]], []

    return _rows_call("mix_in_fwd", fn, t_len, tm, [h], [g_pre, w_a],
                      [(512, F32), (ATT_W, BF), (2 * D_GRP, BF), (768, F32), (256, F32), (F_PAD, F32)])


def _mix_in_bwd(t_len, tm, dh, h, dzc, dqa, dka, dva, dzs, dzp, dzf, g_pre, w_a):
    def fn(i, n, rows, prevs, nexts, fulls):
        dhv, hv, a, dq, dk, dv, c, d, e = rows
        g, w = fulls
        b = jnp.concatenate([_heads_packed(dq), _heads_packed(dk), _heads_packed(dv)], axis=1).astype(BF)
        dz = jnp.concatenate([a, b, c, d, e], axis=1)
        dxn = _dot_nt(dz, w[...])
        dx, dg = _rms_bwd(hv, g[...], dxn)
        xn = _rms_fwd(hv, g[...])
        return [dhv + dx, xn, dz], [dg]

    return _rows_call("mix_in_bwd", fn, t_len, tm, [dh, h, dzc, dqa, dka, dva, dzs, dzp, dzf], [g_pre, w_a],
                      [(D_MODEL, F32), (D_MODEL, BF), (Z_W, BF)], [(1, D_MODEL)])


def _glu_ext(i, zc, zc_prev):
    ext = jnp.concatenate([zc_prev, zc], axis=0)
    u = ext[:, :D_GRP] * _sig(ext[:, D_GRP:])
    row = lax.broadcasted_iota(jnp.int32, u.shape, 0)
    return jnp.where((row >= HALO) | (i > 0), u, 0.0)


def _conf_fwd(t_len, tm, zc, w_dw, ln_g, ln_b, w_pw):
    def fn(i, n, rows, prevs, nexts, fulls):
        (zv,), (zp,) = rows, prevs
        wdw, lg, lb, wpw = fulls
        u = _glu_ext(i, zv, zp)
        cv = jnp.zeros((tm, D_GRP), F32)
        for k in range(CONF_K):
            cv = cv + wdw[k:k + 1, :] * _back(u, CONF_K - 1 - k)[HALO:, :]
        mu = jnp.mean(cv, axis=-1, keepdims=True)
        xc = cv - mu
        ln = xc * lax.rsqrt(jnp.mean(xc * xc, axis=-1, keepdims=True) + EPS) * lg[...] + lb[...]
        s = ln * _sig(ln)
        return [cv, _dot(s.astype(BF), wpw[...])], []

    return _rows_call("conf_fwd", fn, t_len, tm, [zc], [w_dw, ln_g, ln_b, w_pw], [(D_GRP, F32), (D_GRP, BF)], prevs=[zc])


def _conf_bwd(t_len, tm, zc, cv, dy, w_dw, ln_g, ln_b, w_pw):
    def fn(i, n, rows, prevs, nexts, fulls):
        zv, cvv, dyv = rows
        (zp,) = prevs
        cvn, dyn = nexts
        wdw, lg, lb, wpw = fulls
        cve = jnp.concatenate([cvv, cvn], axis=0)
        dye = jnp.concatenate([dyv, dyn], axis=0)
        mu = jnp.mean(cve, axis=-1, keepdims=True)
        xc = cve - mu
        rs = lax.rsqrt(jnp.mean(xc * xc, axis=-1, keepdims=True) + EPS)
        xh = xc * rs
        ln = xh * lg[...] + lb[...]
        sg = _sig(ln)
        s = ln * sg
        ds = _dot_nt(dye.astype(BF), wpw[...])
        dln = ds * (sg * (1.0 + ln * (1.0 - sg)))
        dxh = dln * lg[...]
        dcv = rs * (dxh - jnp.mean(dxh, axis=-1, keepdims=True) - xh * jnp.mean(dxh * xh, axis=-1, keepdims=True))
        row = lax.broadcasted_iota(jnp.int32, dcv.shape, 0)
        dcv = jnp.where((row < tm) | (i < n - 1), dcv, 0.0)
        d_lg = jnp.sum((dln * xh)[:tm], axis=0, keepdims=True)
        d_lb = jnp.sum(dln[:tm], axis=0, keepdims=True)
        d_wpw = _dot_tn(s[:tm].astype(BF), dyv.astype(BF))
        u = _glu_ext(i, zv, zp)
        dcv_cur = dcv[:tm]
        du = jnp.zeros((tm, D_GRP), F32)
        d_wdw = jnp.zeros((32, D_GRP), F32)
        krow = lax.broadcasted_iota(jnp.int32, (32, D_GRP), 0)
        for k in range(CONF_K):
            d = CONF_K - 1 - k
            du = du + wdw[k:k + 1, :] * _ahead(dcv, d)[:tm, :]
            tap = _back(u, d)[HALO:, :]
            d_wdw = d_wdw + jnp.where(krow == k, jnp.sum(dcv_cur * tap, axis=0, keepdims=True), 0.0)
        a, b = zv[:, :D_GRP], zv[:, D_GRP:]
        sb = _sig(b)
        dz = jnp.concatenate([du * sb, du * a * sb * (1.0 - sb)], axis=1)
        return [dz], [d_wdw, d_lg, d_lb, d_wpw]

    return _rows_call("conf_bwd", fn, t_len, tm, [zc, cv, dy], [w_dw, ln_g, ln_b, w_pw], [(512, BF)],
                      [(32, D_GRP), (1, D_GRP), (1, D_GRP), (D_GRP, D_GRP)], prevs=[zc], nexts=[cv, dy])


def _sc_ext(i, zs, zs_prev):
    ext = jnp.concatenate([zs_prev, zs], axis=0)
    e = ext[:, 2 * D_GRP:] * ext[:, :D_GRP]
    row = lax.broadcasted_iota(jnp.int32, e.shape, 0)
    return jnp.where((row >= HALO) | (i > 0), e, 0.0)


def _sconv_fwd(t_len, tm, zs, w_sc):
    def fn(i, n, rows, prevs, nexts, fulls):
        (zv,), (zp,), (w,) = rows, prevs, fulls
        e = _sc_ext(i, zv, zp)
        cv = jnp.zeros((tm, D_GRP), F32)
        for k in range(SC_K):
            cv = cv + w[k:k + 1, :] * _back(e, SC_K - 1 - k)[HALO:, :]
        return [zv[:, D_GRP:2 * D_GRP] * cv], []

    return _rows_call("sconv_fwd", fn, t_len, tm, [zs], [w_sc], [(D_GRP, BF)], prevs=[zs])


def _sconv_bwd(t_len, tm, zs, dy, w_sc):
    def fn(i, n, rows, prevs, nexts, fulls):
        zv, dyv = rows
        (zp,) = prevs
        zn, dyn = nexts
        (w,) = fulls
        e = _sc_ext(i, zv, zp)
        taps = [_back(e, SC_K - 1 - k)[HALO:, :] for k in range(SC_K)]
        cv = w[0:1, :] * taps[0] + w[1:2, :] * taps[1] + w[2:3, :] * taps[2]
        bg = zv[:, D_GRP:2 * D_GRP]
        dcv = jnp.concatenate([dyv * bg, dyn * zn[:, D_GRP:2 * D_GRP]], axis=0)
        row = lax.broadcasted_iota(jnp.int32, dcv.shape, 0)
        dcv = jnp.where((row < tm) | (i < n - 1), dcv, 0.0)
        de = jnp.zeros((tm, D_GRP), F32)
        d_w = jnp.zeros((8, D_GRP), F32)
        krow = lax.broadcasted_iota(jnp.int32, (8, D_GRP), 0)
        for k in range(SC_K):
            de = de + w[k:k + 1, :] * _ahead(dcv, SC_K - 1 - k)[:tm, :]
            d_w = d_w + jnp.where(krow == k, jnp.sum(dcv[:tm] * taps[k], axis=0, keepdims=True), 0.0)
        dz = jnp.concatenate([de * zv[:, 2 * D_GRP:], dyv * cv, de * zv[:, :D_GRP]], axis=1)
        return [dz], [d_w]

    return _rows_call("sconv_bwd", fn, t_len, tm, [zs, dy], [w_sc], [(768, BF)], [(8, D_GRP)], prevs=[zs], nexts=[zs, dy])


def _pool_window(shape):
    grp = lax.broadcasted_iota(jnp.int32, shape, 1) // 64
    return grp, jnp.where(grp == 0, 2.0, jnp.where(grp == 1, 4.0, jnp.where(grp == 2, 8.0, 16.0)))


def _pool_d(i, tm, zv, zp):
    ext = jnp.concatenate([zp, zv], axis=0)
    row = lax.broadcasted_iota(jnp.int32, ext.shape, 0)
    ext = jnp.where((row >= HALO) | (i > 0), ext, 0.0)
    s2 = ext + _back(ext, 1)
    s4 = s2 + _back(s2, 2)
    s8 = s4 + _back(s4, 4)
    s16 = s8 + _back(s8, 8)
    grp, win = _pool_window((tm, D_GRP))
    sel = jnp.where(grp == 0, s2[HALO:], jnp.where(grp == 1, s4[HALO:], jnp.where(grp == 2, s8[HALO:], s16[HALO:])))
    pos = (i * tm + lax.broadcasted_iota(jnp.int32, (tm, D_GRP), 0) + 1).astype(F32)
    return sel / jnp.minimum(pos, win) - zv


def _pool_fwd(t_len, tm, zpool, w_bd, scale):
    def fn(i, n, rows, prevs, nexts, fulls):
        (zv,), (zp,) = rows, prevs
        w, sc = fulls
        d = _pool_d(i, tm, zv, zp)
        return [_dot(d.astype(BF), w[...]) * sc[...]], []

    return _rows_call("pool_fwd", fn, t_len, tm, [zpool], [w_bd, scale], [(D_GRP, BF)], prevs=[zpool])


def _pool_bwd(t_len, tm, zpool, dy, w_bd, scale):
    def fn(i, n, rows, prevs, nexts, fulls):
        zv, dyv = rows
        (zp,) = prevs
        (dyn,) = nexts
        w, sc = fulls
        d = _pool_d(i, tm, zv, zp)
        lin = _dot(d.astype(BF), w[...])
        d_sc = jnp.sum(dyv * lin, axis=0, keepdims=True)
        dye = jnp.concatenate([dyv, dyn], axis=0) * sc[...]
        d_w = _dot_tn(d.astype(BF), dye[:tm].astype(BF))
        dd = _dot_nt(dye.astype(BF), w[...])
        row = lax.broadcasted_iota(jnp.int32, dd.shape, 0)
        dd = jnp.where((row < tm) | (i < n - 1), dd, 0.0)
        grp, win = _pool_window(dd.shape)
        pos = (i * tm + row + 1).astype(F32)
        ddc = dd / jnp.minimum(pos, win)
        f2 = ddc + _ahead(ddc, 1)
        f4 = f2 + _ahead(f2, 2)
        f8 = f4 + _ahead(f4, 4)
        f16 = f8 + _ahead(f8, 8)
        sel = jnp.where(grp == 0, f2, jnp.where(grp == 1, f4, jnp.where(grp == 2, f8, f16)))
        return [(sel - dd)[:tm]], [d_w, d_sc]

    return _rows_call("pool_bwd", fn, t_len, tm, [zpool, dy], [w_bd, scale], [(D_GRP, BF)],
                      [(D_GRP, D_GRP), (1, D_GRP)], prevs=[zpool], nexts=[dy])


def _mix_cat(y_conf, o_att, y_sc, y_pool):
    return jnp.concatenate([y_conf, _heads_packed(o_att).astype(BF), y_sc, y_pool], axis=1)


def _mix_out_fwd(t_len, tm, ys, h, w_out, g_post):
    def fn(i, n, rows, prevs, nexts, fulls):
        y0, y1, y2, y3, hv = rows
        w, g = fulls
        mix = _dot(_mix_cat(y0, y1, y2, y3), w[...])
        return [mix, hv + _rms_fwd(mix, g[...])], []

    return _rows_call("mix_out_fwd", fn, t_len, tm, [*ys, h], [w_out, g_post], [(D_MODEL, F32), (D_MODEL, F32)])


def _mix_out_bwd(t_len, tm, dh, mix, ys, w_out, g_post):
    def fn(i, n, rows, prevs, nexts, fulls):
        dhv, mv, y0, y1, y2, y3 = rows
        w, g = fulls
        dmix, dg = _rms_bwd(mv, g[...], dhv)
        dmb = dmix.astype(BF)
        dcat = _dot_nt(dmb, w[...])
        d_w = _dot_tn(_mix_cat(y0, y1, y2, y3), dmb)
        return [dcat[:, :256], _heads_spread(dcat[:, 256:512]), dcat[:, 512:768], dcat[:, 768:]], [d_w, dg]

    return _rows_call("mix_out_bwd", fn, t_len, tm, [dh, mix, *ys], [w_out, g_post],
                      [(D_GRP, F32), (ATT_W, F32), (D_GRP, F32), (D_GRP, F32)], [(D_MODEL, D_MODEL), (1, D_MODEL)])


def _log_sigmoid(x):
    return jnp.minimum(x, 0.0) - jnp.log(1.0 + jnp.exp(-jnp.abs(x)))


SCAN_BLK = 256


def _fox_prep(t_len, zf, kv, b_f):
    blk = min(SCAN_BLK, t_len)

    def body(zf_ref, kv_ref, b_ref, ka_ref, va_ref, carry_s):
        @pl.when(pl.program_id(0) == 0)
        def _():
            carry_s[...] = jnp.zeros(carry_s.shape, F32)

        tri = (lax.broadcasted_iota(jnp.int32, (blk, blk), 0) >= lax.broadcasted_iota(jnp.int32, (blk, blk), 1)).astype(F32)
        lf = _log_sigmoid(zf_ref[...] + b_ref[...])
        cs = jnp.dot(tri, lf, precision=lax.Precision.HIGHEST, preferred_element_type=F32) + carry_s[...]
        carry_s[...] = cs[blk - 1:blk, :]
        neg = -cs
        hi = neg.astype(BF).astype(F32)
        mid = (neg - hi).astype(BF).astype(F32)
        lo = ((neg - hi) - mid).astype(BF).astype(F32)
        kvv = kv_ref[...].astype(F32)
        lane = _lane((blk, QK_W))
        ka, va = [], []
        for a in range(N_HEADS):
            terms = jnp.where(lane == C_COL, hi[:, a:a + 1], jnp.where(lane == C_COL + 1, mid[:, a:a + 1], jnp.where(
                lane == C_COL + 2, lo[:, a:a + 1], jnp.where(lane == ONE_COL, HEAD_DIM ** 0.5, 0.0))))
            ka.append(jnp.where(lane < HEAD_DIM, _head_low(kvv[:, :D_GRP], a), terms))
            va.append(jnp.where(lane < HEAD_DIM, _head_low(kvv[:, D_GRP:], a), 0.0))
        ka_ref[...] = jnp.concatenate(ka, axis=1).astype(BF)
        va_ref[...] = jnp.concatenate(va, axis=1).astype(BF)

    row = lambda w: pl.BlockSpec((blk, w), lambda i: (i, 0))
    return pl.pallas_call(
        body, name="fox_prep", grid=(t_len // blk,),
        in_specs=[row(F_PAD), row(2 * D_GRP), pl.BlockSpec((1, F_PAD), lambda i: (0, 0))],
        out_specs=[row(ATT_W), row(ATT_W)],
        out_shape=[jax.ShapeDtypeStruct((t_len, ATT_W), BF)] * 2,
        scratch_shapes=[pltpu.VMEM((1, F_PAD), F32)],
        compiler_params=_cparams(("arbitrary",)),
    )(zf, kv, b_f)


def _fox_post(t_len, dqa, dka, zf, b_f):
    blk = min(SCAN_BLK, t_len)
    nb = t_len // blk

    def body(dq_ref, dk_ref, zf_ref, b_ref, dz_ref, db_ref, carry_s):
        @pl.when(pl.program_id(0) == 0)
        def _():
            carry_s[...] = jnp.zeros(carry_s.shape, F32)
            db_ref[...] = jnp.zeros(db_ref.shape, F32)

        tri = (lax.broadcasted_iota(jnp.int32, (blk, blk), 0) <= lax.broadcasted_iota(jnp.int32, (blk, blk), 1)).astype(F32)
        lane = _lane((blk, F_PAD))
        dc = jnp.zeros((blk, F_PAD), F32)
        for a in range(N_HEADS):
            head = slice(QK_W * a, QK_W * (a + 1))
            col = dq_ref[:, head][:, ONE_COL:ONE_COL + 1] - dk_ref[:, head][:, C_COL:C_COL + 1]
            dc = jnp.where(lane == a, col, dc)
        dlf = jnp.dot(tri, dc, precision=lax.Precision.HIGHEST, preferred_element_type=F32) + carry_s[...]
        carry_s[...] = dlf[0:1, :]
        dz = dlf * _sig(-(zf_ref[...] + b_ref[...]))
        dz_ref[...] = dz.astype(dz_ref.dtype)
        db_ref[...] += jnp.sum(dz, axis=0, keepdims=True)

    row = lambda w: pl.BlockSpec((blk, w), lambda i: (nb - 1 - i, 0))
    one = pl.BlockSpec((1, F_PAD), lambda i: (0, 0))
    return pl.pallas_call(
        body, name="fox_post", grid=(nb,),
        in_specs=[row(ATT_W), row(ATT_W), row(F_PAD), one], out_specs=[row(F_PAD), one],
        out_shape=(jax.ShapeDtypeStruct((t_len, F_PAD), BF), jax.ShapeDtypeStruct((1, F_PAD), F32)),
        scratch_shapes=[pltpu.VMEM((1, F_PAD), F32)],
        compiler_params=_cparams(("arbitrary",)),
    )(dqa, dka, zf, b_f)


def _tri_mask(tq, key_rows):
    r = lax.broadcasted_iota(jnp.int32, (tq, tq), 0)
    c = lax.broadcasted_iota(jnp.int32, (tq, tq), 1)
    return (r <= c) if key_rows else (r >= c)


LSE_COL = HEAD_DIM


def _fox_fwd(t_len, tq, qa, ka, va, gather=None):
    nq = t_len // tq
    plan, shards = gather if gather else (None, [])
    n = len(shards)

    def body(q_ref, k_ref, v_ref, *rest):
        ins, o_ref, outs, sems = rest[:n], rest[n], rest[n + 1:2 * n + 1], rest[2 * n + 1:]
        h, i = pl.program_id(0), pl.program_id(1)
        if plan:
            @pl.when((h == 0) & (i == 0))
            def _():
                plan.start(ins, outs, *sems)

        q = q_ref[...]

        def tile(j, carry, diagonal):
            m, l, acc = carry
            rows = pl.ds(pl.multiple_of(j * tq, tq), tq)
            s = _dot_nt(q, k_ref[rows, :])
            if diagonal:
                s = jnp.where(_tri_mask(tq, False), s, NEG)
            m_new = jnp.maximum(m, jnp.max(s, axis=-1, keepdims=True))
            alpha = jnp.exp(m - m_new)
            p = jnp.exp(s - m_new)
            l = alpha * l + jnp.sum(p, axis=-1, keepdims=True)
            acc = alpha * acc + _dot(p.astype(BF), v_ref[rows, :])
            return m_new, l, acc

        init = (jnp.full((tq, 1), NEG, F32), jnp.zeros((tq, 1), F32), jnp.zeros((tq, QK_W), F32))
        carry = lax.fori_loop(0, i, lambda j, c: tile(j, c, False), init)
        m, l, acc = tile(i, carry, True)
        o_ref[...] = jnp.where(_lane((tq, QK_W)) == LSE_COL, m + jnp.log(l), acc / l)
        if plan:
            @pl.when((h == N_HEADS - 1) & (i == nq - 1))
            def _():
                plan.finish(ins, outs, *sems)

    tile_spec = pl.BlockSpec((tq, QK_W), lambda h, i: (i, h))
    head_spec = pl.BlockSpec((t_len, QK_W), lambda h, i: (0, h))
    return pl.pallas_call(
        body, name="fox_fwd_gather" if plan else "fox_fwd", grid=(N_HEADS, nq),
        in_specs=[tile_spec, head_spec, head_spec] + [ANY] * n, out_specs=[tile_spec] + [ANY] * n,
        out_shape=[jax.ShapeDtypeStruct((t_len, ATT_W), F32)] + (plan.out_shape if plan else []),
        scratch_shapes=plan.scratch if plan else [],
        compiler_params=_cparams(("arbitrary", "arbitrary")),
    )(qa, ka, va, *shards)


def _fox_bwd(t_len, tq, qa, ka, va, do, o, comm=None):
    nq = t_len // tq
    plan, comm_in = comm if comm else (None, [])
    n = len(comm_in)

    def body(q_ref, k_ref, v_ref, do_ref, o_ref, *rest):
        c_ins, (dq_ref, dk_ref, dv_ref), c_outs = rest[:n], rest[n:n + 3], rest[n + 3:2 * n + 3]
        acc_s, dl_s, lse_s = rest[2 * n + 3:2 * n + 6]
        sems = rest[2 * n + 6:]
        hd, j = pl.program_id(0), pl.program_id(1)
        if plan:
            @pl.when((hd == 0) & (j == 0))
            def _():
                plan.start(c_ins, c_outs, *sems)

        @pl.when(j == 0)
        def _():
            acc_s[...] = jnp.zeros(acc_s.shape, F32)
            row_of = lambda sel, a: lax.dot_general(sel, a, (((1,), (1,)), ((), ())), precision=lax.Precision.HIGHEST,
                                                    preferred_element_type=F32)
            dl_s[...] = row_of(jnp.ones((8, QK_W), F32), do_ref[...] * o_ref[...])
            lse_s[...] = row_of(jnp.where(_lane((8, QK_W)) == LSE_COL, 1.0, 0.0), o_ref[...])

        kt, vt = k_ref[...], v_ref[...]

        def tile(i, carry, diagonal):
            dk, dv = carry
            rows = pl.ds(pl.multiple_of(i * tq, tq), tq)
            qt = q_ref[rows, :]
            dob = do_ref[rows, :].astype(BF)
            pt = jnp.exp(_dot_nt(kt, qt) - lse_s[0:1, rows])
            if diagonal:
                pt = jnp.where(_tri_mask(tq, True), pt, 0.0)
            dst = (pt * (_dot_nt(vt, dob) - dl_s[0:1, rows])).astype(BF)
            acc_s[rows, :] += _dot_tn(dst, kt)
            return dk + _dot(dst, qt), dv + _dot(pt.astype(BF), dob)

        carry = tile(j, (jnp.zeros((tq, QK_W), F32), jnp.zeros((tq, QK_W), F32)), True)
        dk, dv = lax.fori_loop(j + 1, nq, lambda i, c: tile(i, c, False), carry)
        dk_ref[...] = dk
        dv_ref[...] = dv

        @pl.when(j == nq - 1)
        def _():
            dq_ref[...] = acc_s[...] * (HEAD_DIM ** -0.5)

        if plan:
            @pl.when((hd == N_HEADS - 1) & (j == nq - 1))
            def _():
                plan.finish(c_ins, c_outs, *sems)

    head_spec = pl.BlockSpec((t_len, QK_W), lambda h, j: (0, h))
    tile_spec = pl.BlockSpec((tq, QK_W), lambda h, j: (j, h))
    return pl.pallas_call(
        body, name="fox_bwd_exchange" if plan else "fox_bwd", grid=(N_HEADS, nq),
        in_specs=[head_spec, tile_spec, tile_spec, head_spec, head_spec] + [ANY] * n,
        out_specs=[head_spec, tile_spec, tile_spec] + [ANY] * n,
        out_shape=[jax.ShapeDtypeStruct((t_len, ATT_W), F32)] * 3 + (plan.out_shape if plan else []),
        scratch_shapes=[pltpu.VMEM((t_len, QK_W), F32), pltpu.VMEM((8, t_len), F32), pltpu.VMEM((8, t_len), F32)]
        + (plan.scratch if plan else []),
        compiler_params=_cparams(("arbitrary", "arbitrary")),
    )(qa, ka, va, do, o, *comm_in)


def _mlp_up_fwd(t_len, tm, h, g_pre, w_up):
    def fn(i, n, rows, prevs, nexts, fulls):
        (hv,), (g, w) = rows, fulls
        return [_dot(_rms_fwd(hv, g[...]).astype(BF), w[...])], []

    return _rows_call("mlp_up_fwd", fn, t_len, tm, [h], [g_pre, w_up], [(D_FF, F32)])[0]


def _mlp_down_fwd(t_len, tm, up, h, w_down, g_post):
    def fn(i, n, rows, prevs, nexts, fulls):
        (uv, hv), (w, g) = rows, fulls
        a = jnp.square(jnp.maximum(uv, 0.0))
        ff = _dot(a.astype(BF), w[...])
        return [ff, hv + _rms_fwd(ff, g[...])], []

    return _rows_call("mlp_down_fwd", fn, t_len, tm, [up, h], [w_down, g_post], [(D_MODEL, F32), (D_MODEL, F32)])


def _mlp_bwd_a(t_len, tm, dh, ff, up, w_down, g_post):
    def fn(i, n, rows, prevs, nexts, fulls):
        (dhv, fv, uv), (w, g) = rows, fulls
        dff, dg = _rms_bwd(fv, g[...], dhv)
        dfb = dff.astype(BF)
        dup = _dot_nt(dfb, w[...]) * (2.0 * jnp.maximum(uv, 0.0))
        return [dfb, dup], [dg]

    return _rows_call("mlp_bwd_a", fn, t_len, tm, [dh, ff, up], [w_down, g_post], [(D_MODEL, BF), (D_FF, BF)], [(1, D_MODEL)])


def _mlp_bwd_b(t_len, tm, dh, h, dup, w_up, g_pre):
    def fn(i, n, rows, prevs, nexts, fulls):
        (dhv, hv, duv), (w, g) = rows, fulls
        dhn = _dot_nt(duv, w[...])
        dx, dg = _rms_bwd(hv, g[...], dhn)
        return [dhv + dx, _rms_fwd(hv, g[...])], [dg]

    return _rows_call("mlp_bwd_b", fn, t_len, tm, [dh, h, dup], [w_up, g_pre], [(D_MODEL, F32), (D_MODEL, BF)], [(1, D_MODEL)])


def _ple_fwd(t_len, tm, h, p_i, g_pre, w_gate, w_proj, g_post):
    def fn(i, n, rows, prevs, nexts, fulls):
        (hv, pv), (g, wg, wp, gp) = rows, fulls
        gpre = _dot(_rms_fwd(hv, g[...]).astype(BF), wg[...])
        pe = _dot(pv.astype(BF), wp[...])
        return [gpre, pe, hv + _rms_fwd(pe * _sig(gpre), gp[...])], []

    return _rows_call("ple_fwd", fn, t_len, tm, [h, p_i], [g_pre, w_gate, w_proj, g_post], [(D_MODEL, F32)] * 3)


def _ple_bwd(t_len, tm, dh, h, gpre, pe, p_i, g_pre, w_gate, g_post, comm=None):
    def fn(i, n, rows, prevs, nexts, fulls):
        (dhv, hv, gv, pev, pv), (g, wg, gp) = rows, fulls
        sg = _sig(gv)
        de, d_gp = _rms_bwd(pev * sg, gp[...], dhv)
        dpe = (de * sg).astype(BF)
        dgate = (de * pev * sg * (1.0 - sg)).astype(BF)
        d_wp = _dot_tn(pv.astype(BF), dpe)
        hn = _rms_fwd(hv, g[...])
        d_wg = _dot_tn(hn.astype(BF), dgate)
        dx, d_g = _rms_bwd(hv, g[...], _dot_nt(dgate, wg[...]))
        return [dhv + dx], [d_wg, d_wp, d_g, d_gp]

    return _rows_call("ple_bwd_exchange" if comm else "ple_bwd", fn, t_len, tm, [dh, h, gpre, pe, p_i], [g_pre, w_gate, g_post],
                      [(D_MODEL, F32)], [(D_MODEL, D_MODEL), (D_PLE, D_MODEL), (1, D_MODEL), (1, D_MODEL)], comm=comm)


def _loss_call(t_len, tm, h, target):
    def fn(i, n, rows, prevs, nexts, fulls):
        hv, tv = rows
        err = hv - tv
        part = 0.5 * jnp.sum(jnp.mean(err * err, axis=-1, keepdims=True), axis=0, keepdims=True)
        return [err * (1.0 / D_MODEL)], [jnp.broadcast_to(part, (8, 128))]

    return _rows_call("loss", fn, t_len, tm, [h, target], [], [(D_MODEL, F32)], [(8, 128)])


def _adamw_call(name, w, g, m, v):
    n_l, n_r, n_c = w.shape
    tr = 256 if n_r % 256 == 0 else n_r

    def body(w_ref, g_ref, m_ref, v_ref, d_ref, nm_ref, nv_ref):
        gv = g_ref[...]
        nm = ADAM_B1 * m_ref[...] + (1.0 - ADAM_B1) * gv
        nv = ADAM_B2 * v_ref[...] + (1.0 - ADAM_B2) * jnp.square(gv)
        m_hat = nm / (1.0 - ADAM_B1 ** ADAM_STEP)
        v_hat = nv / (1.0 - ADAM_B2 ** ADAM_STEP)
        d_ref[...] = -ADAM_LR * (m_hat / (jnp.sqrt(v_hat) + ADAM_EPS) + ADAM_WD * w_ref[...])
        nm_ref[...] = nm
        nv_ref[...] = nv

    spec = pl.BlockSpec((1, tr, n_c), lambda l, r: (l, r, 0))
    return pl.pallas_call(
        body, name=name, grid=(n_l, n_r // tr), in_specs=[spec] * 4, out_specs=[spec] * 3,
        out_shape=[jax.ShapeDtypeStruct(w.shape, F32)] * 3,
        compiler_params=_cparams(("parallel", "parallel")),
    )(w, g, m, v)


ANY = pl.BlockSpec(memory_space=pl.ANY)


def _place():
    x, y, c = lax.axis_index("x"), lax.axis_index("y"), lax.axis_index("c")
    chips = [(1 - x, y), (x, 1 - y), (1 - x, 1 - y)]
    return x, y, c, 2 * x + y, chips


def _remote(src, dst, send_sem, recv_sem, dev):
    return pltpu.make_async_remote_copy(src_ref=src, dst_ref=dst, send_sem=send_sem, recv_sem=recv_sem,
                                        device_id=dev, device_id_type=MESH_T)


class _GatherPlan:
    def __init__(self, shards, lo, nl):
        self.shapes = [s.shape for s in shards]
        self.lo, self.nl, self.n = lo, nl, len(shards)
        self.split = [s.shape[1] % 32 == 0 for s in shards]
        self.out_shape = [jax.ShapeDtypeStruct((3, nl, *s.shape[1:]), s.dtype) for s in shards]
        self.scratch = [pltpu.SemaphoreType.DMA((6 * self.n,)), pltpu.SemaphoreType.DMA((6 * self.n,))]

    def _views(self, ins, outs, a, c):
        lay, all_l = pl.ds(self.lo, self.nl), pl.ds(0, self.nl)
        if not self.split[a]:
            return ins[a].at[lay], (lambda j: outs[a].at[j]), None
        hr = self.shapes[a][1] // 2
        mine, other = pl.ds(c * hr, hr), pl.ds((1 - c) * hr, hr)
        return ins[a].at[lay, mine], (lambda j: outs[a].at[j, all_l, mine]), (lambda j: outs[a].at[j, all_l, other])

    def _ici(self, ins, outs, ssem, rsem):
        x, y, c, q, chips = _place()
        cps = []
        for a in range(self.n):
            src, land, _ = self._views(ins, outs, a, c)
            for j, chip in enumerate(chips):
                cps.append(_remote(src, land(j), ssem.at[6 * a + j], rsem.at[6 * a + j], (*chip, c)))
        return cps

    def start(self, ins, outs, ssem, rsem):
        for cp in self._ici(ins, outs, ssem, rsem):
            cp.start()

    def finish(self, ins, outs, ssem, rsem):
        x, y, c, q, chips = _place()
        sib = (x, y, 1 - c)
        cps = self._ici(ins, outs, ssem, rsem)
        for a in range(self.n):
            _, land, _ = self._views(ins, outs, a, c)
            for j in range(3):
                cps[3 * a + j].wait_recv()
                if self.split[a]:
                    cps.append(_remote(land(j), land(j), ssem.at[6 * a + 3 + j], rsem.at[6 * a + 3 + j], sib))
                    cps[-1].start()
        for a in range(self.n):
            _, _, other = self._views(ins, outs, a, c)
            for j in range(3):
                if self.split[a]:
                    _remote(other(j), other(j), ssem.at[6 * a + 3 + j], rsem.at[6 * a + 3 + j], sib).wait_recv()
        for cp in cps:
            cp.wait_send()


def _allgather_weights(shards, lo, nl):
    plan = _GatherPlan(shards, lo, nl)
    n = plan.n

    def body(*refs):
        ins, outs, sems = refs[:n], refs[n:2 * n], refs[2 * n:]
        plan.start(ins, outs, *sems)
        plan.finish(ins, outs, *sems)

    return pl.pallas_call(body, name="allgather_weights", in_specs=[ANY] * n, out_specs=[ANY] * n,
                          out_shape=plan.out_shape, scratch_shapes=plan.scratch)(*shards)


def _by_chip(own, others, chip):
    by_mask = jnp.stack([own, others[1], others[0], others[2]])
    return jnp.stack([lax.dynamic_index_in_dim(by_mask, jnp.bitwise_xor(chip, r), 0, keepdims=False) for r in range(N_CHIPS)])


class _PairPlan:
    def __init__(self, grads):
        self.n = len(grads)
        self.half = [g.shape[2] // 2 for g in grads]
        self.layers = [g.shape[1] for g in grads]
        self.out_shape = [jax.ShapeDtypeStruct((N_CHIPS, g.shape[1], g.shape[2] // 2, g.shape[3]), g.dtype) for g in grads]
        self.scratch = [pltpu.SemaphoreType.DMA((self.n,)), pltpu.SemaphoreType.DMA((self.n,))]

    def _copies(self, ins, outs, ssem, rsem):
        x, y, c, q, chips = _place()
        return [_remote(ins[a].at[pl.ds(0, N_CHIPS), pl.ds(0, self.layers[a]), pl.ds((1 - c) * self.half[a], self.half[a])],
                        outs[a], ssem.at[a], rsem.at[a], (x, y, 1 - c)) for a in range(self.n)]

    def start(self, ins, outs, ssem, rsem):
        for cp in self._copies(ins, outs, ssem, rsem):
            cp.start()

    def finish(self, ins, outs, ssem, rsem):
        for cp in self._copies(ins, outs, ssem, rsem):
            cp.wait()


class _ChipPlan:
    def __init__(self, parts):
        self.n = len(parts)
        self.out_shape = [jax.ShapeDtypeStruct((3, *s.shape[1:]), s.dtype) for s in parts]
        self.scratch = [pltpu.SemaphoreType.DMA((3 * self.n,)), pltpu.SemaphoreType.DMA((3 * self.n,))]

    def _copies(self, ins, outs, ssem, rsem):
        x, y, c, q, chips = _place()
        return [_remote(ins[a].at[2 * cx + cy], outs[a].at[j], ssem.at[3 * a + j], rsem.at[3 * a + j], (cx, cy, c))
                for a in range(self.n) for j, (cx, cy) in enumerate(chips)]

    def start(self, ins, outs, ssem, rsem):
        for cp in self._copies(ins, outs, ssem, rsem):
            cp.start()

    def finish(self, ins, outs, ssem, rsem):
        for cp in self._copies(ins, outs, ssem, rsem):
            cp.wait()


def _exchange_call(name, plan, arrays):
    n = plan.n

    def body(*refs):
        ins, outs, sems = refs[:n], refs[n:2 * n], refs[2 * n:]
        plan.start(ins, outs, *sems)
        plan.finish(ins, outs, *sems)

    return pl.pallas_call(body, name=name, in_specs=[ANY] * n, out_specs=[ANY] * n,
                          out_shape=plan.out_shape, scratch_shapes=plan.scratch)(*arrays)


def _pair_sum(name, g, peer, c_arr):
    _, n_l, half, n_c = peer.shape
    tr = 256 if half % 256 == 0 else half
    nb = half // tr

    def body(c_ref, g_ref, p_ref, o_ref):
        o_ref[...] = (g_ref[...] + p_ref[...]).astype(o_ref.dtype)

    blk = (1, 1, tr, n_c)
    return pl.pallas_call(
        body, name=name,
        grid_spec=pltpu.PrefetchScalarGridSpec(
            num_scalar_prefetch=1, grid=(N_CHIPS, n_l, nb),
            in_specs=[pl.BlockSpec(blk, lambda qi, li, ri, c_ref: (qi, li, c_ref[0] * nb + ri, 0)),
                      pl.BlockSpec(blk, lambda qi, li, ri, c_ref: (qi, li, ri, 0))],
            out_specs=pl.BlockSpec(blk, lambda qi, li, ri, c_ref: (qi, li, ri, 0))),
        out_shape=jax.ShapeDtypeStruct(peer.shape, BF),
        compiler_params=_cparams(("parallel", "parallel", "parallel")),
    )(c_arr, g, peer)


def _chip_sum(name, own, r, chip_arr):
    _, n_l, n_r, n_c = r.shape
    tr = 256 if n_r % 256 == 0 else n_r

    def body(q_ref, r0, r1, r2, r3, o_ref):
        o_ref[...] = ((r0[0].astype(F32) + r1[0].astype(F32)) + r2[0].astype(F32)) + r3[0].astype(F32)

    blk = (1, 1, tr, n_c)
    return pl.pallas_call(
        body, name=name,
        grid_spec=pltpu.PrefetchScalarGridSpec(
            num_scalar_prefetch=1, grid=(n_l, n_r // tr),
            in_specs=[pl.BlockSpec(blk, lambda li, ri, q_ref: (q_ref[0], li, ri, 0))]
            + [pl.BlockSpec(blk, lambda li, ri, q_ref, s=s: (s, li, ri, 0)) for s in range(3)],
            out_specs=pl.BlockSpec((1, tr, n_c), lambda li, ri, q_ref: (li, ri, 0))),
        out_shape=jax.ShapeDtypeStruct((n_l, n_r, n_c), F32),
        compiler_params=_cparams(("parallel", "parallel")),
    )(chip_arr, own, r, r, r)


def _pair_share(halves):
    n = len(halves)

    def body(*refs):
        ins, outs = refs[:n], refs[n:2 * n]
        send_sems, recv_sems = refs[2 * n:]
        x, y, c, q, chips = _place()
        cps = [_remote(ins[a], outs[a], send_sems.at[a], recv_sems.at[a], (x, y, 1 - c)) for a in range(n)]
        for cp in cps:
            cp.start()
        for cp in cps:
            cp.wait()

    return pl.pallas_call(
        body, name="grad_pair_share", in_specs=[ANY] * n, out_specs=[ANY] * n,
        out_shape=[jax.ShapeDtypeStruct(h.shape, h.dtype) for h in halves],
        scratch_shapes=[pltpu.SemaphoreType.DMA((n,)), pltpu.SemaphoreType.DMA((n,))],
    )(*halves)


def _allreduce_small(v):
    n_r = v.shape[0]

    def body(v_ref, o_ref, slots, send_sems, recv_sems):
        x, y, c = lax.axis_index("x"), lax.axis_index("y"), lax.axis_index("c")
        me = 4 * x + 2 * y + c
        slots[me] = v_ref[...]
        cps = []
        for r in range(1, 8):
            px = 1 - x if r & 4 else x
            py = 1 - y if r & 2 else y
            pc = 1 - c if r & 1 else c
            cps.append(_remote(v_ref, slots.at[me], send_sems.at[r - 1], recv_sems.at[r - 1], (px, py, pc)))
            cps[-1].start()
        for cp in cps:
            cp.wait()
        tot = slots[0]
        for d in range(1, 8):
            tot = tot + slots[d]
        o_ref[...] = tot

    return pl.pallas_call(
        body, name="allreduce_small",
        in_specs=[pl.BlockSpec(memory_space=pltpu.VMEM)], out_specs=pl.BlockSpec(memory_space=pltpu.VMEM),
        out_shape=jax.ShapeDtypeStruct(v.shape, F32),
        scratch_shapes=[pltpu.VMEM((8, n_r, 128), F32), pltpu.SemaphoreType.DMA((7,)), pltpu.SemaphoreType.DMA((7,))],
        compiler_params=pltpu.CompilerParams(vmem_limit_bytes=VMEM_LIMIT_BYTES),
    )(v)


def _cols_full(g, l):
    s = g[:, l]
    return s.transpose(1, 0, 2).reshape(s.shape[1], -1)


def _rows_full(g, l):
    s = g[:, l]
    return s.reshape(-1, s.shape[-1])


def _cols_split(full):
    r = full.shape[0]
    return full.reshape(r, N_CHIPS, -1).transpose(1, 0, 2)


def _rows_split(full):
    return full.reshape(N_CHIPS, -1, full.shape[-1])


def _pack(parts):
    flat = []
    for a in parts:
        f = a.reshape(-1).astype(F32)
        flat.append(jnp.pad(f, (0, (-f.shape[0]) % 1024)))
    return jnp.concatenate(flat).reshape(-1, 128)


def _unpack(buf, shapes):
    flat = buf.reshape(-1)
    out, off = [], 0
    for s in shapes:
        size = 1
        for d in s:
            size *= d
        out.append(flat[off:off + size].reshape(s))
        off += size + (-size) % 1024
    return out


def kernel(x, p, g_mix_pre, w_in, b_forget, w_conf_dw, conf_ln_g, conf_ln_b, w_conf_pw, w_sc, w_pool, pool_scale, w_out, g_mix_post, g_mlp_pre, w_up, w_down, g_mlp_post, g_ple_pre, w_ple_gate, w_ple_proj, g_ple_post, loss_target, m_g_mix_pre, m_w_in, m_b_forget, m_w_conf_dw, m_conf_ln_g, m_conf_ln_b, m_w_conf_pw, m_w_sc, m_w_pool, m_pool_scale, m_w_out, m_g_mix_post, m_g_mlp_pre, m_w_up, m_w_down, m_g_mlp_post, m_g_ple_pre, m_w_ple_gate, m_w_ple_proj, m_g_ple_post, v_g_mix_pre, v_w_in, v_b_forget, v_w_conf_dw, v_conf_ln_g, v_conf_ln_b, v_w_conf_pw, v_w_sc, v_w_pool, v_pool_scale, v_w_out, v_g_mix_post, v_g_mlp_pre, v_w_up, v_w_down, v_g_mlp_post, v_g_ple_pre, v_w_ple_gate, v_w_ple_proj, v_g_ple_post):
    names = ['g_mix_pre', 'w_in', 'b_forget', 'w_conf_dw', 'conf_ln_g', 'conf_ln_b', 'w_conf_pw', 'w_sc', 'w_pool', 'pool_scale',
             'w_out', 'g_mix_post', 'g_mlp_pre', 'w_up', 'w_down', 'g_mlp_post', 'g_ple_pre', 'w_ple_gate', 'w_ple_proj', 'g_ple_post']
    env = locals()
    wts = {k: env[k] for k in names}
    mom = {k: env["m_" + k] for k in names}
    var = {k: env["v_" + k] for k in names}

    t_len = x.shape[1]
    tb = min(512, t_len)
    tm = tb
    tq = min(512, max(t_len // 2, 128))
    tt = min(1024, t_len)
    chip = 2 * lax.axis_index("x") + lax.axis_index("y")
    core = lax.axis_index("c")

    big = ['w_in', 'w_conf_pw', 'w_out', 'w_up', 'w_down', 'w_ple_gate', 'w_ple_proj']
    tiny = ['w_conf_dw', 'w_sc']
    own_shards = [wts[k].astype(BF) for k in big] + [wts[k] for k in tiny]

    def chip_order(others, lo, nl):
        return {k: _by_chip(own[lo:lo + nl], oth, chip) for k, own, oth in zip(big + tiny, own_shards, others)}

    def layer_weights(gat, li, l):
        w_full = _cols_full(gat['w_in'], li)
        w_a = jnp.concatenate([w_full[:, :F_OFF], w_full[:, F_OFF + N_HEADS:], w_full[:, F_OFF:F_OFF + N_HEADS],
                               jnp.zeros((D_MODEL, Z_W - D_IN), BF)], axis=1)
        w_bd = jnp.zeros((D_GRP, D_GRP), F32)
        for g in range(4):
            w_bd = lax.dynamic_update_slice(w_bd, wts['w_pool'][l, g], (64 * g, 64 * g))
        row = lambda a: a[l][None, :]
        return dict(
            w_a=w_a, w_dw=jnp.pad(_cols_full(gat['w_conf_dw'], li), ((0, 1), (0, 0))), w_pw=_rows_full(gat['w_conf_pw'], li),
            w_sc=jnp.pad(_cols_full(gat['w_sc'], li), ((0, 5), (0, 0))), w_bd=w_bd.astype(BF),
            w_out=_rows_full(gat['w_out'], li), w_up=_cols_full(gat['w_up'], li), w_down=_rows_full(gat['w_down'], li),
            w_gate=_rows_full(gat['w_ple_gate'], li), w_proj=_cols_full(gat['w_ple_proj'], li),
            b_f=jnp.pad(wts['b_forget'][l], (0, F_PAD - N_HEADS))[None, :],
            ln_g=row(wts['conf_ln_g']), ln_b=row(wts['conf_ln_b']), pool_scale=row(wts['pool_scale']),
            g_mix_pre=row(wts['g_mix_pre']), g_mix_post=row(wts['g_mix_post']), g_mlp_pre=row(wts['g_mlp_pre']),
            g_mlp_post=row(wts['g_mlp_post']), g_ple_pre=row(wts['g_ple_pre']), g_ple_post=row(wts['g_ple_post']))

    lw = [layer_weights(chip_order(_allgather_weights(own_shards, 0, 1), 0, 1), 0, 0)]

    h = x[0]
    saved = []
    for l in range(DEPTH):
        w = lw[l]
        s = dict(h0=h)
        s['zc'], s['qa'], kv, s['zs'], s['zp'], s['zf'] = _mix_in_fwd(t_len, tb, h, w['g_mix_pre'], w['w_a'])
        s['cv'], y_conf = _conf_fwd(t_len, tm, s['zc'], w['w_dw'], w['ln_g'], w['ln_b'], w['w_pw'])
        s['ka'], s['va'] = _fox_prep(t_len, s['zf'], kv, w['b_f'])
        if l == 0:
            s['o'], *others = _fox_fwd(t_len, tq, s['qa'], s['ka'], s['va'],
                                       gather=(_GatherPlan(own_shards, 1, DEPTH - 1), own_shards))
            rest = chip_order(others, 1, DEPTH - 1)
            lw += [layer_weights(rest, li, li + 1) for li in range(DEPTH - 1)]
        else:
            (s['o'],) = _fox_fwd(t_len, tq, s['qa'], s['ka'], s['va'])
        (y_sc,) = _sconv_fwd(t_len, tm, s['zs'], w['w_sc'])
        (y_pool,) = _pool_fwd(t_len, tm, s['zp'], w['w_bd'], w['pool_scale'])
        s['ys'] = [y_conf, s['o'], y_sc, y_pool]
        s['mix'], h = _mix_out_fwd(t_len, tb, s['ys'], h, w['w_out'], w['g_mix_post'])
        s['h1'] = h
        s['up'] = _mlp_up_fwd(t_len, tb, h, w['g_mlp_pre'], w['w_up'])
        s['ff'], h = _mlp_down_fwd(t_len, tb, s['up'], h, w['w_down'], w['g_mlp_post'])
        s['h2'] = h
        s['gpre'], s['pe'], h = _ple_fwd(t_len, tb, h, p[l, 0], w['g_ple_pre'], w['w_gate'], w['w_proj'], w['g_ple_post'])
        saved.append(s)

    dh, loss_part = _loss_call(t_len, tm, h, loss_target[0])

    grads = [None] * DEPTH
    split = dict(w_in=_cols_split, w_conf_pw=_rows_split, w_out=_rows_split, w_up=_cols_split, w_down=_rows_split,
                 w_ple_gate=_rows_split, w_ple_proj=_cols_split)
    c_arr = core.astype(jnp.int32).reshape(1)
    chip_arr = chip.astype(jnp.int32).reshape(1)

    def contrib_of(layers):
        return [jnp.stack([split[k](grads[l][k]) for l in layers], axis=1) for k in big]

    def pair_sums(tag, contrib, peer):
        return [_pair_sum(f"grad_pair_sum_{tag}_{k}", a, b, c_arr) for k, a, b in zip(big, contrib, peer)]

    def finish_reduce(tag, parts, landed):
        halves = [_chip_sum(f"grad_chip_sum_{tag}_{k}", own, r, chip_arr) for k, own, r in zip(big, parts, landed)]
        full = []
        for mine, theirs in zip(halves, _pair_share(halves)):
            both = jnp.stack([mine, theirs])
            full.append(jnp.concatenate([lax.dynamic_index_in_dim(both, core, 0, keepdims=False),
                                         lax.dynamic_index_in_dim(both, 1 - core, 0, keepdims=False)], axis=1))
        return full

    for l in reversed(range(DEPTH)):
        w, s, g = lw[l], saved[l], {}
        contrib13 = contrib_of(range(1, DEPTH)) if l == 0 else []
        dh, g['w_ple_gate'], g['w_ple_proj'], g['g_ple_pre'], g['g_ple_post'], *peer13 = _ple_bwd(
            t_len, tb, dh, s['h2'], s['gpre'], s['pe'], p[l, 0], w['g_ple_pre'], w['w_gate'], w['g_ple_post'],
            comm=(_PairPlan(contrib13), contrib13) if l == 0 else None)
        parts13 = pair_sums("l13", contrib13, peer13) if l == 0 else []
        dff, dup, g['g_mlp_post'] = _mlp_bwd_a(t_len, tm, dh, s['ff'], s['up'], w['w_down'], w['g_mlp_post'])
        dh, hn, g['g_mlp_pre'] = _mlp_bwd_b(t_len, tb, dh, s['h1'], dup, w['w_up'], w['g_mlp_pre'])
        g['w_up'] = _mm_tn("mlp_dw_up", hn, dup, D_MODEL, 1024, tt)
        g['w_down'] = _mm_tn("mlp_dw_down", s['up'], dff, 1024, D_MODEL, tt, pro=lambda u: jnp.square(jnp.maximum(u, 0.0)))
        dy_conf, dy_att, dy_sc, dy_pool, g['w_out'], g['g_mix_post'] = _mix_out_bwd(t_len, tb, dh, s['mix'], s['ys'], w['w_out'], w['g_mix_post'])
        dzc, g['w_conf_dw'], g['conf_ln_g'], g['conf_ln_b'], g['w_conf_pw'] = _conf_bwd(
            t_len, tm, s['zc'], s['cv'], dy_conf, w['w_dw'], w['ln_g'], w['ln_b'], w['w_pw'])
        dzs, g['w_sc'] = _sconv_bwd(t_len, tm, s['zs'], dy_sc, w['w_sc'])
        dzp, d_wbd, g['pool_scale'] = _pool_bwd(t_len, tm, s['zp'], dy_pool, w['w_bd'], w['pool_scale'])
        g['w_pool'] = jnp.stack([d_wbd[64 * a:64 * (a + 1), 64 * a:64 * (a + 1)] for a in range(4)])
        dqa, dka, dva, *landed13 = _fox_bwd(t_len, tq, s['qa'], s['ka'], s['va'], dy_att, s['o'],
                                            comm=(_ChipPlan(parts13), parts13) if l == 0 else None)
        dzf, d_bf = _fox_post(t_len, dqa, dka, s['zf'], w['b_f'])
        g['b_forget'] = d_bf[0, :N_HEADS]
        dh, xn, dz, g['g_mix_pre'] = _mix_in_bwd(t_len, tb, dh, s['h0'], dzc, dqa, dka, dva, dzs, dzp, dzf, w['g_mix_pre'], w['w_a'])
        d_wa = _mm_tn("mix_dw_in", xn, dz, D_MODEL, Z_W, tt)
        g['w_in'] = jnp.concatenate([d_wa[:, :F_OFF], d_wa[:, Z_F:Z_F + N_HEADS], d_wa[:, F_OFF:Z_F]], axis=1)
        g['w_conf_dw'] = g['w_conf_dw'][:CONF_K]
        g['w_sc'] = g['w_sc'][:SC_K]
        grads[l] = g
    grad_x = dh[None]

    contrib0 = contrib_of([0])
    parts0 = pair_sums("l0", contrib0, _exchange_call("grad_pair_exchange", _PairPlan(contrib0), contrib0))
    first = finish_reduce("l0", parts0, _exchange_call("grad_chip_exchange", _ChipPlan(parts0), parts0))
    later = finish_reduce("l13", parts13, landed13)
    reduced = {k: jnp.concatenate([a, b], axis=0) for k, a, b in zip(big, first, later)}

    small = [k for k in names if k not in big]
    small_shapes = [(DEPTH, *grads[0][k].shape) for k in small]
    packed = _pack([jnp.stack([grads[l][k] for l in range(DEPTH)]) for k in small] + [loss_part])
    summed = _allreduce_small(packed)
    small_sum = _unpack(summed, small_shapes + [(8, 128)])
    loss = small_sum[-1][0, 0]
    for k, a in zip(small, small_sum[:-1]):
        if k in tiny:
            a = lax.dynamic_slice_in_dim(a, chip * 64, 64, axis=2)
        reduced[k] = a.reshape(wts[k].shape)

    delta_w, new_m, new_v = {}, {}, {}
    for k in names:
        shp = wts[k].shape
        as3 = (lambda a: a.reshape(shp[0], -1, shp[-1])) if len(shp) > 2 else (lambda a: a.reshape(1, shp[0], shp[1]))
        d, nm, nv = _adamw_call("adamw_" + k, as3(wts[k]), as3(reduced[k]), as3(mom[k]), as3(var[k]))
        delta_w[k], new_m[k], new_v[k] = d.reshape(shp), nm.reshape(shp), nv.reshape(shp)

    return (loss, grad_x, *[reduced[k] for k in names], *[delta_w[k] for k in names],
            *[new_m[k] for k in names], *[new_v[k] for k in names])
```

```python
import functools

import jax
import jax.numpy as jnp
from jax import lax
from jax.experimental import pallas as pl
from jax.experimental.pallas import tpu as pltpu

F32 = jnp.float32
BF = jnp.bfloat16

DEPTH = 4
D_MODEL = 1024
D_GRP = 256
HEAD_DIM = 64
N_HEADS = 4
CONF_K = 31
SC_K = 3
D_FF = 4096
D_PLE = 256
EPS = 1e-6
N_CHIPS = 4
Z_CONF, Z_QKV, Z_SC, Z_POOL, Z_F = 0, 512, 1280, 2048, 2304
Z_W = 2432
F_PAD = 128
D_IN = 2308
F_OFF = 1280

ADAM_LR, ADAM_B1, ADAM_B2, ADAM_EPS, ADAM_WD, ADAM_STEP = 0.001, 0.9, 0.999, 1e-08, 0.01, 10

VMEM_LIMIT_BYTES = 56 * 1024 * 1024
HALO = 32
NEG = -1e30
MESH_T = pl.DeviceIdType.MESH


def _cparams(sem=None):
    return pltpu.CompilerParams(dimension_semantics=sem, vmem_limit_bytes=VMEM_LIMIT_BYTES)


def _dot(a, b):
    return jnp.dot(a, b, preferred_element_type=F32)


def _dot_nt(a, b):
    return lax.dot_general(a, b, (((1,), (1,)), ((), ())), preferred_element_type=F32)


def _dot_tn(a, b):
    return lax.dot_general(a, b, (((0,), (0,)), ((), ())), preferred_element_type=F32)


def _sig(x):
    return jax.nn.sigmoid(x)


def _rms_fwd(x, g):
    r = lax.rsqrt(jnp.mean(x * x, axis=-1, keepdims=True) + EPS)
    return x * r * g


def _rms_bwd(x, g, dy):
    r = lax.rsqrt(jnp.mean(x * x, axis=-1, keepdims=True) + EPS)
    xh = x * r
    dg = jnp.sum(dy * xh, axis=0, keepdims=True)
    dxh = dy * g
    dx = r * (dxh - xh * jnp.mean(dxh * xh, axis=-1, keepdims=True))
    return dx, dg


def _back(ext, d):
    return ext if d == 0 else pltpu.roll(ext, d, 0)


def _ahead(ext, d):
    return ext if d == 0 else pltpu.roll(ext, ext.shape[0] - d, 0)


def _rows_call(name, fn, t_len, tm, rows, fulls, out_rows, out_accs=(), prevs=(), nexts=(), comm=None):
    plan, comm_in = comm if comm else (None, [])
    n_comm = len(comm_in)
    n = t_len // tm
    hb = tm // HALO
    nhb = t_len // HALO
    n_rows, n_prev, n_next, n_full = len(rows), len(prevs), len(nexts), len(fulls)
    in_specs = [pl.BlockSpec((tm, a.shape[1]), lambda i: (i, 0)) for a in rows]
    in_specs += [pl.BlockSpec((HALO, a.shape[1]), lambda i: (jnp.maximum(i * hb - 1, 0), 0)) for a in prevs]
    in_specs += [pl.BlockSpec((HALO, a.shape[1]), lambda i: (jnp.minimum((i + 1) * hb, nhb - 1), 0)) for a in nexts]
    in_specs += [pl.BlockSpec(a.shape, lambda i, nd=a.ndim: (0,) * nd, pipeline_mode=pl.Buffered(1)) for a in fulls]
    out_shape = [jax.ShapeDtypeStruct((t_len, c), dt) for c, dt in out_rows]
    out_shape += [jax.ShapeDtypeStruct(s, F32) for s in out_accs]
    out_specs = [pl.BlockSpec((tm, c), lambda i: (i, 0)) for c, _ in out_rows]
    out_specs += [pl.BlockSpec(s, lambda i, nd=len(s): (0,) * nd) for s in out_accs]
    n_in = n_rows + n_prev + n_next + n_full
    n_ro = len(out_rows)

    n_out = n_ro + len(out_accs)

    def body(*refs):
        i = pl.program_id(0)
        ins, outs = refs[:n_in], refs[n_in + n_comm:n_in + n_comm + n_out]
        c_ins, c_outs, sems = refs[n_in:n_in + n_comm], refs[n_in + n_comm + n_out:n_in + 2 * n_comm + n_out], refs[n_in + 2 * n_comm + n_out:]
        if plan:
            @pl.when(i == 0)
            def _():
                plan.start(c_ins, c_outs, *sems)

        rv = [r[...] for r in ins[:n_rows]]
        pv = [r[...] for r in ins[n_rows:n_rows + n_prev]]
        nv = [r[...] for r in ins[n_rows + n_prev:n_rows + n_prev + n_next]]
        fv = list(ins[n_rows + n_prev + n_next:])
        ro, ao = fn(i, n, rv, pv, nv, fv)
        for r, v in zip(outs[:n_ro], ro):
            r[...] = v.astype(r.dtype)
        if out_accs:
            acc = outs[n_ro:]

            @pl.when(i == 0)
            def _():
                for r in acc:
                    r[...] = jnp.zeros(r.shape, r.dtype)

            for r, v in zip(acc, ao):
                r[...] += v
        if plan:
            @pl.when(i == n - 1)
            def _():
                plan.finish(c_ins, c_outs, *sems)

    res = pl.pallas_call(
        body, name=name, grid=(n,), in_specs=in_specs + [ANY] * n_comm, out_specs=out_specs + [ANY] * n_comm,
        out_shape=out_shape + (plan.out_shape if plan else []), scratch_shapes=plan.scratch if plan else [],
        compiler_params=_cparams(("arbitrary",)),
    )(*rows, *prevs, *nexts, *fulls, *comm_in)
    return res


def _mm_tn(name, x, y, tk, tn, tt, pro=None):
    t_len, k_dim = x.shape
    n_dim = y.shape[1]

    def body(x_ref, y_ref, o_ref):
        @pl.when(pl.program_id(2) == 0)
        def _():
            o_ref[...] = jnp.zeros(o_ref.shape, o_ref.dtype)

        xv = x_ref[...]
        if pro is not None:
            xv = pro(xv)
        o_ref[...] += _dot_tn(xv.astype(BF), y_ref[...].astype(BF))

    return pl.pallas_call(
        body, name=name, grid=(k_dim // tk, n_dim // tn, t_len // tt),
        in_specs=[pl.BlockSpec((tt, tk), lambda a, b, t: (t, a)), pl.BlockSpec((tt, tn), lambda a, b, t: (t, b))],
        out_specs=pl.BlockSpec((tk, tn), lambda a, b, t: (a, b)),
        out_shape=jax.ShapeDtypeStruct((k_dim, n_dim), F32),
        compiler_params=_cparams(("parallel", "parallel", "arbitrary")),
    )(x, y)


QK_W = 128
ATT_W = N_HEADS * QK_W
C_COL = HEAD_DIM
ONE_COL = HEAD_DIM + 3


def _lane(shape):
    return lax.broadcasted_iota(jnp.int32, shape, 1)


def _head_low(a256, h):
    pair = a256[:, QK_W * (h // 2):QK_W * (h // 2 + 1)]
    return pltpu.roll(pair, HEAD_DIM, 1) if h % 2 else pair


def _heads_spread(a256):
    low = _lane((a256.shape[0], QK_W)) < HEAD_DIM
    return jnp.concatenate([jnp.where(low, _head_low(a256, h), 0.0) for h in range(N_HEADS)], axis=1)


def _heads_packed(a512):
    low = _lane((a512.shape[0], QK_W)) < HEAD_DIM
    out = []
    for pair in range(N_HEADS // 2):
        even = a512[:, QK_W * 2 * pair:QK_W * (2 * pair + 1)]
        odd = a512[:, QK_W * (2 * pair + 1):QK_W * (2 * pair + 2)]
        out.append(jnp.where(low, even, pltpu.roll(odd, HEAD_DIM, 1)))
    return jnp.concatenate(out, axis=1)


def _mix_in_fwd(t_len, tm, h, g_pre, w_a):
    def fn(i, n, rows, prevs, nexts, fulls):
        (hv,), (g, w) = rows, fulls
        z = _dot(_rms_fwd(hv, g[...]).astype(BF), w[...])
        lane = _lane((tm, QK_W))
        ones = jnp.where(lane < ONE_COL, 1.0, 0.0)
        zq = z[:, Z_QKV:Z_QKV + D_GRP]
        qa = jnp.concatenate([jnp.where(lane < HEAD_DIM, _head_low(zq, a) * (HEAD_DIM ** -0.5), ones) for a in range(N_HEADS)], axis=1)
        return [z[:, Z_CONF:Z_QKV], qa, z[:, Z_QKV + D_GRP:Z_SC], z[:, Z_SC:Z_POOL], z[:, Z_POOL:Z_F], z[:, Z_F:Z_W]], []

    return _rows_call("mix_in_fwd", fn, t_len, tm, [h], [g_pre, w_a],
                      [(512, F32), (ATT_W, BF), (2 * D_GRP, BF), (768, F32), (256, F32), (F_PAD, F32)])


def _mix_in_bwd(t_len, tm, dh, h, dzc, dqa, dka, dva, dzs, dzp, dzf, g_pre, w_a):
    def fn(i, n, rows, prevs, nexts, fulls):
        dhv, hv, a, dq, dk, dv, c, d, e = rows
        g, w = fulls
        b = jnp.concatenate([_heads_packed(dq), _heads_packed(dk), _heads_packed(dv)], axis=1).astype(BF)
        dz = jnp.concatenate([a, b, c, d, e], axis=1)
        dxn = _dot_nt(dz, w[...])
        dx, dg = _rms_bwd(hv, g[...], dxn)
        xn = _rms_fwd(hv, g[...])
        return [dhv + dx, xn, dz], [dg]

    return _rows_call("mix_in_bwd", fn, t_len, tm, [dh, h, dzc, dqa, dka, dva, dzs, dzp, dzf], [g_pre, w_a],
                      [(D_MODEL, F32), (D_MODEL, BF), (Z_W, BF)], [(1, D_MODEL)])


def _glu_ext(i, zc, zc_prev):
    ext = jnp.concatenate([zc_prev, zc], axis=0)
    u = ext[:, :D_GRP] * _sig(ext[:, D_GRP:])
    row = lax.broadcasted_iota(jnp.int32, u.shape, 0)
    return jnp.where((row >= HALO) | (i > 0), u, 0.0)


def _conf_fwd(t_len, tm, zc, w_dw, ln_g, ln_b, w_pw):
    def fn(i, n, rows, prevs, nexts, fulls):
        (zv,), (zp,) = rows, prevs
        wdw, lg, lb, wpw = fulls
        u = _glu_ext(i, zv, zp)
        cv = jnp.zeros((tm, D_GRP), F32)
        for k in range(CONF_K):
            cv = cv + wdw[k:k + 1, :] * _back(u, CONF_K - 1 - k)[HALO:, :]
        mu = jnp.mean(cv, axis=-1, keepdims=True)
        xc = cv - mu
        ln = xc * lax.rsqrt(jnp.mean(xc * xc, axis=-1, keepdims=True) + EPS) * lg[...] + lb[...]
        s = ln * _sig(ln)
        return [cv, _dot(s.astype(BF), wpw[...])], []

    return _rows_call("conf_fwd", fn, t_len, tm, [zc], [w_dw, ln_g, ln_b, w_pw], [(D_GRP, F32), (D_GRP, BF)], prevs=[zc])


def _conf_bwd(t_len, tm, zc, cv, dy, w_dw, ln_g, ln_b, w_pw):
    def fn(i, n, rows, prevs, nexts, fulls):
        zv, cvv, dyv = rows
        (zp,) = prevs
        cvn, dyn = nexts
        wdw, lg, lb, wpw = fulls
        cve = jnp.concatenate([cvv, cvn], axis=0)
        dye = jnp.concatenate([dyv, dyn], axis=0)
        mu = jnp.mean(cve, axis=-1, keepdims=True)
        xc = cve - mu
        rs = lax.rsqrt(jnp.mean(xc * xc, axis=-1, keepdims=True) + EPS)
        xh = xc * rs
        ln = xh * lg[...] + lb[...]
        sg = _sig(ln)
        s = ln * sg
        ds = _dot_nt(dye.astype(BF), wpw[...])
        dln = ds * (sg * (1.0 + ln * (1.0 - sg)))
        dxh = dln * lg[...]
        dcv = rs * (dxh - jnp.mean(dxh, axis=-1, keepdims=True) - xh * jnp.mean(dxh * xh, axis=-1, keepdims=True))
        row = lax.broadcasted_iota(jnp.int32, dcv.shape, 0)
        dcv = jnp.where((row < tm) | (i < n - 1), dcv, 0.0)
        d_lg = jnp.sum((dln * xh)[:tm], axis=0, keepdims=True)
        d_lb = jnp.sum(dln[:tm], axis=0, keepdims=True)
        d_wpw = _dot_tn(s[:tm].astype(BF), dyv.astype(BF))
        u = _glu_ext(i, zv, zp)
        dcv_cur = dcv[:tm]
        du = jnp.zeros((tm, D_GRP), F32)
        d_wdw = jnp.zeros((32, D_GRP), F32)
        krow = lax.broadcasted_iota(jnp.int32, (32, D_GRP), 0)
        for k in range(CONF_K):
            d = CONF_K - 1 - k
            du = du + wdw[k:k + 1, :] * _ahead(dcv, d)[:tm, :]
            tap = _back(u, d)[HALO:, :]
            d_wdw = d_wdw + jnp.where(krow == k, jnp.sum(dcv_cur * tap, axis=0, keepdims=True), 0.0)
        a, b = zv[:, :D_GRP], zv[:, D_GRP:]
        sb = _sig(b)
        dz = jnp.concatenate([du * sb, du * a * sb * (1.0 - sb)], axis=1)
        return [dz], [d_wdw, d_lg, d_lb, d_wpw]

    return _rows_call("conf_bwd", fn, t_len, tm, [zc, cv, dy], [w_dw, ln_g, ln_b, w_pw], [(512, BF)],
                      [(32, D_GRP), (1, D_GRP), (1, D_GRP), (D_GRP, D_GRP)], prevs=[zc], nexts=[cv, dy])


def _sc_ext(i, zs, zs_prev):
    ext = jnp.concatenate([zs_prev, zs], axis=0)
    e = ext[:, 2 * D_GRP:] * ext[:, :D_GRP]
    row = lax.broadcasted_iota(jnp.int32, e.shape, 0)
    return jnp.where((row >= HALO) | (i > 0), e, 0.0)


def _sconv_fwd(t_len, tm, zs, w_sc):
    def fn(i, n, rows, prevs, nexts, fulls):
        (zv,), (zp,), (w,) = rows, prevs, fulls
        e = _sc_ext(i, zv, zp)
        cv = jnp.zeros((tm, D_GRP), F32)
        for k in range(SC_K):
            cv = cv + w[k:k + 1, :] * _back(e, SC_K - 1 - k)[HALO:, :]
        return [zv[:, D_GRP:2 * D_GRP] * cv], []

    return _rows_call("sconv_fwd", fn, t_len, tm, [zs], [w_sc], [(D_GRP, BF)], prevs=[zs])


def _sconv_bwd(t_len, tm, zs, dy, w_sc):
    def fn(i, n, rows, prevs, nexts, fulls):
        zv, dyv = rows
        (zp,) = prevs
        zn, dyn = nexts
        (w,) = fulls
        e = _sc_ext(i, zv, zp)
        taps = [_back(e, SC_K - 1 - k)[HALO:, :] for k in range(SC_K)]
        cv = w[0:1, :] * taps[0] + w[1:2, :] * taps[1] + w[2:3, :] * taps[2]
        bg = zv[:, D_GRP:2 * D_GRP]
        dcv = jnp.concatenate([dyv * bg, dyn * zn[:, D_GRP:2 * D_GRP]], axis=0)
        row = lax.broadcasted_iota(jnp.int32, dcv.shape, 0)
        dcv = jnp.where((row < tm) | (i < n - 1), dcv, 0.0)
        de = jnp.zeros((tm, D_GRP), F32)
        d_w = jnp.zeros((8, D_GRP), F32)
        krow = lax.broadcasted_iota(jnp.int32, (8, D_GRP), 0)
        for k in range(SC_K):
            de = de + w[k:k + 1, :] * _ahead(dcv, SC_K - 1 - k)[:tm, :]
            d_w = d_w + jnp.where(krow == k, jnp.sum(dcv[:tm] * taps[k], axis=0, keepdims=True), 0.0)
        dz = jnp.concatenate([de * zv[:, 2 * D_GRP:], dyv * cv, de * zv[:, :D_GRP]], axis=1)
        return [dz], [d_w]

    return _rows_call("sconv_bwd", fn, t_len, tm, [zs, dy], [w_sc], [(768, BF)], [(8, D_GRP)], prevs=[zs], nexts=[zs, dy])


def _pool_window(shape):
    grp = lax.broadcasted_iota(jnp.int32, shape, 1) // 64
    return grp, jnp.where(grp == 0, 2.0, jnp.where(grp == 1, 4.0, jnp.where(grp == 2, 8.0, 16.0)))


def _pool_d(i, tm, zv, zp):
    ext = jnp.concatenate([zp, zv], axis=0)
    row = lax.broadcasted_iota(jnp.int32, ext.shape, 0)
    ext = jnp.where((row >= HALO) | (i > 0), ext, 0.0)
    s2 = ext + _back(ext, 1)
    s4 = s2 + _back(s2, 2)
    s8 = s4 + _back(s4, 4)
    s16 = s8 + _back(s8, 8)
    grp, win = _pool_window((tm, D_GRP))
    sel = jnp.where(grp == 0, s2[HALO:], jnp.where(grp == 1, s4[HALO:], jnp.where(grp == 2, s8[HALO:], s16[HALO:])))
    pos = (i * tm + lax.broadcasted_iota(jnp.int32, (tm, D_GRP), 0) + 1).astype(F32)
    return sel / jnp.minimum(pos, win) - zv


def _pool_fwd(t_len, tm, zpool, w_bd, scale):
    def fn(i, n, rows, prevs, nexts, fulls):
        (zv,), (zp,) = rows, prevs
        w, sc = fulls
        d = _pool_d(i, tm, zv, zp)
        return [_dot(d.astype(BF), w[...]) * sc[...]], []

    return _rows_call("pool_fwd", fn, t_len, tm, [zpool], [w_bd, scale], [(D_GRP, BF)], prevs=[zpool])


def _pool_bwd(t_len, tm, zpool, dy, w_bd, scale):
    def fn(i, n, rows, prevs, nexts, fulls):
        zv, dyv = rows
        (zp,) = prevs
        (dyn,) = nexts
        w, sc = fulls
        d = _pool_d(i, tm, zv, zp)
        lin = _dot(d.astype(BF), w[...])
        d_sc = jnp.sum(dyv * lin, axis=0, keepdims=True)
        dye = jnp.concatenate([dyv, dyn], axis=0) * sc[...]
        d_w = _dot_tn(d.astype(BF), dye[:tm].astype(BF))
        dd = _dot_nt(dye.astype(BF), w[...])
        row = lax.broadcasted_iota(jnp.int32, dd.shape, 0)
        dd = jnp.where((row < tm) | (i < n - 1), dd, 0.0)
        grp, win = _pool_window(dd.shape)
        pos = (i * tm + row + 1).astype(F32)
        ddc = dd / jnp.minimum(pos, win)
        f2 = ddc + _ahead(ddc, 1)
        f4 = f2 + _ahead(f2, 2)
        f8 = f4 + _ahead(f4, 4)
        f16 = f8 + _ahead(f8, 8)
        sel = jnp.where(grp == 0, f2, jnp.where(grp == 1, f4, jnp.where(grp == 2, f8, f16)))
        return [(sel - dd)[:tm]], [d_w, d_sc]

    return _rows_call("pool_bwd", fn, t_len, tm, [zpool, dy], [w_bd, scale], [(D_GRP, BF)],
                      [(D_GRP, D_GRP), (1, D_GRP)], prevs=[zpool], nexts=[dy])


def _mix_cat(y_conf, o_att, y_sc, y_pool):
    return jnp.concatenate([y_conf, _heads_packed(o_att).astype(BF), y_sc, y_pool], axis=1)


def _mix_out_fwd(t_len, tm, ys, h, w_out, g_post):
    def fn(i, n, rows, prevs, nexts, fulls):
        y0, y1, y2, y3, hv = rows
        w, g = fulls
        mix = _dot(_mix_cat(y0, y1, y2, y3), w[...])
        return [mix, hv + _rms_fwd(mix, g[...])], []

    return _rows_call("mix_out_fwd", fn, t_len, tm, [*ys, h], [w_out, g_post], [(D_MODEL, F32), (D_MODEL, F32)])


def _mix_out_bwd(t_len, tm, dh, mix, ys, w_out, g_post):
    def fn(i, n, rows, prevs, nexts, fulls):
        dhv, mv, y0, y1, y2, y3 = rows
        w, g = fulls
        dmix, dg = _rms_bwd(mv, g[...], dhv)
        dmb = dmix.astype(BF)
        dcat = _dot_nt(dmb, w[...])
        d_w = _dot_tn(_mix_cat(y0, y1, y2, y3), dmb)
        return [dcat[:, :256], _heads_spread(dcat[:, 256:512]), dcat[:, 512:768], dcat[:, 768:]], [d_w, dg]

    return _rows_call("mix_out_bwd", fn, t_len, tm, [dh, mix, *ys], [w_out, g_post],
                      [(D_GRP, F32), (ATT_W, F32), (D_GRP, F32), (D_GRP, F32)], [(D_MODEL, D_MODEL), (1, D_MODEL)])


def _log_sigmoid(x):
    return jnp.minimum(x, 0.0) - jnp.log(1.0 + jnp.exp(-jnp.abs(x)))


SCAN_BLK = 256


def _fox_prep(t_len, zf, kv, b_f):
    blk = min(SCAN_BLK, t_len)

    def body(zf_ref, kv_ref, b_ref, ka_ref, va_ref, carry_s):
        @pl.when(pl.program_id(0) == 0)
        def _():
            carry_s[...] = jnp.zeros(carry_s.shape, F32)

        tri = (lax.broadcasted_iota(jnp.int32, (blk, blk), 0) >= lax.broadcasted_iota(jnp.int32, (blk, blk), 1)).astype(F32)
        lf = _log_sigmoid(zf_ref[...] + b_ref[...])
        cs = jnp.dot(tri, lf, precision=lax.Precision.HIGHEST, preferred_element_type=F32) + carry_s[...]
        carry_s[...] = cs[blk - 1:blk, :]
        neg = -cs
        hi = neg.astype(BF).astype(F32)
        mid = (neg - hi).astype(BF).astype(F32)
        lo = ((neg - hi) - mid).astype(BF).astype(F32)
        kvv = kv_ref[...].astype(F32)
        lane = _lane((blk, QK_W))
        ka, va = [], []
        for a in range(N_HEADS):
            terms = jnp.where(lane == C_COL, hi[:, a:a + 1], jnp.where(lane == C_COL + 1, mid[:, a:a + 1], jnp.where(
                lane == C_COL + 2, lo[:, a:a + 1], jnp.where(lane == ONE_COL, HEAD_DIM ** 0.5, 0.0))))
            ka.append(jnp.where(lane < HEAD_DIM, _head_low(kvv[:, :D_GRP], a), terms))
            va.append(jnp.where(lane < HEAD_DIM, _head_low(kvv[:, D_GRP:], a), 0.0))
        ka_ref[...] = jnp.concatenate(ka, axis=1).astype(BF)
        va_ref[...] = jnp.concatenate(va, axis=1).astype(BF)

    row = lambda w: pl.BlockSpec((blk, w), lambda i: (i, 0))
    return pl.pallas_call(
        body, name="fox_prep", grid=(t_len // blk,),
        in_specs=[row(F_PAD), row(2 * D_GRP), pl.BlockSpec((1, F_PAD), lambda i: (0, 0))],
        out_specs=[row(ATT_W), row(ATT_W)],
        out_shape=[jax.ShapeDtypeStruct((t_len, ATT_W), BF)] * 2,
        scratch_shapes=[pltpu.VMEM((1, F_PAD), F32)],
        compiler_params=_cparams(("arbitrary",)),
    )(zf, kv, b_f)


def _fox_post(t_len, dqa, dka, zf, b_f):
    blk = min(SCAN_BLK, t_len)
    nb = t_len // blk

    def body(dq_ref, dk_ref, zf_ref, b_ref, dz_ref, db_ref, carry_s):
        @pl.when(pl.program_id(0) == 0)
        def _():
            carry_s[...] = jnp.zeros(carry_s.shape, F32)
            db_ref[...] = jnp.zeros(db_ref.shape, F32)

        tri = (lax.broadcasted_iota(jnp.int32, (blk, blk), 0) <= lax.broadcasted_iota(jnp.int32, (blk, blk), 1)).astype(F32)
        lane = _lane((blk, F_PAD))
        dc = jnp.zeros((blk, F_PAD), F32)
        for a in range(N_HEADS):
            head = slice(QK_W * a, QK_W * (a + 1))
            col = dq_ref[:, head][:, ONE_COL:ONE_COL + 1] - dk_ref[:, head][:, C_COL:C_COL + 1]
            dc = jnp.where(lane == a, col, dc)
        dlf = jnp.dot(tri, dc, precision=lax.Precision.HIGHEST, preferred_element_type=F32) + carry_s[...]
        carry_s[...] = dlf[0:1, :]
        dz = dlf * _sig(-(zf_ref[...] + b_ref[...]))
        dz_ref[...] = dz.astype(dz_ref.dtype)
        db_ref[...] += jnp.sum(dz, axis=0, keepdims=True)

    row = lambda w: pl.BlockSpec((blk, w), lambda i: (nb - 1 - i, 0))
    one = pl.BlockSpec((1, F_PAD), lambda i: (0, 0))
    return pl.pallas_call(
        body, name="fox_post", grid=(nb,),
        in_specs=[row(ATT_W), row(ATT_W), row(F_PAD), one], out_specs=[row(F_PAD), one],
        out_shape=(jax.ShapeDtypeStruct((t_len, F_PAD), BF), jax.ShapeDtypeStruct((1, F_PAD), F32)),
        scratch_shapes=[pltpu.VMEM((1, F_PAD), F32)],
        compiler_params=_cparams(("arbitrary",)),
    )(dqa, dka, zf, b_f)


def _tri_mask(tq, key_rows):
    r = lax.broadcasted_iota(jnp.int32, (tq, tq), 0)
    c = lax.broadcasted_iota(jnp.int32, (tq, tq), 1)
    return (r <= c) if key_rows else (r >= c)


LSE_COL = HEAD_DIM


def _fox_fwd(t_len, tq, qa, ka, va, gather=None):
    nq = t_len // tq
    plan, shards = gather if gather else (None, [])
    n = len(shards)

    def body(q_ref, k_ref, v_ref, *rest):
        ins, o_ref, outs, sems = rest[:n], rest[n], rest[n + 1:2 * n + 1], rest[2 * n + 1:]
        h, i = pl.program_id(0), pl.program_id(1)
        if plan:
            @pl.when((h == 0) & (i == 0))
            def _():
                plan.start(ins, outs, *sems)

        q = q_ref[...]

        def tile(j, carry, diagonal):
            m, l, acc = carry
            rows = pl.ds(pl.multiple_of(j * tq, tq), tq)
            s = _dot_nt(q, k_ref[rows, :])
            if diagonal:
                s = jnp.where(_tri_mask(tq, False), s, NEG)
            m_new = jnp.maximum(m, jnp.max(s, axis=-1, keepdims=True))
            alpha = jnp.exp(m - m_new)
            p = jnp.exp(s - m_new)
            l = alpha * l + jnp.sum(p, axis=-1, keepdims=True)
            acc = alpha * acc + _dot(p.astype(BF), v_ref[rows, :])
            return m_new, l, acc

        init = (jnp.full((tq, 1), NEG, F32), jnp.zeros((tq, 1), F32), jnp.zeros((tq, QK_W), F32))
        carry = lax.fori_loop(0, i, lambda j, c: tile(j, c, False), init)
        m, l, acc = tile(i, carry, True)
        o_ref[...] = jnp.where(_lane((tq, QK_W)) == LSE_COL, m + jnp.log(l), acc / l)
        if plan:
            @pl.when((h == N_HEADS - 1) & (i == nq - 1))
            def _():
                plan.finish(ins, outs, *sems)

    tile_spec = pl.BlockSpec((tq, QK_W), lambda h, i: (i, h))
    head_spec = pl.BlockSpec((t_len, QK_W), lambda h, i: (0, h))
    return pl.pallas_call(
        body, name="fox_fwd_gather" if plan else "fox_fwd", grid=(N_HEADS, nq),
        in_specs=[tile_spec, head_spec, head_spec] + [ANY] * n, out_specs=[tile_spec] + [ANY] * n,
        out_shape=[jax.ShapeDtypeStruct((t_len, ATT_W), F32)] + (plan.out_shape if plan else []),
        scratch_shapes=plan.scratch if plan else [],
        compiler_params=_cparams(("arbitrary", "arbitrary")),
    )(qa, ka, va, *shards)


def _fox_bwd(t_len, tq, qa, ka, va, do, o, comm=None):
    nq = t_len // tq
    plan, comm_in = comm if comm else (None, [])
    n = len(comm_in)

    def body(q_ref, k_ref, v_ref, do_ref, o_ref, *rest):
        c_ins, (dq_ref, dk_ref, dv_ref), c_outs = rest[:n], rest[n:n + 3], rest[n + 3:2 * n + 3]
        acc_s, dl_s, lse_s = rest[2 * n + 3:2 * n + 6]
        sems = rest[2 * n + 6:]
        hd, j = pl.program_id(0), pl.program_id(1)
        if plan:
            @pl.when((hd == 0) & (j == 0))
            def _():
                plan.start(c_ins, c_outs, *sems)

        @pl.when(j == 0)
        def _():
            acc_s[...] = jnp.zeros(acc_s.shape, F32)
            row_of = lambda sel, a: lax.dot_general(sel, a, (((1,), (1,)), ((), ())), precision=lax.Precision.HIGHEST,
                                                    preferred_element_type=F32)
            dl_s[...] = row_of(jnp.ones((8, QK_W), F32), do_ref[...] * o_ref[...])
            lse_s[...] = row_of(jnp.where(_lane((8, QK_W)) == LSE_COL, 1.0, 0.0), o_ref[...])

        kt, vt = k_ref[...], v_ref[...]

        def tile(i, carry, diagonal):
            dk, dv = carry
            rows = pl.ds(pl.multiple_of(i * tq, tq), tq)
            qt = q_ref[rows, :]
            dob = do_ref[rows, :].astype(BF)
            pt = jnp.exp(_dot_nt(kt, qt) - lse_s[0:1, rows])
            if diagonal:
                pt = jnp.where(_tri_mask(tq, True), pt, 0.0)
            dst = (pt * (_dot_nt(vt, dob) - dl_s[0:1, rows])).astype(BF)
            acc_s[rows, :] += _dot_tn(dst, kt)
            return dk + _dot(dst, qt), dv + _dot(pt.astype(BF), dob)

        carry = tile(j, (jnp.zeros((tq, QK_W), F32), jnp.zeros((tq, QK_W), F32)), True)
        dk, dv = lax.fori_loop(j + 1, nq, lambda i, c: tile(i, c, False), carry)
        dk_ref[...] = dk
        dv_ref[...] = dv

        @pl.when(j == nq - 1)
        def _():
            dq_ref[...] = acc_s[...] * (HEAD_DIM ** -0.5)

        if plan:
            @pl.when((hd == N_HEADS - 1) & (j == nq - 1))
            def _():
                plan.finish(c_ins, c_outs, *sems)

    head_spec = pl.BlockSpec((t_len, QK_W), lambda h, j: (0, h))
    tile_spec = pl.BlockSpec((tq, QK_W), lambda h, j: (j, h))
    return pl.pallas_call(
        body, name="fox_bwd_exchange" if plan else "fox_bwd", grid=(N_HEADS, nq),
        in_specs=[head_spec, tile_spec, tile_spec, head_spec, head_spec] + [ANY] * n,
        out_specs=[head_spec, tile_spec, tile_spec] + [ANY] * n,
        out_shape=[jax.ShapeDtypeStruct((t_len, ATT_W), F32)] * 3 + (plan.out_shape if plan else []),
        scratch_shapes=[pltpu.VMEM((t_len, QK_W), F32), pltpu.VMEM((8, t_len), F32), pltpu.VMEM((8, t_len), F32)]
        + (plan.scratch if plan else []),
        compiler_params=_cparams(("arbitrary", "arbitrary")),
    )(qa, ka, va, do, o, *comm_in)


def _mlp_up_fwd(t_len, tm, h, g_pre, w_up):
    def fn(i, n, rows, prevs, nexts, fulls):
        (hv,), (g, w) = rows, fulls
        return [_dot(_rms_fwd(hv, g[...]).astype(BF), w[...])], []

    return _rows_call("mlp_up_fwd", fn, t_len, tm, [h], [g_pre, w_up], [(D_FF, F32)])[0]


def _mlp_down_fwd(t_len, tm, up, h, w_down, g_post):
    def fn(i, n, rows, prevs, nexts, fulls):
        (uv, hv), (w, g) = rows, fulls
        a = jnp.square(jnp.maximum(uv, 0.0))
        ff = _dot(a.astype(BF), w[...])
        return [ff, hv + _rms_fwd(ff, g[...])], []

    return _rows_call("mlp_down_fwd", fn, t_len, tm, [up, h], [w_down, g_post], [(D_MODEL, F32), (D_MODEL, F32)])


def _mlp_bwd_a(t_len, tm, dh, ff, up, w_down, g_post):
    def fn(i, n, rows, prevs, nexts, fulls):
        (dhv, fv, uv), (w, g) = rows, fulls
        dff, dg = _rms_bwd(fv, g[...], dhv)
        dfb = dff.astype(BF)
        dup = _dot_nt(dfb, w[...]) * (2.0 * jnp.maximum(uv, 0.0))
        return [dfb, dup], [dg]

    return _rows_call("mlp_bwd_a", fn, t_len, tm, [dh, ff, up], [w_down, g_post], [(D_MODEL, BF), (D_FF, BF)], [(1, D_MODEL)])


def _mlp_bwd_b(t_len, tm, dh, h, dup, w_up, g_pre):
    def fn(i, n, rows, prevs, nexts, fulls):
        (dhv, hv, duv), (w, g) = rows, fulls
        dhn = _dot_nt(duv, w[...])
        dx, dg = _rms_bwd(hv, g[...], dhn)
        return [dhv + dx, _rms_fwd(hv, g[...])], [dg]

    return _rows_call("mlp_bwd_b", fn, t_len, tm, [dh, h, dup], [w_up, g_pre], [(D_MODEL, F32), (D_MODEL, BF)], [(1, D_MODEL)])


def _ple_fwd(t_len, tm, h, p_i, g_pre, w_gate, w_proj, g_post):
    def fn(i, n, rows, prevs, nexts, fulls):
        (hv, pv), (g, wg, wp, gp) = rows, fulls
        gpre = _dot(_rms_fwd(hv, g[...]).astype(BF), wg[...])
        pe = _dot(pv.astype(BF), wp[...])
        return [gpre, pe, hv + _rms_fwd(pe * _sig(gpre), gp[...])], []

    return _rows_call("ple_fwd", fn, t_len, tm, [h, p_i], [g_pre, w_gate, w_proj, g_post], [(D_MODEL, F32)] * 3)


def _ple_bwd(t_len, tm, dh, h, gpre, pe, p_i, g_pre, w_gate, g_post, comm=None):
    def fn(i, n, rows, prevs, nexts, fulls):
        (dhv, hv, gv, pev, pv), (g, wg, gp) = rows, fulls
        sg = _sig(gv)
        de, d_gp = _rms_bwd(pev * sg, gp[...], dhv)
        dpe = (de * sg).astype(BF)
        dgate = (de * pev * sg * (1.0 - sg)).astype(BF)
        d_wp = _dot_tn(pv.astype(BF), dpe)
        hn = _rms_fwd(hv, g[...])
        d_wg = _dot_tn(hn.astype(BF), dgate)
        dx, d_g = _rms_bwd(hv, g[...], _dot_nt(dgate, wg[...]))
        return [dhv + dx], [d_wg, d_wp, d_g, d_gp]

    return _rows_call("ple_bwd_exchange" if comm else "ple_bwd", fn, t_len, tm, [dh, h, gpre, pe, p_i], [g_pre, w_gate, g_post],
                      [(D_MODEL, F32)], [(D_MODEL, D_MODEL), (D_PLE, D_MODEL), (1, D_MODEL), (1, D_MODEL)], comm=comm)


def _loss_call(t_len, tm, h, target):
    def fn(i, n, rows, prevs, nexts, fulls):
        hv, tv = rows
        err = hv - tv
        part = 0.5 * jnp.sum(jnp.mean(err * err, axis=-1, keepdims=True), axis=0, keepdims=True)
        return [err * (1.0 / D_MODEL)], [jnp.broadcast_to(part, (8, 128))]

    return _rows_call("loss", fn, t_len, tm, [h, target], [], [(D_MODEL, F32)], [(8, 128)])


def _adamw_call(name, w, g, m, v):
    n_l, n_r, n_c = w.shape
    tr = 256 if n_r % 256 == 0 else n_r

    def body(w_ref, g_ref, m_ref, v_ref, d_ref, nm_ref, nv_ref):
        gv = g_ref[...]
        nm = ADAM_B1 * m_ref[...] + (1.0 - ADAM_B1) * gv
        nv = ADAM_B2 * v_ref[...] + (1.0 - ADAM_B2) * jnp.square(gv)
        m_hat = nm / (1.0 - ADAM_B1 ** ADAM_STEP)
        v_hat = nv / (1.0 - ADAM_B2 ** ADAM_STEP)
        d_ref[...] = -ADAM_LR * (m_hat / (jnp.sqrt(v_hat) + ADAM_EPS) + ADAM_WD * w_ref[...])
        nm_ref[...] = nm
        nv_ref[...] = nv

    spec = pl.BlockSpec((1, tr, n_c), lambda l, r: (l, r, 0))
    return pl.pallas_call(
        body, name=name, grid=(n_l, n_r // tr), in_specs=[spec] * 4, out_specs=[spec] * 3,
        out_shape=[jax.ShapeDtypeStruct(w.shape, F32)] * 3,
        compiler_params=_cparams(("parallel", "parallel")),
    )(w, g, m, v)


ANY = pl.BlockSpec(memory_space=pl.ANY)


def _place():
    x, y, c = lax.axis_index("x"), lax.axis_index("y"), lax.axis_index("c")
    chips = [(1 - x, y), (x, 1 - y), (1 - x, 1 - y)]
    return x, y, c, 2 * x + y, chips


def _remote(src, dst, send_sem, recv_sem, dev):
    return pltpu.make_async_remote_copy(src_ref=src, dst_ref=dst, send_sem=send_sem, recv_sem=recv_sem,
                                        device_id=dev, device_id_type=MESH_T)


class _GatherPlan:
    def __init__(self, shards, lo, nl):
        self.shapes = [s.shape for s in shards]
        self.lo, self.nl, self.n = lo, nl, len(shards)
        self.split = [s.shape[1] % 32 == 0 for s in shards]
        self.out_shape = [jax.ShapeDtypeStruct((3, nl, *s.shape[1:]), s.dtype) for s in shards]
        self.scratch = [pltpu.SemaphoreType.DMA((6 * self.n,)), pltpu.SemaphoreType.DMA((6 * self.n,))]

    def _views(self, ins, outs, a, c):
        lay, all_l = pl.ds(self.lo, self.nl), pl.ds(0, self.nl)
        if not self.split[a]:
            return ins[a].at[lay], (lambda j: outs[a].at[j]), None
        hr = self.shapes[a][1] // 2
        mine, other = pl.ds(c * hr, hr), pl.ds((1 - c) * hr, hr)
        return ins[a].at[lay, mine], (lambda j: outs[a].at[j, all_l, mine]), (lambda j: outs[a].at[j, all_l, other])

    def _ici(self, ins, outs, ssem, rsem):
        x, y, c, q, chips = _place()
        cps = []
        for a in range(self.n):
            src, land, _ = self._views(ins, outs, a, c)
            for j, chip in enumerate(chips):
                cps.append(_remote(src, land(j), ssem.at[6 * a + j], rsem.at[6 * a + j], (*chip, c)))
        return cps

    def start(self, ins, outs, ssem, rsem):
        for cp in self._ici(ins, outs, ssem, rsem):
            cp.start()

    def finish(self, ins, outs, ssem, rsem):
        x, y, c, q, chips = _place()
        sib = (x, y, 1 - c)
        cps = self._ici(ins, outs, ssem, rsem)
        for a in range(self.n):
            _, land, _ = self._views(ins, outs, a, c)
            for j in range(3):
                cps[3 * a + j].wait_recv()
                if self.split[a]:
                    cps.append(_remote(land(j), land(j), ssem.at[6 * a + 3 + j], rsem.at[6 * a + 3 + j], sib))
                    cps[-1].start()
        for a in range(self.n):
            _, _, other = self._views(ins, outs, a, c)
            for j in range(3):
                if self.split[a]:
                    _remote(other(j), other(j), ssem.at[6 * a + 3 + j], rsem.at[6 * a + 3 + j], sib).wait_recv()
        for cp in cps:
            cp.wait_send()


def _allgather_weights(shards, lo, nl):
    plan = _GatherPlan(shards, lo, nl)
    n = plan.n

    def body(*refs):
        ins, outs, sems = refs[:n], refs[n:2 * n], refs[2 * n:]
        plan.start(ins, outs, *sems)
        plan.finish(ins, outs, *sems)

    return pl.pallas_call(body, name="allgather_weights", in_specs=[ANY] * n, out_specs=[ANY] * n,
                          out_shape=plan.out_shape, scratch_shapes=plan.scratch)(*shards)


def _by_chip(own, others, chip):
    by_mask = jnp.stack([own, others[1], others[0], others[2]])
    return jnp.stack([lax.dynamic_index_in_dim(by_mask, jnp.bitwise_xor(chip, r), 0, keepdims=False) for r in range(N_CHIPS)])


class _PairPlan:
    def __init__(self, grads):
        self.n = len(grads)
        self.half = [g.shape[2] // 2 for g in grads]
        self.layers = [g.shape[1] for g in grads]
        self.out_shape = [jax.ShapeDtypeStruct((N_CHIPS, g.shape[1], g.shape[2] // 2, g.shape[3]), g.dtype) for g in grads]
        self.scratch = [pltpu.SemaphoreType.DMA((self.n,)), pltpu.SemaphoreType.DMA((self.n,))]

    def _copies(self, ins, outs, ssem, rsem):
        x, y, c, q, chips = _place()
        return [_remote(ins[a].at[pl.ds(0, N_CHIPS), pl.ds(0, self.layers[a]), pl.ds((1 - c) * self.half[a], self.half[a])],
                        outs[a], ssem.at[a], rsem.at[a], (x, y, 1 - c)) for a in range(self.n)]

    def start(self, ins, outs, ssem, rsem):
        for cp in self._copies(ins, outs, ssem, rsem):
            cp.start()

    def finish(self, ins, outs, ssem, rsem):
        for cp in self._copies(ins, outs, ssem, rsem):
            cp.wait()


class _ChipPlan:
    def __init__(self, parts):
        self.n = len(parts)
        self.out_shape = [jax.ShapeDtypeStruct((3, *s.shape[1:]), s.dtype) for s in parts]
        self.scratch = [pltpu.SemaphoreType.DMA((3 * self.n,)), pltpu.SemaphoreType.DMA((3 * self.n,))]

    def _copies(self, ins, outs, ssem, rsem):
        x, y, c, q, chips = _place()
        return [_remote(ins[a].at[2 * cx + cy], outs[a].at[j], ssem.at[3 * a + j], rsem.at[3 * a + j], (cx, cy, c))
                for a in range(self.n) for j, (cx, cy) in enumerate(chips)]

    def start(self, ins, outs, ssem, rsem):
        for cp in self._copies(ins, outs, ssem, rsem):
            cp.start()

    def finish(self, ins, outs, ssem, rsem):
        for cp in self._copies(ins, outs, ssem, rsem):
            cp.wait()


def _exchange_call(name, plan, arrays):
    n = plan.n

    def body(*refs):
        ins, outs, sems = refs[:n], refs[n:2 * n], refs[2 * n:]
        plan.start(ins, outs, *sems)
        plan.finish(ins, outs, *sems)

    return pl.pallas_call(body, name=name, in_specs=[ANY] * n, out_specs=[ANY] * n,
                          out_shape=plan.out_shape, scratch_shapes=plan.scratch)(*arrays)


def _pair_sum(name, g, peer, c_arr):
    _, n_l, half, n_c = peer.shape
    tr = 256 if half % 256 == 0 else half
    nb = half // tr

    def body(c_ref, g_ref, p_ref, o_ref):
        o_ref[...] = (g_ref[...] + p_ref[...]).astype(o_ref.dtype)

    blk = (1, 1, tr, n_c)
    return pl.pallas_call(
        body, name=name,
        grid_spec=pltpu.PrefetchScalarGridSpec(
            num_scalar_prefetch=1, grid=(N_CHIPS, n_l, nb),
            in_specs=[pl.BlockSpec(blk, lambda qi, li, ri, c_ref: (qi, li, c_ref[0] * nb + ri, 0)),
                      pl.BlockSpec(blk, lambda qi, li, ri, c_ref: (qi, li, ri, 0))],
            out_specs=pl.BlockSpec(blk, lambda qi, li, ri, c_ref: (qi, li, ri, 0))),
        out_shape=jax.ShapeDtypeStruct(peer.shape, BF),
        compiler_params=_cparams(("parallel", "parallel", "parallel")),
    )(c_arr, g, peer)


def _chip_sum(name, own, r, chip_arr):
    _, n_l, n_r, n_c = r.shape
    tr = 256 if n_r % 256 == 0 else n_r

    def body(q_ref, r0, r1, r2, r3, o_ref):
        o_ref[...] = ((r0[0].astype(F32) + r1[0].astype(F32)) + r2[0].astype(F32)) + r3[0].astype(F32)

    blk = (1, 1, tr, n_c)
    return pl.pallas_call(
        body, name=name,
        grid_spec=pltpu.PrefetchScalarGridSpec(
            num_scalar_prefetch=1, grid=(n_l, n_r // tr),
            in_specs=[pl.BlockSpec(blk, lambda li, ri, q_ref: (q_ref[0], li, ri, 0))]
            + [pl.BlockSpec(blk, lambda li, ri, q_ref, s=s: (s, li, ri, 0)) for s in range(3)],
            out_specs=pl.BlockSpec((1, tr, n_c), lambda li, ri, q_ref: (li, ri, 0))),
        out_shape=jax.ShapeDtypeStruct((n_l, n_r, n_c), F32),
        compiler_params=_cparams(("parallel", "parallel")),
    )(chip_arr, own, r, r, r)


def _pair_share(halves):
    n = len(halves)

    def body(*refs):
        ins, outs = refs[:n], refs[n:2 * n]
        send_sems, recv_sems = refs[2 * n:]
        x, y, c, q, chips = _place()
        cps = [_remote(ins[a], outs[a], send_sems.at[a], recv_sems.at[a], (x, y, 1 - c)) for a in range(n)]
        for cp in cps:
            cp.start()
        for cp in cps:
            cp.wait()

    return pl.pallas_call(
        body, name="grad_pair_share", in_specs=[ANY] * n, out_specs=[ANY] * n,
        out_shape=[jax.ShapeDtypeStruct(h.shape, h.dtype) for h in halves],
        scratch_shapes=[pltpu.SemaphoreType.DMA((n,)), pltpu.SemaphoreType.DMA((n,))],
    )(*halves)


def _allreduce_small(v):
    n_r = v.shape[0]

    def body(v_ref, o_ref, slots, send_sems, recv_sems):
        x, y, c = lax.axis_index("x"), lax.axis_index("y"), lax.axis_index("c")
        me = 4 * x + 2 * y + c
        slots[me] = v_ref[...]
        cps = []
        for r in range(1, 8):
            px = 1 - x if r & 4 else x
            py = 1 - y if r & 2 else y
            pc = 1 - c if r & 1 else c
            cps.append(_remote(v_ref, slots.at[me], send_sems.at[r - 1], recv_sems.at[r - 1], (px, py, pc)))
            cps[-1].start()
        for cp in cps:
            cp.wait()
        tot = slots[0]
        for d in range(1, 8):
            tot = tot + slots[d]
        o_ref[...] = tot

    return pl.pallas_call(
        body, name="allreduce_small",
        in_specs=[pl.BlockSpec(memory_space=pltpu.VMEM)], out_specs=pl.BlockSpec(memory_space=pltpu.VMEM),
        out_shape=jax.ShapeDtypeStruct(v.shape, F32),
        scratch_shapes=[pltpu.VMEM((8, n_r, 128), F32), pltpu.SemaphoreType.DMA((7,)), pltpu.SemaphoreType.DMA((7,))],
        compiler_params=pltpu.CompilerParams(vmem_limit_bytes=VMEM_LIMIT_BYTES),
    )(v)


def _cols_full(g, l):
    s = g[:, l]
    return s.transpose(1, 0, 2).reshape(s.shape[1], -1)


def _rows_full(g, l):
    s = g[:, l]
    return s.reshape(-1, s.shape[-1])


def _cols_split(full):
    r = full.shape[0]
    return full.reshape(r, N_CHIPS, -1).transpose(1, 0, 2)


def _rows_split(full):
    return full.reshape(N_CHIPS, -1, full.shape[-1])


def _pack(parts):
    flat = []
    for a in parts:
        f = a.reshape(-1).astype(F32)
        flat.append(jnp.pad(f, (0, (-f.shape[0]) % 1024)))
    return jnp.concatenate(flat).reshape(-1, 128)


def _unpack(buf, shapes):
    flat = buf.reshape(-1)
    out, off = [], 0
    for s in shapes:
        size = 1
        for d in s:
            size *= d
        out.append(flat[off:off + size].reshape(s))
        off += size + (-size) % 1024
    return out


def kernel(x, p, g_mix_pre, w_in, b_forget, w_conf_dw, conf_ln_g, conf_ln_b, w_conf_pw, w_sc, w_pool, pool_scale, w_out, g_mix_post, g_mlp_pre, w_up, w_down, g_mlp_post, g_ple_pre, w_ple_gate, w_ple_proj, g_ple_post, loss_target, m_g_mix_pre, m_w_in, m_b_forget, m_w_conf_dw, m_conf_ln_g, m_conf_ln_b, m_w_conf_pw, m_w_sc, m_w_pool, m_pool_scale, m_w_out, m_g_mix_post, m_g_mlp_pre, m_w_up, m_w_down, m_g_mlp_post, m_g_ple_pre, m_w_ple_gate, m_w_ple_proj, m_g_ple_post, v_g_mix_pre, v_w_in, v_b_forget, v_w_conf_dw, v_conf_ln_g, v_conf_ln_b, v_w_conf_pw, v_w_sc, v_w_pool, v_pool_scale, v_w_out, v_g_mix_post, v_g_mlp_pre, v_w_up, v_w_down, v_g_mlp_post, v_g_ple_pre, v_w_ple_gate, v_w_ple_proj, v_g_ple_post):
    names = ['g_mix_pre', 'w_in', 'b_forget', 'w_conf_dw', 'conf_ln_g', 'conf_ln_b', 'w_conf_pw', 'w_sc', 'w_pool', 'pool_scale',
             'w_out', 'g_mix_post', 'g_mlp_pre', 'w_up', 'w_down', 'g_mlp_post', 'g_ple_pre', 'w_ple_gate', 'w_ple_proj', 'g_ple_post']
    env = locals()
    wts = {k: env[k] for k in names}
    mom = {k: env["m_" + k] for k in names}
    var = {k: env["v_" + k] for k in names}

    t_len = x.shape[1]
    tb = min(512, t_len)
    tm = tb
    th = min(1024, t_len)
    tq = min(512, max(t_len // 2, 128))
    tt = min(1024, t_len)
    chip = 2 * lax.axis_index("x") + lax.axis_index("y")
    core = lax.axis_index("c")

    big = ['w_in', 'w_conf_pw', 'w_out', 'w_up', 'w_down', 'w_ple_gate', 'w_ple_proj']
    tiny = ['w_conf_dw', 'w_sc']
    own_shards = [wts[k].astype(BF) for k in big] + [wts[k] for k in tiny]

    def chip_order(others, lo, nl):
        return {k: _by_chip(own[lo:lo + nl], oth, chip) for k, own, oth in zip(big + tiny, own_shards, others)}

    def layer_weights(gat, li, l):
        w_full = _cols_full(gat['w_in'], li)
        w_a = jnp.concatenate([w_full[:, :F_OFF], w_full[:, F_OFF + N_HEADS:], w_full[:, F_OFF:F_OFF + N_HEADS],
                               jnp.zeros((D_MODEL, Z_W - D_IN), BF)], axis=1)
        w_bd = jnp.zeros((D_GRP, D_GRP), F32)
        for g in range(4):
            w_bd = lax.dynamic_update_slice(w_bd, wts['w_pool'][l, g], (64 * g, 64 * g))
        row = lambda a: a[l][None, :]
        return dict(
            w_a=w_a, w_dw=jnp.pad(_cols_full(gat['w_conf_dw'], li), ((0, 1), (0, 0))), w_pw=_rows_full(gat['w_conf_pw'], li),
            w_sc=jnp.pad(_cols_full(gat['w_sc'], li), ((0, 5), (0, 0))), w_bd=w_bd.astype(BF),
            w_out=_rows_full(gat['w_out'], li), w_up=_cols_full(gat['w_up'], li), w_down=_rows_full(gat['w_down'], li),
            w_gate=_rows_full(gat['w_ple_gate'], li), w_proj=_cols_full(gat['w_ple_proj'], li),
            b_f=jnp.pad(wts['b_forget'][l], (0, F_PAD - N_HEADS))[None, :],
            ln_g=row(wts['conf_ln_g']), ln_b=row(wts['conf_ln_b']), pool_scale=row(wts['pool_scale']),
            g_mix_pre=row(wts['g_mix_pre']), g_mix_post=row(wts['g_mix_post']), g_mlp_pre=row(wts['g_mlp_pre']),
            g_mlp_post=row(wts['g_mlp_post']), g_ple_pre=row(wts['g_ple_pre']), g_ple_post=row(wts['g_ple_post']))

    lw = [layer_weights(chip_order(_allgather_weights(own_shards, 0, 1), 0, 1), 0, 0)]

    h = x[0]
    saved = []
    for l in range(DEPTH):
        w = lw[l]
        s = dict(h0=h)
        s['zc'], s['qa'], kv, s['zs'], s['zp'], s['zf'] = _mix_in_fwd(t_len, tb, h, w['g_mix_pre'], w['w_a'])
        s['cv'], y_conf = _conf_fwd(t_len, th, s['zc'], w['w_dw'], w['ln_g'], w['ln_b'], w['w_pw'])
        s['ka'], s['va'] = _fox_prep(t_len, s['zf'], kv, w['b_f'])
        if l == 0:
            s['o'], *others = _fox_fwd(t_len, tq, s['qa'], s['ka'], s['va'],
                                       gather=(_GatherPlan(own_shards, 1, DEPTH - 1), own_shards))
            rest = chip_order(others, 1, DEPTH - 1)
            lw += [layer_weights(rest, li, li + 1) for li in range(DEPTH - 1)]
        else:
            (s['o'],) = _fox_fwd(t_len, tq, s['qa'], s['ka'], s['va'])
        (y_sc,) = _sconv_fwd(t_len, th, s['zs'], w['w_sc'])
        (y_pool,) = _pool_fwd(t_len, th, s['zp'], w['w_bd'], w['pool_scale'])
        s['ys'] = [y_conf, s['o'], y_sc, y_pool]
        s['mix'], h = _mix_out_fwd(t_len, tb, s['ys'], h, w['w_out'], w['g_mix_post'])
        s['h1'] = h
        s['up'] = _mlp_up_fwd(t_len, tb, h, w['g_mlp_pre'], w['w_up'])
        s['ff'], h = _mlp_down_fwd(t_len, tb, s['up'], h, w['w_down'], w['g_mlp_post'])
        s['h2'] = h
        s['gpre'], s['pe'], h = _ple_fwd(t_len, tb, h, p[l, 0], w['g_ple_pre'], w['w_gate'], w['w_proj'], w['g_ple_post'])
        saved.append(s)

    dh, loss_part = _loss_call(t_len, tm, h, loss_target[0])

    grads = [None] * DEPTH
    split = dict(w_in=_cols_split, w_conf_pw=_rows_split, w_out=_rows_split, w_up=_cols_split, w_down=_rows_split,
                 w_ple_gate=_rows_split, w_ple_proj=_cols_split)
    c_arr = core.astype(jnp.int32).reshape(1)
    chip_arr = chip.astype(jnp.int32).reshape(1)

    def contrib_of(layers):
        return [jnp.stack([split[k](grads[l][k]) for l in layers], axis=1) for k in big]

    def pair_sums(tag, contrib, peer):
        return [_pair_sum(f"grad_pair_sum_{tag}_{k}", a, b, c_arr) for k, a, b in zip(big, contrib, peer)]

    def finish_reduce(tag, parts, landed):
        halves = [_chip_sum(f"grad_chip_sum_{tag}_{k}", own, r, chip_arr) for k, own, r in zip(big, parts, landed)]
        full = []
        for mine, theirs in zip(halves, _pair_share(halves)):
            both = jnp.stack([mine, theirs])
            full.append(jnp.concatenate([lax.dynamic_index_in_dim(both, core, 0, keepdims=False),
                                         lax.dynamic_index_in_dim(both, 1 - core, 0, keepdims=False)], axis=1))
        return full

    for l in reversed(range(DEPTH)):
        w, s, g = lw[l], saved[l], {}
        contrib13 = contrib_of(range(1, DEPTH)) if l == 0 else []
        dh, g['w_ple_gate'], g['w_ple_proj'], g['g_ple_pre'], g['g_ple_post'], *peer13 = _ple_bwd(
            t_len, tb, dh, s['h2'], s['gpre'], s['pe'], p[l, 0], w['g_ple_pre'], w['w_gate'], w['g_ple_post'],
            comm=(_PairPlan(contrib13), contrib13) if l == 0 else None)
        parts13 = pair_sums("l13", contrib13, peer13) if l == 0 else []
        dff, dup, g['g_mlp_post'] = _mlp_bwd_a(t_len, tm, dh, s['ff'], s['up'], w['w_down'], w['g_mlp_post'])
        dh, hn, g['g_mlp_pre'] = _mlp_bwd_b(t_len, tb, dh, s['h1'], dup, w['w_up'], w['g_mlp_pre'])
        g['w_up'] = _mm_tn("mlp_dw_up", hn, dup, D_MODEL, 1024, tt)
        g['w_down'] = _mm_tn("mlp_dw_down", s['up'], dff, 1024, D_MODEL, tt, pro=lambda u: jnp.square(jnp.maximum(u, 0.0)))
        dy_conf, dy_att, dy_sc, dy_pool, g['w_out'], g['g_mix_post'] = _mix_out_bwd(t_len, tb, dh, s['mix'], s['ys'], w['w_out'], w['g_mix_post'])
        dzc, g['w_conf_dw'], g['conf_ln_g'], g['conf_ln_b'], g['w_conf_pw'] = _conf_bwd(
            t_len, tm, s['zc'], s['cv'], dy_conf, w['w_dw'], w['ln_g'], w['ln_b'], w['w_pw'])
        dzs, g['w_sc'] = _sconv_bwd(t_len, th, s['zs'], dy_sc, w['w_sc'])
        dzp, d_wbd, g['pool_scale'] = _pool_bwd(t_len, th, s['zp'], dy_pool, w['w_bd'], w['pool_scale'])
        g['w_pool'] = jnp.stack([d_wbd[64 * a:64 * (a + 1), 64 * a:64 * (a + 1)] for a in range(4)])
        dqa, dka, dva, *landed13 = _fox_bwd(t_len, tq, s['qa'], s['ka'], s['va'], dy_att, s['o'],
                                            comm=(_ChipPlan(parts13), parts13) if l == 0 else None)
        dzf, d_bf = _fox_post(t_len, dqa, dka, s['zf'], w['b_f'])
        g['b_forget'] = d_bf[0, :N_HEADS]
        dh, xn, dz, g['g_mix_pre'] = _mix_in_bwd(t_len, tb, dh, s['h0'], dzc, dqa, dka, dva, dzs, dzp, dzf, w['g_mix_pre'], w['w_a'])
        d_wa = _mm_tn("mix_dw_in", xn, dz, D_MODEL, Z_W, tt)
        g['w_in'] = jnp.concatenate([d_wa[:, :F_OFF], d_wa[:, Z_F:Z_F + N_HEADS], d_wa[:, F_OFF:Z_F]], axis=1)
        g['w_conf_dw'] = g['w_conf_dw'][:CONF_K]
        g['w_sc'] = g['w_sc'][:SC_K]
        grads[l] = g
    grad_x = dh[None]

    contrib0 = contrib_of([0])
    parts0 = pair_sums("l0", contrib0, _exchange_call("grad_pair_exchange", _PairPlan(contrib0), contrib0))
    first = finish_reduce("l0", parts0, _exchange_call("grad_chip_exchange", _ChipPlan(parts0), parts0))
    later = finish_reduce("l13", parts13, landed13)
    reduced = {k: jnp.concatenate([a, b], axis=0) for k, a, b in zip(big, first, later)}

    small = [k for k in names if k not in big]
    small_shapes = [(DEPTH, *grads[0][k].shape) for k in small]
    packed = _pack([jnp.stack([grads[l][k] for l in range(DEPTH)]) for k in small] + [loss_part])
    summed = _allreduce_small(packed)
    small_sum = _unpack(summed, small_shapes + [(8, 128)])
    loss = small_sum[-1][0, 0]
    for k, a in zip(small, small_sum[:-1]):
        if k in tiny:
            a = lax.dynamic_slice_in_dim(a, chip * 64, 64, axis=2)
        reduced[k] = a.reshape(wts[k].shape)

    delta_w, new_m, new_v = {}, {}, {}
    for k in names:
        shp = wts[k].shape
        as3 = (lambda a: a.reshape(shp[0], -1, shp[-1])) if len(shp) > 2 else (lambda a: a.reshape(1, shp[0], shp[1]))
        d, nm, nv = _adamw_call("adamw_" + k, as3(wts[k]), as3(reduced[k]), as3(mom[k]), as3(var[k]))
        delta_w[k], new_m[k], new_v[k] = d.reshape(shp), nm.reshape(shp), nv.reshape(shp)

    return (loss, grad_x, *[reduced[k] for k in names], *[delta_w[k] for k in names],
            *[new_m[k] for k in names], *[new_v[k] for k in names])
```

```python
import functools

import jax
import jax.numpy as jnp
from jax import lax
from jax.experimental import pallas as pl
from jax.experimental.pallas import tpu as pltpu

F32 = jnp.float32
BF = jnp.bfloat16

DEPTH = 4
D_MODEL = 1024
D_GRP = 256
HEAD_DIM = 64
N_HEADS = 4
CONF_K = 31
SC_K = 3
D_FF = 4096
D_PLE = 256
EPS = 1e-6
N_CHIPS = 4
Z_CONF, Z_QKV, Z_SC, Z_POOL, Z_F = 0, 512, 1280, 2048, 2304
Z_W = 2432
F_PAD = 128
D_IN = 2308
F_OFF = 1280

ADAM_LR, ADAM_B1, ADAM_B2, ADAM_EPS, ADAM_WD, ADAM_STEP = 0.001, 0.9, 0.999, 1e-08, 0.01, 10

VMEM_LIMIT_BYTES = 56 * 1024 * 1024
HALO = 32
NEG = -1e30
MESH_T = pl.DeviceIdType.MESH


def _cparams(sem=None):
    return pltpu.CompilerParams(dimension_semantics=sem, vmem_limit_bytes=VMEM_LIMIT_BYTES)


def _dot(a, b):
    return jnp.dot(a, b, preferred_element_type=F32)


def _dot_nt(a, b):
    return lax.dot_general(a, b, (((1,), (1,)), ((), ())), preferred_element_type=F32)


def _dot_tn(a, b):
    return lax.dot_general(a, b, (((0,), (0,)), ((), ())), preferred_element_type=F32)


def _sig(x):
    return jax.nn.sigmoid(x)


def _rms_fwd(x, g):
    r = lax.rsqrt(jnp.mean(x * x, axis=-1, keepdims=True) + EPS)
    return x * r * g


def _rms_bwd(x, g, dy):
    r = lax.rsqrt(jnp.mean(x * x, axis=-1, keepdims=True) + EPS)
    xh = x * r
    dg = jnp.sum(dy * xh, axis=0, keepdims=True)
    dxh = dy * g
    dx = r * (dxh - xh * jnp.mean(dxh * xh, axis=-1, keepdims=True))
    return dx, dg


def _back(ext, d):
    return ext if d == 0 else pltpu.roll(ext, d, 0)


def _ahead(ext, d):
    return ext if d == 0 else pltpu.roll(ext, ext.shape[0] - d, 0)


def _rows_call(name, fn, t_len, tm, rows, fulls, out_rows, out_accs=(), prevs=(), nexts=(), comm=None):
    plan, comm_in = comm if comm else (None, [])
    n_comm = len(comm_in)
    n = t_len // tm
    hb = tm // HALO
    nhb = t_len // HALO
    n_rows, n_prev, n_next, n_full = len(rows), len(prevs), len(nexts), len(fulls)
    in_specs = [pl.BlockSpec((tm, a.shape[1]), lambda i: (i, 0)) for a in rows]
    in_specs += [pl.BlockSpec((HALO, a.shape[1]), lambda i: (jnp.maximum(i * hb - 1, 0), 0)) for a in prevs]
    in_specs += [pl.BlockSpec((HALO, a.shape[1]), lambda i: (jnp.minimum((i + 1) * hb, nhb - 1), 0)) for a in nexts]
    in_specs += [pl.BlockSpec(a.shape, lambda i, nd=a.ndim: (0,) * nd, pipeline_mode=pl.Buffered(1)) for a in fulls]
    out_shape = [jax.ShapeDtypeStruct((t_len, c), dt) for c, dt in out_rows]
    out_shape += [jax.ShapeDtypeStruct(s, F32) for s in out_accs]
    out_specs = [pl.BlockSpec((tm, c), lambda i: (i, 0)) for c, _ in out_rows]
    out_specs += [pl.BlockSpec(s, lambda i, nd=len(s): (0,) * nd) for s in out_accs]
    n_in = n_rows + n_prev + n_next + n_full
    n_ro = len(out_rows)

    n_out = n_ro + len(out_accs)

    def body(*refs):
        i = pl.program_id(0)
        ins, outs = refs[:n_in], refs[n_in + n_comm:n_in + n_comm + n_out]
        c_ins, c_outs, sems = refs[n_in:n_in + n_comm], refs[n_in + n_comm + n_out:n_in + 2 * n_comm + n_out], refs[n_in + 2 * n_comm + n_out:]
        if plan:
            @pl.when(i == 0)
            def _():
                plan.start(c_ins, c_outs, *sems)

        rv = [r[...] for r in ins[:n_rows]]
        pv = [r[...] for r in ins[n_rows:n_rows + n_prev]]
        nv = [r[...] for r in ins[n_rows + n_prev:n_rows + n_prev + n_next]]
        fv = list(ins[n_rows + n_prev + n_next:])
        ro, ao = fn(i, n, rv, pv, nv, fv)
        for r, v in zip(outs[:n_ro], ro):
            r[...] = v.astype(r.dtype)
        if out_accs:
            acc = outs[n_ro:]

            @pl.when(i == 0)
            def _():
                for r in acc:
                    r[...] = jnp.zeros(r.shape, r.dtype)

            for r, v in zip(acc, ao):
                r[...] += v
        if plan:
            @pl.when(i == n - 1)
            def _():
                plan.finish(c_ins, c_outs, *sems)

    res = pl.pallas_call(
        body, name=name, grid=(n,), in_specs=in_specs + [ANY] * n_comm, out_specs=out_specs + [ANY] * n_comm,
        out_shape=out_shape + (plan.out_shape if plan else []), scratch_shapes=plan.scratch if plan else [],
        compiler_params=_cparams(("arbitrary",)),
    )(*rows, *prevs, *nexts, *fulls, *comm_in)
    return res


def _mm_tn(name, x, y, tk, tn, tt, pro=None):
    t_len, k_dim = x.shape
    n_dim = y.shape[1]

    def body(x_ref, y_ref, o_ref):
        @pl.when(pl.program_id(2) == 0)
        def _():
            o_ref[...] = jnp.zeros(o_ref.shape, o_ref.dtype)

        xv = x_ref[...]
        if pro is not None:
            xv = pro(xv)
        o_ref[...] += _dot_tn(xv.astype(BF), y_ref[...].astype(BF))

    return pl.pallas_call(
        body, name=name, grid=(k_dim // tk, n_dim // tn, t_len // tt),
        in_specs=[pl.BlockSpec((tt, tk), lambda a, b, t: (t, a)), pl.BlockSpec((tt, tn), lambda a, b, t: (t, b))],
        out_specs=pl.BlockSpec((tk, tn), lambda a, b, t: (a, b)),
        out_shape=jax.ShapeDtypeStruct((k_dim, n_dim), F32),
        compiler_params=_cparams(("parallel", "parallel", "arbitrary")),
    )(x, y)


QK_W = 128
ATT_W = N_HEADS * QK_W
C_COL = HEAD_DIM
ONE_COL = HEAD_DIM + 3


def _lane(shape):
    return lax.broadcasted_iota(jnp.int32, shape, 1)


def _head_low(a256, h):
    pair = a256[:, QK_W * (h // 2):QK_W * (h // 2 + 1)]
    return pltpu.roll(pair, HEAD_DIM, 1) if h % 2 else pair


def _heads_spread(a256):
    low = _lane((a256.shape[0], QK_W)) < HEAD_DIM
    return jnp.concatenate([jnp.where(low, _head_low(a256, h), 0.0) for h in range(N_HEADS)], axis=1)


def _heads_packed(a512):
    low = _lane((a512.shape[0], QK_W)) < HEAD_DIM
    out = []
    for pair in range(N_HEADS // 2):
        even = a512[:, QK_W * 2 * pair:QK_W * (2 * pair + 1)]
        odd = a512[:, QK_W * (2 * pair + 1):QK_W * (2 * pair + 2)]
        out.append(jnp.where(low, even, pltpu.roll(odd, HEAD_DIM, 1)))
    return jnp.concatenate(out, axis=1)


def _mix_in_fwd(t_len, tm, h, g_pre, w_a):
    def fn(i, n, rows, prevs, nexts, fulls):
        (hv,), (g, w) = rows, fulls
        z = _dot(_rms_fwd(hv, g[...]).astype(BF), w[...])
        lane = _lane((tm, QK_W))
        ones = jnp.where(lane < ONE_COL, 1.0, 0.0)
        zq = z[:, Z_QKV:Z_QKV + D_GRP]
        qa = jnp.concatenate([jnp.where(lane < HEAD_DIM, _head_low(zq, a) * (HEAD_DIM ** -0.5), ones) for a in range(N_HEADS)], axis=1)
        return [z[:, Z_CONF:Z_QKV], qa, z[:, Z_QKV + D_GRP:Z_SC], z[:, Z_SC:Z_POOL], z[:, Z_POOL:Z_F], z[:, Z_F:Z_W]], []

    return _rows_call("mix_in_fwd", fn, t_len, tm, [h], [g_pre, w_a],
                      [(512, F32), (ATT_W, BF), (2 * D_GRP, BF), (768, F32), (256, F32), (F_PAD, F32)])


def _mix_in_bwd(t_len, tm, dh, h, dzc, dqa, dka, dva, dzs, dzp, dzf, g_pre, w_a):
    def fn(i, n, rows, prevs, nexts, fulls):
        dhv, hv, a, dq, dk, dv, c, d, e = rows
        g, w = fulls
        b = jnp.concatenate([_heads_packed(dq), _heads_packed(dk), _heads_packed(dv)], axis=1).astype(BF)
        dz = jnp.concatenate([a, b, c, d, e], axis=1)
        dxn = _dot_nt(dz, w[...])
        dx, dg = _rms_bwd(hv, g[...], dxn)
        xn = _rms_fwd(hv, g[...])
        return [dhv + dx, xn, dz], [dg]

    return _rows_call("mix_in_bwd", fn, t_len, tm, [dh, h, dzc, dqa, dka, dva, dzs, dzp, dzf], [g_pre, w_a],
                      [(D_MODEL, F32), (D_MODEL, BF), (Z_W, BF)], [(1, D_MODEL)])


def _glu_ext(i, zc, zc_prev):
    ext = jnp.concatenate([zc_prev, zc], axis=0)
    u = ext[:, :D_GRP] * _sig(ext[:, D_GRP:])
    row = lax.broadcasted_iota(jnp.int32, u.shape, 0)
    return jnp.where((row >= HALO) | (i > 0), u, 0.0)


def _conf_fwd(t_len, tm, zc, w_dw, ln_g, ln_b, w_pw):
    def fn(i, n, rows, prevs, nexts, fulls):
        (zv,), (zp,) = rows, prevs
        wdw, lg, lb, wpw = fulls
        u = _glu_ext(i, zv, zp)
        cv = jnp.zeros((tm, D_GRP), F32)
        for k in range(CONF_K):
            cv = cv + wdw[k:k + 1, :] * _back(u, CONF_K - 1 - k)[HALO:, :]
        mu = jnp.mean(cv, axis=-1, keepdims=True)
        xc = cv - mu
        ln = xc * lax.rsqrt(jnp.mean(xc * xc, axis=-1, keepdims=True) + EPS) * lg[...] + lb[...]
        s = ln * _sig(ln)
        return [cv, _dot(s.astype(BF), wpw[...])], []

    return _rows_call("conf_fwd", fn, t_len, tm, [zc], [w_dw, ln_g, ln_b, w_pw], [(D_GRP, F32), (D_GRP, BF)], prevs=[zc])


def _conf_bwd(t_len, tm, zc, cv, dy, w_dw, ln_g, ln_b, w_pw):
    def fn(i, n, rows, prevs, nexts, fulls):
        zv, cvv, dyv = rows
        (zp,) = prevs
        cvn, dyn = nexts
        wdw, lg, lb, wpw = fulls
        cve = jnp.concatenate([cvv, cvn], axis=0)
        dye = jnp.concatenate([dyv, dyn], axis=0)
        mu = jnp.mean(cve, axis=-1, keepdims=True)
        xc = cve - mu
        rs = lax.rsqrt(jnp.mean(xc * xc, axis=-1, keepdims=True) + EPS)
        xh = xc * rs
        ln = xh * lg[...] + lb[...]
        sg = _sig(ln)
        s = ln * sg
        ds = _dot_nt(dye.astype(BF), wpw[...])
        dln = ds * (sg * (1.0 + ln * (1.0 - sg)))
        dxh = dln * lg[...]
        dcv = rs * (dxh - jnp.mean(dxh, axis=-1, keepdims=True) - xh * jnp.mean(dxh * xh, axis=-1, keepdims=True))
        row = lax.broadcasted_iota(jnp.int32, dcv.shape, 0)
        dcv = jnp.where((row < tm) | (i < n - 1), dcv, 0.0)
        d_lg = jnp.sum((dln * xh)[:tm], axis=0, keepdims=True)
        d_lb = jnp.sum(dln[:tm], axis=0, keepdims=True)
        d_wpw = _dot_tn(s[:tm].astype(BF), dyv.astype(BF))
        u = _glu_ext(i, zv, zp)
        dcv_cur = dcv[:tm]
        du = jnp.zeros((tm, D_GRP), F32)
        d_wdw = jnp.zeros((32, D_GRP), F32)
        krow = lax.broadcasted_iota(jnp.int32, (32, D_GRP), 0)
        for k in range(CONF_K):
            d = CONF_K - 1 - k
            du = du + wdw[k:k + 1, :] * _ahead(dcv, d)[:tm, :]
            tap = _back(u, d)[HALO:, :]
            d_wdw = d_wdw + jnp.where(krow == k, jnp.sum(dcv_cur * tap, axis=0, keepdims=True), 0.0)
        a, b = zv[:, :D_GRP], zv[:, D_GRP:]
        sb = _sig(b)
        dz = jnp.concatenate([du * sb, du * a * sb * (1.0 - sb)], axis=1)
        return [dz], [d_wdw, d_lg, d_lb, d_wpw]

    return _rows_call("conf_bwd", fn, t_len, tm, [zc, cv, dy], [w_dw, ln_g, ln_b, w_pw], [(512, BF)],
                      [(32, D_GRP), (1, D_GRP), (1, D_GRP), (D_GRP, D_GRP)], prevs=[zc], nexts=[cv, dy])


def _sc_ext(i, zs, zs_prev):
    ext = jnp.concatenate([zs_prev, zs], axis=0)
    e = ext[:, 2 * D_GRP:] * ext[:, :D_GRP]
    row = lax.broadcasted_iota(jnp.int32, e.shape, 0)
    return jnp.where((row >= HALO) | (i > 0), e, 0.0)


def _sconv_fwd(t_len, tm, zs, w_sc):
    def fn(i, n, rows, prevs, nexts, fulls):
        (zv,), (zp,), (w,) = rows, prevs, fulls
        e = _sc_ext(i, zv, zp)
        cv = jnp.zeros((tm, D_GRP), F32)
        for k in range(SC_K):
            cv = cv + w[k:k + 1, :] * _back(e, SC_K - 1 - k)[HALO:, :]
        return [zv[:, D_GRP:2 * D_GRP] * cv], []

    return _rows_call("sconv_fwd", fn, t_len, tm, [zs], [w_sc], [(D_GRP, BF)], prevs=[zs])


def _sconv_bwd(t_len, tm, zs, dy, w_sc):
    def fn(i, n, rows, prevs, nexts, fulls):
        zv, dyv = rows
        (zp,) = prevs
        zn, dyn = nexts
        (w,) = fulls
        e = _sc_ext(i, zv, zp)
        taps = [_back(e, SC_K - 1 - k)[HALO:, :] for k in range(SC_K)]
        cv = w[0:1, :] * taps[0] + w[1:2, :] * taps[1] + w[2:3, :] * taps[2]
        bg = zv[:, D_GRP:2 * D_GRP]
        dcv = jnp.concatenate([dyv * bg, dyn * zn[:, D_GRP:2 * D_GRP]], axis=0)
        row = lax.broadcasted_iota(jnp.int32, dcv.shape, 0)
        dcv = jnp.where((row < tm) | (i < n - 1), dcv, 0.0)
        de = jnp.zeros((tm, D_GRP), F32)
        d_w = jnp.zeros((8, D_GRP), F32)
        krow = lax.broadcasted_iota(jnp.int32, (8, D_GRP), 0)
        for k in range(SC_K):
            de = de + w[k:k + 1, :] * _ahead(dcv, SC_K - 1 - k)[:tm, :]
            d_w = d_w + jnp.where(krow == k, jnp.sum(dcv[:tm] * taps[k], axis=0, keepdims=True), 0.0)
        dz = jnp.concatenate([de * zv[:, 2 * D_GRP:], dyv * cv, de * zv[:, :D_GRP]], axis=1)
        return [dz], [d_w]

    return _rows_call("sconv_bwd", fn, t_len, tm, [zs, dy], [w_sc], [(768, BF)], [(8, D_GRP)], prevs=[zs], nexts=[zs, dy])


def _pool_window(shape):
    grp = lax.broadcasted_iota(jnp.int32, shape, 1) // 64
    return grp, jnp.where(grp == 0, 2.0, jnp.where(grp == 1, 4.0, jnp.where(grp == 2, 8.0, 16.0)))


def _pool_d(i, tm, zv, zp):
    ext = jnp.concatenate([zp, zv], axis=0)
    row = lax.broadcasted_iota(jnp.int32, ext.shape, 0)
    ext = jnp.where((row >= HALO) | (i > 0), ext, 0.0)
    s2 = ext + _back(ext, 1)
    s4 = s2 + _back(s2, 2)
    s8 = s4 + _back(s4, 4)
    s16 = s8 + _back(s8, 8)
    grp, win = _pool_window((tm, D_GRP))
    sel = jnp.where(grp == 0, s2[HALO:], jnp.where(grp == 1, s4[HALO:], jnp.where(grp == 2, s8[HALO:], s16[HALO:])))
    pos = (i * tm + lax.broadcasted_iota(jnp.int32, (tm, D_GRP), 0) + 1).astype(F32)
    return sel / jnp.minimum(pos, win) - zv


def _sconv_pool_fwd(t_len, tm, zs, zpool, w_sc, w_bd, scale):
    def fn(i, n, rows, prevs, nexts, fulls):
        (zv, pv), (zp, pp) = rows, prevs
        w, wb, sc = fulls
        e = _sc_ext(i, zv, zp)
        cv = jnp.zeros((tm, D_GRP), F32)
        for k in range(SC_K):
            cv = cv + w[k:k + 1, :] * _back(e, SC_K - 1 - k)[HALO:, :]
        d = _pool_d(i, tm, pv, pp)
        return [zv[:, D_GRP:2 * D_GRP] * cv, _dot(d.astype(BF), wb[...]) * sc[...]], []

    return _rows_call("sconv_pool_fwd", fn, t_len, tm, [zs, zpool], [w_sc, w_bd, scale], [(D_GRP, BF), (D_GRP, BF)],
                      prevs=[zs, zpool])


def _pool_bwd(t_len, tm, zpool, dy, w_bd, scale):
    def fn(i, n, rows, prevs, nexts, fulls):
        zv, dyv = rows
        (zp,) = prevs
        (dyn,) = nexts
        w, sc = fulls
        d = _pool_d(i, tm, zv, zp)
        lin = _dot(d.astype(BF), w[...])
        d_sc = jnp.sum(dyv * lin, axis=0, keepdims=True)
        dye = jnp.concatenate([dyv, dyn], axis=0) * sc[...]
        d_w = _dot_tn(d.astype(BF), dye[:tm].astype(BF))
        dd = _dot_nt(dye.astype(BF), w[...])
        row = lax.broadcasted_iota(jnp.int32, dd.shape, 0)
        dd = jnp.where((row < tm) | (i < n - 1), dd, 0.0)
        grp, win = _pool_window(dd.shape)
        pos = (i * tm + row + 1).astype(F32)
        ddc = dd / jnp.minimum(pos, win)
        f2 = ddc + _ahead(ddc, 1)
        f4 = f2 + _ahead(f2, 2)
        f8 = f4 + _ahead(f4, 4)
        f16 = f8 + _ahead(f8, 8)
        sel = jnp.where(grp == 0, f2, jnp.where(grp == 1, f4, jnp.where(grp == 2, f8, f16)))
        return [(sel - dd)[:tm]], [d_w, d_sc]

    return _rows_call("pool_bwd", fn, t_len, tm, [zpool, dy], [w_bd, scale], [(D_GRP, BF)],
                      [(D_GRP, D_GRP), (1, D_GRP)], prevs=[zpool], nexts=[dy])


def _mix_cat(y_conf, o_att, y_sc, y_pool):
    return jnp.concatenate([y_conf, _heads_packed(o_att).astype(BF), y_sc, y_pool], axis=1)


def _mix_out_fwd(t_len, tm, ys, h, w_out, g_post):
    def fn(i, n, rows, prevs, nexts, fulls):
        y0, y1, y2, y3, hv = rows
        w, g = fulls
        mix = _dot(_mix_cat(y0, y1, y2, y3), w[...])
        return [mix, hv + _rms_fwd(mix, g[...])], []

    return _rows_call("mix_out_fwd", fn, t_len, tm, [*ys, h], [w_out, g_post], [(D_MODEL, F32), (D_MODEL, F32)])


def _mix_out_bwd(t_len, tm, dh, mix, ys, w_out, g_post):
    def fn(i, n, rows, prevs, nexts, fulls):
        dhv, mv, y0, y1, y2, y3 = rows
        w, g = fulls
        dmix, dg = _rms_bwd(mv, g[...], dhv)
        dmb = dmix.astype(BF)
        dcat = _dot_nt(dmb, w[...])
        d_w = _dot_tn(_mix_cat(y0, y1, y2, y3), dmb)
        return [dcat[:, :256], _heads_spread(dcat[:, 256:512]), dcat[:, 512:768], dcat[:, 768:]], [d_w, dg]

    return _rows_call("mix_out_bwd", fn, t_len, tm, [dh, mix, *ys], [w_out, g_post],
                      [(D_GRP, F32), (ATT_W, F32), (D_GRP, F32), (D_GRP, F32)], [(D_MODEL, D_MODEL), (1, D_MODEL)])


def _log_sigmoid(x):
    return jnp.minimum(x, 0.0) - jnp.log(1.0 + jnp.exp(-jnp.abs(x)))


SCAN_BLK = 256


def _fox_prep(t_len, zf, kv, b_f):
    blk = min(SCAN_BLK, t_len)

    def body(zf_ref, kv_ref, b_ref, ka_ref, va_ref, carry_s):
        @pl.when(pl.program_id(0) == 0)
        def _():
            carry_s[...] = jnp.zeros(carry_s.shape, F32)

        tri = (lax.broadcasted_iota(jnp.int32, (blk, blk), 0) >= lax.broadcasted_iota(jnp.int32, (blk, blk), 1)).astype(F32)
        lf = _log_sigmoid(zf_ref[...] + b_ref[...])
        cs = jnp.dot(tri, lf, precision=lax.Precision.HIGHEST, preferred_element_type=F32) + carry_s[...]
        carry_s[...] = cs[blk - 1:blk, :]
        neg = -cs
        hi = neg.astype(BF).astype(F32)
        mid = (neg - hi).astype(BF).astype(F32)
        lo = ((neg - hi) - mid).astype(BF).astype(F32)
        kvv = kv_ref[...].astype(F32)
        lane = _lane((blk, QK_W))
        ka, va = [], []
        for a in range(N_HEADS):
            terms = jnp.where(lane == C_COL, hi[:, a:a + 1], jnp.where(lane == C_COL + 1, mid[:, a:a + 1], jnp.where(
                lane == C_COL + 2, lo[:, a:a + 1], jnp.where(lane == ONE_COL, HEAD_DIM ** 0.5, 0.0))))
            ka.append(jnp.where(lane < HEAD_DIM, _head_low(kvv[:, :D_GRP], a), terms))
            va.append(jnp.where(lane < HEAD_DIM, _head_low(kvv[:, D_GRP:], a), 0.0))
        ka_ref[...] = jnp.concatenate(ka, axis=1).astype(BF)
        va_ref[...] = jnp.concatenate(va, axis=1).astype(BF)

    row = lambda w: pl.BlockSpec((blk, w), lambda i: (i, 0))
    return pl.pallas_call(
        body, name="fox_prep", grid=(t_len // blk,),
        in_specs=[row(F_PAD), row(2 * D_GRP), pl.BlockSpec((1, F_PAD), lambda i: (0, 0))],
        out_specs=[row(ATT_W), row(ATT_W)],
        out_shape=[jax.ShapeDtypeStruct((t_len, ATT_W), BF)] * 2,
        scratch_shapes=[pltpu.VMEM((1, F_PAD), F32)],
        compiler_params=_cparams(("arbitrary",)),
    )(zf, kv, b_f)


def _fox_post(t_len, dqa, dka, zf, b_f):
    blk = min(SCAN_BLK, t_len)
    nb = t_len // blk

    def body(dq_ref, dk_ref, zf_ref, b_ref, dz_ref, db_ref, carry_s):
        @pl.when(pl.program_id(0) == 0)
        def _():
            carry_s[...] = jnp.zeros(carry_s.shape, F32)
            db_ref[...] = jnp.zeros(db_ref.shape, F32)

        tri = (lax.broadcasted_iota(jnp.int32, (blk, blk), 0) <= lax.broadcasted_iota(jnp.int32, (blk, blk), 1)).astype(F32)
        lane = _lane((blk, F_PAD))
        dc = jnp.zeros((blk, F_PAD), F32)
        for a in range(N_HEADS):
            head = slice(QK_W * a, QK_W * (a + 1))
            col = dq_ref[:, head][:, ONE_COL:ONE_COL + 1] - dk_ref[:, head][:, C_COL:C_COL + 1]
            dc = jnp.where(lane == a, col, dc)
        dlf = jnp.dot(tri, dc, precision=lax.Precision.HIGHEST, preferred_element_type=F32) + carry_s[...]
        carry_s[...] = dlf[0:1, :]
        dz = dlf * _sig(-(zf_ref[...] + b_ref[...]))
        dz_ref[...] = dz.astype(dz_ref.dtype)
        db_ref[...] += jnp.sum(dz, axis=0, keepdims=True)

    row = lambda w: pl.BlockSpec((blk, w), lambda i: (nb - 1 - i, 0))
    one = pl.BlockSpec((1, F_PAD), lambda i: (0, 0))
    return pl.pallas_call(
        body, name="fox_post", grid=(nb,),
        in_specs=[row(ATT_W), row(ATT_W), row(F_PAD), one], out_specs=[row(F_PAD), one],
        out_shape=(jax.ShapeDtypeStruct((t_len, F_PAD), BF), jax.ShapeDtypeStruct((1, F_PAD), F32)),
        scratch_shapes=[pltpu.VMEM((1, F_PAD), F32)],
        compiler_params=_cparams(("arbitrary",)),
    )(dqa, dka, zf, b_f)


def _tri_mask(tq, key_rows):
    r = lax.broadcasted_iota(jnp.int32, (tq, tq), 0)
    c = lax.broadcasted_iota(jnp.int32, (tq, tq), 1)
    return (r <= c) if key_rows else (r >= c)


LSE_COL = HEAD_DIM


def _fox_fwd(t_len, tq, qa, ka, va, gather=None):
    nq = t_len // tq
    plan, shards = gather if gather else (None, [])
    n = len(shards)

    def body(q_ref, k_ref, v_ref, *rest):
        ins, o_ref, outs, sems = rest[:n], rest[n], rest[n + 1:2 * n + 1], rest[2 * n + 1:]
        h, i = pl.program_id(0), pl.program_id(1)
        if plan:
            @pl.when((h == 0) & (i == 0))
            def _():
                plan.start(ins, outs, *sems)

        q = q_ref[...]

        def tile(j, carry, diagonal):
            m, l, acc = carry
            rows = pl.ds(pl.multiple_of(j * tq, tq), tq)
            s = _dot_nt(q, k_ref[rows, :])
            if diagonal:
                s = jnp.where(_tri_mask(tq, False), s, NEG)
            m_new = jnp.maximum(m, jnp.max(s, axis=-1, keepdims=True))
            alpha = jnp.exp(m - m_new)
            p = jnp.exp(s - m_new)
            l = alpha * l + jnp.sum(p, axis=-1, keepdims=True)
            acc = alpha * acc + _dot(p.astype(BF), v_ref[rows, :])
            return m_new, l, acc

        init = (jnp.full((tq, 1), NEG, F32), jnp.zeros((tq, 1), F32), jnp.zeros((tq, QK_W), F32))
        carry = lax.fori_loop(0, i, lambda j, c: tile(j, c, False), init)
        m, l, acc = tile(i, carry, True)
        o_ref[...] = jnp.where(_lane((tq, QK_W)) == LSE_COL, m + jnp.log(l), acc / l)
        if plan:
            @pl.when((h == N_HEADS - 1) & (i == nq - 1))
            def _():
                plan.finish(ins, outs, *sems)

    tile_spec = pl.BlockSpec((tq, QK_W), lambda h, i: (i, h))
    head_spec = pl.BlockSpec((t_len, QK_W), lambda h, i: (0, h))
    return pl.pallas_call(
        body, name="fox_fwd_gather" if plan else "fox_fwd", grid=(N_HEADS, nq),
        in_specs=[tile_spec, head_spec, head_spec] + [ANY] * n, out_specs=[tile_spec] + [ANY] * n,
        out_shape=[jax.ShapeDtypeStruct((t_len, ATT_W), F32)] + (plan.out_shape if plan else []),
        scratch_shapes=plan.scratch if plan else [],
        compiler_params=_cparams(("arbitrary", "arbitrary")),
    )(qa, ka, va, *shards)


def _fox_bwd(t_len, tq, qa, ka, va, do, o, comm=None):
    nq = t_len // tq
    plan, comm_in = comm if comm else (None, [])
    n = len(comm_in)

    def body(q_ref, k_ref, v_ref, do_ref, o_ref, *rest):
        c_ins, (dq_ref, dk_ref, dv_ref), c_outs = rest[:n], rest[n:n + 3], rest[n + 3:2 * n + 3]
        acc_s, dl_s, lse_s = rest[2 * n + 3:2 * n + 6]
        sems = rest[2 * n + 6:]
        hd, j = pl.program_id(0), pl.program_id(1)
        if plan:
            @pl.when((hd == 0) & (j == 0))
            def _():
                plan.start(c_ins, c_outs, *sems)

        @pl.when(j == 0)
        def _():
            acc_s[...] = jnp.zeros(acc_s.shape, F32)
            row_of = lambda sel, a: lax.dot_general(sel, a, (((1,), (1,)), ((), ())), precision=lax.Precision.HIGHEST,
                                                    preferred_element_type=F32)
            dl_s[...] = row_of(jnp.ones((8, QK_W), F32), do_ref[...] * o_ref[...])
            lse_s[...] = row_of(jnp.where(_lane((8, QK_W)) == LSE_COL, 1.0, 0.0), o_ref[...])

        kt, vt = k_ref[...], v_ref[...]

        def tile(i, carry, diagonal):
            dk, dv = carry
            rows = pl.ds(pl.multiple_of(i * tq, tq), tq)
            qt = q_ref[rows, :]
            dob = do_ref[rows, :].astype(BF)
            pt = jnp.exp(_dot_nt(kt, qt) - lse_s[0:1, rows])
            if diagonal:
                pt = jnp.where(_tri_mask(tq, True), pt, 0.0)
            dst = (pt * (_dot_nt(vt, dob) - dl_s[0:1, rows])).astype(BF)
            acc_s[rows, :] += _dot_tn(dst, kt)
            return dk + _dot(dst, qt), dv + _dot(pt.astype(BF), dob)

        carry = tile(j, (jnp.zeros((tq, QK_W), F32), jnp.zeros((tq, QK_W), F32)), True)
        dk, dv = lax.fori_loop(j + 1, nq, lambda i, c: tile(i, c, False), carry)
        dk_ref[...] = dk
        dv_ref[...] = dv

        @pl.when(j == nq - 1)
        def _():
            dq_ref[...] = acc_s[...] * (HEAD_DIM ** -0.5)

        if plan:
            @pl.when((hd == N_HEADS - 1) & (j == nq - 1))
            def _():
                plan.finish(c_ins, c_outs, *sems)

    head_spec = pl.BlockSpec((t_len, QK_W), lambda h, j: (0, h))
    tile_spec = pl.BlockSpec((tq, QK_W), lambda h, j: (j, h))
    return pl.pallas_call(
        body, name="fox_bwd_exchange" if plan else "fox_bwd", grid=(N_HEADS, nq),
        in_specs=[head_spec, tile_spec, tile_spec, head_spec, head_spec] + [ANY] * n,
        out_specs=[head_spec, tile_spec, tile_spec] + [ANY] * n,
        out_shape=[jax.ShapeDtypeStruct((t_len, ATT_W), F32)] * 3 + (plan.out_shape if plan else []),
        scratch_shapes=[pltpu.VMEM((t_len, QK_W), F32), pltpu.VMEM((8, t_len), F32), pltpu.VMEM((8, t_len), F32)]
        + (plan.scratch if plan else []),
        compiler_params=_cparams(("arbitrary", "arbitrary")),
    )(qa, ka, va, do, o, *comm_in)


def _mlp_up_fwd(t_len, tm, h, g_pre, w_up):
    def fn(i, n, rows, prevs, nexts, fulls):
        (hv,), (g, w) = rows, fulls
        return [_dot(_rms_fwd(hv, g[...]).astype(BF), w[...])], []

    return _rows_call("mlp_up_fwd", fn, t_len, tm, [h], [g_pre, w_up], [(D_FF, F32)])[0]


def _mlp_down_fwd(t_len, tm, up, h, w_down, g_post):
    def fn(i, n, rows, prevs, nexts, fulls):
        (uv, hv), (w, g) = rows, fulls
        a = jnp.square(jnp.maximum(uv, 0.0))
        ff = _dot(a.astype(BF), w[...])
        return [ff, hv + _rms_fwd(ff, g[...])], []

    return _rows_call("mlp_down_fwd", fn, t_len, tm, [up, h], [w_down, g_post], [(D_MODEL, F32), (D_MODEL, F32)])


def _mlp_bwd_a(t_len, tm, dh, ff, up, w_down, g_post):
    def fn(i, n, rows, prevs, nexts, fulls):
        (dhv, fv, uv), (w, g) = rows, fulls
        dff, dg = _rms_bwd(fv, g[...], dhv)
        dfb = dff.astype(BF)
        dup = _dot_nt(dfb, w[...]) * (2.0 * jnp.maximum(uv, 0.0))
        return [dfb, dup], [dg]

    return _rows_call("mlp_bwd_a", fn, t_len, tm, [dh, ff, up], [w_down, g_post], [(D_MODEL, BF), (D_FF, BF)], [(1, D_MODEL)])


def _mlp_bwd_b(t_len, tm, dh, h, dup, w_up, g_pre):
    def fn(i, n, rows, prevs, nexts, fulls):
        (dhv, hv, duv), (w, g) = rows, fulls
        dhn = _dot_nt(duv, w[...])
        dx, dg = _rms_bwd(hv, g[...], dhn)
        return [dhv + dx, _rms_fwd(hv, g[...])], [dg]

    return _rows_call("mlp_bwd_b", fn, t_len, tm, [dh, h, dup], [w_up, g_pre], [(D_MODEL, F32), (D_MODEL, BF)], [(1, D_MODEL)])


def _ple_fwd(t_len, tm, h, p_i, g_pre, w_gate, w_proj, g_post):
    def fn(i, n, rows, prevs, nexts, fulls):
        (hv, pv), (g, wg, wp, gp) = rows, fulls
        gpre = _dot(_rms_fwd(hv, g[...]).astype(BF), wg[...])
        pe = _dot(pv.astype(BF), wp[...])
        return [gpre, pe, hv + _rms_fwd(pe * _sig(gpre), gp[...])], []

    return _rows_call("ple_fwd", fn, t_len, tm, [h, p_i], [g_pre, w_gate, w_proj, g_post], [(D_MODEL, F32)] * 3)


def _ple_bwd(t_len, tm, dh, h, gpre, pe, p_i, g_pre, w_gate, g_post, comm=None):
    def fn(i, n, rows, prevs, nexts, fulls):
        (dhv, hv, gv, pev, pv), (g, wg, gp) = rows, fulls
        sg = _sig(gv)
        de, d_gp = _rms_bwd(pev * sg, gp[...], dhv)
        dpe = (de * sg).astype(BF)
        dgate = (de * pev * sg * (1.0 - sg)).astype(BF)
        d_wp = _dot_tn(pv.astype(BF), dpe)
        hn = _rms_fwd(hv, g[...])
        d_wg = _dot_tn(hn.astype(BF), dgate)
        dx, d_g = _rms_bwd(hv, g[...], _dot_nt(dgate, wg[...]))
        return [dhv + dx], [d_wg, d_wp, d_g, d_gp]

    return _rows_call("ple_bwd_exchange" if comm else "ple_bwd", fn, t_len, tm, [dh, h, gpre, pe, p_i], [g_pre, w_gate, g_post],
                      [(D_MODEL, F32)], [(D_MODEL, D_MODEL), (D_PLE, D_MODEL), (1, D_MODEL), (1, D_MODEL)], comm=comm)


def _loss_call(t_len, tm, h, target):
    def fn(i, n, rows, prevs, nexts, fulls):
        hv, tv = rows
        err = hv - tv
        part = 0.5 * jnp.sum(jnp.mean(err * err, axis=-1, keepdims=True), axis=0, keepdims=True)
        return [err * (1.0 / D_MODEL)], [jnp.broadcast_to(part, (8, 128))]

    return _rows_call("loss", fn, t_len, tm, [h, target], [], [(D_MODEL, F32)], [(8, 128)])


def _adamw_call(name, w, g, m, v):
    n_l, n_r, n_c = w.shape
    tr = 256 if n_r % 256 == 0 else n_r

    def body(w_ref, g_ref, m_ref, v_ref, d_ref, nm_ref, nv_ref):
        gv = g_ref[...]
        nm = ADAM_B1 * m_ref[...] + (1.0 - ADAM_B1) * gv
        nv = ADAM_B2 * v_ref[...] + (1.0 - ADAM_B2) * jnp.square(gv)
        m_hat = nm / (1.0 - ADAM_B1 ** ADAM_STEP)
        v_hat = nv / (1.0 - ADAM_B2 ** ADAM_STEP)
        d_ref[...] = -ADAM_LR * (m_hat / (jnp.sqrt(v_hat) + ADAM_EPS) + ADAM_WD * w_ref[...])
        nm_ref[...] = nm
        nv_ref[...] = nv

    spec = pl.BlockSpec((1, tr, n_c), lambda l, r: (l, r, 0))
    return pl.pallas_call(
        body, name=name, grid=(n_l, n_r // tr), in_specs=[spec] * 4, out_specs=[spec] * 3,
        out_shape=[jax.ShapeDtypeStruct(w.shape, F32)] * 3,
        compiler_params=_cparams(("parallel", "parallel")),
    )(w, g, m, v)


ANY = pl.BlockSpec(memory_space=pl.ANY)


def _place():
    x, y, c = lax.axis_index("x"), lax.axis_index("y"), lax.axis_index("c")
    chips = [(1 - x, y), (x, 1 - y), (1 - x, 1 - y)]
    return x, y, c, 2 * x + y, chips


def _remote(src, dst, send_sem, recv_sem, dev):
    return pltpu.make_async_remote_copy(src_ref=src, dst_ref=dst, send_sem=send_sem, recv_sem=recv_sem,
                                        device_id=dev, device_id_type=MESH_T)


class _GatherPlan:
    def __init__(self, shards, lo, nl):
        self.shapes = [s.shape for s in shards]
        self.lo, self.nl, self.n = lo, nl, len(shards)
        self.split = [s.shape[1] % 32 == 0 for s in shards]
        self.out_shape = [jax.ShapeDtypeStruct((3, nl, *s.shape[1:]), s.dtype) for s in shards]
        self.scratch = [pltpu.SemaphoreType.DMA((6 * self.n,)), pltpu.SemaphoreType.DMA((6 * self.n,))]

    def _views(self, ins, outs, a, c):
        lay, all_l = pl.ds(self.lo, self.nl), pl.ds(0, self.nl)
        if not self.split[a]:
            return ins[a].at[lay], (lambda j: outs[a].at[j]), None
        hr = self.shapes[a][1] // 2
        mine, other = pl.ds(c * hr, hr), pl.ds((1 - c) * hr, hr)
        return ins[a].at[lay, mine], (lambda j: outs[a].at[j, all_l, mine]), (lambda j: outs[a].at[j, all_l, other])

    def _ici(self, ins, outs, ssem, rsem):
        x, y, c, q, chips = _place()
        cps = []
        for a in range(self.n):
            src, land, _ = self._views(ins, outs, a, c)
            for j, chip in enumerate(chips):
                cps.append(_remote(src, land(j), ssem.at[6 * a + j], rsem.at[6 * a + j], (*chip, c)))
        return cps

    def start(self, ins, outs, ssem, rsem):
        for cp in self._ici(ins, outs, ssem, rsem):
            cp.start()

    def finish(self, ins, outs, ssem, rsem):
        x, y, c, q, chips = _place()
        sib = (x, y, 1 - c)
        cps = self._ici(ins, outs, ssem, rsem)
        for a in range(self.n):
            _, land, _ = self._views(ins, outs, a, c)
            for j in range(3):
                cps[3 * a + j].wait_recv()
                if self.split[a]:
                    cps.append(_remote(land(j), land(j), ssem.at[6 * a + 3 + j], rsem.at[6 * a + 3 + j], sib))
                    cps[-1].start()
        for a in range(self.n):
            _, _, other = self._views(ins, outs, a, c)
            for j in range(3):
                if self.split[a]:
                    _remote(other(j), other(j), ssem.at[6 * a + 3 + j], rsem.at[6 * a + 3 + j], sib).wait_recv()
        for cp in cps:
            cp.wait_send()


def _allgather_weights(shards, lo, nl):
    plan = _GatherPlan(shards, lo, nl)
    n = plan.n

    def body(*refs):
        ins, outs, sems = refs[:n], refs[n:2 * n], refs[2 * n:]
        plan.start(ins, outs, *sems)
        plan.finish(ins, outs, *sems)

    return pl.pallas_call(body, name="allgather_weights", in_specs=[ANY] * n, out_specs=[ANY] * n,
                          out_shape=plan.out_shape, scratch_shapes=plan.scratch)(*shards)


def _by_chip(own, others, chip):
    by_mask = jnp.stack([own, others[1], others[0], others[2]])
    return jnp.stack([lax.dynamic_index_in_dim(by_mask, jnp.bitwise_xor(chip, r), 0, keepdims=False) for r in range(N_CHIPS)])


class _PairPlan:
    def __init__(self, grads):
        self.n = len(grads)
        self.half = [g.shape[2] // 2 for g in grads]
        self.layers = [g.shape[1] for g in grads]
        self.out_shape = [jax.ShapeDtypeStruct((N_CHIPS, g.shape[1], g.shape[2] // 2, g.shape[3]), g.dtype) for g in grads]
        self.scratch = [pltpu.SemaphoreType.DMA((self.n,)), pltpu.SemaphoreType.DMA((self.n,))]

    def _copies(self, ins, outs, ssem, rsem):
        x, y, c, q, chips = _place()
        return [_remote(ins[a].at[pl.ds(0, N_CHIPS), pl.ds(0, self.layers[a]), pl.ds((1 - c) * self.half[a], self.half[a])],
                        outs[a], ssem.at[a], rsem.at[a], (x, y, 1 - c)) for a in range(self.n)]

    def start(self, ins, outs, ssem, rsem):
        for cp in self._copies(ins, outs, ssem, rsem):
            cp.start()

    def finish(self, ins, outs, ssem, rsem):
        for cp in self._copies(ins, outs, ssem, rsem):
            cp.wait()


class _ChipPlan:
    def __init__(self, parts):
        self.n = len(parts)
        self.out_shape = [jax.ShapeDtypeStruct((3, *s.shape[1:]), s.dtype) for s in parts]
        self.scratch = [pltpu.SemaphoreType.DMA((3 * self.n,)), pltpu.SemaphoreType.DMA((3 * self.n,))]

    def _copies(self, ins, outs, ssem, rsem):
        x, y, c, q, chips = _place()
        return [_remote(ins[a].at[2 * cx + cy], outs[a].at[j], ssem.at[3 * a + j], rsem.at[3 * a + j], (cx, cy, c))
                for a in range(self.n) for j, (cx, cy) in enumerate(chips)]

    def start(self, ins, outs, ssem, rsem):
        for cp in self._copies(ins, outs, ssem, rsem):
            cp.start()

    def finish(self, ins, outs, ssem, rsem):
        for cp in self._copies(ins, outs, ssem, rsem):
            cp.wait()


def _exchange_call(name, plan, arrays):
    n = plan.n

    def body(*refs):
        ins, outs, sems = refs[:n], refs[n:2 * n], refs[2 * n:]
        plan.start(ins, outs, *sems)
        plan.finish(ins, outs, *sems)

    return pl.pallas_call(body, name=name, in_specs=[ANY] * n, out_specs=[ANY] * n,
                          out_shape=plan.out_shape, scratch_shapes=plan.scratch)(*arrays)


def _pair_sum(name, g, peer, c_arr):
    _, n_l, half, n_c = peer.shape
    tr = 256 if half % 256 == 0 else half
    nb = half // tr

    def body(c_ref, g_ref, p_ref, o_ref):
        o_ref[...] = (g_ref[...] + p_ref[...]).astype(o_ref.dtype)

    blk = (1, 1, tr, n_c)
    return pl.pallas_call(
        body, name=name,
        grid_spec=pltpu.PrefetchScalarGridSpec(
            num_scalar_prefetch=1, grid=(N_CHIPS, n_l, nb),
            in_specs=[pl.BlockSpec(blk, lambda qi, li, ri, c_ref: (qi, li, c_ref[0] * nb + ri, 0)),
                      pl.BlockSpec(blk, lambda qi, li, ri, c_ref: (qi, li, ri, 0))],
            out_specs=pl.BlockSpec(blk, lambda qi, li, ri, c_ref: (qi, li, ri, 0))),
        out_shape=jax.ShapeDtypeStruct(peer.shape, BF),
        compiler_params=_cparams(("parallel", "parallel", "parallel")),
    )(c_arr, g, peer)


def _chip_sum(name, own, r, chip_arr):
    _, n_l, n_r, n_c = r.shape
    tr = 256 if n_r % 256 == 0 else n_r

    def body(q_ref, r0, r1, r2, r3, o_ref):
        o_ref[...] = ((r0[0].astype(F32) + r1[0].astype(F32)) + r2[0].astype(F32)) + r3[0].astype(F32)

    blk = (1, 1, tr, n_c)
    return pl.pallas_call(
        body, name=name,
        grid_spec=pltpu.PrefetchScalarGridSpec(
            num_scalar_prefetch=1, grid=(n_l, n_r // tr),
            in_specs=[pl.BlockSpec(blk, lambda li, ri, q_ref: (q_ref[0], li, ri, 0))]
            + [pl.BlockSpec(blk, lambda li, ri, q_ref, s=s: (s, li, ri, 0)) for s in range(3)],
            out_specs=pl.BlockSpec((1, tr, n_c), lambda li, ri, q_ref: (li, ri, 0))),
        out_shape=jax.ShapeDtypeStruct((n_l, n_r, n_c), F32),
        compiler_params=_cparams(("parallel", "parallel")),
    )(chip_arr, own, r, r, r)


def _pair_share(halves):
    n = len(halves)

    def body(*refs):
        ins, outs = refs[:n], refs[n:2 * n]
        send_sems, recv_sems = refs[2 * n:]
        x, y, c, q, chips = _place()
        cps = [_remote(ins[a], outs[a], send_sems.at[a], recv_sems.at[a], (x, y, 1 - c)) for a in range(n)]
        for cp in cps:
            cp.start()
        for cp in cps:
            cp.wait()

    return pl.pallas_call(
        body, name="grad_pair_share", in_specs=[ANY] * n, out_specs=[ANY] * n,
        out_shape=[jax.ShapeDtypeStruct(h.shape, h.dtype) for h in halves],
        scratch_shapes=[pltpu.SemaphoreType.DMA((n,)), pltpu.SemaphoreType.DMA((n,))],
    )(*halves)


def _allreduce_small(v):
    n_r = v.shape[0]

    def body(v_ref, o_ref, slots, send_sems, recv_sems):
        x, y, c = lax.axis_index("x"), lax.axis_index("y"), lax.axis_index("c")
        me = 4 * x + 2 * y + c
        slots[me] = v_ref[...]
        cps = []
        for r in range(1, 8):
            px = 1 - x if r & 4 else x
            py = 1 - y if r & 2 else y
            pc = 1 - c if r & 1 else c
            cps.append(_remote(v_ref, slots.at[me], send_sems.at[r - 1], recv_sems.at[r - 1], (px, py, pc)))
            cps[-1].start()
        for cp in cps:
            cp.wait()
        tot = slots[0]
        for d in range(1, 8):
            tot = tot + slots[d]
        o_ref[...] = tot

    return pl.pallas_call(
        body, name="allreduce_small",
        in_specs=[pl.BlockSpec(memory_space=pltpu.VMEM)], out_specs=pl.BlockSpec(memory_space=pltpu.VMEM),
        out_shape=jax.ShapeDtypeStruct(v.shape, F32),
        scratch_shapes=[pltpu.VMEM((8, n_r, 128), F32), pltpu.SemaphoreType.DMA((7,)), pltpu.SemaphoreType.DMA((7,))],
        compiler_params=pltpu.CompilerParams(vmem_limit_bytes=VMEM_LIMIT_BYTES),
    )(v)


def _cols_full(g, l):
    s = g[:, l]
    return s.transpose(1, 0, 2).reshape(s.shape[1], -1)


def _rows_full(g, l):
    s = g[:, l]
    return s.reshape(-1, s.shape[-1])


def _cols_split(full):
    r = full.shape[0]
    return full.reshape(r, N_CHIPS, -1).transpose(1, 0, 2)


def _rows_split(full):
    return full.reshape(N_CHIPS, -1, full.shape[-1])


def _pack(parts):
    flat = []
    for a in parts:
        f = a.reshape(-1).astype(F32)
        flat.append(jnp.pad(f, (0, (-f.shape[0]) % 1024)))
    return jnp.concatenate(flat).reshape(-1, 128)


def _unpack(buf, shapes):
    flat = buf.reshape(-1)
    out, off = [], 0
    for s in shapes:
        size = 1
        for d in s:
            size *= d
        out.append(flat[off:off + size].reshape(s))
        off += size + (-size) % 1024
    return out


def kernel(x, p, g_mix_pre, w_in, b_forget, w_conf_dw, conf_ln_g, conf_ln_b, w_conf_pw, w_sc, w_pool, pool_scale, w_out, g_mix_post, g_mlp_pre, w_up, w_down, g_mlp_post, g_ple_pre, w_ple_gate, w_ple_proj, g_ple_post, loss_target, m_g_mix_pre, m_w_in, m_b_forget, m_w_conf_dw, m_conf_ln_g, m_conf_ln_b, m_w_conf_pw, m_w_sc, m_w_pool, m_pool_scale, m_w_out, m_g_mix_post, m_g_mlp_pre, m_w_up, m_w_down, m_g_mlp_post, m_g_ple_pre, m_w_ple_gate, m_w_ple_proj, m_g_ple_post, v_g_mix_pre, v_w_in, v_b_forget, v_w_conf_dw, v_conf_ln_g, v_conf_ln_b, v_w_conf_pw, v_w_sc, v_w_pool, v_pool_scale, v_w_out, v_g_mix_post, v_g_mlp_pre, v_w_up, v_w_down, v_g_mlp_post, v_g_ple_pre, v_w_ple_gate, v_w_ple_proj, v_g_ple_post):
    names = ['g_mix_pre', 'w_in', 'b_forget', 'w_conf_dw', 'conf_ln_g', 'conf_ln_b', 'w_conf_pw', 'w_sc', 'w_pool', 'pool_scale',
             'w_out', 'g_mix_post', 'g_mlp_pre', 'w_up', 'w_down', 'g_mlp_post', 'g_ple_pre', 'w_ple_gate', 'w_ple_proj', 'g_ple_post']
    env = locals()
    wts = {k: env[k] for k in names}
    mom = {k: env["m_" + k] for k in names}
    var = {k: env["v_" + k] for k in names}

    t_len = x.shape[1]
    tb = min(512, t_len)
    tm = tb
    th = min(1024, t_len)
    tq = min(512, max(t_len // 2, 128))
    tt = min(1024, t_len)
    chip = 2 * lax.axis_index("x") + lax.axis_index("y")
    core = lax.axis_index("c")

    big = ['w_in', 'w_conf_pw', 'w_out', 'w_up', 'w_down', 'w_ple_gate', 'w_ple_proj']
    tiny = ['w_conf_dw', 'w_sc']
    own_shards = [wts[k].astype(BF) for k in big] + [wts[k] for k in tiny]

    def chip_order(others, lo, nl):
        return {k: _by_chip(own[lo:lo + nl], oth, chip) for k, own, oth in zip(big + tiny, own_shards, others)}

    def layer_weights(gat, li, l):
        w_full = _cols_full(gat['w_in'], li)
        w_a = jnp.concatenate([w_full[:, :F_OFF], w_full[:, F_OFF + N_HEADS:], w_full[:, F_OFF:F_OFF + N_HEADS],
                               jnp.zeros((D_MODEL, Z_W - D_IN), BF)], axis=1)
        w_bd = jnp.zeros((D_GRP, D_GRP), F32)
        for g in range(4):
            w_bd = lax.dynamic_update_slice(w_bd, wts['w_pool'][l, g], (64 * g, 64 * g))
        row = lambda a: a[l][None, :]
        return dict(
            w_a=w_a, w_dw=jnp.pad(_cols_full(gat['w_conf_dw'], li), ((0, 1), (0, 0))), w_pw=_rows_full(gat['w_conf_pw'], li),
            w_sc=jnp.pad(_cols_full(gat['w_sc'], li), ((0, 5), (0, 0))), w_bd=w_bd.astype(BF),
            w_out=_rows_full(gat['w_out'], li), w_up=_cols_full(gat['w_up'], li), w_down=_rows_full(gat['w_down'], li),
            w_gate=_rows_full(gat['w_ple_gate'], li), w_proj=_cols_full(gat['w_ple_proj'], li),
            b_f=jnp.pad(wts['b_forget'][l], (0, F_PAD - N_HEADS))[None, :],
            ln_g=row(wts['conf_ln_g']), ln_b=row(wts['conf_ln_b']), pool_scale=row(wts['pool_scale']),
            g_mix_pre=row(wts['g_mix_pre']), g_mix_post=row(wts['g_mix_post']), g_mlp_pre=row(wts['g_mlp_pre']),
            g_mlp_post=row(wts['g_mlp_post']), g_ple_pre=row(wts['g_ple_pre']), g_ple_post=row(wts['g_ple_post']))

    lw = [layer_weights(chip_order(_allgather_weights(own_shards, 0, 1), 0, 1), 0, 0)]

    h = x[0]
    saved = []
    for l in range(DEPTH):
        w = lw[l]
        s = dict(h0=h)
        s['zc'], s['qa'], kv, s['zs'], s['zp'], s['zf'] = _mix_in_fwd(t_len, tb, h, w['g_mix_pre'], w['w_a'])
        s['cv'], y_conf = _conf_fwd(t_len, th, s['zc'], w['w_dw'], w['ln_g'], w['ln_b'], w['w_pw'])
        s['ka'], s['va'] = _fox_prep(t_len, s['zf'], kv, w['b_f'])
        if l == 0:
            s['o'], *others = _fox_fwd(t_len, tq, s['qa'], s['ka'], s['va'],
                                       gather=(_GatherPlan(own_shards, 1, DEPTH - 1), own_shards))
            rest = chip_order(others, 1, DEPTH - 1)
            lw += [layer_weights(rest, li, li + 1) for li in range(DEPTH - 1)]
        else:
            (s['o'],) = _fox_fwd(t_len, tq, s['qa'], s['ka'], s['va'])
        y_sc, y_pool = _sconv_pool_fwd(t_len, th, s['zs'], s['zp'], w['w_sc'], w['w_bd'], w['pool_scale'])
        s['ys'] = [y_conf, s['o'], y_sc, y_pool]
        s['mix'], h = _mix_out_fwd(t_len, tb, s['ys'], h, w['w_out'], w['g_mix_post'])
        s['h1'] = h
        s['up'] = _mlp_up_fwd(t_len, tb, h, w['g_mlp_pre'], w['w_up'])
        s['ff'], h = _mlp_down_fwd(t_len, tb, s['up'], h, w['w_down'], w['g_mlp_post'])
        s['h2'] = h
        s['gpre'], s['pe'], h = _ple_fwd(t_len, tb, h, p[l, 0], w['g_ple_pre'], w['w_gate'], w['w_proj'], w['g_ple_post'])
        saved.append(s)

    dh, loss_part = _loss_call(t_len, tm, h, loss_target[0])

    grads = [None] * DEPTH
    split = dict(w_in=_cols_split, w_conf_pw=_rows_split, w_out=_rows_split, w_up=_cols_split, w_down=_rows_split,
                 w_ple_gate=_rows_split, w_ple_proj=_cols_split)
    c_arr = core.astype(jnp.int32).reshape(1)
    chip_arr = chip.astype(jnp.int32).reshape(1)

    def contrib_of(layers):
        return [jnp.stack([split[k](grads[l][k]) for l in layers], axis=1) for k in big]

    def pair_sums(tag, contrib, peer):
        return [_pair_sum(f"grad_pair_sum_{tag}_{k}", a, b, c_arr) for k, a, b in zip(big, contrib, peer)]

    def finish_reduce(tag, parts, landed):
        halves = [_chip_sum(f"grad_chip_sum_{tag}_{k}", own, r, chip_arr) for k, own, r in zip(big, parts, landed)]
        full = []
        for mine, theirs in zip(halves, _pair_share(halves)):
            both = jnp.stack([mine, theirs])
            full.append(jnp.concatenate([lax.dynamic_index_in_dim(both, core, 0, keepdims=False),
                                         lax.dynamic_index_in_dim(both, 1 - core, 0, keepdims=False)], axis=1))
        return full

    for l in reversed(range(DEPTH)):
        w, s, g = lw[l], saved[l], {}
        contrib13 = contrib_of(range(1, DEPTH)) if l == 0 else []
        dh, g['w_ple_gate'], g['w_ple_proj'], g['g_ple_pre'], g['g_ple_post'], *peer13 = _ple_bwd(
            t_len, tb, dh, s['h2'], s['gpre'], s['pe'], p[l, 0], w['g_ple_pre'], w['w_gate'], w['g_ple_post'],
            comm=(_PairPlan(contrib13), contrib13) if l == 0 else None)
        parts13 = pair_sums("l13", contrib13, peer13) if l == 0 else []
        dff, dup, g['g_mlp_post'] = _mlp_bwd_a(t_len, tm, dh, s['ff'], s['up'], w['w_down'], w['g_mlp_post'])
        dh, hn, g['g_mlp_pre'] = _mlp_bwd_b(t_len, tb, dh, s['h1'], dup, w['w_up'], w['g_mlp_pre'])
        g['w_up'] = _mm_tn("mlp_dw_up", hn, dup, D_MODEL, 1024, tt)
        g['w_down'] = _mm_tn("mlp_dw_down", s['up'], dff, 1024, D_MODEL, tt, pro=lambda u: jnp.square(jnp.maximum(u, 0.0)))
        dy_conf, dy_att, dy_sc, dy_pool, g['w_out'], g['g_mix_post'] = _mix_out_bwd(t_len, tb, dh, s['mix'], s['ys'], w['w_out'], w['g_mix_post'])
        dzc, g['w_conf_dw'], g['conf_ln_g'], g['conf_ln_b'], g['w_conf_pw'] = _conf_bwd(
            t_len, tm, s['zc'], s['cv'], dy_conf, w['w_dw'], w['ln_g'], w['ln_b'], w['w_pw'])
        dzs, g['w_sc'] = _sconv_bwd(t_len, th, s['zs'], dy_sc, w['w_sc'])
        dzp, d_wbd, g['pool_scale'] = _pool_bwd(t_len, th, s['zp'], dy_pool, w['w_bd'], w['pool_scale'])
        g['w_pool'] = jnp.stack([d_wbd[64 * a:64 * (a + 1), 64 * a:64 * (a + 1)] for a in range(4)])
        dqa, dka, dva, *landed13 = _fox_bwd(t_len, tq, s['qa'], s['ka'], s['va'], dy_att, s['o'],
                                            comm=(_ChipPlan(parts13), parts13) if l == 0 else None)
        dzf, d_bf = _fox_post(t_len, dqa, dka, s['zf'], w['b_f'])
        g['b_forget'] = d_bf[0, :N_HEADS]
        dh, xn, dz, g['g_mix_pre'] = _mix_in_bwd(t_len, tb, dh, s['h0'], dzc, dqa, dka, dva, dzs, dzp, dzf, w['g_mix_pre'], w['w_a'])
        d_wa = _mm_tn("mix_dw_in", xn, dz, D_MODEL, Z_W, tt)
        g['w_in'] = jnp.concatenate([d_wa[:, :F_OFF], d_wa[:, Z_F:Z_F + N_HEADS], d_wa[:, F_OFF:Z_F]], axis=1)
        g['w_conf_dw'] = g['w_conf_dw'][:CONF_K]
        g['w_sc'] = g['w_sc'][:SC_K]
        grads[l] = g
    grad_x = dh[None]

    contrib0 = contrib_of([0])
    parts0 = pair_sums("l0", contrib0, _exchange_call("grad_pair_exchange", _PairPlan(contrib0), contrib0))
    first = finish_reduce("l0", parts0, _exchange_call("grad_chip_exchange", _ChipPlan(parts0), parts0))
    later = finish_reduce("l13", parts13, landed13)
    reduced = {k: jnp.concatenate([a, b], axis=0) for k, a, b in zip(big, first, later)}

    small = [k for k in names if k not in big]
    small_shapes = [(DEPTH, *grads[0][k].shape) for k in small]
    packed = _pack([jnp.stack([grads[l][k] for l in range(DEPTH)]) for k in small] + [loss_part])
    summed = _allreduce_small(packed)
    small_sum = _unpack(summed, small_shapes + [(8, 128)])
    loss = small_sum[-1][0, 0]
    for k, a in zip(small, small_sum[:-1]):
        if k in tiny:
            a = lax.dynamic_slice_in_dim(a, chip * 64, 64, axis=2)
        reduced[k] = a.reshape(wts[k].shape)

    delta_w, new_m, new_v = {}, {}, {}
    for k in names:
        shp = wts[k].shape
        as3 = (lambda a: a.reshape(shp[0], -1, shp[-1])) if len(shp) > 2 else (lambda a: a.reshape(1, shp[0], shp[1]))
        d, nm, nv = _adamw_call("adamw_" + k, as3(wts[k]), as3(reduced[k]), as3(mom[k]), as3(var[k]))
        delta_w[k], new_m[k], new_v[k] = d.reshape(shp), nm.reshape(shp), nv.reshape(shp)

    return (loss, grad_x, *[reduced[k] for k in names], *[delta_w[k] for k in names],
            *[new_m[k] for k in names], *[new_v[k] for k in names])
```
